```python
import jax, jax.numpy as jnp
from jax import lax
import numpy as np

D_MODEL = 1024
BATCH = 8
SEQ = 2048
DEPTH = 1

MIX_WIDTH = D_MODEL
HG_WIDTH = MIX_WIDTH // 2
HG_HEAD_DIM = 128
HG_HEADS = HG_WIDTH // HG_HEAD_DIM
GL_WIDTH = MIX_WIDTH - HG_WIDTH
GL_HEADS = 4
GL_DV = GL_WIDTH // GL_HEADS
GL_DK = GL_DV // 2
GL_QK = GL_HEADS * GL_DK
GATE_RANK = 16
GATE_TAU = 16.0
CHUNK = 16
N_GROUPS = 8
EXPERTS_PER_GROUP = 8
N_EXPERTS = N_GROUPS * EXPERTS_PER_GROUP
TOP_K = 2
D_EXPERT = D_MODEL // 2
MOE_BLOCK = 128
DEEPNORM_ALPHA = (2.0 * DEPTH) ** 0.25
DEEPNORM_BETA = (8.0 * DEPTH) ** -0.25
LN_EPS = 1e-5
IN_SPLITS = (HG_WIDTH, HG_WIDTH, HG_WIDTH, HG_WIDTH, GL_QK, GL_QK, GL_WIDTH, GATE_RANK, GL_WIDTH)
IN_COLS = sum(IN_SPLITS)

kernel_name = 'hymba_hgrn2_gla_hmoe_deepnorm'


def layer_norm(x, g, b):
    xf = x.astype(jnp.float32)
    mu = jnp.mean(xf, -1, keepdims=True)
    var = jnp.mean(jnp.square(xf - mu), -1, keepdims=True)
    return ((xf - mu) * lax.rsqrt(var + LN_EPS) * g.astype(jnp.float32) + b.astype(jnp.float32)).astype(x.dtype)


def head_rms_norm(o, gain):
    B, T, H, dv = o.shape
    o = o * lax.rsqrt(jnp.mean(jnp.square(o), -1, keepdims=True) + LN_EPS)
    return o.reshape(B, T, H * dv) * gain.astype(jnp.float32)


def to_chunks(t):
    B, T, H, d = t.shape
    return t.reshape(B, T // CHUNK, CHUNK, H, d).transpose(0, 3, 1, 2, 4)


def chunk_gla(q, k, v, log_g):
    B, T, H, dk = q.shape
    dv = v.shape[-1]
    q, k, v, lg = [to_chunks(t.astype(jnp.float32)) for t in (q, k, v, log_g)]
    b = jnp.cumsum(lg, axis=3)
    b_last = b[..., -1:, :]
    causal = jnp.tril(jnp.ones((CHUNK, CHUNK), bool))[:, :, None]
    diff = b[..., :, None, :] - b[..., None, :, :]
    decay = jnp.exp(jnp.where(causal, diff, -jnp.inf))
    scores = jnp.einsum('bhntsk,bhnsk->bhnts', q[..., :, None, :] * decay, k)
    o_intra = jnp.einsum('bhnts,bhnsv->bhntv', scores, v)
    u = jnp.einsum('bhnsk,bhnsv->bhnkv', k * jnp.exp(b_last - b), v)
    g_chunk = jnp.exp(b_last[..., 0, :])

    def step(S, inp):
        g_n, u_n = inp
        return g_n[..., None] * S + u_n, S

    _, s_prev = lax.scan(step, jnp.zeros((B, H, dk, dv), jnp.float32),
                         (jnp.moveaxis(g_chunk, 2, 0), jnp.moveaxis(u, 2, 0)))
    s_prev = jnp.moveaxis(s_prev, 0, 2)
    o_inter = jnp.einsum('bhntk,bhnkv->bhntv', q * jnp.exp(b), s_prev)
    o = o_intra + o_inter
    return o.transpose(0, 2, 3, 1, 4).reshape(B, T, H, dv)


def hybrid_mixer(x, w_in, w_a2, b_a, lb, norm_h, norm_g, w_out):
    B, T, _ = x.shape
    split_at = [int(i) for i in np.cumsum(IN_SPLITS)[:-1]]
    hq, hf, hi, hg, gq, gk, gv, ga, gg = jnp.split(x @ w_in, split_at, axis=-1)

    f = lb + (1.0 - lb) * jax.nn.sigmoid(hf.astype(jnp.float32))
    o_h = chunk_gla((hq * HG_HEAD_DIM ** -0.5).reshape(B, T, HG_HEADS, HG_HEAD_DIM),
                    (1.0 - f).reshape(B, T, HG_HEADS, HG_HEAD_DIM),
                    hi.reshape(B, T, HG_HEADS, HG_HEAD_DIM),
                    jnp.log(f).reshape(B, T, HG_HEADS, HG_HEAD_DIM))
    o_h = head_rms_norm(o_h, norm_h) * jax.nn.silu(hg.astype(jnp.float32))

    log_a = jax.nn.log_sigmoid((ga @ w_a2 + b_a).astype(jnp.float32)) / GATE_TAU
    o_g = chunk_gla((gq * GL_DK ** -0.5).reshape(B, T, GL_HEADS, GL_DK),
                    gk.reshape(B, T, GL_HEADS, GL_DK),
                    gv.reshape(B, T, GL_HEADS, GL_DV),
                    log_a.reshape(B, T, GL_HEADS, GL_DK))
    o_g = head_rms_norm(o_g, norm_g) * jax.nn.silu(gg.astype(jnp.float32))

    o = jnp.concatenate([o_h, o_g], axis=-1).astype(x.dtype)
    return o @ w_out


def hierarchical_moe(x, w_group_router, w_expert_router, w_gate, w_up, w_down):
    B, T, D = x.shape
    n_tok = B * T
    xf = x.reshape(n_tok, D)
    p_group = jax.nn.softmax((xf @ w_group_router).astype(jnp.float32), axis=-1)
    p_top_group, g_idx = lax.top_k(p_group, 1)
    e_logits = (xf @ w_expert_router).astype(jnp.float32).reshape(n_tok, N_GROUPS, EXPERTS_PER_GROUP)
    e_logits = jnp.take_along_axis(e_logits, g_idx[:, :, None], axis=1)[:, 0]
    p_top, e_local = lax.top_k(jax.nn.softmax(e_logits, axis=-1), TOP_K)
    gates = p_top_group * p_top / jnp.sum(p_top, -1, keepdims=True)
    expert_id = g_idx * EXPERTS_PER_GROUP + e_local

    n_assign = n_tok * TOP_K
    flat_e = expert_id.reshape(-1)
    flat_tok = jnp.repeat(jnp.arange(n_tok, dtype=jnp.int32), TOP_K)
    flat_w = gates.reshape(-1)
    order = jnp.argsort(flat_e)
    sorted_e = flat_e[order]
    counts = jnp.bincount(flat_e, length=N_EXPERTS)
    starts = jnp.cumsum(counts) - counts
    padded = (counts + MOE_BLOCK - 1) // MOE_BLOCK * MOE_BLOCK
    padded_ends = jnp.cumsum(padded)
    padded_starts = padded_ends - padded
    dest = padded_starts[sorted_e] + jnp.arange(n_assign, dtype=jnp.int32) - starts[sorted_e]
    n_blocks = -(-n_assign // MOE_BLOCK) + N_EXPERTS
    n_slots = n_blocks * MOE_BLOCK
    slot_tok = jnp.full((n_slots,), n_tok, jnp.int32).at[dest].set(flat_tok[order])
    slot_w = jnp.zeros((n_slots,), jnp.float32).at[dest].set(flat_w[order])
    block_expert = jnp.minimum(
        jnp.searchsorted(padded_ends, jnp.arange(n_blocks, dtype=jnp.int32) * MOE_BLOCK, side='right'),
        N_EXPERTS - 1)
    x_pad = jnp.concatenate([xf, jnp.zeros((1, D), xf.dtype)], axis=0)
    xb = x_pad[slot_tok].reshape(n_blocks, MOE_BLOCK, D)

    def expert_block(args):
        xb_, e = args
        h = jax.nn.silu(xb_ @ w_gate[e]) * (xb_ @ w_up[e])
        return h @ w_down[e]

    yb = lax.map(expert_block, (xb, block_expert)).reshape(n_slots, D)
    y = jnp.zeros((n_tok + 1, D), jnp.float32).at[slot_tok].add(yb.astype(jnp.float32) * slot_w[:, None])
    return y[:n_tok].astype(x.dtype).reshape(B, T, D)


def setup_inputs(seed: int = 0) -> dict:
    key = jax.random.key(seed)
    ks = jax.random.split(key, 20)
    L = DEPTH
    f32 = jnp.float32

    def nrm(k, shape, scale):
        return jax.random.normal(k, shape, f32) * scale

    col_scale = jnp.concatenate([jnp.full((n,), DEEPNORM_BETA if i in (2, 6) else 1.0, f32)
                                 for i, n in enumerate(IN_SPLITS)])
    return {
        'x': nrm(ks[0], (BATCH, SEQ, D_MODEL), 1.0),
        'w_in': nrm(ks[1], (L, D_MODEL, IN_COLS), D_MODEL ** -0.5) * col_scale,
        'w_a2': nrm(ks[2], (L, GATE_RANK, GL_QK), GATE_RANK ** -0.5),
        'b_a': nrm(ks[3], (L, GL_QK), 0.1),
        'lb_logits': nrm(ks[4], (L + 1, HG_WIDTH), 0.5),
        'norm_h': 1.0 + nrm(ks[5], (L, HG_WIDTH), 0.02),
        'norm_g': 1.0 + nrm(ks[6], (L, GL_WIDTH), 0.02),
        'w_out': nrm(ks[7], (L, MIX_WIDTH, D_MODEL), MIX_WIDTH ** -0.5 * DEEPNORM_BETA),
        'ln1_g': 1.0 + nrm(ks[8], (L, D_MODEL), 0.02),
        'ln1_b': nrm(ks[9], (L, D_MODEL), 0.02),
        'w_group_router': nrm(ks[10], (L, D_MODEL, N_GROUPS), D_MODEL ** -0.5),
        'w_expert_router': nrm(ks[11], (L, D_MODEL, N_EXPERTS), D_MODEL ** -0.5),
        'w_gate': nrm(ks[12], (L, N_EXPERTS, D_MODEL, D_EXPERT), D_MODEL ** -0.5 * DEEPNORM_BETA),
        'w_up': nrm(ks[13], (L, N_EXPERTS, D_MODEL, D_EXPERT), D_MODEL ** -0.5 * DEEPNORM_BETA),
        'w_down': nrm(ks[14], (L, N_EXPERTS, D_EXPERT, D_MODEL), D_EXPERT ** -0.5 * DEEPNORM_BETA),
        'ln2_g': 1.0 + nrm(ks[15], (L, D_MODEL), 0.02),
        'ln2_b': nrm(ks[16], (L, D_MODEL), 0.02),
    }


def reference(x, w_in, w_a2, b_a, lb_logits, norm_h, norm_g, w_out, ln1_g, ln1_b,
              w_group_router, w_expert_router, w_gate, w_up, w_down, ln2_g, ln2_b):
    lb_all = jnp.cumsum(jax.nn.softmax(lb_logits.astype(jnp.float32), axis=0), axis=0)
    for l in range(DEPTH):
        h = hybrid_mixer(x, w_in[l], w_a2[l], b_a[l], lb_all[l], norm_h[l], norm_g[l], w_out[l])
        x = layer_norm(DEEPNORM_ALPHA * x + h, ln1_g[l], ln1_b[l])
        h = hierarchical_moe(x, w_group_router[l], w_expert_router[l], w_gate[l], w_up[l], w_down[l])
        x = layer_norm(DEEPNORM_ALPHA * x + h, ln2_g[l], ln2_b[l])
    return x
```

```python
import functools

import numpy as np
import jax
import jax.numpy as jnp
from jax import lax
from jax.experimental import pallas as pl
from jax.experimental.pallas import tpu as pltpu

D_MODEL = 1024
HG_WIDTH = 512
HG_HEAD_DIM = 128
HG_HEADS = 4
GL_WIDTH = 512
GL_HEADS = 4
GL_DV = 128
GL_DK = 64
GL_QK = 256
GATE_RANK = 16
GATE_TAU = 16.0
N_GROUPS = 8
EXPERTS_PER_GROUP = 8
N_EXPERTS = 64
TOP_K = 2
D_EXPERT = 512
DEPTH = 1
DEEPNORM_ALPHA = (2.0 * DEPTH) ** 0.25
LN_EPS = 1e-5
IN_SPLITS = (HG_WIDTH, HG_WIDTH, HG_WIDTH, HG_WIDTH, GL_QK, GL_QK, GL_WIDTH, GATE_RANK, GL_WIDTH)

LANES = 128
SUB = 128
N_LEVELS = 7
TT = 256
TD = 256
BLK = 256
IN_COLS_PAD = 3712
ROUTE_BASE = 8
VMEM_LIMIT = 56 * 1024 * 1024

F32 = jnp.float32
BF16 = jnp.bfloat16
NT_DIMS = (((1,), (1,)), ((), ()))
TN_DIMS = (((0,), (0,)), ((), ()))


def _decay_matrices():
    t = np.arange(SUB)[:, None]
    j = np.arange(SUB)[None, :]
    mats = [(j <= t), (j > t)]
    for lvl in range(1, N_LEVELS + 1):
        blk = 1 << lvl
        half = blk >> 1
        r = (t // blk) * blk + half - 1
        right = (t % blk) >= half
        mats.append(np.where(right, (j > r) & (j <= t), (j > t) & (j <= r)))
    w = np.concatenate(mats, axis=0).astype(np.float32)
    return np.concatenate([w, w, w], axis=1)


def _level_map():
    t = np.arange(SUB)[:, None]
    s = np.arange(SUB)[None, :]
    x = np.bitwise_xor(t, s)
    lvl = np.floor(np.log2(np.maximum(x, 1))).astype(np.int32) + 1
    lvl = np.where(x == 0, 0, lvl)
    return np.where(s > t, -1, lvl).astype(np.int32)


def _layer_norm(y, g, b):
    mu = jnp.mean(y, axis=-1, keepdims=True)
    yc = y - mu
    var = jnp.mean(yc * yc, axis=-1, keepdims=True)
    return yc * lax.rsqrt(var + LN_EPS) * g + b


def _mixer_kernel(x_ref, win_ref, wa2_ref, ba_ref, lbl_ref, gain_ref, wout_ref, ln1g_ref, ln1b_ref,
                  wrh_ref, wrl_ref, wstk_ref, lev_ref,
                  x1_ref, route_ref, counts_ref,
                  p_scr, o_scr, st_scr, carry_scr):
    bi = pl.program_id(0)
    ji = pl.program_id(1)

    @pl.when(ji == 0)
    def _():
        st_scr[...] = jnp.zeros_like(st_scr)

    @pl.when((bi == 0) & (ji == 0))
    def _():
        carry_scr[...] = jnp.zeros_like(carry_scr)

    xt = x_ref[...]
    p_scr[...] = jnp.dot(xt.astype(BF16), win_ref[...], preferred_element_type=F32)

    lbl = lbl_ref[...]
    lbe = jnp.exp(lbl - jnp.max(lbl, axis=0, keepdims=True))
    lb = lbe[0:1, :] / jnp.sum(lbe, axis=0, keepdims=True)

    lev = lev_ref[...]
    tcol = lax.broadcasted_iota(jnp.int32, (SUB, 1), 0)
    qrow = [(tcol & ((1 << l) - 1)) >= (1 << (l - 1)) for l in range(1, N_LEVELS + 1)]
    lane = lax.broadcasted_iota(jnp.int32, (1, LANES), 1)
    half_mask = [(lane < GL_DK), (lane >= GL_DK)]

    def subtile(s, carry):
        r0 = pl.multiple_of(s * SUB, SUB)
        rows = pl.ds(r0, SUB)
        hq = p_scr[rows, 0:512]
        hf = p_scr[rows, 512:1024]
        hv = p_scr[rows, 1024:1536]
        hgate = p_scr[rows, 1536:2048]
        gq = p_scr[rows, 2048:2304]
        gk = p_scr[rows, 2304:2560]
        gv = p_scr[rows, 2560:3072]
        ggate = p_scr[rows, 3072:3584]
        ga = p_scr[rows, 3584:3712]

        f = lb + (1.0 - lb) * jax.nn.sigmoid(hf)
        kh = 1.0 - f
        lgh = jnp.log(f)
        z = jnp.dot(ga.astype(BF16), wa2_ref[...], preferred_element_type=F32) + ba_ref[...]
        lga = (jnp.minimum(z, 0.0) - jnp.log1p(jnp.exp(-jnp.abs(z)))) * (1.0 / GATE_TAU)
        lg = jnp.concatenate([lgh, lga], axis=1)

        l_hi = lg.astype(BF16)
        rem = lg - l_hi.astype(F32)
        l_mid = rem.astype(BF16)
        l_lo = (rem - l_mid.astype(F32)).astype(BF16)
        expo = jnp.dot(wstk_ref[...], jnp.concatenate([l_hi, l_mid, l_lo], axis=0),
                       preferred_element_type=F32)
        dec = jnp.exp(expo)

        def group_terms(q, k, c0):
            e_pre = dec[0:SUB, c0:c0 + LANES]
            e_suf = dec[SUB:2 * SUB, c0:c0 + LANES]
            us = []
            for l in range(1, N_LEVELS + 1):
                e_l = dec[(1 + l) * SUB:(2 + l) * SUB, c0:c0 + LANES]
                us.append((jnp.where(qrow[l - 1], q, k) * e_l).astype(BF16))
            return us, (q * e_pre).astype(BF16), k * e_suf, e_pre[SUB - 1:SUB, :]

        def head(hidx, us, qd, kd, qk, v, g_last, mask):
            sc = jnp.where(lev == 0, jnp.sum(qk, axis=-1, keepdims=True), 0.0)
            for l in range(1, N_LEVELS + 1):
                u = us[l - 1]
                lhs = u if mask is None else jnp.where(mask, u, jnp.zeros_like(u))
                g = lax.dot_general(lhs, u, NT_DIMS, preferred_element_type=F32)
                sc = sc + jnp.where(lev == l, g, 0.0)
            vb = v.astype(BF16)
            st = st_scr[hidx]
            o = jnp.dot(sc.astype(BF16), vb, preferred_element_type=F32)
            o = o + lax.dot_general(qd, st.astype(BF16), NT_DIMS, preferred_element_type=F32)
            ut = lax.dot_general(vb, kd.astype(BF16), TN_DIMS, preferred_element_type=F32)
            st_scr[hidx] = st * g_last + ut
            return o

        def finish(o, gate, c0):
            o = o * lax.rsqrt(jnp.mean(o * o, axis=-1, keepdims=True) + LN_EPS)
            o = o * gain_ref[:, c0:c0 + LANES] * (gate * jax.nn.sigmoid(gate))
            o_scr[rows, c0:c0 + LANES] = o

        for g in range(HG_HEADS):
            c0 = g * LANES
            q = hq[:, c0:c0 + LANES] * (HG_HEAD_DIM ** -0.5)
            k = kh[:, c0:c0 + LANES]
            us, qd, kd, g_last = group_terms(q, k, c0)
            o = head(g, us, qd, kd, q * k, hv[:, c0:c0 + LANES], g_last, None)
            finish(o, hgate[:, c0:c0 + LANES], c0)

        for p in range(GL_HEADS // 2):
            c0 = p * LANES
            q = gq[:, c0:c0 + LANES] * (GL_DK ** -0.5)
            k = gk[:, c0:c0 + LANES]
            us, qd, kd, g_last = group_terms(q, k, HG_WIDTH + c0)
            for hh in range(2):
                h = 2 * p + hh
                m = half_mask[hh]
                o = head(HG_HEADS + h, us, qd, jnp.where(m, kd, 0.0), jnp.where(m, q * k, 0.0),
                         gv[:, h * LANES:(h + 1) * LANES], g_last, m)
                finish(o, ggate[:, h * LANES:(h + 1) * LANES], HG_WIDTH + h * LANES)
        return carry

    lax.fori_loop(0, TT // SUB, subtile, 0)

    hmix = jnp.dot(o_scr[...].astype(BF16), wout_ref[...], preferred_element_type=F32)
    x1 = _layer_norm(DEEPNORM_ALPHA * xt + hmix, ln1g_ref[...], ln1b_ref[...])
    x1_ref[...] = x1

    x1h = x1.astype(BF16)
    x1l = (x1 - x1h.astype(F32)).astype(BF16)
    logits = (jnp.dot(x1h, wrh_ref[...], preferred_element_type=F32)
              + jnp.dot(x1l, wrh_ref[...], preferred_element_type=F32)
              + jnp.dot(x1h, wrl_ref[...], preferred_element_type=F32))

    lane_i = lax.broadcasted_iota(jnp.int32, (TT, LANES), 1)
    lane_f = lane_i.astype(F32)
    neg = jnp.float32(-jnp.inf)
    big = jnp.float32(1e9)

    gl = jnp.where(lane_i < N_GROUPS, logits, neg)
    gmax = jnp.max(gl, axis=-1, keepdims=True)
    gidx = jnp.min(jnp.where(gl == gmax, lane_f, big), axis=-1, keepdims=True)
    p_group = 1.0 / jnp.sum(jnp.exp(gl - gmax), axis=-1, keepdims=True)

    e_lo = ROUTE_BASE + EXPERTS_PER_GROUP * gidx
    el = jnp.where((lane_f >= e_lo) & (lane_f < e_lo + EXPERTS_PER_GROUP), logits, neg)
    m1 = jnp.max(el, axis=-1, keepdims=True)
    i1 = jnp.min(jnp.where(el == m1, lane_f, big), axis=-1, keepdims=True)
    el2 = jnp.where(lane_f == i1, neg, el)
    m2 = jnp.max(el2, axis=-1, keepdims=True)
    i2 = jnp.min(jnp.where(el2 == m2, lane_f, big), axis=-1, keepdims=True)
    r2 = jnp.exp(m2 - m1)
    gate1 = p_group / (1.0 + r2)
    gate2 = p_group * r2 / (1.0 + r2)

    hot1 = lane_f == i1
    hot2 = lane_f == i2
    cnt = (hot1 | hot2).astype(F32)
    tr = lax.broadcasted_iota(jnp.int32, (TT, TT), 0)
    tc = lax.broadcasted_iota(jnp.int32, (TT, TT), 1)
    strict = (tc < tr).astype(BF16)
    base = carry_scr[...] + jnp.dot(strict, cnt.astype(BF16), preferred_element_type=F32)
    rank1 = jnp.sum(jnp.where(hot1, base, 0.0), axis=-1, keepdims=True)
    rank2 = jnp.sum(jnp.where(hot2, base, 0.0), axis=-1, keepdims=True)
    total = carry_scr[...] + jnp.sum(cnt, axis=0, keepdims=True)
    carry_scr[...] = total
    counts_ref[...] = total

    route = jnp.where(lane_i == 0, i1 - ROUTE_BASE, 0.0)
    route = jnp.where(lane_i == 1, i2 - ROUTE_BASE, route)
    route = jnp.where(lane_i == 2, gate1, route)
    route = jnp.where(lane_i == 3, gate2, route)
    route = jnp.where(lane_i == 4, rank1, route)
    route = jnp.where(lane_i == 5, rank2, route)
    route_ref[...] = route


def _expert_kernel(be_ref, nv_ref, tok_ref, tok_next_ref, x1_ref, wg_ref, wu_ref, wd_ref, yb_ref,
                   xbuf, wg_s, wu_s, wd_s, sems):
    i = pl.program_id(0)
    nv = nv_ref[0]

    def row_copy(idx_ref, slot, r):
        t = idx_ref[0, 0, r]
        return pltpu.make_async_copy(x1_ref.at[pl.ds(t, 1), :], xbuf.at[slot, pl.ds(r, 1), :], sems.at[slot])

    def start_gather(idx_ref, slot):
        def body(r, c):
            row_copy(idx_ref, slot, r).start()
            return c
        lax.fori_loop(0, BLK, body, 0, unroll=8)

    def wait_gather(idx_ref, slot):
        def body(r, c):
            row_copy(idx_ref, slot, r).wait()
            return c
        lax.fori_loop(0, BLK, body, 0, unroll=8)

    @pl.when(i < nv)
    def _():
        slot = i % 2

        @pl.when(i == 0)
        def _():
            start_gather(tok_ref, 0)

        @pl.when(i + 1 < nv)
        def _():
            start_gather(tok_next_ref, 1 - slot)

        prev = be_ref[jnp.maximum(i - 1, 0)]

        @pl.when((i == 0) | (be_ref[i] != prev))
        def _():
            wg_s[...] = wg_ref[...].astype(BF16)
            wu_s[...] = wu_ref[...].astype(BF16)
            wd_s[...] = wd_ref[...].astype(BF16)

        wait_gather(tok_ref, slot)
        xb = xbuf[slot].astype(BF16)
        a = jnp.dot(xb, wg_s[...], preferred_element_type=F32)
        u = jnp.dot(xb, wu_s[...], preferred_element_type=F32)
        h = (a * jax.nn.sigmoid(a)) * u
        yb_ref[...] = jnp.dot(h.astype(BF16), wd_s[...], preferred_element_type=F32)

    @pl.when(i >= nv_ref[0])
    def _():
        yb_ref[...] = jnp.zeros_like(yb_ref)


def _combine_kernel(dest_ref, x1_ref, route_ref, yb_ref, g_ref, b_ref, out_ref, ybuf, sem):
    def row_copy(r, k):
        d = dest_ref[0, 0, 2 * r + k]
        return pltpu.make_async_copy(yb_ref.at[pl.ds(d, 1), :], ybuf.at[k, pl.ds(r, 1), :], sem)

    def start(r, c):
        row_copy(r, 0).start()
        row_copy(r, 1).start()
        return c

    def wait(r, c):
        row_copy(r, 0).wait()
        row_copy(r, 1).wait()
        return c

    lax.fori_loop(0, TD, start, 0)
    lax.fori_loop(0, TD, wait, 0)

    y = route_ref[:, 2:3] * ybuf[0] + route_ref[:, 3:4] * ybuf[1]
    out_ref[...] = _layer_norm(DEEPNORM_ALPHA * x1_ref[...] + y, g_ref[...], b_ref[...])


def _const_spec(shape):
    nd = len(shape)
    return pl.BlockSpec(shape, lambda *_: (0,) * nd, pipeline_mode=pl.Buffered(1))


def kernel(x, w_in, w_a2, b_a, lb_logits, norm_h, norm_g, w_out, ln1_g, ln1_b, w_group_router, w_expert_router,
           w_gate, w_up, w_down, ln2_g, ln2_b):
    B, T, D = x.shape
    n_tok = B * T
    assert D == D_MODEL and T % TT == 0 and n_tok % TD == 0
    n_blocks = n_tok * TOP_K // BLK + N_EXPERTS
    n_slots = n_blocks * BLK

    parts = jnp.split(w_in[0], [int(c) for c in np.cumsum(IN_SPLITS)[:-1]], axis=-1)
    hq, hf, hi, hg, gq, gk, gv, ga, gg = parts
    win = jnp.concatenate([hq, hf, hi, hg, gq, gk, gv, gg, ga], axis=-1)
    win = jnp.pad(win, ((0, 0), (0, IN_COLS_PAD - win.shape[1]))).astype(BF16)
    wa2 = jnp.pad(w_a2[0], ((0, LANES - GATE_RANK), (0, 0))).astype(BF16)
    ba = b_a[0].reshape(1, GL_QK).astype(F32)
    gain = jnp.concatenate([norm_h[0], norm_g[0]]).reshape(1, D).astype(F32)
    wout = w_out[0].astype(BF16)
    wr = jnp.concatenate([w_group_router[0], w_expert_router[0]], axis=-1).astype(F32)
    wr = jnp.pad(wr, ((0, 0), (0, LANES - wr.shape[1])))
    wrh = wr.astype(BF16)
    wrl = (wr - wrh.astype(F32)).astype(BF16)
    wstk = jnp.asarray(_decay_matrices(), dtype=BF16)
    lev = jnp.asarray(_level_map())
    n_w = wstk.shape[0]

    x1, route, counts = pl.pallas_call(
        _mixer_kernel,
        grid=(B, T // TT),
        in_specs=[
            pl.BlockSpec((None, TT, D), lambda b, j: (b, j, 0)),
            _const_spec((D, IN_COLS_PAD)),
            _const_spec((LANES, GL_QK)),
            _const_spec((1, GL_QK)),
            _const_spec((DEPTH + 1, HG_WIDTH)),
            _const_spec((1, D)),
            _const_spec((D, D)),
            _const_spec((1, D)),
            _const_spec((1, D)),
            _const_spec((D, LANES)),
            _const_spec((D, LANES)),
            _const_spec((n_w, 3 * SUB)),
            _const_spec((SUB, SUB)),
        ],
        out_specs=[
            pl.BlockSpec((TT, D), lambda b, j: (b * (T // TT) + j, 0)),
            pl.BlockSpec((TT, LANES), lambda b, j: (b * (T // TT) + j, 0)),
            pl.BlockSpec((1, LANES), lambda b, j: (0, 0)),
        ],
        out_shape=[
            jax.ShapeDtypeStruct((n_tok, D), F32),
            jax.ShapeDtypeStruct((n_tok, LANES), F32),
            jax.ShapeDtypeStruct((1, LANES), F32),
        ],
        scratch_shapes=[
            pltpu.VMEM((TT, IN_COLS_PAD), F32),
            pltpu.VMEM((TT, D), F32),
            pltpu.VMEM((HG_HEADS + GL_HEADS, LANES, LANES), F32),
            pltpu.VMEM((1, LANES), F32),
        ],
        compiler_params=pltpu.CompilerParams(
            dimension_semantics=("arbitrary", "arbitrary"), vmem_limit_bytes=VMEM_LIMIT),
        name="mixer",
    )(x, win, wa2, ba, lb_logits.astype(F32), gain, wout, ln1_g[0].reshape(1, D), ln1_b[0].reshape(1, D),
      wrh, wrl, wstk, lev)

    cnt = counts[0, ROUTE_BASE:ROUTE_BASE + N_EXPERTS].astype(jnp.int32)
    padded = (cnt + BLK - 1) // BLK * BLK
    ends = jnp.cumsum(padded)
    starts = ends - padded
    n_valid = (ends[-1] // BLK).astype(jnp.int32)
    blk_ids = jnp.minimum(jnp.arange(n_blocks, dtype=jnp.int32), n_valid - 1)
    block_expert = jnp.sum((ends[None, :] <= (blk_ids * BLK)[:, None]).astype(jnp.int32), axis=1)
    block_expert = jnp.minimum(block_expert, N_EXPERTS - 1)
    e_ids = route[:, 0:TOP_K].astype(jnp.int32)
    dest = (starts[e_ids] + route[:, 4:4 + TOP_K].astype(jnp.int32)).reshape(n_tok // TD, 1, TOP_K * TD)

    tok_ids = jnp.repeat(jnp.arange(n_tok, dtype=jnp.int32), TOP_K)
    slot_tok = jnp.zeros((n_slots,), jnp.int32).at[dest.reshape(-1)].set(tok_ids).reshape(n_blocks, 1, BLK)

    yb = pl.pallas_call(
        _expert_kernel,
        grid_spec=pltpu.PrefetchScalarGridSpec(
            num_scalar_prefetch=2,
            grid=(n_blocks,),
            in_specs=[
                pl.BlockSpec((1, 1, BLK), lambda i, be, nv: (jnp.minimum(i, nv[0] - 1), 0, 0),
                             memory_space=pltpu.SMEM),
                pl.BlockSpec((1, 1, BLK), lambda i, be, nv: (jnp.minimum(i + 1, nv[0] - 1), 0, 0),
                             memory_space=pltpu.SMEM),
                pl.BlockSpec(memory_space=pl.ANY),
                pl.BlockSpec((None, D, D_EXPERT), lambda i, be, nv: (be[i], 0, 0)),
                pl.BlockSpec((None, D, D_EXPERT), lambda i, be, nv: (be[i], 0, 0)),
                pl.BlockSpec((None, D_EXPERT, D), lambda i, be, nv: (be[i], 0, 0)),
            ],
            out_specs=pl.BlockSpec((BLK, D), lambda i, be, nv: (i, 0)),
            scratch_shapes=[
                pltpu.VMEM((2, BLK, D), F32),
                pltpu.VMEM((D, D_EXPERT), BF16),
                pltpu.VMEM((D, D_EXPERT), BF16),
                pltpu.VMEM((D_EXPERT, D), BF16),
                pltpu.SemaphoreType.DMA((2,)),
            ],
        ),
        out_shape=jax.ShapeDtypeStruct((n_slots, D), F32),
        compiler_params=pltpu.CompilerParams(
            dimension_semantics=("arbitrary",), vmem_limit_bytes=VMEM_LIMIT),
        name="experts",
    )(block_expert, n_valid.reshape(1), slot_tok, slot_tok, x1, w_gate[0], w_up[0], w_down[0])

    out = pl.pallas_call(
        _combine_kernel,
        grid=(n_tok // TD,),
        in_specs=[
            pl.BlockSpec((1, 1, TOP_K * TD), lambda i: (i, 0, 0), memory_space=pltpu.SMEM),
            pl.BlockSpec((TD, D), lambda i: (i, 0)),
            pl.BlockSpec((TD, LANES), lambda i: (i, 0)),
            pl.BlockSpec(memory_space=pl.ANY),
            pl.BlockSpec((1, D), lambda i: (0, 0)),
            pl.BlockSpec((1, D), lambda i: (0, 0)),
        ],
        out_specs=pl.BlockSpec((TD, D), lambda i: (i, 0)),
        out_shape=jax.ShapeDtypeStruct((n_tok, D), F32),
        scratch_shapes=[pltpu.VMEM((TOP_K, TD, D), F32), pltpu.SemaphoreType.DMA],
        compiler_params=pltpu.CompilerParams(dimension_semantics=("arbitrary",)),
        name="combine",
    )(dest, x1, route, yb, ln2_g[0].reshape(1, D), ln2_b[0].reshape(1, D))

    return out.reshape(B, T, D).astype(x.dtype)
```

```python
import functools

import numpy as np
import jax
import jax.numpy as jnp
from jax import lax
from jax.experimental import pallas as pl
from jax.experimental.pallas import tpu as pltpu

D_MODEL = 1024
HG_WIDTH = 512
HG_HEAD_DIM = 128
HG_HEADS = 4
GL_WIDTH = 512
GL_HEADS = 4
GL_DV = 128
GL_DK = 64
GL_QK = 256
GATE_RANK = 16
GATE_TAU = 16.0
N_GROUPS = 8
EXPERTS_PER_GROUP = 8
N_EXPERTS = 64
TOP_K = 2
D_EXPERT = 512
DEPTH = 1
DEEPNORM_ALPHA = (2.0 * DEPTH) ** 0.25
LN_EPS = 1e-5
IN_SPLITS = (HG_WIDTH, HG_WIDTH, HG_WIDTH, HG_WIDTH, GL_QK, GL_QK, GL_WIDTH, GATE_RANK, GL_WIDTH)

LANES = 128
SUB = 128
N_LEVELS = 7
TT = 256
TD = 256
BLK = 256
IN_COLS_PAD = 3712
ROUTE_BASE = 8
SLAB = 8
META_ROWS = 8
VMEM_LIMIT = 56 * 1024 * 1024

F32 = jnp.float32
BF16 = jnp.bfloat16
NT_DIMS = (((1,), (1,)), ((), ()))
TN_DIMS = (((0,), (0,)), ((), ()))


def _decay_matrices():
    t = np.arange(SUB)[:, None]
    j = np.arange(SUB)[None, :]
    mats = [(j <= t), (j > t)]
    for lvl in range(1, N_LEVELS + 1):
        blk = 1 << lvl
        half = blk >> 1
        r = (t // blk) * blk + half - 1
        right = (t % blk) >= half
        mats.append(np.where(right, (j > r) & (j <= t), (j > t) & (j <= r)))
    w = np.concatenate(mats, axis=0).astype(np.float32)
    return np.concatenate([w, w, w], axis=1)


def _level_map():
    t = np.arange(SUB)[:, None]
    s = np.arange(SUB)[None, :]
    x = np.bitwise_xor(t, s)
    lvl = np.floor(np.log2(np.maximum(x, 1))).astype(np.int32) + 1
    lvl = np.where(x == 0, 0, lvl)
    return np.where(s > t, -1, lvl).astype(np.int32)


def _layer_norm(y, g, b):
    mu = jnp.mean(y, axis=-1, keepdims=True)
    yc = y - mu
    var = jnp.mean(yc * yc, axis=-1, keepdims=True)
    return yc * lax.rsqrt(var + LN_EPS) * g + b


def _mixer_kernel(x_ref, win_ref, wa2_ref, ba_ref, lbl_ref, gain_ref, wout_ref, ln1g_ref, ln1b_ref,
                  wrh_ref, wrl_ref, wstk_ref, lev_ref,
                  x1s_ref, route_ref, meta_ref, counts_ref,
                  p_scr, o_scr, st_scr, carry_scr):
    bi = pl.program_id(0)
    ji = pl.program_id(1)

    @pl.when(ji == 0)
    def _():
        st_scr[...] = jnp.zeros_like(st_scr)

    @pl.when((bi == 0) & (ji == 0))
    def _():
        carry_scr[...] = jnp.zeros_like(carry_scr)

    xt = x_ref[...]
    p_scr[...] = jnp.dot(xt.astype(BF16), win_ref[...], preferred_element_type=F32)

    lbl = lbl_ref[...]
    lbe = jnp.exp(lbl - jnp.max(lbl, axis=0, keepdims=True))
    lb = lbe[0:1, :] / jnp.sum(lbe, axis=0, keepdims=True)

    lev = lev_ref[...]
    tcol = lax.broadcasted_iota(jnp.int32, (SUB, 1), 0)
    qrow = [(tcol & ((1 << l) - 1)) >= (1 << (l - 1)) for l in range(1, N_LEVELS + 1)]
    lane = lax.broadcasted_iota(jnp.int32, (1, LANES), 1)
    half_mask = [(lane < GL_DK), (lane >= GL_DK)]

    def subtile(s, carry):
        r0 = pl.multiple_of(s * SUB, SUB)
        rows = pl.ds(r0, SUB)
        hq = p_scr[rows, 0:512]
        hf = p_scr[rows, 512:1024]
        hv = p_scr[rows, 1024:1536]
        hgate = p_scr[rows, 1536:2048]
        gq = p_scr[rows, 2048:2304]
        gk = p_scr[rows, 2304:2560]
        gv = p_scr[rows, 2560:3072]
        ggate = p_scr[rows, 3072:3584]
        ga = p_scr[rows, 3584:3712]

        f = lb + (1.0 - lb) * jax.nn.sigmoid(hf)
        kh = 1.0 - f
        lgh = jnp.log(f)
        z = jnp.dot(ga.astype(BF16), wa2_ref[...], preferred_element_type=F32) + ba_ref[...]
        lga = (jnp.minimum(z, 0.0) - jnp.log1p(jnp.exp(-jnp.abs(z)))) * (1.0 / GATE_TAU)
        lg = jnp.concatenate([lgh, lga], axis=1)

        l_hi = lg.astype(BF16)
        rem = lg - l_hi.astype(F32)
        l_mid = rem.astype(BF16)
        l_lo = (rem - l_mid.astype(F32)).astype(BF16)
        expo = jnp.dot(wstk_ref[...], jnp.concatenate([l_hi, l_mid, l_lo], axis=0),
                       preferred_element_type=F32)
        dec = jnp.exp(expo)

        def group_terms(q, k, c0):
            e_pre = dec[0:SUB, c0:c0 + LANES]
            e_suf = dec[SUB:2 * SUB, c0:c0 + LANES]
            us = []
            for l in range(1, N_LEVELS + 1):
                e_l = dec[(1 + l) * SUB:(2 + l) * SUB, c0:c0 + LANES]
                us.append((jnp.where(qrow[l - 1], q, k) * e_l).astype(BF16))
            return us, (q * e_pre).astype(BF16), k * e_suf, e_pre[SUB - 1:SUB, :]

        def head(hidx, us, qd, kd, qk, v, g_last, mask):
            sc = jnp.where(lev == 0, jnp.sum(qk, axis=-1, keepdims=True), 0.0)
            for l in range(1, N_LEVELS + 1):
                u = us[l - 1]
                lhs = u if mask is None else jnp.where(mask, u, jnp.zeros_like(u))
                g = lax.dot_general(lhs, u, NT_DIMS, preferred_element_type=F32)
                sc = sc + jnp.where(lev == l, g, 0.0)
            vb = v.astype(BF16)
            st = st_scr[hidx]
            o = jnp.dot(sc.astype(BF16), vb, preferred_element_type=F32)
            o = o + lax.dot_general(qd, st.astype(BF16), NT_DIMS, preferred_element_type=F32)
            ut = lax.dot_general(vb, kd.astype(BF16), TN_DIMS, preferred_element_type=F32)
            st_scr[hidx] = st * g_last + ut
            return o

        def finish(o, gate, c0):
            o = o * lax.rsqrt(jnp.mean(o * o, axis=-1, keepdims=True) + LN_EPS)
            o = o * gain_ref[:, c0:c0 + LANES] * (gate * jax.nn.sigmoid(gate))
            o_scr[rows, c0:c0 + LANES] = o

        for g in range(HG_HEADS):
            c0 = g * LANES
            q = hq[:, c0:c0 + LANES] * (HG_HEAD_DIM ** -0.5)
            k = kh[:, c0:c0 + LANES]
            us, qd, kd, g_last = group_terms(q, k, c0)
            o = head(g, us, qd, kd, q * k, hv[:, c0:c0 + LANES], g_last, None)
            finish(o, hgate[:, c0:c0 + LANES], c0)

        for p in range(GL_HEADS // 2):
            c0 = p * LANES
            q = gq[:, c0:c0 + LANES] * (GL_DK ** -0.5)
            k = gk[:, c0:c0 + LANES]
            us, qd, kd, g_last = group_terms(q, k, HG_WIDTH + c0)
            for hh in range(2):
                h = 2 * p + hh
                m = half_mask[hh]
                o = head(HG_HEADS + h, us, qd, jnp.where(m, kd, 0.0), jnp.where(m, q * k, 0.0),
                         gv[:, h * LANES:(h + 1) * LANES], g_last, m)
                finish(o, ggate[:, h * LANES:(h + 1) * LANES], HG_WIDTH + h * LANES)
        return carry

    lax.fori_loop(0, TT // SUB, subtile, 0)

    hmix = jnp.dot(o_scr[...].astype(BF16), wout_ref[...], preferred_element_type=F32)
    x1 = _layer_norm(DEEPNORM_ALPHA * xt + hmix, ln1g_ref[...], ln1b_ref[...])
    for s in range(SLAB):
        x1s_ref[pl.ds(s, TT, stride=SLAB), :] = x1[:, s * LANES:(s + 1) * LANES]

    x1h = x1.astype(BF16)
    x1l = (x1 - x1h.astype(F32)).astype(BF16)
    logits = (jnp.dot(x1h, wrh_ref[...], preferred_element_type=F32)
              + jnp.dot(x1l, wrh_ref[...], preferred_element_type=F32)
              + jnp.dot(x1h, wrl_ref[...], preferred_element_type=F32))

    lane_i = lax.broadcasted_iota(jnp.int32, (TT, LANES), 1)
    lane_f = lane_i.astype(F32)
    neg = jnp.float32(-jnp.inf)
    big = jnp.float32(1e9)

    gl = jnp.where(lane_i < N_GROUPS, logits, neg)
    gmax = jnp.max(gl, axis=-1, keepdims=True)
    gidx = jnp.min(jnp.where(gl == gmax, lane_f, big), axis=-1, keepdims=True)
    p_group = 1.0 / jnp.sum(jnp.exp(gl - gmax), axis=-1, keepdims=True)

    e_lo = ROUTE_BASE + EXPERTS_PER_GROUP * gidx
    el = jnp.where((lane_f >= e_lo) & (lane_f < e_lo + EXPERTS_PER_GROUP), logits, neg)
    m1 = jnp.max(el, axis=-1, keepdims=True)
    i1 = jnp.min(jnp.where(el == m1, lane_f, big), axis=-1, keepdims=True)
    el2 = jnp.where(lane_f == i1, neg, el)
    m2 = jnp.max(el2, axis=-1, keepdims=True)
    i2 = jnp.min(jnp.where(el2 == m2, lane_f, big), axis=-1, keepdims=True)
    r2 = jnp.exp(m2 - m1)
    gate1 = p_group / (1.0 + r2)
    gate2 = p_group * r2 / (1.0 + r2)

    hot1 = lane_f == i1
    hot2 = lane_f == i2
    cnt = (hot1 | hot2).astype(F32)
    tr = lax.broadcasted_iota(jnp.int32, (TT, TT), 0)
    tc = lax.broadcasted_iota(jnp.int32, (TT, TT), 1)
    strict = (tc < tr).astype(BF16)
    base = carry_scr[...] + jnp.dot(strict, cnt.astype(BF16), preferred_element_type=F32)
    rank1 = jnp.sum(jnp.where(hot1, base, 0.0), axis=-1, keepdims=True)
    rank2 = jnp.sum(jnp.where(hot2, base, 0.0), axis=-1, keepdims=True)
    total = carry_scr[...] + jnp.sum(cnt, axis=0, keepdims=True)
    carry_scr[...] = total
    counts_ref[...] = total

    route = jnp.where(lane_i == 0, i1 - ROUTE_BASE, 0.0)
    route = jnp.where(lane_i == 1, i2 - ROUTE_BASE, route)
    route = jnp.where(lane_i == 2, gate1, route)
    route = jnp.where(lane_i == 3, gate2, route)
    route = jnp.where(lane_i == 4, rank1, route)
    route = jnp.where(lane_i == 5, rank2, route)
    route_ref[...] = route
    meta_ref[...] = route.T[0:META_ROWS, :].astype(jnp.int32)


def _slots_kernel(starts_ref, ends_ref, cnt_ref, meta_ref, slot_tok_ref):
    i = pl.program_id(0)

    @pl.when(i == 0)
    def _():
        def zero_range(lo, hi):
            def body(j, c):
                slot_tok_ref[j] = 0
                return c
            lax.fori_loop(lo, hi, body, 0)

        def per_expert(e, c):
            zero_range(starts_ref[e] + cnt_ref[e], ends_ref[e])
            return c
        lax.fori_loop(0, N_EXPERTS, per_expert, 0)
        zero_range(ends_ref[N_EXPERTS - 1], slot_tok_ref.shape[0])

    def body(n, c):
        t = i * TT + n
        for k in range(TOP_K):
            slot_tok_ref[starts_ref[meta_ref[k, n]] + meta_ref[4 + k, n]] = t
        return c
    lax.fori_loop(0, TT, body, 0, unroll=8)


def _expert_kernel(be_ref, nv_ref, tok_ref, tok_next_ref, x1s_ref, wg_ref, wu_ref, wd_ref, yb_ref,
                   xbuf, wg_s, wu_s, wd_s, sems):
    i = pl.program_id(0)
    nv = nv_ref[0]
    blk_rows = BLK * SLAB

    def start_gather(idx_ref, slot):
        def body(r, c):
            t = idx_ref[0, 0, r]
            pltpu.make_async_copy(x1s_ref.at[pl.ds(t * SLAB, SLAB), :],
                                  xbuf.at[pl.ds((slot * BLK + r) * SLAB, SLAB), :], sems.at[slot]).start()
            return c
        lax.fori_loop(0, BLK, body, 0, unroll=8)

    def wait_gather(slot):
        pltpu.make_async_copy(x1s_ref.at[pl.ds(0, blk_rows), :],
                              xbuf.at[pl.ds(slot * blk_rows, blk_rows), :], sems.at[slot]).wait()

    @pl.when(i < nv)
    def _():
        slot = i % 2

        @pl.when(i == 0)
        def _():
            start_gather(tok_ref, 0)

        @pl.when(i + 1 < nv)
        def _():
            start_gather(tok_next_ref, 1 - slot)

        prev = be_ref[jnp.maximum(i - 1, 0)]

        @pl.when((i == 0) | (be_ref[i] != prev))
        def _():
            wg_s[...] = wg_ref[...].astype(BF16)
            wu_s[...] = wu_ref[...].astype(BF16)
            wd_s[...] = wd_ref[...].astype(BF16)

        wait_gather(slot)
        base = pl.multiple_of(slot * blk_rows, blk_rows)
        xb = jnp.concatenate(
            [xbuf[pl.ds(base + s, BLK, stride=SLAB), :].astype(BF16) for s in range(SLAB)], axis=1)
        a = jnp.dot(xb, wg_s[...], preferred_element_type=F32)
        u = jnp.dot(xb, wu_s[...], preferred_element_type=F32)
        h = (a * jax.nn.sigmoid(a)) * u
        y = jnp.dot(h.astype(BF16), wd_s[...], preferred_element_type=F32)
        for s in range(SLAB):
            yb_ref[pl.ds(s, BLK, stride=SLAB), :] = y[:, s * LANES:(s + 1) * LANES]

    @pl.when(i >= nv)
    def _():
        yb_ref[...] = jnp.zeros_like(yb_ref)


def _combine_kernel(starts_ref, meta_ref, meta_next_ref, x1s_ref, route_ref, yb_ref, g_ref, b_ref, out_ref,
                    ybuf, sems):
    i = pl.program_id(0)
    n_steps = pl.num_programs(0)
    half_rows = TD * SLAB
    par_rows = TOP_K * half_rows

    def start_gather(m_ref, par):
        def body(r, c):
            for k in range(TOP_K):
                d = starts_ref[m_ref[k, r]] + m_ref[4 + k, r]
                pltpu.make_async_copy(yb_ref.at[pl.ds(d * SLAB, SLAB), :],
                                      ybuf.at[pl.ds(par * par_rows + (k * TD + r) * SLAB, SLAB), :],
                                      sems.at[par]).start()
            return c
        lax.fori_loop(0, TD, body, 0, unroll=8)

    par = i % 2

    @pl.when(i == 0)
    def _():
        start_gather(meta_ref, 0)

    @pl.when(i + 1 < n_steps)
    def _():
        start_gather(meta_next_ref, 1 - par)

    pltpu.make_async_copy(yb_ref.at[pl.ds(0, par_rows), :],
                          ybuf.at[pl.ds(par * par_rows, par_rows), :], sems.at[par]).wait()

    base = pl.multiple_of(par * par_rows, par_rows)
    g1 = route_ref[:, 2:3]
    g2 = route_ref[:, 3:4]
    zs = []
    for s in range(SLAB):
        y1 = ybuf[pl.ds(base + s, TD, stride=SLAB), :]
        y2 = ybuf[pl.ds(base + half_rows + s, TD, stride=SLAB), :]
        zs.append(DEEPNORM_ALPHA * x1s_ref[pl.ds(s, TD, stride=SLAB), :] + (g1 * y1 + g2 * y2))
    z = jnp.concatenate(zs, axis=1)
    out_ref[...] = _layer_norm(z, g_ref[...], b_ref[...])


def _const_spec(shape):
    nd = len(shape)
    return pl.BlockSpec(shape, lambda *_: (0,) * nd, pipeline_mode=pl.Buffered(1))


def kernel(x, w_in, w_a2, b_a, lb_logits, norm_h, norm_g, w_out, ln1_g, ln1_b, w_group_router, w_expert_router,
           w_gate, w_up, w_down, ln2_g, ln2_b):
    B, T, D = x.shape
    n_tok = B * T
    assert D == D_MODEL == SLAB * LANES and T % TT == 0 and TD == TT
    n_blocks = n_tok * TOP_K // BLK + N_EXPERTS
    n_slots = n_blocks * BLK

    parts = jnp.split(w_in[0], [int(c) for c in np.cumsum(IN_SPLITS)[:-1]], axis=-1)
    hq, hf, hi, hg, gq, gk, gv, ga, gg = parts
    win = jnp.concatenate([hq, hf, hi, hg, gq, gk, gv, gg, ga], axis=-1)
    win = jnp.pad(win, ((0, 0), (0, IN_COLS_PAD - win.shape[1]))).astype(BF16)
    wa2 = jnp.pad(w_a2[0], ((0, LANES - GATE_RANK), (0, 0))).astype(BF16)
    ba = b_a[0].reshape(1, GL_QK).astype(F32)
    gain = jnp.concatenate([norm_h[0], norm_g[0]]).reshape(1, D).astype(F32)
    wout = w_out[0].astype(BF16)
    wr = jnp.concatenate([w_group_router[0], w_expert_router[0]], axis=-1).astype(F32)
    wr = jnp.pad(wr, ((0, 0), (0, LANES - wr.shape[1])))
    wrh = wr.astype(BF16)
    wrl = (wr - wrh.astype(F32)).astype(BF16)
    wstk = jnp.asarray(_decay_matrices(), dtype=BF16)
    lev = jnp.asarray(_level_map())
    n_w = wstk.shape[0]

    n_steps = n_tok // TT
    x1s, route, meta, counts = pl.pallas_call(
        _mixer_kernel,
        grid=(B, T // TT),
        in_specs=[
            pl.BlockSpec((None, TT, D), lambda b, j: (b, j, 0)),
            _const_spec((D, IN_COLS_PAD)),
            _const_spec((LANES, GL_QK)),
            _const_spec((1, GL_QK)),
            _const_spec((DEPTH + 1, HG_WIDTH)),
            _const_spec((1, D)),
            _const_spec((D, D)),
            _const_spec((1, D)),
            _const_spec((1, D)),
            _const_spec((D, LANES)),
            _const_spec((D, LANES)),
            _const_spec((n_w, 3 * SUB)),
            _const_spec((SUB, SUB)),
        ],
        out_specs=[
            pl.BlockSpec((TT * SLAB, LANES), lambda b, j: (b * (T // TT) + j, 0)),
            pl.BlockSpec((TT, LANES), lambda b, j: (b * (T // TT) + j, 0)),
            pl.BlockSpec((META_ROWS, TT), lambda b, j: (b * (T // TT) + j, 0)),
            pl.BlockSpec((1, LANES), lambda b, j: (0, 0)),
        ],
        out_shape=[
            jax.ShapeDtypeStruct((n_tok * SLAB, LANES), F32),
            jax.ShapeDtypeStruct((n_tok, LANES), F32),
            jax.ShapeDtypeStruct((n_steps * META_ROWS, TT), jnp.int32),
            jax.ShapeDtypeStruct((1, LANES), F32),
        ],
        scratch_shapes=[
            pltpu.VMEM((TT, IN_COLS_PAD), F32),
            pltpu.VMEM((TT, D), F32),
            pltpu.VMEM((HG_HEADS + GL_HEADS, LANES, LANES), F32),
            pltpu.VMEM((1, LANES), F32),
        ],
        compiler_params=pltpu.CompilerParams(
            dimension_semantics=("arbitrary", "arbitrary"), vmem_limit_bytes=VMEM_LIMIT),
        name="mixer",
    )(x, win, wa2, ba, lb_logits.astype(F32), gain, wout, ln1_g[0].reshape(1, D), ln1_b[0].reshape(1, D),
      wrh, wrl, wstk, lev)

    cnt = counts[0, ROUTE_BASE:ROUTE_BASE + N_EXPERTS].astype(jnp.int32)
    padded = (cnt + BLK - 1) // BLK * BLK
    ends = jnp.cumsum(padded)
    starts = ends - padded
    n_valid = (ends[-1] // BLK).astype(jnp.int32)
    blk_ids = jnp.minimum(jnp.arange(n_blocks, dtype=jnp.int32), n_valid - 1)
    block_expert = jnp.sum((ends[None, :] <= (blk_ids * BLK)[:, None]).astype(jnp.int32), axis=1)
    block_expert = jnp.minimum(block_expert, N_EXPERTS - 1)

    slot_tok = pl.pallas_call(
        _slots_kernel,
        grid_spec=pltpu.PrefetchScalarGridSpec(
            num_scalar_prefetch=3,
            grid=(n_steps,),
            in_specs=[pl.BlockSpec((META_ROWS, TT), lambda i, *_: (i, 0), memory_space=pltpu.SMEM)],
            out_specs=pl.BlockSpec(memory_space=pltpu.SMEM),
        ),
        out_shape=jax.ShapeDtypeStruct((n_slots,), jnp.int32),
        compiler_params=pltpu.CompilerParams(dimension_semantics=("arbitrary",)),
        name="slots",
    )(starts, ends, cnt, meta).reshape(n_blocks, 1, BLK)

    yb = pl.pallas_call(
        _expert_kernel,
        grid_spec=pltpu.PrefetchScalarGridSpec(
            num_scalar_prefetch=2,
            grid=(n_blocks,),
            in_specs=[
                pl.BlockSpec((1, 1, BLK), lambda i, be, nv: (jnp.minimum(i, nv[0] - 1), 0, 0),
                             memory_space=pltpu.SMEM),
                pl.BlockSpec((1, 1, BLK), lambda i, be, nv: (jnp.minimum(i + 1, nv[0] - 1), 0, 0),
                             memory_space=pltpu.SMEM),
                pl.BlockSpec(memory_space=pl.ANY),
                pl.BlockSpec((None, D, D_EXPERT), lambda i, be, nv: (be[i], 0, 0)),
                pl.BlockSpec((None, D, D_EXPERT), lambda i, be, nv: (be[i], 0, 0)),
                pl.BlockSpec((None, D_EXPERT, D), lambda i, be, nv: (be[i], 0, 0)),
            ],
            out_specs=pl.BlockSpec((BLK * SLAB, LANES), lambda i, be, nv: (i, 0)),
            scratch_shapes=[
                pltpu.VMEM((2 * BLK * SLAB, LANES), F32),
                pltpu.VMEM((D, D_EXPERT), BF16),
                pltpu.VMEM((D, D_EXPERT), BF16),
                pltpu.VMEM((D_EXPERT, D), BF16),
                pltpu.SemaphoreType.DMA((2,)),
            ],
        ),
        out_shape=jax.ShapeDtypeStruct((n_slots * SLAB, LANES), F32),
        compiler_params=pltpu.CompilerParams(
            dimension_semantics=("arbitrary",), vmem_limit_bytes=VMEM_LIMIT),
        name="experts",
    )(block_expert, n_valid.reshape(1), slot_tok, slot_tok, x1s, w_gate[0], w_up[0], w_down[0])

    last = n_tok // TD - 1
    out = pl.pallas_call(
        _combine_kernel,
        grid_spec=pltpu.PrefetchScalarGridSpec(
            num_scalar_prefetch=1,
            grid=(n_tok // TD,),
            in_specs=[
                pl.BlockSpec((META_ROWS, TD), lambda i, st: (i, 0), memory_space=pltpu.SMEM),
                pl.BlockSpec((META_ROWS, TD), lambda i, st: (jnp.minimum(i + 1, last), 0), memory_space=pltpu.SMEM),
                pl.BlockSpec((TD * SLAB, LANES), lambda i, st: (i, 0)),
                pl.BlockSpec((TD, LANES), lambda i, st: (i, 0)),
                pl.BlockSpec(memory_space=pl.ANY),
                pl.BlockSpec((1, D), lambda i, st: (0, 0)),
                pl.BlockSpec((1, D), lambda i, st: (0, 0)),
            ],
            out_specs=pl.BlockSpec((TD, D), lambda i, st: (i, 0)),
            scratch_shapes=[pltpu.VMEM((2 * TOP_K * TD * SLAB, LANES), F32), pltpu.SemaphoreType.DMA((2,))],
        ),
        out_shape=jax.ShapeDtypeStruct((n_tok, D), F32),
        compiler_params=pltpu.CompilerParams(dimension_semantics=("arbitrary",)),
        name="combine",
    )(starts, meta, meta, x1s, route, yb, ln2_g[0].reshape(1, D), ln2_b[0].reshape(1, D))

    return out.reshape(B, T, D).astype(x.dtype)
```

```python
import functools

import numpy as np
import jax
import jax.numpy as jnp
from jax import lax
from jax.experimental import pallas as pl
from jax.experimental.pallas import tpu as pltpu

D_MODEL = 1024
HG_WIDTH = 512
HG_HEAD_DIM = 128
HG_HEADS = 4
GL_WIDTH = 512
GL_HEADS = 4
GL_DV = 128
GL_DK = 64
GL_QK = 256
GATE_RANK = 16
GATE_TAU = 16.0
N_GROUPS = 8
EXPERTS_PER_GROUP = 8
N_EXPERTS = 64
TOP_K = 2
D_EXPERT = 512
DEPTH = 1
DEEPNORM_ALPHA = (2.0 * DEPTH) ** 0.25
LN_EPS = 1e-5
IN_SPLITS = (HG_WIDTH, HG_WIDTH, HG_WIDTH, HG_WIDTH, GL_QK, GL_QK, GL_WIDTH, GATE_RANK, GL_WIDTH)

LANES = 128
SUB = 128
N_LEVELS = 7
MXU_LEVELS = 3
TT = 256
TD = 256
BLK = 256
IN_COLS_PAD = 3712
ROUTE_BASE = 8
SLAB = 8
META_ROWS = 8
VMEM_LIMIT = 56 * 1024 * 1024

F32 = jnp.float32
BF16 = jnp.bfloat16
NT_DIMS = (((1,), (1,)), ((), ()))
TN_DIMS = (((0,), (0,)), ((), ()))


def _decay_matrices():
    t = np.arange(SUB)[:, None]
    j = np.arange(SUB)[None, :]
    mats = [(j <= t)]
    for lvl in range(1, MXU_LEVELS + 1):
        blk = 1 << lvl
        half = blk >> 1
        r = (t // blk) * blk + half - 1
        right = (t % blk) >= half
        mats.append(np.where(right, (j > r) & (j <= t), (j > t) & (j <= r)))
    w = np.concatenate(mats, axis=0).astype(np.float32)
    return np.concatenate([w, w, w], axis=1)


def _level_map():
    t = np.arange(SUB)[:, None]
    s = np.arange(SUB)[None, :]
    x = np.bitwise_xor(t, s)
    lvl = np.floor(np.log2(np.maximum(x, 1))).astype(np.int32) + 1
    lvl = np.where(x == 0, 0, lvl)
    return np.where(s > t, -1, lvl).astype(np.int32)


def _layer_norm(y, g, b):
    mu = jnp.mean(y, axis=-1, keepdims=True)
    yc = y - mu
    var = jnp.mean(yc * yc, axis=-1, keepdims=True)
    return yc * lax.rsqrt(var + LN_EPS) * g + b


def _mixer_kernel(x_ref, win_ref, wa2_ref, ba_ref, lbl_ref, gain_ref, wout_ref, ln1g_ref, ln1b_ref,
                  wrh_ref, wrl_ref, wstk_ref, lev_ref,
                  x1s_ref, route_ref, meta_ref, counts_ref,
                  p_scr, o_scr, b_scr, st_scr, carry_scr):
    bi = pl.program_id(0)
    ji = pl.program_id(1)

    @pl.when(ji == 0)
    def _():
        st_scr[...] = jnp.zeros_like(st_scr)

    @pl.when((bi == 0) & (ji == 0))
    def _():
        carry_scr[...] = jnp.zeros_like(carry_scr)

    xt = x_ref[...]
    p_scr[...] = jnp.dot(xt.astype(BF16), win_ref[...], preferred_element_type=F32)

    lbl = lbl_ref[...]
    lbe = jnp.exp(lbl - jnp.max(lbl, axis=0, keepdims=True))
    lb = lbe[0:1, :] / jnp.sum(lbe, axis=0, keepdims=True)

    lev = lev_ref[...]
    tcol = lax.broadcasted_iota(jnp.int32, (SUB, 1), 0)
    qrow = [(tcol & ((1 << l) - 1)) >= (1 << (l - 1)) for l in range(1, N_LEVELS + 1)]
    lane = lax.broadcasted_iota(jnp.int32, (1, LANES), 1)
    half_mask = [(lane < GL_DK), (lane >= GL_DK)]

    def subtile(s, carry):
        r0 = pl.multiple_of(s * SUB, SUB)
        rows = pl.ds(r0, SUB)
        hq = p_scr[rows, 0:512]
        hf = p_scr[rows, 512:1024]
        hv = p_scr[rows, 1024:1536]
        hgate = p_scr[rows, 1536:2048]
        gq = p_scr[rows, 2048:2304]
        gk = p_scr[rows, 2304:2560]
        gv = p_scr[rows, 2560:3072]
        ggate = p_scr[rows, 3072:3584]
        ga = p_scr[rows, 3584:3712]

        f = lb + (1.0 - lb) * jax.nn.sigmoid(hf)
        kh = 1.0 - f
        lgh = jnp.log(f)
        z = jnp.dot(ga.astype(BF16), wa2_ref[...], preferred_element_type=F32) + ba_ref[...]
        lga = (jnp.minimum(z, 0.0) - jnp.log1p(jnp.exp(-jnp.abs(z)))) * (1.0 / GATE_TAU)
        lg = jnp.concatenate([lgh, lga], axis=1)

        l_hi = lg.astype(BF16)
        rem = lg - l_hi.astype(F32)
        l_mid = rem.astype(BF16)
        l_lo = (rem - l_mid.astype(F32)).astype(BF16)
        sums = jnp.dot(wstk_ref[...], jnp.concatenate([l_hi, l_mid, l_lo], axis=0),
                       preferred_element_type=F32)
        bcum = sums[0:SUB]
        b_scr[...] = bcum
        expos = [bcum, b_scr[SUB - 1:SUB, :] - bcum]
        for l in range(1, N_LEVELS + 1):
            if l <= MXU_LEVELS:
                expos.append(sums[l * SUB:(l + 1) * SUB])
            else:
                blk = 1 << l
                bound = jnp.concatenate(
                    [jnp.broadcast_to(b_scr[j * blk + blk // 2 - 1:j * blk + blk // 2, :], (blk, lg.shape[1]))
                     for j in range(SUB // blk)], axis=0)
                expos.append(jnp.where(qrow[l - 1], bcum - bound, bound - bcum))
        decs = [jnp.exp(e) for e in expos]

        def group_terms(q, k, c0):
            e_pre = decs[0][:, c0:c0 + LANES]
            e_suf = decs[1][:, c0:c0 + LANES]
            us = []
            for l in range(1, N_LEVELS + 1):
                e_l = decs[1 + l][:, c0:c0 + LANES]
                us.append((jnp.where(qrow[l - 1], q, k) * e_l).astype(BF16))
            return us, (q * e_pre).astype(BF16), k * e_suf, e_pre[SUB - 1:SUB, :]

        def head(hidx, us, qd, kd, qk, v, g_last, mask):
            sc = jnp.where(lev == 0, jnp.sum(qk, axis=-1, keepdims=True), 0.0)
            for l in range(1, N_LEVELS + 1):
                u = us[l - 1]
                lhs = u if mask is None else jnp.where(mask, u, jnp.zeros_like(u))
                g = lax.dot_general(lhs, u, NT_DIMS, preferred_element_type=F32)
                sc = jnp.where(lev == l, g, sc)
            vb = v.astype(BF16)
            st = st_scr[hidx]
            o = jnp.dot(sc.astype(BF16), vb, preferred_element_type=F32)
            o = o + lax.dot_general(qd, st.astype(BF16), NT_DIMS, preferred_element_type=F32)
            ut = lax.dot_general(vb, kd.astype(BF16), TN_DIMS, preferred_element_type=F32)
            st_scr[hidx] = st * g_last + ut
            return o

        def finish(o, gate, c0):
            o = o * lax.rsqrt(jnp.mean(o * o, axis=-1, keepdims=True) + LN_EPS)
            o = o * gain_ref[:, c0:c0 + LANES] * (gate * jax.nn.sigmoid(gate))
            o_scr[rows, c0:c0 + LANES] = o

        for g in range(HG_HEADS):
            c0 = g * LANES
            q = hq[:, c0:c0 + LANES] * (HG_HEAD_DIM ** -0.5)
            k = kh[:, c0:c0 + LANES]
            us, qd, kd, g_last = group_terms(q, k, c0)
            o = head(g, us, qd, kd, q * k, hv[:, c0:c0 + LANES], g_last, None)
            finish(o, hgate[:, c0:c0 + LANES], c0)

        for p in range(GL_HEADS // 2):
            c0 = p * LANES
            q = gq[:, c0:c0 + LANES] * (GL_DK ** -0.5)
            k = gk[:, c0:c0 + LANES]
            us, qd, kd, g_last = group_terms(q, k, HG_WIDTH + c0)
            for hh in range(2):
                h = 2 * p + hh
                m = half_mask[hh]
                o = head(HG_HEADS + h, us, qd, jnp.where(m, kd, 0.0), jnp.where(m, q * k, 0.0),
                         gv[:, h * LANES:(h + 1) * LANES], g_last, m)
                finish(o, ggate[:, h * LANES:(h + 1) * LANES], HG_WIDTH + h * LANES)
        return carry

    lax.fori_loop(0, TT // SUB, subtile, 0)

    hmix = jnp.dot(o_scr[...].astype(BF16), wout_ref[...], preferred_element_type=F32)
    x1 = _layer_norm(DEEPNORM_ALPHA * xt + hmix, ln1g_ref[...], ln1b_ref[...])
    for s in range(SLAB):
        x1s_ref[pl.ds(s, TT, stride=SLAB), :] = x1[:, s * LANES:(s + 1) * LANES]

    x1h = x1.astype(BF16)
    x1l = (x1 - x1h.astype(F32)).astype(BF16)
    logits = (jnp.dot(x1h, wrh_ref[...], preferred_element_type=F32)
              + jnp.dot(x1l, wrh_ref[...], preferred_element_type=F32)
              + jnp.dot(x1h, wrl_ref[...], preferred_element_type=F32))

    lane_i = lax.broadcasted_iota(jnp.int32, (TT, LANES), 1)
    lane_f = lane_i.astype(F32)
    neg = jnp.float32(-jnp.inf)
    big = jnp.float32(1e9)

    gl = jnp.where(lane_i < N_GROUPS, logits, neg)
    gmax = jnp.max(gl, axis=-1, keepdims=True)
    gidx = jnp.min(jnp.where(gl == gmax, lane_f, big), axis=-1, keepdims=True)
    p_group = 1.0 / jnp.sum(jnp.exp(gl - gmax), axis=-1, keepdims=True)

    e_lo = ROUTE_BASE + EXPERTS_PER_GROUP * gidx
    el = jnp.where((lane_f >= e_lo) & (lane_f < e_lo + EXPERTS_PER_GROUP), logits, neg)
    m1 = jnp.max(el, axis=-1, keepdims=True)
    i1 = jnp.min(jnp.where(el == m1, lane_f, big), axis=-1, keepdims=True)
    el2 = jnp.where(lane_f == i1, neg, el)
    m2 = jnp.max(el2, axis=-1, keepdims=True)
    i2 = jnp.min(jnp.where(el2 == m2, lane_f, big), axis=-1, keepdims=True)
    r2 = jnp.exp(m2 - m1)
    gate1 = p_group / (1.0 + r2)
    gate2 = p_group * r2 / (1.0 + r2)

    hot1 = lane_f == i1
    hot2 = lane_f == i2
    cnt = (hot1 | hot2).astype(F32)
    tr = lax.broadcasted_iota(jnp.int32, (TT, TT), 0)
    tc = lax.broadcasted_iota(jnp.int32, (TT, TT), 1)
    strict = (tc < tr).astype(BF16)
    base = carry_scr[...] + jnp.dot(strict, cnt.astype(BF16), preferred_element_type=F32)
    rank1 = jnp.sum(jnp.where(hot1, base, 0.0), axis=-1, keepdims=True)
    rank2 = jnp.sum(jnp.where(hot2, base, 0.0), axis=-1, keepdims=True)
    total = carry_scr[...] + jnp.sum(cnt, axis=0, keepdims=True)
    carry_scr[...] = total
    counts_ref[...] = total

    route = jnp.where(lane_i == 0, i1 - ROUTE_BASE, 0.0)
    route = jnp.where(lane_i == 1, i2 - ROUTE_BASE, route)
    route = jnp.where(lane_i == 2, gate1, route)
    route = jnp.where(lane_i == 3, gate2, route)
    route = jnp.where(lane_i == 4, rank1, route)
    route = jnp.where(lane_i == 5, rank2, route)
    route_ref[...] = route
    meta_ref[...] = route.T[0:META_ROWS, :].astype(jnp.int32)


def _slots_kernel(ends_ref, nv_ref, dest_ref, slot_tok_ref):
    i = pl.program_id(0)
    n_blocks = slot_tok_ref.shape[0] // BLK

    def zero_block(base):
        for j in range(BLK):
            slot_tok_ref[base + j] = 0

    @pl.when(i == 0)
    def _():
        def per_expert(e, c):
            zero_block(jnp.maximum(ends_ref[e] - BLK, 0))
            return c
        lax.fori_loop(0, N_EXPERTS, per_expert, 0)

        def per_tail(b, c):
            zero_block(b * BLK)
            return c
        lax.fori_loop(nv_ref[0], n_blocks, per_tail, 0)

    def body(n, c):
        t = i * TT + n
        for k in range(TOP_K):
            slot_tok_ref[dest_ref[0, k, n]] = t
        return c
    lax.fori_loop(0, TT, body, 0, unroll=8)


def _expert_kernel(be_ref, nv_ref, tok_ref, tok_next_ref, x1s_ref, wg_ref, wu_ref, wd_ref, yb_ref,
                   xbuf, wg_s, wu_s, wd_s, sems):
    i = pl.program_id(0)
    nv = nv_ref[0]
    blk_rows = BLK * SLAB

    def start_gather(idx_ref, slot):
        for r in range(BLK):
            t = idx_ref[0, 0, r]
            pltpu.make_async_copy(x1s_ref.at[pl.ds(pl.multiple_of(t * SLAB, SLAB), SLAB), :],
                                  xbuf.at[pl.ds(pl.multiple_of(slot * blk_rows + r * SLAB, SLAB), SLAB), :],
                                  sems.at[slot]).start(priority=r % 2)

    def wait_gather(slot):
        pltpu.make_async_copy(x1s_ref.at[pl.ds(0, blk_rows), :],
                              xbuf.at[pl.ds(slot * blk_rows, blk_rows), :], sems.at[slot]).wait()

    @pl.when(i < nv)
    def _():
        slot = i % 2

        @pl.when(i == 0)
        def _():
            start_gather(tok_ref, 0)

        prev = be_ref[jnp.maximum(i - 1, 0)]

        @pl.when((i == 0) | (be_ref[i] != prev))
        def _():
            wg_s[...] = wg_ref[...].astype(BF16)
            wu_s[...] = wu_ref[...].astype(BF16)
            wd_s[...] = wd_ref[...].astype(BF16)

        wait_gather(slot)
        base = pl.multiple_of(slot * blk_rows, blk_rows)
        xb = jnp.concatenate(
            [xbuf[pl.ds(base + s, BLK, stride=SLAB), :].astype(BF16) for s in range(SLAB)], axis=1)
        start_gather(tok_next_ref, 1 - slot)
        a = jnp.dot(xb, wg_s[...], preferred_element_type=F32)
        u = jnp.dot(xb, wu_s[...], preferred_element_type=F32)
        h = (a * jax.nn.sigmoid(a)) * u
        y = jnp.dot(h.astype(BF16), wd_s[...], preferred_element_type=F32)
        for s in range(SLAB):
            yb_ref[pl.ds(s, BLK, stride=SLAB), :] = y[:, s * LANES:(s + 1) * LANES]

        @pl.when(i + 1 >= nv)
        def _():
            wait_gather(1 - slot)

    @pl.when(i >= nv)
    def _():
        yb_ref[...] = jnp.zeros_like(yb_ref)


def _combine_kernel(dest_ref, dest_next_ref, x1s_ref, route_ref, yb_ref, g_ref, b_ref, out_ref, ybuf, sems):
    i = pl.program_id(0)
    n_steps = pl.num_programs(0)
    half_rows = TD * SLAB
    par_rows = TOP_K * half_rows

    def start_gather(d_ref, par):
        for r in range(TD):
            for k in range(TOP_K):
                src = pl.multiple_of(d_ref[0, k, r] * SLAB, SLAB)
                dst = pl.multiple_of(par * par_rows + (k * TD + r) * SLAB, SLAB)
                pltpu.make_async_copy(yb_ref.at[pl.ds(src, SLAB), :], ybuf.at[pl.ds(dst, SLAB), :],
                                      sems.at[par]).start(priority=k)

    def wait_gather(par):
        pltpu.make_async_copy(yb_ref.at[pl.ds(0, par_rows), :],
                              ybuf.at[pl.ds(par * par_rows, par_rows), :], sems.at[par]).wait()

    par = i % 2

    @pl.when(i == 0)
    def _():
        start_gather(dest_ref, 0)

    wait_gather(par)
    start_gather(dest_next_ref, 1 - par)

    base = pl.multiple_of(par * par_rows, par_rows)
    g1 = route_ref[:, 2:3]
    g2 = route_ref[:, 3:4]
    zs = []
    for s in range(SLAB):
        y1 = ybuf[pl.ds(base + s, TD, stride=SLAB), :]
        y2 = ybuf[pl.ds(base + half_rows + s, TD, stride=SLAB), :]
        zs.append(DEEPNORM_ALPHA * x1s_ref[pl.ds(s, TD, stride=SLAB), :] + (g1 * y1 + g2 * y2))
    z = jnp.concatenate(zs, axis=1)
    out_ref[...] = _layer_norm(z, g_ref[...], b_ref[...])

    @pl.when(i + 1 >= n_steps)
    def _():
        wait_gather(1 - par)


def _const_spec(shape):
    nd = len(shape)
    return pl.BlockSpec(shape, lambda *_: (0,) * nd, pipeline_mode=pl.Buffered(1))


def kernel(x, w_in, w_a2, b_a, lb_logits, norm_h, norm_g, w_out, ln1_g, ln1_b, w_group_router, w_expert_router,
           w_gate, w_up, w_down, ln2_g, ln2_b):
    B, T, D = x.shape
    n_tok = B * T
    assert D == D_MODEL == SLAB * LANES and T % TT == 0 and TD == TT
    n_blocks = n_tok * TOP_K // BLK + N_EXPERTS
    n_slots = n_blocks * BLK

    parts = jnp.split(w_in[0], [int(c) for c in np.cumsum(IN_SPLITS)[:-1]], axis=-1)
    hq, hf, hi, hg, gq, gk, gv, ga, gg = parts
    win = jnp.concatenate([hq, hf, hi, hg, gq, gk, gv, gg, ga], axis=-1)
    win = jnp.pad(win, ((0, 0), (0, IN_COLS_PAD - win.shape[1]))).astype(BF16)
    wa2 = jnp.pad(w_a2[0], ((0, LANES - GATE_RANK), (0, 0))).astype(BF16)
    ba = b_a[0].reshape(1, GL_QK).astype(F32)
    gain = jnp.concatenate([norm_h[0], norm_g[0]]).reshape(1, D).astype(F32)
    wout = w_out[0].astype(BF16)
    wr = jnp.concatenate([w_group_router[0], w_expert_router[0]], axis=-1).astype(F32)
    wr = jnp.pad(wr, ((0, 0), (0, LANES - wr.shape[1])))
    wrh = wr.astype(BF16)
    wrl = (wr - wrh.astype(F32)).astype(BF16)
    wstk = jnp.asarray(_decay_matrices(), dtype=BF16)
    lev = jnp.asarray(_level_map())
    n_w = wstk.shape[0]

    n_steps = n_tok // TT
    x1s, route, meta, counts = pl.pallas_call(
        _mixer_kernel,
        grid=(B, T // TT),
        in_specs=[
            pl.BlockSpec((None, TT, D), lambda b, j: (b, j, 0)),
            _const_spec((D, IN_COLS_PAD)),
            _const_spec((LANES, GL_QK)),
            _const_spec((1, GL_QK)),
            _const_spec((DEPTH + 1, HG_WIDTH)),
            _const_spec((1, D)),
            _const_spec((D, D)),
            _const_spec((1, D)),
            _const_spec((1, D)),
            _const_spec((D, LANES)),
            _const_spec((D, LANES)),
            _const_spec((n_w, 3 * SUB)),
            _const_spec((SUB, SUB)),
        ],
        out_specs=[
            pl.BlockSpec((TT * SLAB, LANES), lambda b, j: (b * (T // TT) + j, 0)),
            pl.BlockSpec((TT, LANES), lambda b, j: (b * (T // TT) + j, 0)),
            pl.BlockSpec((META_ROWS, TT), lambda b, j: (b * (T // TT) + j, 0)),
            pl.BlockSpec((1, LANES), lambda b, j: (0, 0)),
        ],
        out_shape=[
            jax.ShapeDtypeStruct((n_tok * SLAB, LANES), F32),
            jax.ShapeDtypeStruct((n_tok, LANES), F32),
            jax.ShapeDtypeStruct((n_steps * META_ROWS, TT), jnp.int32),
            jax.ShapeDtypeStruct((1, LANES), F32),
        ],
        scratch_shapes=[
            pltpu.VMEM((TT, IN_COLS_PAD), F32),
            pltpu.VMEM((TT, D), F32),
            pltpu.VMEM((SUB, HG_WIDTH + GL_QK), F32),
            pltpu.VMEM((HG_HEADS + GL_HEADS, LANES, LANES), F32),
            pltpu.VMEM((1, LANES), F32),
        ],
        compiler_params=pltpu.CompilerParams(
            dimension_semantics=("arbitrary", "arbitrary"), vmem_limit_bytes=VMEM_LIMIT),
        name="mixer",
    )(x, win, wa2, ba, lb_logits.astype(F32), gain, wout, ln1_g[0].reshape(1, D), ln1_b[0].reshape(1, D),
      wrh, wrl, wstk, lev)

    cnt = counts[0, ROUTE_BASE:ROUTE_BASE + N_EXPERTS].astype(jnp.int32)
    padded = (cnt + BLK - 1) // BLK * BLK
    ends = jnp.cumsum(padded)
    starts = ends - padded
    n_valid = (ends[-1] // BLK).astype(jnp.int32)
    blk_ids = jnp.minimum(jnp.arange(n_blocks, dtype=jnp.int32), n_valid - 1)
    block_expert = jnp.sum((ends[None, :] <= (blk_ids * BLK)[:, None]).astype(jnp.int32), axis=1)
    block_expert = jnp.minimum(block_expert, N_EXPERTS - 1)
    meta3 = meta.reshape(n_steps, META_ROWS, TT)
    e_ids = meta3[:, 0:TOP_K, :]
    hit = e_ids[None] == jnp.arange(N_EXPERTS, dtype=jnp.int32)[:, None, None, None]
    dest = meta3[:, 4:4 + TOP_K, :] + jnp.sum(jnp.where(hit, starts[:, None, None, None], 0), axis=0)

    slot_tok = pl.pallas_call(
        _slots_kernel,
        grid_spec=pltpu.PrefetchScalarGridSpec(
            num_scalar_prefetch=2,
            grid=(n_steps,),
            in_specs=[pl.BlockSpec((1, TOP_K, TT), lambda i, *_: (i, 0, 0), memory_space=pltpu.SMEM)],
            out_specs=pl.BlockSpec(memory_space=pltpu.SMEM),
        ),
        out_shape=jax.ShapeDtypeStruct((n_slots,), jnp.int32),
        compiler_params=pltpu.CompilerParams(dimension_semantics=("arbitrary",)),
        name="slots",
    )(ends, n_valid.reshape(1), dest).reshape(n_blocks, 1, BLK)

    yb = pl.pallas_call(
        _expert_kernel,
        grid_spec=pltpu.PrefetchScalarGridSpec(
            num_scalar_prefetch=2,
            grid=(n_blocks,),
            in_specs=[
                pl.BlockSpec((1, 1, BLK), lambda i, be, nv: (jnp.minimum(i, nv[0] - 1), 0, 0),
                             memory_space=pltpu.SMEM),
                pl.BlockSpec((1, 1, BLK), lambda i, be, nv: (jnp.minimum(i + 1, nv[0] - 1), 0, 0),
                             memory_space=pltpu.SMEM),
                pl.BlockSpec(memory_space=pl.ANY),
                pl.BlockSpec((None, D, D_EXPERT), lambda i, be, nv: (be[i], 0, 0)),
                pl.BlockSpec((None, D, D_EXPERT), lambda i, be, nv: (be[i], 0, 0)),
                pl.BlockSpec((None, D_EXPERT, D), lambda i, be, nv: (be[i], 0, 0)),
            ],
            out_specs=pl.BlockSpec((BLK * SLAB, LANES), lambda i, be, nv: (i, 0)),
            scratch_shapes=[
                pltpu.VMEM((2 * BLK * SLAB, LANES), F32),
                pltpu.VMEM((D, D_EXPERT), BF16),
                pltpu.VMEM((D, D_EXPERT), BF16),
                pltpu.VMEM((D_EXPERT, D), BF16),
                pltpu.SemaphoreType.DMA((2,)),
            ],
        ),
        out_shape=jax.ShapeDtypeStruct((n_slots * SLAB, LANES), F32),
        compiler_params=pltpu.CompilerParams(
            dimension_semantics=("arbitrary",), vmem_limit_bytes=VMEM_LIMIT),
        name="experts",
    )(block_expert, n_valid.reshape(1), slot_tok, slot_tok, x1s, w_gate[0], w_up[0], w_down[0])

    last = n_tok // TD - 1
    out = pl.pallas_call(
        _combine_kernel,
        grid=(n_tok // TD,),
        in_specs=[
            pl.BlockSpec((1, TOP_K, TD), lambda i: (i, 0, 0), memory_space=pltpu.SMEM),
            pl.BlockSpec((1, TOP_K, TD), lambda i: (jnp.minimum(i + 1, last), 0, 0), memory_space=pltpu.SMEM),
            pl.BlockSpec((TD * SLAB, LANES), lambda i: (i, 0)),
            pl.BlockSpec((TD, LANES), lambda i: (i, 0)),
            pl.BlockSpec(memory_space=pl.ANY),
            pl.BlockSpec((1, D), lambda i: (0, 0)),
            pl.BlockSpec((1, D), lambda i: (0, 0)),
        ],
        out_specs=pl.BlockSpec((TD, D), lambda i: (i, 0)),
        scratch_shapes=[pltpu.VMEM((2 * TOP_K * TD * SLAB, LANES), F32), pltpu.SemaphoreType.DMA((2,))],
        out_shape=jax.ShapeDtypeStruct((n_tok, D), F32),
        compiler_params=pltpu.CompilerParams(dimension_semantics=("arbitrary",)),
        name="combine",
    )(dest, dest, x1s, route, yb, ln2_g[0].reshape(1, D), ln2_b[0].reshape(1, D))

    return out.reshape(B, T, D).astype(x.dtype)
```

```python
import functools

import numpy as np
import jax
import jax.numpy as jnp
from jax import lax
from jax.experimental import pallas as pl
from jax.experimental.pallas import tpu as pltpu

D_MODEL = 1024
HG_WIDTH = 512
HG_HEAD_DIM = 128
HG_HEADS = 4
GL_WIDTH = 512
GL_HEADS = 4
GL_DV = 128
GL_DK = 64
GL_QK = 256
GATE_RANK = 16
GATE_TAU = 16.0
N_GROUPS = 8
EXPERTS_PER_GROUP = 8
N_EXPERTS = 64
TOP_K = 2
D_EXPERT = 512
DEPTH = 1
DEEPNORM_ALPHA = (2.0 * DEPTH) ** 0.25
LN_EPS = 1e-5
IN_SPLITS = (HG_WIDTH, HG_WIDTH, HG_WIDTH, HG_WIDTH, GL_QK, GL_QK, GL_WIDTH, GATE_RANK, GL_WIDTH)

LANES = 128
SUB = 128
N_LEVELS = 7
MXU_LEVELS = 3
TT = 256
TD = 256
BLK = 256
IN_COLS_PAD = 3712
ROUTE_BASE = 8
SLAB = 8
META_ROWS = 8
VMEM_LIMIT = 56 * 1024 * 1024

F32 = jnp.float32
BF16 = jnp.bfloat16
NT_DIMS = (((1,), (1,)), ((), ()))
TN_DIMS = (((0,), (0,)), ((), ()))


def _decay_matrices():
    t = np.arange(SUB)[:, None]
    j = np.arange(SUB)[None, :]
    mats = [(j <= t)]
    for lvl in range(1, MXU_LEVELS + 1):
        blk = 1 << lvl
        half = blk >> 1
        r = (t // blk) * blk + half - 1
        right = (t % blk) >= half
        mats.append(np.where(right, (j > r) & (j <= t), (j > t) & (j <= r)))
    w = np.concatenate(mats, axis=0).astype(np.float32)
    return np.concatenate([w, w, w], axis=1)


def _level_map():
    t = np.arange(SUB)[:, None]
    s = np.arange(SUB)[None, :]
    x = np.bitwise_xor(t, s)
    lvl = np.floor(np.log2(np.maximum(x, 1))).astype(np.int32) + 1
    lvl = np.where(x == 0, 0, lvl)
    return np.where(s > t, -1, lvl).astype(np.int32)


def _layer_norm(y, g, b):
    mu = jnp.mean(y, axis=-1, keepdims=True)
    yc = y - mu
    var = jnp.mean(yc * yc, axis=-1, keepdims=True)
    return yc * lax.rsqrt(var + LN_EPS) * g + b


def _mixer_kernel(x_ref, win_ref, wa2_ref, ba_ref, lbl_ref, gain_ref, wout_ref, ln1g_ref, ln1b_ref,
                  wrh_ref, wrl_ref, wstk_ref, lev_ref,
                  x1s_ref, route_ref, meta_ref, counts_ref,
                  p_scr, o_scr, b_scr, st_scr, carry_scr):
    bi = pl.program_id(0)
    ji = pl.program_id(1)

    @pl.when(ji == 0)
    def _():
        st_scr[...] = jnp.zeros_like(st_scr)

    @pl.when((bi == 0) & (ji == 0))
    def _():
        carry_scr[...] = jnp.zeros_like(carry_scr)

    xt = x_ref[...]
    p_scr[...] = jnp.dot(xt.astype(BF16), win_ref[...], preferred_element_type=F32)

    lbl = lbl_ref[...]
    lbe = jnp.exp(lbl - jnp.max(lbl, axis=0, keepdims=True))
    lb = lbe[0:1, :] / jnp.sum(lbe, axis=0, keepdims=True)

    lev = lev_ref[...]
    tcol = lax.broadcasted_iota(jnp.int32, (SUB, 1), 0)
    qrow = [(tcol & ((1 << l) - 1)) >= (1 << (l - 1)) for l in range(1, N_LEVELS + 1)]
    lane = lax.broadcasted_iota(jnp.int32, (1, LANES), 1)
    half_mask = [(lane < GL_DK), (lane >= GL_DK)]

    def subtile(s, carry):
        r0 = pl.multiple_of(s * SUB, SUB)
        rows = pl.ds(r0, SUB)
        hq = p_scr[rows, 0:512]
        hf = p_scr[rows, 512:1024]
        hv = p_scr[rows, 1024:1536]
        hgate = p_scr[rows, 1536:2048]
        gq = p_scr[rows, 2048:2304]
        gk = p_scr[rows, 2304:2560]
        gv = p_scr[rows, 2560:3072]
        ggate = p_scr[rows, 3072:3584]
        ga = p_scr[rows, 3584:3712]

        f = lb + (1.0 - lb) * jax.nn.sigmoid(hf)
        kh = 1.0 - f
        lgh = jnp.log(f)
        z = jnp.dot(ga.astype(BF16), wa2_ref[...], preferred_element_type=F32) + ba_ref[...]
        lga = (jnp.minimum(z, 0.0) - jnp.log1p(jnp.exp(-jnp.abs(z)))) * (1.0 / GATE_TAU)
        lg = jnp.concatenate([lgh, lga], axis=1)

        l_hi = lg.astype(BF16)
        rem = lg - l_hi.astype(F32)
        l_mid = rem.astype(BF16)
        l_lo = (rem - l_mid.astype(F32)).astype(BF16)
        sums = jnp.dot(wstk_ref[...], jnp.concatenate([l_hi, l_mid, l_lo], axis=0),
                       preferred_element_type=F32)
        bcum = sums[0:SUB]
        b_scr[...] = bcum
        expos = [bcum, b_scr[SUB - 1:SUB, :] - bcum]
        for l in range(1, N_LEVELS + 1):
            if l <= MXU_LEVELS:
                expos.append(sums[l * SUB:(l + 1) * SUB])
            else:
                blk = 1 << l
                bound = jnp.concatenate(
                    [jnp.broadcast_to(b_scr[j * blk + blk // 2 - 1:j * blk + blk // 2, :], (blk, lg.shape[1]))
                     for j in range(SUB // blk)], axis=0)
                expos.append(jnp.where(qrow[l - 1], bcum - bound, bound - bcum))
        decs = [jnp.exp(e) for e in expos]

        def group_terms(q, k, c0):
            e_pre = decs[0][:, c0:c0 + LANES]
            e_suf = decs[1][:, c0:c0 + LANES]
            us = []
            for l in range(1, N_LEVELS + 1):
                e_l = decs[1 + l][:, c0:c0 + LANES]
                us.append((jnp.where(qrow[l - 1], q, k) * e_l).astype(BF16))
            return us, (q * e_pre).astype(BF16), k * e_suf, e_pre[SUB - 1:SUB, :]

        def head(hidx, us, qd, kd, qk, v, g_last, mask):
            sc = jnp.where(lev == 0, jnp.sum(qk, axis=-1, keepdims=True), 0.0)
            for l in range(1, N_LEVELS + 1):
                u = us[l - 1]
                lhs = u if mask is None else jnp.where(mask, u, jnp.zeros_like(u))
                g = lax.dot_general(lhs, u, NT_DIMS, preferred_element_type=F32)
                sc = jnp.where(lev == l, g, sc)
            vb = v.astype(BF16)
            st = st_scr[hidx]
            o = jnp.dot(sc.astype(BF16), vb, preferred_element_type=F32)
            o = o + lax.dot_general(qd, st.astype(BF16), NT_DIMS, preferred_element_type=F32)
            ut = lax.dot_general(vb, kd.astype(BF16), TN_DIMS, preferred_element_type=F32)
            st_scr[hidx] = st * g_last + ut
            return o

        def finish(o, gate, c0):
            o = o * lax.rsqrt(jnp.mean(o * o, axis=-1, keepdims=True) + LN_EPS)
            o = o * gain_ref[:, c0:c0 + LANES] * (gate * jax.nn.sigmoid(gate))
            o_scr[rows, c0:c0 + LANES] = o

        for g in range(HG_HEADS):
            c0 = g * LANES
            q = hq[:, c0:c0 + LANES] * (HG_HEAD_DIM ** -0.5)
            k = kh[:, c0:c0 + LANES]
            us, qd, kd, g_last = group_terms(q, k, c0)
            o = head(g, us, qd, kd, q * k, hv[:, c0:c0 + LANES], g_last, None)
            finish(o, hgate[:, c0:c0 + LANES], c0)

        for p in range(GL_HEADS // 2):
            c0 = p * LANES
            q = gq[:, c0:c0 + LANES] * (GL_DK ** -0.5)
            k = gk[:, c0:c0 + LANES]
            us, qd, kd, g_last = group_terms(q, k, HG_WIDTH + c0)
            for hh in range(2):
                h = 2 * p + hh
                m = half_mask[hh]
                o = head(HG_HEADS + h, us, qd, jnp.where(m, kd, 0.0), jnp.where(m, q * k, 0.0),
                         gv[:, h * LANES:(h + 1) * LANES], g_last, m)
                finish(o, ggate[:, h * LANES:(h + 1) * LANES], HG_WIDTH + h * LANES)
        return carry

    lax.fori_loop(0, TT // SUB, subtile, 0)

    hmix = jnp.dot(o_scr[...].astype(BF16), wout_ref[...], preferred_element_type=F32)
    x1 = _layer_norm(DEEPNORM_ALPHA * xt + hmix, ln1g_ref[...], ln1b_ref[...])
    for s in range(SLAB):
        x1s_ref[pl.ds(s, TT, stride=SLAB), :] = x1[:, s * LANES:(s + 1) * LANES]

    x1h = x1.astype(BF16)
    x1l = (x1 - x1h.astype(F32)).astype(BF16)
    logits = (jnp.dot(x1h, wrh_ref[...], preferred_element_type=F32)
              + jnp.dot(x1l, wrh_ref[...], preferred_element_type=F32)
              + jnp.dot(x1h, wrl_ref[...], preferred_element_type=F32))

    lane_i = lax.broadcasted_iota(jnp.int32, (TT, LANES), 1)
    lane_f = lane_i.astype(F32)
    neg = jnp.float32(-jnp.inf)
    big = jnp.float32(1e9)

    gl = jnp.where(lane_i < N_GROUPS, logits, neg)
    gmax = jnp.max(gl, axis=-1, keepdims=True)
    gidx = jnp.min(jnp.where(gl == gmax, lane_f, big), axis=-1, keepdims=True)
    p_group = 1.0 / jnp.sum(jnp.exp(gl - gmax), axis=-1, keepdims=True)

    e_lo = ROUTE_BASE + EXPERTS_PER_GROUP * gidx
    el = jnp.where((lane_f >= e_lo) & (lane_f < e_lo + EXPERTS_PER_GROUP), logits, neg)
    m1 = jnp.max(el, axis=-1, keepdims=True)
    i1 = jnp.min(jnp.where(el == m1, lane_f, big), axis=-1, keepdims=True)
    el2 = jnp.where(lane_f == i1, neg, el)
    m2 = jnp.max(el2, axis=-1, keepdims=True)
    i2 = jnp.min(jnp.where(el2 == m2, lane_f, big), axis=-1, keepdims=True)
    r2 = jnp.exp(m2 - m1)
    gate1 = p_group / (1.0 + r2)
    gate2 = p_group * r2 / (1.0 + r2)

    hot1 = lane_f == i1
    hot2 = lane_f == i2
    cnt = (hot1 | hot2).astype(F32)
    tr = lax.broadcasted_iota(jnp.int32, (TT, TT), 0)
    tc = lax.broadcasted_iota(jnp.int32, (TT, TT), 1)
    strict = (tc < tr).astype(BF16)
    base = carry_scr[...] + jnp.dot(strict, cnt.astype(BF16), preferred_element_type=F32)
    rank1 = jnp.sum(jnp.where(hot1, base, 0.0), axis=-1, keepdims=True)
    rank2 = jnp.sum(jnp.where(hot2, base, 0.0), axis=-1, keepdims=True)
    total = carry_scr[...] + jnp.sum(cnt, axis=0, keepdims=True)
    carry_scr[...] = total
    counts_ref[...] = total

    route = jnp.where(lane_i == 0, i1 - ROUTE_BASE, 0.0)
    route = jnp.where(lane_i == 1, i2 - ROUTE_BASE, route)
    route = jnp.where(lane_i == 2, gate1, route)
    route = jnp.where(lane_i == 3, gate2, route)
    route = jnp.where(lane_i == 4, rank1, route)
    route = jnp.where(lane_i == 5, rank2, route)
    route_ref[...] = route
    meta_ref[...] = route.T[0:META_ROWS, :].astype(jnp.int32)


def _dispatch_kernel(dest_ref, x1s_ref, xs_ref, zbuf, sems):
    i = pl.program_id(0)
    n_steps = pl.num_programs(0)
    step_rows = TOP_K * TT * SLAB
    par = i % 2

    @pl.when(i == 0)
    def _():
        zbuf[...] = jnp.zeros_like(zbuf)
        fill = pltpu.make_async_copy(zbuf, xs_ref.at[pl.ds(xs_ref.shape[0] - BLK * SLAB, BLK * SLAB), :], sems.at[2])
        fill.start()
        fill.wait()

    def wait_step(p):
        pltpu.make_async_copy(x1s_ref.at[pl.ds(0, step_rows), :], xs_ref.at[pl.ds(0, step_rows), :],
                              sems.at[p]).wait()

    for r in range(TT):
        src = pl.multiple_of((i * TT + r) * SLAB, SLAB)
        for k in range(TOP_K):
            dst = pl.multiple_of(dest_ref[0, k, r] * SLAB, SLAB)
            pltpu.make_async_copy(x1s_ref.at[pl.ds(src, SLAB), :], xs_ref.at[pl.ds(dst, SLAB), :],
                                  sems.at[par]).start(priority=k)

    @pl.when(i > 0)
    def _():
        wait_step(1 - par)

    @pl.when(i + 1 >= n_steps)
    def _():
        wait_step(par)


def _expert_kernel(be_ref, nv_ref, row0_ref, xs_ref, wg_ref, wu_ref, wd_ref, yb_ref,
                   xbuf, wg_s, wu_s, wd_s, sems):
    i = pl.program_id(0)
    nv = nv_ref[0]
    blk_rows = BLK * SLAB

    def block_copy(b, slot):
        r = pl.multiple_of(row0_ref[b] * SLAB, SLAB)
        return pltpu.make_async_copy(xs_ref.at[pl.ds(r, blk_rows), :],
                                     xbuf.at[pl.ds(slot * blk_rows, blk_rows), :], sems.at[slot])

    @pl.when(i < nv)
    def _():
        slot = i % 2

        @pl.when(i == 0)
        def _():
            block_copy(0, 0).start()

        @pl.when(i + 1 < nv)
        def _():
            block_copy(i + 1, 1 - slot).start()

        prev = be_ref[jnp.maximum(i - 1, 0)]

        @pl.when((i == 0) | (be_ref[i] != prev))
        def _():
            wg_s[...] = wg_ref[...].astype(BF16)
            wu_s[...] = wu_ref[...].astype(BF16)
            wd_s[...] = wd_ref[...].astype(BF16)

        block_copy(i, slot).wait()
        base = pl.multiple_of(slot * blk_rows, blk_rows)
        xb = jnp.concatenate(
            [xbuf[pl.ds(base + s, BLK, stride=SLAB), :].astype(BF16) for s in range(SLAB)], axis=1)
        a = jnp.dot(xb, wg_s[...], preferred_element_type=F32)
        u = jnp.dot(xb, wu_s[...], preferred_element_type=F32)
        h = (a * jax.nn.sigmoid(a)) * u
        y = jnp.dot(h.astype(BF16), wd_s[...], preferred_element_type=F32)
        for s in range(SLAB):
            yb_ref[pl.ds(s, BLK, stride=SLAB), :] = y[:, s * LANES:(s + 1) * LANES]

    @pl.when(i >= nv)
    def _():
        yb_ref[...] = jnp.zeros_like(yb_ref)


def _combine_kernel(dest_ref, dest_next_ref, x1s_ref, route_ref, yb_ref, g_ref, b_ref, out_ref, ybuf, sems):
    i = pl.program_id(0)
    n_steps = pl.num_programs(0)
    half_rows = TD * SLAB
    par_rows = TOP_K * half_rows

    def start_gather(d_ref, par):
        for r in range(TD):
            for k in range(TOP_K):
                src = pl.multiple_of(d_ref[0, k, r] * SLAB, SLAB)
                dst = pl.multiple_of(par * par_rows + (k * TD + r) * SLAB, SLAB)
                pltpu.make_async_copy(yb_ref.at[pl.ds(src, SLAB), :], ybuf.at[pl.ds(dst, SLAB), :],
                                      sems.at[par]).start(priority=k)

    def wait_gather(par):
        pltpu.make_async_copy(yb_ref.at[pl.ds(0, par_rows), :],
                              ybuf.at[pl.ds(par * par_rows, par_rows), :], sems.at[par]).wait()

    par = i % 2

    @pl.when(i == 0)
    def _():
        start_gather(dest_ref, 0)

    wait_gather(par)
    start_gather(dest_next_ref, 1 - par)

    base = pl.multiple_of(par * par_rows, par_rows)
    g1 = route_ref[:, 2:3]
    g2 = route_ref[:, 3:4]
    zs = []
    for s in range(SLAB):
        y1 = ybuf[pl.ds(base + s, TD, stride=SLAB), :]
        y2 = ybuf[pl.ds(base + half_rows + s, TD, stride=SLAB), :]
        zs.append(DEEPNORM_ALPHA * x1s_ref[pl.ds(s, TD, stride=SLAB), :] + (g1 * y1 + g2 * y2))
    z = jnp.concatenate(zs, axis=1)
    out_ref[...] = _layer_norm(z, g_ref[...], b_ref[...])

    @pl.when(i + 1 >= n_steps)
    def _():
        wait_gather(1 - par)


def _const_spec(shape):
    nd = len(shape)
    return pl.BlockSpec(shape, lambda *_: (0,) * nd, pipeline_mode=pl.Buffered(1))


def kernel(x, w_in, w_a2, b_a, lb_logits, norm_h, norm_g, w_out, ln1_g, ln1_b, w_group_router, w_expert_router,
           w_gate, w_up, w_down, ln2_g, ln2_b):
    B, T, D = x.shape
    n_tok = B * T
    assert D == D_MODEL == SLAB * LANES and T % TT == 0 and TD == TT
    n_blocks = n_tok * TOP_K // BLK + N_EXPERTS
    n_slots = n_blocks * BLK

    parts = jnp.split(w_in[0], [int(c) for c in np.cumsum(IN_SPLITS)[:-1]], axis=-1)
    hq, hf, hi, hg, gq, gk, gv, ga, gg = parts
    win = jnp.concatenate([hq, hf, hi, hg, gq, gk, gv, gg, ga], axis=-1)
    win = jnp.pad(win, ((0, 0), (0, IN_COLS_PAD - win.shape[1]))).astype(BF16)
    wa2 = jnp.pad(w_a2[0], ((0, LANES - GATE_RANK), (0, 0))).astype(BF16)
    ba = b_a[0].reshape(1, GL_QK).astype(F32)
    gain = jnp.concatenate([norm_h[0], norm_g[0]]).reshape(1, D).astype(F32)
    wout = w_out[0].astype(BF16)
    wr = jnp.concatenate([w_group_router[0], w_expert_router[0]], axis=-1).astype(F32)
    wr = jnp.pad(wr, ((0, 0), (0, LANES - wr.shape[1])))
    wrh = wr.astype(BF16)
    wrl = (wr - wrh.astype(F32)).astype(BF16)
    wstk = jnp.asarray(_decay_matrices(), dtype=BF16)
    lev = jnp.asarray(_level_map())
    n_w = wstk.shape[0]

    n_steps = n_tok // TT
    x1s, route, meta, counts = pl.pallas_call(
        _mixer_kernel,
        grid=(B, T // TT),
        in_specs=[
            pl.BlockSpec((None, TT, D), lambda b, j: (b, j, 0)),
            _const_spec((D, IN_COLS_PAD)),
            _const_spec((LANES, GL_QK)),
            _const_spec((1, GL_QK)),
            _const_spec((DEPTH + 1, HG_WIDTH)),
            _const_spec((1, D)),
            _const_spec((D, D)),
            _const_spec((1, D)),
            _const_spec((1, D)),
            _const_spec((D, LANES)),
            _const_spec((D, LANES)),
            _const_spec((n_w, 3 * SUB)),
            _const_spec((SUB, SUB)),
        ],
        out_specs=[
            pl.BlockSpec((TT * SLAB, LANES), lambda b, j: (b * (T // TT) + j, 0)),
            pl.BlockSpec((TT, LANES), lambda b, j: (b * (T // TT) + j, 0)),
            pl.BlockSpec((META_ROWS, TT), lambda b, j: (b * (T // TT) + j, 0)),
            pl.BlockSpec((1, LANES), lambda b, j: (0, 0)),
        ],
        out_shape=[
            jax.ShapeDtypeStruct((n_tok * SLAB, LANES), F32),
            jax.ShapeDtypeStruct((n_tok, LANES), F32),
            jax.ShapeDtypeStruct((n_steps * META_ROWS, TT), jnp.int32),
            jax.ShapeDtypeStruct((1, LANES), F32),
        ],
        scratch_shapes=[
            pltpu.VMEM((TT, IN_COLS_PAD), F32),
            pltpu.VMEM((TT, D), F32),
            pltpu.VMEM((SUB, HG_WIDTH + GL_QK), F32),
            pltpu.VMEM((HG_HEADS + GL_HEADS, LANES, LANES), F32),
            pltpu.VMEM((1, LANES), F32),
        ],
        compiler_params=pltpu.CompilerParams(
            dimension_semantics=("arbitrary", "arbitrary"), vmem_limit_bytes=VMEM_LIMIT),
        name="mixer",
    )(x, win, wa2, ba, lb_logits.astype(F32), gain, wout, ln1_g[0].reshape(1, D), ln1_b[0].reshape(1, D),
      wrh, wrl, wstk, lev)

    cnt = counts[0, ROUTE_BASE:ROUTE_BASE + N_EXPERTS].astype(jnp.int32)
    padded = (cnt + BLK - 1) // BLK * BLK
    ends = jnp.cumsum(padded)
    starts = ends - padded
    n_valid = (ends[-1] // BLK).astype(jnp.int32)
    blk_ids = jnp.minimum(jnp.arange(n_blocks, dtype=jnp.int32), n_valid - 1)
    block_expert = jnp.sum((ends[None, :] <= (blk_ids * BLK)[:, None]).astype(jnp.int32), axis=1)
    block_expert = jnp.minimum(block_expert, N_EXPERTS - 1)
    dstarts = jnp.cumsum(cnt) - cnt
    first_blk = starts // BLK
    pick = block_expert[None, :] == jnp.arange(N_EXPERTS, dtype=jnp.int32)[:, None]
    row0 = jnp.sum(jnp.where(pick, (dstarts - first_blk * BLK)[:, None], 0), axis=0) + blk_ids * BLK
    meta3 = meta.reshape(n_steps, META_ROWS, TT)
    hit = meta3[None, :, 0:TOP_K, :] == jnp.arange(N_EXPERTS, dtype=jnp.int32)[:, None, None, None]
    rank = meta3[:, 4:4 + TOP_K, :]
    dest_x = rank + jnp.sum(jnp.where(hit, dstarts[:, None, None, None], 0), axis=0)
    dest = rank + jnp.sum(jnp.where(hit, starts[:, None, None, None], 0), axis=0)

    xs_rows = (n_tok * TOP_K + BLK) * SLAB
    xs = pl.pallas_call(
        _dispatch_kernel,
        grid=(n_steps,),
        in_specs=[
            pl.BlockSpec((1, TOP_K, TT), lambda i: (i, 0, 0), memory_space=pltpu.SMEM),
            pl.BlockSpec(memory_space=pl.ANY),
        ],
        out_specs=pl.BlockSpec(memory_space=pl.ANY),
        out_shape=jax.ShapeDtypeStruct((xs_rows, LANES), F32),
        scratch_shapes=[pltpu.VMEM((BLK * SLAB, LANES), F32), pltpu.SemaphoreType.DMA((3,))],
        compiler_params=pltpu.CompilerParams(dimension_semantics=("arbitrary",)),
        name="dispatch",
    )(dest_x, x1s)

    yb = pl.pallas_call(
        _expert_kernel,
        grid_spec=pltpu.PrefetchScalarGridSpec(
            num_scalar_prefetch=3,
            grid=(n_blocks,),
            in_specs=[
                pl.BlockSpec(memory_space=pl.ANY),
                pl.BlockSpec((None, D, D_EXPERT), lambda i, be, nv, r0: (be[i], 0, 0)),
                pl.BlockSpec((None, D, D_EXPERT), lambda i, be, nv, r0: (be[i], 0, 0)),
                pl.BlockSpec((None, D_EXPERT, D), lambda i, be, nv, r0: (be[i], 0, 0)),
            ],
            out_specs=pl.BlockSpec((BLK * SLAB, LANES), lambda i, be, nv, r0: (i, 0)),
            scratch_shapes=[
                pltpu.VMEM((2 * BLK * SLAB, LANES), F32),
                pltpu.VMEM((D, D_EXPERT), BF16),
                pltpu.VMEM((D, D_EXPERT), BF16),
                pltpu.VMEM((D_EXPERT, D), BF16),
                pltpu.SemaphoreType.DMA((2,)),
            ],
        ),
        out_shape=jax.ShapeDtypeStruct((n_slots * SLAB, LANES), F32),
        compiler_params=pltpu.CompilerParams(
            dimension_semantics=("arbitrary",), vmem_limit_bytes=VMEM_LIMIT),
        name="experts",
    )(block_expert, n_valid.reshape(1), row0, xs, w_gate[0], w_up[0], w_down[0])

    last = n_tok // TD - 1
    out = pl.pallas_call(
        _combine_kernel,
        grid=(n_tok // TD,),
        in_specs=[
            pl.BlockSpec((1, TOP_K, TD), lambda i: (i, 0, 0), memory_space=pltpu.SMEM),
            pl.BlockSpec((1, TOP_K, TD), lambda i: (jnp.minimum(i + 1, last), 0, 0), memory_space=pltpu.SMEM),
            pl.BlockSpec((TD * SLAB, LANES), lambda i: (i, 0)),
            pl.BlockSpec((TD, LANES), lambda i: (i, 0)),
            pl.BlockSpec(memory_space=pl.ANY),
            pl.BlockSpec((1, D), lambda i: (0, 0)),
            pl.BlockSpec((1, D), lambda i: (0, 0)),
        ],
        out_specs=pl.BlockSpec((TD, D), lambda i: (i, 0)),
        scratch_shapes=[pltpu.VMEM((2 * TOP_K * TD * SLAB, LANES), F32), pltpu.SemaphoreType.DMA((2,))],
        out_shape=jax.ShapeDtypeStruct((n_tok, D), F32),
        compiler_params=pltpu.CompilerParams(dimension_semantics=("arbitrary",)),
        name="combine",
    )(dest, dest, x1s, route, yb, ln2_g[0].reshape(1, D), ln2_b[0].reshape(1, D))

    return out.reshape(B, T, D).astype(x.dtype)
```

```python
import functools

import numpy as np
import jax
import jax.numpy as jnp
from jax import lax
from jax.experimental import pallas as pl
from jax.experimental.pallas import tpu as pltpu

D_MODEL = 1024
HG_WIDTH = 512
HG_HEAD_DIM = 128
HG_HEADS = 4
GL_WIDTH = 512
GL_HEADS = 4
GL_DV = 128
GL_DK = 64
GL_QK = 256
GATE_RANK = 16
GATE_TAU = 16.0
N_GROUPS = 8
EXPERTS_PER_GROUP = 8
N_EXPERTS = 64
TOP_K = 2
D_EXPERT = 512
DEPTH = 1
DEEPNORM_ALPHA = (2.0 * DEPTH) ** 0.25
LN_EPS = 1e-5
IN_SPLITS = (HG_WIDTH, HG_WIDTH, HG_WIDTH, HG_WIDTH, GL_QK, GL_QK, GL_WIDTH, GATE_RANK, GL_WIDTH)

LANES = 128
SUB = 128
N_LEVELS = 7
MXU_LEVELS = 3
TT = 256
TD = 256
BLK = 256
IN_COLS_PAD = 3712
ROUTE_BASE = 8
SLAB = 8
META_ROWS = 8
VMEM_LIMIT = 56 * 1024 * 1024

F32 = jnp.float32
BF16 = jnp.bfloat16
NT_DIMS = (((1,), (1,)), ((), ()))
TN_DIMS = (((0,), (0,)), ((), ()))


def _decay_matrices():
    t = np.arange(SUB)[:, None]
    j = np.arange(SUB)[None, :]
    mats = [(j <= t)]
    for lvl in range(1, MXU_LEVELS + 1):
        blk = 1 << lvl
        half = blk >> 1
        r = (t // blk) * blk + half - 1
        right = (t % blk) >= half
        mats.append(np.where(right, (j > r) & (j <= t), (j > t) & (j <= r)))
    w = np.concatenate(mats, axis=0).astype(np.float32)
    return np.concatenate([w, w, w], axis=1)


def _level_map():
    t = np.arange(SUB)[:, None]
    s = np.arange(SUB)[None, :]
    x = np.bitwise_xor(t, s)
    lvl = np.floor(np.log2(np.maximum(x, 1))).astype(np.int32) + 1
    lvl = np.where(x == 0, 0, lvl)
    return np.where(s > t, -1, lvl).astype(np.int32)


def _layer_norm(y, g, b):
    mu = jnp.mean(y, axis=-1, keepdims=True)
    yc = y - mu
    var = jnp.mean(yc * yc, axis=-1, keepdims=True)
    return yc * lax.rsqrt(var + LN_EPS) * g + b


def _mixer_kernel(x_ref, win_ref, wa2_ref, ba_ref, lbl_ref, gain_ref, wout_ref, ln1g_ref, ln1b_ref,
                  wrh_ref, wrl_ref, wstk_ref, lev_ref,
                  x1s_ref, route_ref, meta_ref, counts_ref,
                  p_scr, o_scr, b_scr, st_scr, carry_scr):
    bi = pl.program_id(0)
    ji = pl.program_id(1)

    @pl.when(ji == 0)
    def _():
        st_scr[...] = jnp.zeros_like(st_scr)

    @pl.when((bi == 0) & (ji == 0))
    def _():
        carry_scr[...] = jnp.zeros_like(carry_scr)

    xt = x_ref[...]
    p_scr[...] = jnp.dot(xt.astype(BF16), win_ref[...], preferred_element_type=F32)

    lbl = lbl_ref[...]
    lbe = jnp.exp(lbl - jnp.max(lbl, axis=0, keepdims=True))
    lb = lbe[0:1, :] / jnp.sum(lbe, axis=0, keepdims=True)

    lev = lev_ref[...]
    tcol = lax.broadcasted_iota(jnp.int32, (SUB, 1), 0)
    qrow = [(tcol & ((1 << l) - 1)) >= (1 << (l - 1)) for l in range(1, N_LEVELS + 1)]
    lane = lax.broadcasted_iota(jnp.int32, (1, LANES), 1)
    half_mask = [(lane < GL_DK), (lane >= GL_DK)]

    def subtile(s, carry):
        r0 = pl.multiple_of(s * SUB, SUB)
        rows = pl.ds(r0, SUB)
        hq = p_scr[rows, 0:512]
        hf = p_scr[rows, 512:1024]
        hv = p_scr[rows, 1024:1536]
        hgate = p_scr[rows, 1536:2048]
        gq = p_scr[rows, 2048:2304]
        gk = p_scr[rows, 2304:2560]
        gv = p_scr[rows, 2560:3072]
        ggate = p_scr[rows, 3072:3584]
        ga = p_scr[rows, 3584:3712]

        f = lb + (1.0 - lb) * jax.nn.sigmoid(hf)
        kh = 1.0 - f
        lgh = jnp.log(f)
        z = jnp.dot(ga.astype(BF16), wa2_ref[...], preferred_element_type=F32) + ba_ref[...]
        lga = (jnp.minimum(z, 0.0) - jnp.log1p(jnp.exp(-jnp.abs(z)))) * (1.0 / GATE_TAU)
        lg = jnp.concatenate([lgh, lga], axis=1)

        l_hi = lg.astype(BF16)
        rem = lg - l_hi.astype(F32)
        l_mid = rem.astype(BF16)
        l_lo = (rem - l_mid.astype(F32)).astype(BF16)
        sums = jnp.dot(wstk_ref[...], jnp.concatenate([l_hi, l_mid, l_lo], axis=0),
                       preferred_element_type=F32)
        bcum = sums[0:SUB]
        b_scr[...] = bcum
        expos = [bcum, b_scr[SUB - 1:SUB, :] - bcum]
        for l in range(1, N_LEVELS + 1):
            if l <= MXU_LEVELS:
                expos.append(sums[l * SUB:(l + 1) * SUB])
            else:
                blk = 1 << l
                bound = jnp.concatenate(
                    [jnp.broadcast_to(b_scr[j * blk + blk // 2 - 1:j * blk + blk // 2, :], (blk, lg.shape[1]))
                     for j in range(SUB // blk)], axis=0)
                expos.append(jnp.where(qrow[l - 1], bcum - bound, bound - bcum))
        decs = [jnp.exp(e) for e in expos]

        def group_terms(q, k, c0):
            e_pre = decs[0][:, c0:c0 + LANES]
            e_suf = decs[1][:, c0:c0 + LANES]
            us = []
            for l in range(1, N_LEVELS + 1):
                e_l = decs[1 + l][:, c0:c0 + LANES]
                us.append((jnp.where(qrow[l - 1], q, k) * e_l).astype(BF16))
            return us, (q * e_pre).astype(BF16), k * e_suf, e_pre[SUB - 1:SUB, :]

        def head(hidx, us, qd, kd, qk, v, g_last, mask):
            sc = jnp.where(lev == 0, jnp.sum(qk, axis=-1, keepdims=True), 0.0)
            for l in range(1, N_LEVELS + 1):
                u = us[l - 1]
                lhs = u if mask is None else jnp.where(mask, u, jnp.zeros_like(u))
                g = lax.dot_general(lhs, u, NT_DIMS, preferred_element_type=F32)
                sc = jnp.where(lev == l, g, sc)
            vb = v.astype(BF16)
            st = st_scr[hidx]
            o = jnp.dot(sc.astype(BF16), vb, preferred_element_type=F32)
            o = o + lax.dot_general(qd, st.astype(BF16), NT_DIMS, preferred_element_type=F32)
            ut = lax.dot_general(vb, kd.astype(BF16), TN_DIMS, preferred_element_type=F32)
            st_scr[hidx] = st * g_last + ut
            return o

        def finish(o, gate, c0):
            o = o * lax.rsqrt(jnp.mean(o * o, axis=-1, keepdims=True) + LN_EPS)
            o = o * gain_ref[:, c0:c0 + LANES] * (gate * jax.nn.sigmoid(gate))
            o_scr[rows, c0:c0 + LANES] = o

        for g in range(HG_HEADS):
            c0 = g * LANES
            q = hq[:, c0:c0 + LANES] * (HG_HEAD_DIM ** -0.5)
            k = kh[:, c0:c0 + LANES]
            us, qd, kd, g_last = group_terms(q, k, c0)
            o = head(g, us, qd, kd, q * k, hv[:, c0:c0 + LANES], g_last, None)
            finish(o, hgate[:, c0:c0 + LANES], c0)

        for p in range(GL_HEADS // 2):
            c0 = p * LANES
            q = gq[:, c0:c0 + LANES] * (GL_DK ** -0.5)
            k = gk[:, c0:c0 + LANES]
            us, qd, kd, g_last = group_terms(q, k, HG_WIDTH + c0)
            for hh in range(2):
                h = 2 * p + hh
                m = half_mask[hh]
                o = head(HG_HEADS + h, us, qd, jnp.where(m, kd, 0.0), jnp.where(m, q * k, 0.0),
                         gv[:, h * LANES:(h + 1) * LANES], g_last, m)
                finish(o, ggate[:, h * LANES:(h + 1) * LANES], HG_WIDTH + h * LANES)
        return carry

    lax.fori_loop(0, TT // SUB, subtile, 0)

    hmix = jnp.dot(o_scr[...].astype(BF16), wout_ref[...], preferred_element_type=F32)
    x1 = _layer_norm(DEEPNORM_ALPHA * xt + hmix, ln1g_ref[...], ln1b_ref[...])
    for s in range(SLAB):
        x1s_ref[pl.ds(s, TT, stride=SLAB), :] = x1[:, s * LANES:(s + 1) * LANES]

    x1h = x1.astype(BF16)
    x1l = (x1 - x1h.astype(F32)).astype(BF16)
    logits = (jnp.dot(x1h, wrh_ref[...], preferred_element_type=F32)
              + jnp.dot(x1l, wrh_ref[...], preferred_element_type=F32)
              + jnp.dot(x1h, wrl_ref[...], preferred_element_type=F32))

    lane_i = lax.broadcasted_iota(jnp.int32, (TT, LANES), 1)
    lane_f = lane_i.astype(F32)
    neg = jnp.float32(-jnp.inf)
    big = jnp.float32(1e9)

    gl = jnp.where(lane_i < N_GROUPS, logits, neg)
    gmax = jnp.max(gl, axis=-1, keepdims=True)
    gidx = jnp.min(jnp.where(gl == gmax, lane_f, big), axis=-1, keepdims=True)
    p_group = 1.0 / jnp.sum(jnp.exp(gl - gmax), axis=-1, keepdims=True)

    e_lo = ROUTE_BASE + EXPERTS_PER_GROUP * gidx
    el = jnp.where((lane_f >= e_lo) & (lane_f < e_lo + EXPERTS_PER_GROUP), logits, neg)
    m1 = jnp.max(el, axis=-1, keepdims=True)
    i1 = jnp.min(jnp.where(el == m1, lane_f, big), axis=-1, keepdims=True)
    el2 = jnp.where(lane_f == i1, neg, el)
    m2 = jnp.max(el2, axis=-1, keepdims=True)
    i2 = jnp.min(jnp.where(el2 == m2, lane_f, big), axis=-1, keepdims=True)
    r2 = jnp.exp(m2 - m1)
    gate1 = p_group / (1.0 + r2)
    gate2 = p_group * r2 / (1.0 + r2)

    hot1 = lane_f == i1
    hot2 = lane_f == i2
    cnt = (hot1 | hot2).astype(F32)
    tr = lax.broadcasted_iota(jnp.int32, (TT, TT), 0)
    tc = lax.broadcasted_iota(jnp.int32, (TT, TT), 1)
    strict = (tc < tr).astype(BF16)
    base = carry_scr[...] + jnp.dot(strict, cnt.astype(BF16), preferred_element_type=F32)
    rank1 = jnp.sum(jnp.where(hot1, base, 0.0), axis=-1, keepdims=True)
    rank2 = jnp.sum(jnp.where(hot2, base, 0.0), axis=-1, keepdims=True)
    total = carry_scr[...] + jnp.sum(cnt, axis=0, keepdims=True)
    carry_scr[...] = total
    counts_ref[...] = total

    route = jnp.where(lane_i == 0, i1 - ROUTE_BASE, 0.0)
    route = jnp.where(lane_i == 1, i2 - ROUTE_BASE, route)
    route = jnp.where(lane_i == 2, gate1, route)
    route = jnp.where(lane_i == 3, gate2, route)
    route = jnp.where(lane_i == 4, rank1, route)
    route = jnp.where(lane_i == 5, rank2, route)
    route_ref[...] = route
    meta_ref[...] = route.T[0:META_ROWS, :].astype(jnp.int32)


def _dispatch_kernel(dest_ref, x1s_ref, xs_ref, zbuf, sems):
    i = pl.program_id(0)
    tile_rows = TT * SLAB

    @pl.when(i == 0)
    def _():
        zbuf[...] = jnp.zeros_like(zbuf)
        fill = pltpu.make_async_copy(zbuf, xs_ref.at[pl.ds(xs_ref.shape[0] - BLK * SLAB, BLK * SLAB), :], sems.at[1])
        fill.start()
        fill.wait()

    for r in range(TT):
        for k in range(TOP_K):
            dst = pl.multiple_of(dest_ref[0, k, r] * SLAB, SLAB)
            pltpu.make_async_copy(x1s_ref.at[pl.ds(r * SLAB, SLAB), :], xs_ref.at[pl.ds(dst, SLAB), :],
                                  sems.at[0]).start(priority=k)

    for k in range(TOP_K):
        pltpu.make_async_copy(x1s_ref, xs_ref.at[pl.ds(0, tile_rows), :], sems.at[0]).wait()


def _expert_kernel(be_ref, nv_ref, row0_ref, xs_ref, wg_ref, wu_ref, wd_ref, yb_ref,
                   xbuf, wg_s, wu_s, wd_s, sems):
    i = pl.program_id(0)
    nv = nv_ref[0]
    blk_rows = BLK * SLAB

    def block_copy(b, slot):
        r = pl.multiple_of(row0_ref[b] * SLAB, SLAB)
        return pltpu.make_async_copy(xs_ref.at[pl.ds(r, blk_rows), :],
                                     xbuf.at[pl.ds(slot * blk_rows, blk_rows), :], sems.at[slot])

    @pl.when(i < nv)
    def _():
        slot = i % 2

        @pl.when(i == 0)
        def _():
            block_copy(0, 0).start()

        @pl.when(i + 1 < nv)
        def _():
            block_copy(i + 1, 1 - slot).start()

        prev = be_ref[jnp.maximum(i - 1, 0)]

        @pl.when((i == 0) | (be_ref[i] != prev))
        def _():
            wg_s[...] = wg_ref[...].astype(BF16)
            wu_s[...] = wu_ref[...].astype(BF16)
            wd_s[...] = wd_ref[...].astype(BF16)

        block_copy(i, slot).wait()
        base = pl.multiple_of(slot * blk_rows, blk_rows)
        xb = jnp.concatenate(
            [xbuf[pl.ds(base + s, BLK, stride=SLAB), :].astype(BF16) for s in range(SLAB)], axis=1)
        a = jnp.dot(xb, wg_s[...], preferred_element_type=F32)
        u = jnp.dot(xb, wu_s[...], preferred_element_type=F32)
        h = (a * jax.nn.sigmoid(a)) * u
        y = jnp.dot(h.astype(BF16), wd_s[...], preferred_element_type=F32)
        for s in range(SLAB):
            yb_ref[pl.ds(s, BLK, stride=SLAB), :] = y[:, s * LANES:(s + 1) * LANES]

    @pl.when(i >= nv)
    def _():
        yb_ref[...] = jnp.zeros_like(yb_ref)


def _combine_kernel(dest_ref, dest_next_ref, x1s_ref, route_ref, yb_ref, g_ref, b_ref, out_ref, ybuf, sems):
    i = pl.program_id(0)
    n_steps = pl.num_programs(0)
    half_rows = TD * SLAB
    par_rows = TOP_K * half_rows

    def start_gather(d_ref, par):
        for r in range(TD):
            for k in range(TOP_K):
                src = pl.multiple_of(d_ref[0, k, r] * SLAB, SLAB)
                dst = pl.multiple_of(par * par_rows + (k * TD + r) * SLAB, SLAB)
                pltpu.make_async_copy(yb_ref.at[pl.ds(src, SLAB), :], ybuf.at[pl.ds(dst, SLAB), :],
                                      sems.at[par]).start(priority=k)

    def wait_gather(par):
        pltpu.make_async_copy(yb_ref.at[pl.ds(0, par_rows), :],
                              ybuf.at[pl.ds(par * par_rows, par_rows), :], sems.at[par]).wait()

    par = i % 2

    @pl.when(i == 0)
    def _():
        start_gather(dest_ref, 0)

    wait_gather(par)
    start_gather(dest_next_ref, 1 - par)

    base = pl.multiple_of(par * par_rows, par_rows)
    g1 = route_ref[:, 2:3]
    g2 = route_ref[:, 3:4]
    zs = []
    for s in range(SLAB):
        y1 = ybuf[pl.ds(base + s, TD, stride=SLAB), :]
        y2 = ybuf[pl.ds(base + half_rows + s, TD, stride=SLAB), :]
        zs.append(DEEPNORM_ALPHA * x1s_ref[pl.ds(s, TD, stride=SLAB), :] + (g1 * y1 + g2 * y2))
    z = jnp.concatenate(zs, axis=1)
    out_ref[...] = _layer_norm(z, g_ref[...], b_ref[...])

    @pl.when(i + 1 >= n_steps)
    def _():
        wait_gather(1 - par)


def _const_spec(shape):
    nd = len(shape)
    return pl.BlockSpec(shape, lambda *_: (0,) * nd, pipeline_mode=pl.Buffered(1))


def kernel(x, w_in, w_a2, b_a, lb_logits, norm_h, norm_g, w_out, ln1_g, ln1_b, w_group_router, w_expert_router,
           w_gate, w_up, w_down, ln2_g, ln2_b):
    B, T, D = x.shape
    n_tok = B * T
    assert D == D_MODEL == SLAB * LANES and T % TT == 0 and TD == TT
    n_blocks = n_tok * TOP_K // BLK + N_EXPERTS
    n_slots = n_blocks * BLK

    parts = jnp.split(w_in[0], [int(c) for c in np.cumsum(IN_SPLITS)[:-1]], axis=-1)
    hq, hf, hi, hg, gq, gk, gv, ga, gg = parts
    win = jnp.concatenate([hq, hf, hi, hg, gq, gk, gv, gg, ga], axis=-1)
    win = jnp.pad(win, ((0, 0), (0, IN_COLS_PAD - win.shape[1]))).astype(BF16)
    wa2 = jnp.pad(w_a2[0], ((0, LANES - GATE_RANK), (0, 0))).astype(BF16)
    ba = b_a[0].reshape(1, GL_QK).astype(F32)
    gain = jnp.concatenate([norm_h[0], norm_g[0]]).reshape(1, D).astype(F32)
    wout = w_out[0].astype(BF16)
    wr = jnp.concatenate([w_group_router[0], w_expert_router[0]], axis=-1).astype(F32)
    wr = jnp.pad(wr, ((0, 0), (0, LANES - wr.shape[1])))
    wrh = wr.astype(BF16)
    wrl = (wr - wrh.astype(F32)).astype(BF16)
    wstk = jnp.asarray(_decay_matrices(), dtype=BF16)
    lev = jnp.asarray(_level_map())
    n_w = wstk.shape[0]

    n_steps = n_tok // TT
    x1s, route, meta, counts = pl.pallas_call(
        _mixer_kernel,
        grid=(B, T // TT),
        in_specs=[
            pl.BlockSpec((None, TT, D), lambda b, j: (b, j, 0)),
            _const_spec((D, IN_COLS_PAD)),
            _const_spec((LANES, GL_QK)),
            _const_spec((1, GL_QK)),
            _const_spec((DEPTH + 1, HG_WIDTH)),
            _const_spec((1, D)),
            _const_spec((D, D)),
            _const_spec((1, D)),
            _const_spec((1, D)),
            _const_spec((D, LANES)),
            _const_spec((D, LANES)),
            _const_spec((n_w, 3 * SUB)),
            _const_spec((SUB, SUB)),
        ],
        out_specs=[
            pl.BlockSpec((TT * SLAB, LANES), lambda b, j: (b * (T // TT) + j, 0)),
            pl.BlockSpec((TT, LANES), lambda b, j: (b * (T // TT) + j, 0)),
            pl.BlockSpec((META_ROWS, TT), lambda b, j: (b * (T // TT) + j, 0)),
            pl.BlockSpec((1, LANES), lambda b, j: (0, 0)),
        ],
        out_shape=[
            jax.ShapeDtypeStruct((n_tok * SLAB, LANES), F32),
            jax.ShapeDtypeStruct((n_tok, LANES), F32),
            jax.ShapeDtypeStruct((n_steps * META_ROWS, TT), jnp.int32),
            jax.ShapeDtypeStruct((1, LANES), F32),
        ],
        scratch_shapes=[
            pltpu.VMEM((TT, IN_COLS_PAD), F32),
            pltpu.VMEM((TT, D), F32),
            pltpu.VMEM((SUB, HG_WIDTH + GL_QK), F32),
            pltpu.VMEM((HG_HEADS + GL_HEADS, LANES, LANES), F32),
            pltpu.VMEM((1, LANES), F32),
        ],
        compiler_params=pltpu.CompilerParams(
            dimension_semantics=("arbitrary", "arbitrary"), vmem_limit_bytes=VMEM_LIMIT),
        name="mixer",
    )(x, win, wa2, ba, lb_logits.astype(F32), gain, wout, ln1_g[0].reshape(1, D), ln1_b[0].reshape(1, D),
      wrh, wrl, wstk, lev)

    cnt = counts[0, ROUTE_BASE:ROUTE_BASE + N_EXPERTS].astype(jnp.int32)
    padded = (cnt + BLK - 1) // BLK * BLK
    ends = jnp.cumsum(padded)
    starts = ends - padded
    n_valid = (ends[-1] // BLK).astype(jnp.int32)
    blk_ids = jnp.minimum(jnp.arange(n_blocks, dtype=jnp.int32), n_valid - 1)
    block_expert = jnp.sum((ends[None, :] <= (blk_ids * BLK)[:, None]).astype(jnp.int32), axis=1)
    block_expert = jnp.minimum(block_expert, N_EXPERTS - 1)
    dstarts = jnp.cumsum(cnt) - cnt
    first_blk = starts // BLK
    pick = block_expert[None, :] == jnp.arange(N_EXPERTS, dtype=jnp.int32)[:, None]
    row0 = jnp.sum(jnp.where(pick, (dstarts - first_blk * BLK)[:, None], 0), axis=0) + blk_ids * BLK
    meta3 = meta.reshape(n_steps, META_ROWS, TT)
    hit = meta3[None, :, 0:TOP_K, :] == jnp.arange(N_EXPERTS, dtype=jnp.int32)[:, None, None, None]
    rank = meta3[:, 4:4 + TOP_K, :]
    dest_x = rank + jnp.sum(jnp.where(hit, dstarts[:, None, None, None], 0), axis=0)
    dest = rank + jnp.sum(jnp.where(hit, starts[:, None, None, None], 0), axis=0)

    xs_rows = (n_tok * TOP_K + BLK) * SLAB
    xs = pl.pallas_call(
        _dispatch_kernel,
        grid=(n_steps,),
        in_specs=[
            pl.BlockSpec((1, TOP_K, TT), lambda i: (i, 0, 0), memory_space=pltpu.SMEM),
            pl.BlockSpec((TT * SLAB, LANES), lambda i: (i, 0)),
        ],
        out_specs=pl.BlockSpec(memory_space=pl.ANY),
        out_shape=jax.ShapeDtypeStruct((xs_rows, LANES), F32),
        scratch_shapes=[pltpu.VMEM((BLK * SLAB, LANES), F32), pltpu.SemaphoreType.DMA((2,))],
        compiler_params=pltpu.CompilerParams(dimension_semantics=("arbitrary",)),
        name="dispatch",
    )(dest_x, x1s)

    yb = pl.pallas_call(
        _expert_kernel,
        grid_spec=pltpu.PrefetchScalarGridSpec(
            num_scalar_prefetch=3,
            grid=(n_blocks,),
            in_specs=[
                pl.BlockSpec(memory_space=pl.ANY),
                pl.BlockSpec((None, D, D_EXPERT), lambda i, be, nv, r0: (be[i], 0, 0)),
                pl.BlockSpec((None, D, D_EXPERT), lambda i, be, nv, r0: (be[i], 0, 0)),
                pl.BlockSpec((None, D_EXPERT, D), lambda i, be, nv, r0: (be[i], 0, 0)),
            ],
            out_specs=pl.BlockSpec((BLK * SLAB, LANES), lambda i, be, nv, r0: (i, 0)),
            scratch_shapes=[
                pltpu.VMEM((2 * BLK * SLAB, LANES), F32),
                pltpu.VMEM((D, D_EXPERT), BF16),
                pltpu.VMEM((D, D_EXPERT), BF16),
                pltpu.VMEM((D_EXPERT, D), BF16),
                pltpu.SemaphoreType.DMA((2,)),
            ],
        ),
        out_shape=jax.ShapeDtypeStruct((n_slots * SLAB, LANES), F32),
        compiler_params=pltpu.CompilerParams(
            dimension_semantics=("arbitrary",), vmem_limit_bytes=VMEM_LIMIT),
        name="experts",
    )(block_expert, n_valid.reshape(1), row0, xs, w_gate[0], w_up[0], w_down[0])

    last = n_tok // TD - 1
    out = pl.pallas_call(
        _combine_kernel,
        grid=(n_tok // TD,),
        in_specs=[
            pl.BlockSpec((1, TOP_K, TD), lambda i: (i, 0, 0), memory_space=pltpu.SMEM),
            pl.BlockSpec((1, TOP_K, TD), lambda i: (jnp.minimum(i + 1, last), 0, 0), memory_space=pltpu.SMEM),
            pl.BlockSpec((TD * SLAB, LANES), lambda i: (i, 0)),
            pl.BlockSpec((TD, LANES), lambda i: (i, 0)),
            pl.BlockSpec(memory_space=pl.ANY),
            pl.BlockSpec((1, D), lambda i: (0, 0)),
            pl.BlockSpec((1, D), lambda i: (0, 0)),
        ],
        out_specs=pl.BlockSpec((TD, D), lambda i: (i, 0)),
        scratch_shapes=[pltpu.VMEM((2 * TOP_K * TD * SLAB, LANES), F32), pltpu.SemaphoreType.DMA((2,))],
        out_shape=jax.ShapeDtypeStruct((n_tok, D), F32),
        compiler_params=pltpu.CompilerParams(dimension_semantics=("arbitrary",)),
        name="combine",
    )(dest, dest, x1s, route, yb, ln2_g[0].reshape(1, D), ln2_b[0].reshape(1, D))

    return out.reshape(B, T, D).astype(x.dtype)
```

```python
import functools

import numpy as np
import jax
import jax.numpy as jnp
from jax import lax
from jax.experimental import pallas as pl
from jax.experimental.pallas import tpu as pltpu

D_MODEL = 1024
HG_WIDTH = 512
HG_HEAD_DIM = 128
HG_HEADS = 4
GL_WIDTH = 512
GL_HEADS = 4
GL_DV = 128
GL_DK = 64
GL_QK = 256
GATE_RANK = 16
GATE_TAU = 16.0
N_GROUPS = 8
EXPERTS_PER_GROUP = 8
N_EXPERTS = 64
TOP_K = 2
D_EXPERT = 512
DEPTH = 1
DEEPNORM_ALPHA = (2.0 * DEPTH) ** 0.25
LN_EPS = 1e-5
IN_SPLITS = (HG_WIDTH, HG_WIDTH, HG_WIDTH, HG_WIDTH, GL_QK, GL_QK, GL_WIDTH, GATE_RANK, GL_WIDTH)

LANES = 128
SUB = 128
N_LEVELS = 7
MXU_LEVELS = 3
TT = 256
TD = 256
BLK = 256
IN_COLS_PAD = 3712
ROUTE_BASE = 8
SLAB = 8
META_ROWS = 8
VMEM_LIMIT = 56 * 1024 * 1024

F32 = jnp.float32
BF16 = jnp.bfloat16
NT_DIMS = (((1,), (1,)), ((), ()))
TN_DIMS = (((0,), (0,)), ((), ()))


def _decay_matrices():
    t = np.arange(SUB)[:, None]
    j = np.arange(SUB)[None, :]
    mats = [(j <= t)]
    for lvl in range(1, MXU_LEVELS + 1):
        blk = 1 << lvl
        half = blk >> 1
        r = (t // blk) * blk + half - 1
        right = (t % blk) >= half
        mats.append(np.where(right, (j > r) & (j <= t), (j > t) & (j <= r)))
    w = np.concatenate(mats, axis=0).astype(np.float32)
    return np.concatenate([w, w, w], axis=1)


def _level_map():
    t = np.arange(SUB)[:, None]
    s = np.arange(SUB)[None, :]
    x = np.bitwise_xor(t, s)
    lvl = np.floor(np.log2(np.maximum(x, 1))).astype(np.int32) + 1
    lvl = np.where(x == 0, 0, lvl)
    return np.where(s > t, -1, lvl).astype(np.int32)


def _layer_norm(y, g, b):
    mu = jnp.mean(y, axis=-1, keepdims=True)
    yc = y - mu
    var = jnp.mean(yc * yc, axis=-1, keepdims=True)
    return yc * lax.rsqrt(var + LN_EPS) * g + b


def _mixer_kernel(x_ref, win_ref, wa2_ref, ba_ref, lbl_ref, gain_ref, wout_ref, ln1g_ref, ln1b_ref,
                  wrh_ref, wrl_ref, wstk_ref, lev_ref,
                  x1s_ref, route_ref, meta_ref, counts_ref,
                  p_scr, o_scr, b_scr, st_scr, carry_scr):
    bi = pl.program_id(0)
    ji = pl.program_id(1)

    @pl.when(ji == 0)
    def _():
        st_scr[...] = jnp.zeros_like(st_scr)

    @pl.when((bi == 0) & (ji == 0))
    def _():
        carry_scr[...] = jnp.zeros_like(carry_scr)

    xt = x_ref[...]
    p_scr[...] = jnp.dot(xt.astype(BF16), win_ref[...], preferred_element_type=F32)

    lbl = lbl_ref[...]
    lbe = jnp.exp(lbl - jnp.max(lbl, axis=0, keepdims=True))
    lb = lbe[0:1, :] / jnp.sum(lbe, axis=0, keepdims=True)

    lev = lev_ref[...]
    tcol = lax.broadcasted_iota(jnp.int32, (SUB, 1), 0)
    qrow = [(tcol & ((1 << l) - 1)) >= (1 << (l - 1)) for l in range(1, N_LEVELS + 1)]
    lane = lax.broadcasted_iota(jnp.int32, (1, LANES), 1)
    half_mask = [(lane < GL_DK), (lane >= GL_DK)]

    def by_halves(l, left_fn, right_fn):
        blk = 1 << l
        half = blk >> 1
        parts = []
        for j in range(SUB // blk):
            lo = j * blk
            parts += [left_fn(lo, lo + half), right_fn(lo + half, lo + blk)]
        return jnp.concatenate(parts, axis=0)

    def subtile(s, carry):
        r0 = pl.multiple_of(s * SUB, SUB)
        rows = pl.ds(r0, SUB)
        hq = p_scr[rows, 0:512]
        hf = p_scr[rows, 512:1024]
        hv = p_scr[rows, 1024:1536]
        hgate = p_scr[rows, 1536:2048]
        gq = p_scr[rows, 2048:2304]
        gk = p_scr[rows, 2304:2560]
        gv = p_scr[rows, 2560:3072]
        ggate = p_scr[rows, 3072:3584]
        ga = p_scr[rows, 3584:3712]

        f = lb + (1.0 - lb) * jax.nn.sigmoid(hf)
        kh = 1.0 - f
        lgh = jnp.log(f)
        z = jnp.dot(ga.astype(BF16), wa2_ref[...], preferred_element_type=F32) + ba_ref[...]
        lga = (jnp.minimum(z, 0.0) - jnp.log1p(jnp.exp(-jnp.abs(z)))) * (1.0 / GATE_TAU)
        lg = jnp.concatenate([lgh, lga], axis=1)

        l_hi = lg.astype(BF16)
        rem = lg - l_hi.astype(F32)
        l_mid = rem.astype(BF16)
        l_lo = (rem - l_mid.astype(F32)).astype(BF16)
        sums = jnp.dot(wstk_ref[...], jnp.concatenate([l_hi, l_mid, l_lo], axis=0),
                       preferred_element_type=F32)
        bcum = sums[0:SUB]
        b_scr[...] = bcum
        expos = [bcum, b_scr[SUB - 1:SUB, :] - bcum]
        for l in range(1, N_LEVELS + 1):
            if l <= MXU_LEVELS:
                expos.append(sums[l * SUB:(l + 1) * SUB])
            else:
                expos.append(by_halves(l, lambda lo, mid: b_scr[mid - 1:mid, :] - bcum[lo:mid],
                                       lambda mid, hi: bcum[mid:hi] - b_scr[mid - 1:mid, :]))
        decs = [jnp.exp(e) for e in expos]

        def group_terms(q, k, c0):
            e_pre = decs[0][:, c0:c0 + LANES]
            e_suf = decs[1][:, c0:c0 + LANES]
            us = []
            for l in range(1, N_LEVELS + 1):
                e_l = decs[1 + l][:, c0:c0 + LANES]
                if l <= MXU_LEVELS:
                    sel = jnp.where(qrow[l - 1], q, k)
                else:
                    sel = by_halves(l, lambda lo, mid: k[lo:mid], lambda mid, hi: q[mid:hi])
                u = sel * e_l
                us.append((u, u.astype(BF16)))
            return us, (q * e_pre).astype(BF16), k * e_suf, e_pre[SUB - 1:SUB, :]

        def head(hidx, us, qd, kd, qk, v, g_last, mask):
            sc = jnp.where(lev == 0, jnp.sum(qk, axis=-1, keepdims=True), 0.0)
            for l in range(1, N_LEVELS + 1):
                u, ub = us[l - 1]
                if l <= MXU_LEVELS:
                    lhs = ub if mask is None else jnp.where(mask, ub, jnp.zeros_like(ub))
                    g = lax.dot_general(lhs, ub, NT_DIMS, preferred_element_type=F32)
                    sc = jnp.where(lev == l, g, sc)
                else:
                    blk = 1 << l
                    half = blk >> 1
                    mids = [j * blk + half for j in range(SUB // blk)]
                    uq = jnp.concatenate([u[m:m + half] for m in mids], axis=0)
                    if mask is not None:
                        uq = jnp.where(mask, uq, 0.0)
                    g = lax.dot_general(uq.astype(BF16), ub, NT_DIMS, preferred_element_type=F32)
                    parts = []
                    for n, m in enumerate(mids):
                        parts.append(sc[m - half:m])
                        parts.append(jnp.where(lev[m:m + half] == l, g[n * half:(n + 1) * half], sc[m:m + half]))
                    sc = jnp.concatenate(parts, axis=0)
            vb = v.astype(BF16)
            st = st_scr[hidx]
            o = jnp.dot(sc.astype(BF16), vb, preferred_element_type=F32)
            o = o + lax.dot_general(qd, st.astype(BF16), NT_DIMS, preferred_element_type=F32)
            ut = lax.dot_general(vb, kd.astype(BF16), TN_DIMS, preferred_element_type=F32)
            st_scr[hidx] = st * g_last + ut
            return o

        def finish(o, gate, c0):
            o = o * lax.rsqrt(jnp.mean(o * o, axis=-1, keepdims=True) + LN_EPS)
            o = o * gain_ref[:, c0:c0 + LANES] * (gate * jax.nn.sigmoid(gate))
            o_scr[rows, c0:c0 + LANES] = o

        for g in range(HG_HEADS):
            c0 = g * LANES
            q = hq[:, c0:c0 + LANES] * (HG_HEAD_DIM ** -0.5)
            k = kh[:, c0:c0 + LANES]
            us, qd, kd, g_last = group_terms(q, k, c0)
            o = head(g, us, qd, kd, q * k, hv[:, c0:c0 + LANES], g_last, None)
            finish(o, hgate[:, c0:c0 + LANES], c0)

        for p in range(GL_HEADS // 2):
            c0 = p * LANES
            q = gq[:, c0:c0 + LANES] * (GL_DK ** -0.5)
            k = gk[:, c0:c0 + LANES]
            us, qd, kd, g_last = group_terms(q, k, HG_WIDTH + c0)
            for hh in range(2):
                h = 2 * p + hh
                m = half_mask[hh]
                o = head(HG_HEADS + h, us, qd, jnp.where(m, kd, 0.0), jnp.where(m, q * k, 0.0),
                         gv[:, h * LANES:(h + 1) * LANES], g_last, m)
                finish(o, ggate[:, h * LANES:(h + 1) * LANES], HG_WIDTH + h * LANES)
        return carry

    lax.fori_loop(0, TT // SUB, subtile, 0)

    hmix = jnp.dot(o_scr[...].astype(BF16), wout_ref[...], preferred_element_type=F32)
    x1 = _layer_norm(DEEPNORM_ALPHA * xt + hmix, ln1g_ref[...], ln1b_ref[...])
    for s in range(SLAB):
        x1s_ref[pl.ds(s, TT, stride=SLAB), :] = x1[:, s * LANES:(s + 1) * LANES]

    x1h = x1.astype(BF16)
    x1l = (x1 - x1h.astype(F32)).astype(BF16)
    logits = (jnp.dot(x1h, wrh_ref[...], preferred_element_type=F32)
              + jnp.dot(x1l, wrh_ref[...], preferred_element_type=F32)
              + jnp.dot(x1h, wrl_ref[...], preferred_element_type=F32))

    lane_i = lax.broadcasted_iota(jnp.int32, (TT, LANES), 1)
    lane_f = lane_i.astype(F32)
    neg = jnp.float32(-jnp.inf)
    big = jnp.float32(1e9)

    gl = jnp.where(lane_i < N_GROUPS, logits, neg)
    gmax = jnp.max(gl, axis=-1, keepdims=True)
    gidx = jnp.min(jnp.where(gl == gmax, lane_f, big), axis=-1, keepdims=True)
    p_group = 1.0 / jnp.sum(jnp.exp(gl - gmax), axis=-1, keepdims=True)

    e_lo = ROUTE_BASE + EXPERTS_PER_GROUP * gidx
    el = jnp.where((lane_f >= e_lo) & (lane_f < e_lo + EXPERTS_PER_GROUP), logits, neg)
    m1 = jnp.max(el, axis=-1, keepdims=True)
    i1 = jnp.min(jnp.where(el == m1, lane_f, big), axis=-1, keepdims=True)
    el2 = jnp.where(lane_f == i1, neg, el)
    m2 = jnp.max(el2, axis=-1, keepdims=True)
    i2 = jnp.min(jnp.where(el2 == m2, lane_f, big), axis=-1, keepdims=True)
    r2 = jnp.exp(m2 - m1)
    gate1 = p_group / (1.0 + r2)
    gate2 = p_group * r2 / (1.0 + r2)

    hot1 = lane_f == i1
    hot2 = lane_f == i2
    cnt = (hot1 | hot2).astype(F32)
    tr = lax.broadcasted_iota(jnp.int32, (TT, TT), 0)
    tc = lax.broadcasted_iota(jnp.int32, (TT, TT), 1)
    strict = (tc < tr).astype(BF16)
    base = carry_scr[...] + jnp.dot(strict, cnt.astype(BF16), preferred_element_type=F32)
    rank1 = jnp.sum(jnp.where(hot1, base, 0.0), axis=-1, keepdims=True)
    rank2 = jnp.sum(jnp.where(hot2, base, 0.0), axis=-1, keepdims=True)
    total = carry_scr[...] + jnp.sum(cnt, axis=0, keepdims=True)
    carry_scr[...] = total
    counts_ref[...] = total

    route = jnp.where(lane_i == 0, i1 - ROUTE_BASE, 0.0)
    route = jnp.where(lane_i == 1, i2 - ROUTE_BASE, route)
    route = jnp.where(lane_i == 2, gate1, route)
    route = jnp.where(lane_i == 3, gate2, route)
    route = jnp.where(lane_i == 4, rank1, route)
    route = jnp.where(lane_i == 5, rank2, route)
    route_ref[...] = route
    meta_ref[...] = route.T[0:META_ROWS, :].astype(jnp.int32)


def _dispatch_kernel(dest_ref, x1s_ref, xs_ref, zbuf, sems):
    i = pl.program_id(0)
    tile_rows = TT * SLAB

    @pl.when(i == 0)
    def _():
        zbuf[...] = jnp.zeros_like(zbuf)
        fill = pltpu.make_async_copy(zbuf, xs_ref.at[pl.ds(xs_ref.shape[0] - BLK * SLAB, BLK * SLAB), :], sems.at[1])
        fill.start()
        fill.wait()

    for r in range(TT):
        for k in range(TOP_K):
            dst = pl.multiple_of(dest_ref[0, k, r] * SLAB, SLAB)
            pltpu.make_async_copy(x1s_ref.at[pl.ds(r * SLAB, SLAB), :], xs_ref.at[pl.ds(dst, SLAB), :],
                                  sems.at[0]).start(priority=k)

    for k in range(TOP_K):
        pltpu.make_async_copy(x1s_ref, xs_ref.at[pl.ds(0, tile_rows), :], sems.at[0]).wait()


def _expert_kernel(be_ref, nv_ref, row0_ref, ord_ref, nblk_ref, xs_ref, wg_ref, wu_ref, wd_ref, yb_ref,
                   xbuf, wg_f, wu_f, wd_f, wg_s, wu_s, wd_s, sems, wsems):
    i = pl.program_id(0)
    nv = nv_ref[0]
    blk_rows = BLK * SLAB

    def block_copy(b, slot):
        r = pl.multiple_of(row0_ref[b] * SLAB, SLAB)
        return pltpu.make_async_copy(xs_ref.at[pl.ds(r, blk_rows), :],
                                     xbuf.at[pl.ds(slot * blk_rows, blk_rows), :], sems.at[slot])

    def weight_copies(e, ws):
        return [pltpu.make_async_copy(src.at[e], dst.at[ws], wsems.at[ws])
                for src, dst in ((wg_ref, wg_f), (wu_ref, wu_f), (wd_ref, wd_f))]

    @pl.when(i < nv)
    def _():
        slot = i % 2
        e = be_ref[i]
        ws = ord_ref[i] % 2

        @pl.when(i == 0)
        def _():
            block_copy(0, 0).start()
            for c in weight_copies(e, ws):
                c.start()

        @pl.when(i + 1 < nv)
        def _():
            block_copy(i + 1, 1 - slot).start()

        prev = be_ref[jnp.maximum(i - 1, 0)]

        @pl.when((i == 0) | (e != prev))
        def _():
            j = i + nblk_ref[e]

            @pl.when(j < nv)
            def _():
                for c in weight_copies(be_ref[jnp.minimum(j, nv - 1)], 1 - ws):
                    c.start()

            for c in weight_copies(e, ws):
                c.wait()
            wg_s[...] = wg_f[ws].astype(BF16)
            wu_s[...] = wu_f[ws].astype(BF16)
            wd_s[...] = wd_f[ws].astype(BF16)

        block_copy(i, slot).wait()
        base = pl.multiple_of(slot * blk_rows, blk_rows)
        xb = jnp.concatenate(
            [xbuf[pl.ds(base + s, BLK, stride=SLAB), :].astype(BF16) for s in range(SLAB)], axis=1)
        a = jnp.dot(xb, wg_s[...], preferred_element_type=F32)
        u = jnp.dot(xb, wu_s[...], preferred_element_type=F32)
        h = (a * jax.nn.sigmoid(a)) * u
        y = jnp.dot(h.astype(BF16), wd_s[...], preferred_element_type=F32)
        for s in range(SLAB):
            yb_ref[pl.ds(s, BLK, stride=SLAB), :] = y[:, s * LANES:(s + 1) * LANES]

    @pl.when(i >= nv)
    def _():
        yb_ref[...] = jnp.zeros_like(yb_ref)


def _combine_kernel(dest_ref, dest_next_ref, x1s_ref, route_ref, yb_ref, g_ref, b_ref, out_ref, ybuf, sems):
    i = pl.program_id(0)
    n_steps = pl.num_programs(0)
    half_rows = TD * SLAB
    par_rows = TOP_K * half_rows

    def start_gather(d_ref, par):
        for r in range(TD):
            for k in range(TOP_K):
                src = pl.multiple_of(d_ref[0, k, r] * SLAB, SLAB)
                dst = pl.multiple_of(par * par_rows + (k * TD + r) * SLAB, SLAB)
                pltpu.make_async_copy(yb_ref.at[pl.ds(src, SLAB), :], ybuf.at[pl.ds(dst, SLAB), :],
                                      sems.at[par]).start(priority=k)

    def wait_gather(par):
        pltpu.make_async_copy(yb_ref.at[pl.ds(0, par_rows), :],
                              ybuf.at[pl.ds(par * par_rows, par_rows), :], sems.at[par]).wait()

    par = i % 2

    @pl.when(i == 0)
    def _():
        start_gather(dest_ref, 0)

    wait_gather(par)
    start_gather(dest_next_ref, 1 - par)

    base = pl.multiple_of(par * par_rows, par_rows)
    g1 = route_ref[:, 2:3]
    g2 = route_ref[:, 3:4]
    zs = []
    for s in range(SLAB):
        y1 = ybuf[pl.ds(base + s, TD, stride=SLAB), :]
        y2 = ybuf[pl.ds(base + half_rows + s, TD, stride=SLAB), :]
        zs.append(DEEPNORM_ALPHA * x1s_ref[pl.ds(s, TD, stride=SLAB), :] + (g1 * y1 + g2 * y2))
    z = jnp.concatenate(zs, axis=1)
    out_ref[...] = _layer_norm(z, g_ref[...], b_ref[...])

    @pl.when(i + 1 >= n_steps)
    def _():
        wait_gather(1 - par)


def _const_spec(shape):
    nd = len(shape)
    return pl.BlockSpec(shape, lambda *_: (0,) * nd, pipeline_mode=pl.Buffered(1))


def kernel(x, w_in, w_a2, b_a, lb_logits, norm_h, norm_g, w_out, ln1_g, ln1_b, w_group_router, w_expert_router,
           w_gate, w_up, w_down, ln2_g, ln2_b):
    B, T, D = x.shape
    n_tok = B * T
    assert D == D_MODEL == SLAB * LANES and T % TT == 0 and TD == TT
    n_blocks = n_tok * TOP_K // BLK + N_EXPERTS
    n_slots = n_blocks * BLK

    parts = jnp.split(w_in[0], [int(c) for c in np.cumsum(IN_SPLITS)[:-1]], axis=-1)
    hq, hf, hi, hg, gq, gk, gv, ga, gg = parts
    win = jnp.concatenate([hq, hf, hi, hg, gq, gk, gv, gg, ga], axis=-1)
    win = jnp.pad(win, ((0, 0), (0, IN_COLS_PAD - win.shape[1]))).astype(BF16)
    wa2 = jnp.pad(w_a2[0], ((0, LANES - GATE_RANK), (0, 0))).astype(BF16)
    ba = b_a[0].reshape(1, GL_QK).astype(F32)
    gain = jnp.concatenate([norm_h[0], norm_g[0]]).reshape(1, D).astype(F32)
    wout = w_out[0].astype(BF16)
    wr = jnp.concatenate([w_group_router[0], w_expert_router[0]], axis=-1).astype(F32)
    wr = jnp.pad(wr, ((0, 0), (0, LANES - wr.shape[1])))
    wrh = wr.astype(BF16)
    wrl = (wr - wrh.astype(F32)).astype(BF16)
    wstk = jnp.asarray(_decay_matrices(), dtype=BF16)
    lev = jnp.asarray(_level_map())
    n_w = wstk.shape[0]

    n_steps = n_tok // TT
    x1s, route, meta, counts = pl.pallas_call(
        _mixer_kernel,
        grid=(B, T // TT),
        in_specs=[
            pl.BlockSpec((None, TT, D), lambda b, j: (b, j, 0)),
            _const_spec((D, IN_COLS_PAD)),
            _const_spec((LANES, GL_QK)),
            _const_spec((1, GL_QK)),
            _const_spec((DEPTH + 1, HG_WIDTH)),
            _const_spec((1, D)),
            _const_spec((D, D)),
            _const_spec((1, D)),
            _const_spec((1, D)),
            _const_spec((D, LANES)),
            _const_spec((D, LANES)),
            _const_spec((n_w, 3 * SUB)),
            _const_spec((SUB, SUB)),
        ],
        out_specs=[
            pl.BlockSpec((TT * SLAB, LANES), lambda b, j: (b * (T // TT) + j, 0)),
            pl.BlockSpec((TT, LANES), lambda b, j: (b * (T // TT) + j, 0)),
            pl.BlockSpec((META_ROWS, TT), lambda b, j: (b * (T // TT) + j, 0)),
            pl.BlockSpec((1, LANES), lambda b, j: (0, 0)),
        ],
        out_shape=[
            jax.ShapeDtypeStruct((n_tok * SLAB, LANES), F32),
            jax.ShapeDtypeStruct((n_tok, LANES), F32),
            jax.ShapeDtypeStruct((n_steps * META_ROWS, TT), jnp.int32),
            jax.ShapeDtypeStruct((1, LANES), F32),
        ],
        scratch_shapes=[
            pltpu.VMEM((TT, IN_COLS_PAD), F32),
            pltpu.VMEM((TT, D), F32),
            pltpu.VMEM((SUB, HG_WIDTH + GL_QK), F32),
            pltpu.VMEM((HG_HEADS + GL_HEADS, LANES, LANES), F32),
            pltpu.VMEM((1, LANES), F32),
        ],
        compiler_params=pltpu.CompilerParams(
            dimension_semantics=("arbitrary", "arbitrary"), vmem_limit_bytes=VMEM_LIMIT),
        name="mixer",
    )(x, win, wa2, ba, lb_logits.astype(F32), gain, wout, ln1_g[0].reshape(1, D), ln1_b[0].reshape(1, D),
      wrh, wrl, wstk, lev)

    cnt = counts[0, ROUTE_BASE:ROUTE_BASE + N_EXPERTS].astype(jnp.int32)
    padded = (cnt + BLK - 1) // BLK * BLK
    ends = jnp.cumsum(padded)
    starts = ends - padded
    n_valid = (ends[-1] // BLK).astype(jnp.int32)
    blk_ids = jnp.minimum(jnp.arange(n_blocks, dtype=jnp.int32), n_valid - 1)
    block_expert = jnp.sum((ends[None, :] <= (blk_ids * BLK)[:, None]).astype(jnp.int32), axis=1)
    block_expert = jnp.minimum(block_expert, N_EXPERTS - 1)
    blk_ord = jnp.cumsum(jnp.concatenate(
        [jnp.zeros((1,), jnp.int32), (block_expert[1:] != block_expert[:-1]).astype(jnp.int32)]))
    dstarts = jnp.cumsum(cnt) - cnt
    first_blk = starts // BLK
    pick = block_expert[None, :] == jnp.arange(N_EXPERTS, dtype=jnp.int32)[:, None]
    row0 = jnp.sum(jnp.where(pick, (dstarts - first_blk * BLK)[:, None], 0), axis=0) + blk_ids * BLK
    meta3 = meta.reshape(n_steps, META_ROWS, TT)
    hit = meta3[None, :, 0:TOP_K, :] == jnp.arange(N_EXPERTS, dtype=jnp.int32)[:, None, None, None]
    rank = meta3[:, 4:4 + TOP_K, :]
    dest_x = rank + jnp.sum(jnp.where(hit, dstarts[:, None, None, None], 0), axis=0)
    dest = rank + jnp.sum(jnp.where(hit, starts[:, None, None, None], 0), axis=0)

    xs_rows = (n_tok * TOP_K + BLK) * SLAB
    xs = pl.pallas_call(
        _dispatch_kernel,
        grid=(n_steps,),
        in_specs=[
            pl.BlockSpec((1, TOP_K, TT), lambda i: (i, 0, 0), memory_space=pltpu.SMEM),
            pl.BlockSpec((TT * SLAB, LANES), lambda i: (i, 0)),
        ],
        out_specs=pl.BlockSpec(memory_space=pl.ANY),
        out_shape=jax.ShapeDtypeStruct((xs_rows, LANES), F32),
        scratch_shapes=[pltpu.VMEM((BLK * SLAB, LANES), F32), pltpu.SemaphoreType.DMA((2,))],
        compiler_params=pltpu.CompilerParams(dimension_semantics=("arbitrary",)),
        name="dispatch",
    )(dest_x, x1s)

    yb = pl.pallas_call(
        _expert_kernel,
        grid_spec=pltpu.PrefetchScalarGridSpec(
            num_scalar_prefetch=5,
            grid=(n_blocks,),
            in_specs=[pl.BlockSpec(memory_space=pl.ANY)] * 4,
            out_specs=pl.BlockSpec((BLK * SLAB, LANES), lambda i, *_: (i, 0)),
            scratch_shapes=[
                pltpu.VMEM((2 * BLK * SLAB, LANES), F32),
                pltpu.VMEM((2, D, D_EXPERT), F32),
                pltpu.VMEM((2, D, D_EXPERT), F32),
                pltpu.VMEM((2, D_EXPERT, D), F32),
                pltpu.VMEM((D, D_EXPERT), BF16),
                pltpu.VMEM((D, D_EXPERT), BF16),
                pltpu.VMEM((D_EXPERT, D), BF16),
                pltpu.SemaphoreType.DMA((2,)),
                pltpu.SemaphoreType.DMA((2,)),
            ],
        ),
        out_shape=jax.ShapeDtypeStruct((n_slots * SLAB, LANES), F32),
        compiler_params=pltpu.CompilerParams(
            dimension_semantics=("arbitrary",), vmem_limit_bytes=VMEM_LIMIT),
        name="experts",
    )(block_expert, n_valid.reshape(1), row0, blk_ord, padded // BLK, xs, w_gate[0], w_up[0], w_down[0])

    last = n_tok // TD - 1
    out = pl.pallas_call(
        _combine_kernel,
        grid=(n_tok // TD,),
        in_specs=[
            pl.BlockSpec((1, TOP_K, TD), lambda i: (i, 0, 0), memory_space=pltpu.SMEM),
            pl.BlockSpec((1, TOP_K, TD), lambda i: (jnp.minimum(i + 1, last), 0, 0), memory_space=pltpu.SMEM),
            pl.BlockSpec((TD * SLAB, LANES), lambda i: (i, 0)),
            pl.BlockSpec((TD, LANES), lambda i: (i, 0)),
            pl.BlockSpec(memory_space=pl.ANY),
            pl.BlockSpec((1, D), lambda i: (0, 0)),
            pl.BlockSpec((1, D), lambda i: (0, 0)),
        ],
        out_specs=pl.BlockSpec((TD, D), lambda i: (i, 0)),
        scratch_shapes=[pltpu.VMEM((2 * TOP_K * TD * SLAB, LANES), F32), pltpu.SemaphoreType.DMA((2,))],
        out_shape=jax.ShapeDtypeStruct((n_tok, D), F32),
        compiler_params=pltpu.CompilerParams(dimension_semantics=("arbitrary",)),
        name="combine",
    )(dest, dest, x1s, route, yb, ln2_g[0].reshape(1, D), ln2_b[0].reshape(1, D))

    return out.reshape(B, T, D).astype(x.dtype)
```

```python
import functools

import numpy as np
import jax
import jax.numpy as jnp
from jax import lax
from jax.experimental import pallas as pl
from jax.experimental.pallas import tpu as pltpu

D_MODEL = 1024
HG_WIDTH = 512
HG_HEAD_DIM = 128
HG_HEADS = 4
GL_WIDTH = 512
GL_HEADS = 4
GL_DV = 128
GL_DK = 64
GL_QK = 256
GATE_RANK = 16
GATE_TAU = 16.0
N_GROUPS = 8
EXPERTS_PER_GROUP = 8
N_EXPERTS = 64
TOP_K = 2
D_EXPERT = 512
DEPTH = 1
DEEPNORM_ALPHA = (2.0 * DEPTH) ** 0.25
LN_EPS = 1e-5
LOG2E = 1.4426950408889634
IN_SPLITS = (HG_WIDTH, HG_WIDTH, HG_WIDTH, HG_WIDTH, GL_QK, GL_QK, GL_WIDTH, GATE_RANK, GL_WIDTH)

LANES = 128
SUB = 128
N_LEVELS = 7
MXU_LEVELS = 3
TT = 256
TD = 256
BLK = 256
PROJ_COLS = 3712
ROUTE_BASE = 8
SLAB = 8
META_ROWS = 8
VMEM_LIMIT = 56 * 1024 * 1024

F32 = jnp.float32
BF16 = jnp.bfloat16
NT_DIMS = (((1,), (1,)), ((), ()))
TN_DIMS = (((0,), (0,)), ((), ()))


def _decay_matrices():
    t = np.arange(SUB)[:, None]
    j = np.arange(SUB)[None, :]
    mats = [(j <= t)]
    for lvl in range(1, MXU_LEVELS + 1):
        blk = 1 << lvl
        half = blk >> 1
        r = (t // blk) * blk + half - 1
        right = (t % blk) >= half
        mats.append(np.where(right, (j > r) & (j <= t), (j > t) & (j <= r)))
    w = np.concatenate(mats, axis=0).astype(np.float32)
    return np.concatenate([w, w, w], axis=1)


def _level_map():
    t = np.arange(SUB)[:, None]
    s = np.arange(SUB)[None, :]
    x = np.bitwise_xor(t, s)
    lvl = np.floor(np.log2(np.maximum(x, 1))).astype(np.int32) + 1
    lvl = np.where(x == 0, 0, lvl)
    return np.where(s > t, -1, lvl).astype(np.int32)


def _layer_norm(y, g, b):
    mu = jnp.mean(y, axis=-1, keepdims=True)
    yc = y - mu
    var = jnp.mean(yc * yc, axis=-1, keepdims=True)
    return yc * lax.rsqrt(var + LN_EPS) * g + b


def _mixer_kernel(x_ref, win_ref, wa2_ref, ba_ref, lbl_ref, gain_ref, wout_ref, ln1g_ref, ln1b_ref,
                  wrh_ref, wrl_ref, wstk_ref, lev_ref,
                  x1s_ref, route_ref, meta_ref, counts_ref,
                  p_scr, o_scr, b_scr, st_scr, carry_scr):
    bi = pl.program_id(0)
    ji = pl.program_id(1)

    @pl.when(ji == 0)
    def _():
        st_scr[...] = jnp.zeros_like(st_scr)

    @pl.when((bi == 0) & (ji == 0))
    def _():
        carry_scr[...] = jnp.zeros_like(carry_scr)

    xt = x_ref[...]
    p_scr[...] = jnp.dot(xt.astype(BF16), win_ref[...], preferred_element_type=F32)

    lbl = lbl_ref[...]
    lbe = jnp.exp(lbl - jnp.max(lbl, axis=0, keepdims=True))
    lb = lbe[0:1, :] / jnp.sum(lbe, axis=0, keepdims=True)

    lev = lev_ref[...]
    tcol = lax.broadcasted_iota(jnp.int32, (SUB, 1), 0)
    qrow = [(tcol & ((1 << l) - 1)) >= (1 << (l - 1)) for l in range(1, N_LEVELS + 1)]
    lane = lax.broadcasted_iota(jnp.int32, (1, LANES), 1)
    half_mask = [(lane < GL_DK), (lane >= GL_DK)]

    def by_halves(l, left_fn, right_fn):
        blk = 1 << l
        half = blk >> 1
        parts = []
        for j in range(SUB // blk):
            lo = j * blk
            parts += [left_fn(lo, lo + half), right_fn(lo + half, lo + blk)]
        return jnp.concatenate(parts, axis=0)

    def subtile(s, carry):
        r0 = pl.multiple_of(s * SUB, SUB)
        rows = pl.ds(r0, SUB)
        hq = p_scr[rows, 0:512]
        hf = p_scr[rows, 512:1024]
        hv = p_scr[rows, 1024:1536]
        hgate = p_scr[rows, 1536:2048]
        gq = p_scr[rows, 2048:2304]
        gk = p_scr[rows, 2304:2560]
        gv = p_scr[rows, 2560:3072]
        ggate = p_scr[rows, 3072:3584]
        ga = p_scr[rows, 3584:3712]

        f = lb + (1.0 - lb) * jax.nn.sigmoid(hf)
        kh = 1.0 - f
        lgh = jnp.log(f)
        z = jnp.dot(ga.astype(BF16), wa2_ref[...], preferred_element_type=F32) + ba_ref[...]
        lga = (jnp.minimum(z, 0.0) - jnp.log1p(jnp.exp(-jnp.abs(z)))) * (1.0 / GATE_TAU)
        lg = jnp.concatenate([lgh, lga], axis=1) * LOG2E

        l_hi = lg.astype(BF16)
        rem = lg - l_hi.astype(F32)
        l_mid = rem.astype(BF16)
        l_lo = (rem - l_mid.astype(F32)).astype(BF16)
        sums = jnp.dot(wstk_ref[...], jnp.concatenate([l_hi, l_mid, l_lo], axis=0),
                       preferred_element_type=F32)
        bcum = sums[0:SUB]
        b_scr[...] = bcum
        expos = [bcum, b_scr[SUB - 1:SUB, :] - bcum]
        for l in range(1, N_LEVELS + 1):
            if l <= MXU_LEVELS:
                expos.append(sums[l * SUB:(l + 1) * SUB])
            else:
                expos.append(by_halves(l, lambda lo, mid: b_scr[mid - 1:mid, :] - bcum[lo:mid],
                                       lambda mid, hi: bcum[mid:hi] - b_scr[mid - 1:mid, :]))
        decs = [jnp.exp2(e) for e in expos]

        def group_terms(q, k, c0):
            e_pre = decs[0][:, c0:c0 + LANES]
            e_suf = decs[1][:, c0:c0 + LANES]
            us = []
            for l in range(1, N_LEVELS + 1):
                e_l = decs[1 + l][:, c0:c0 + LANES]
                if l <= MXU_LEVELS:
                    sel = jnp.where(qrow[l - 1], q, k)
                else:
                    sel = by_halves(l, lambda lo, mid: k[lo:mid], lambda mid, hi: q[mid:hi])
                u = sel * e_l
                us.append((u, u.astype(BF16)))
            return us, (q * e_pre).astype(BF16), k * e_suf, e_pre[SUB - 1:SUB, :]

        def head(hidx, us, qd, kd, qk, v, g_last, mask):
            sc = jnp.where(lev == 0, jnp.sum(qk, axis=-1, keepdims=True), 0.0)
            for l in range(1, N_LEVELS + 1):
                u, ub = us[l - 1]
                if l <= MXU_LEVELS:
                    lhs = ub if mask is None else jnp.where(mask, ub, jnp.zeros_like(ub))
                    g = lax.dot_general(lhs, ub, NT_DIMS, preferred_element_type=F32)
                    sc = jnp.where(lev == l, g, sc)
                else:
                    blk = 1 << l
                    half = blk >> 1
                    mids = [j * blk + half for j in range(SUB // blk)]
                    uq = jnp.concatenate([u[m:m + half] for m in mids], axis=0)
                    if mask is not None:
                        uq = jnp.where(mask, uq, 0.0)
                    g = lax.dot_general(uq.astype(BF16), ub, NT_DIMS, preferred_element_type=F32)
                    parts = []
                    for n, m in enumerate(mids):
                        parts.append(sc[m - half:m])
                        parts.append(jnp.where(lev[m:m + half] == l, g[n * half:(n + 1) * half], sc[m:m + half]))
                    sc = jnp.concatenate(parts, axis=0)
            vb = v.astype(BF16)
            st = st_scr[hidx]
            o = jnp.dot(sc.astype(BF16), vb, preferred_element_type=F32)
            o = o + lax.dot_general(qd, st.astype(BF16), NT_DIMS, preferred_element_type=F32)
            ut = lax.dot_general(vb, kd.astype(BF16), TN_DIMS, preferred_element_type=F32)
            st_scr[hidx] = st * g_last + ut
            return o

        def finish(o, gate, c0):
            o = o * lax.rsqrt(jnp.mean(o * o, axis=-1, keepdims=True) + LN_EPS)
            o = o * gain_ref[:, c0:c0 + LANES] * (gate * jax.nn.sigmoid(gate))
            o_scr[rows, c0:c0 + LANES] = o

        for g in range(HG_HEADS):
            c0 = g * LANES
            q = hq[:, c0:c0 + LANES] * (HG_HEAD_DIM ** -0.5)
            k = kh[:, c0:c0 + LANES]
            us, qd, kd, g_last = group_terms(q, k, c0)
            o = head(g, us, qd, kd, q * k, hv[:, c0:c0 + LANES], g_last, None)
            finish(o, hgate[:, c0:c0 + LANES], c0)

        for p in range(GL_HEADS // 2):
            c0 = p * LANES
            q = gq[:, c0:c0 + LANES] * (GL_DK ** -0.5)
            k = gk[:, c0:c0 + LANES]
            us, qd, kd, g_last = group_terms(q, k, HG_WIDTH + c0)
            for hh in range(2):
                h = 2 * p + hh
                m = half_mask[hh]
                o = head(HG_HEADS + h, us, qd, jnp.where(m, kd, 0.0), jnp.where(m, q * k, 0.0),
                         gv[:, h * LANES:(h + 1) * LANES], g_last, m)
                finish(o, ggate[:, h * LANES:(h + 1) * LANES], HG_WIDTH + h * LANES)
        return carry

    lax.fori_loop(0, TT // SUB, subtile, 0)

    hmix = jnp.dot(o_scr[...].astype(BF16), wout_ref[...], preferred_element_type=F32)
    x1 = _layer_norm(DEEPNORM_ALPHA * xt + hmix, ln1g_ref[...], ln1b_ref[...])
    for s in range(SLAB):
        x1s_ref[pl.ds(s, TT, stride=SLAB), :] = x1[:, s * LANES:(s + 1) * LANES]

    x1h = x1.astype(BF16)
    x1l = (x1 - x1h.astype(F32)).astype(BF16)
    logits = (jnp.dot(x1h, wrh_ref[...], preferred_element_type=F32)
              + jnp.dot(x1l, wrh_ref[...], preferred_element_type=F32)
              + jnp.dot(x1h, wrl_ref[...], preferred_element_type=F32))

    lane_i = lax.broadcasted_iota(jnp.int32, (TT, LANES), 1)
    lane_f = lane_i.astype(F32)
    neg = jnp.float32(-jnp.inf)
    big = jnp.float32(1e9)

    gl = jnp.where(lane_i < N_GROUPS, logits, neg)
    gmax = jnp.max(gl, axis=-1, keepdims=True)
    gidx = jnp.min(jnp.where(gl == gmax, lane_f, big), axis=-1, keepdims=True)
    p_group = 1.0 / jnp.sum(jnp.exp(gl - gmax), axis=-1, keepdims=True)

    e_lo = ROUTE_BASE + EXPERTS_PER_GROUP * gidx
    el = jnp.where((lane_f >= e_lo) & (lane_f < e_lo + EXPERTS_PER_GROUP), logits, neg)
    m1 = jnp.max(el, axis=-1, keepdims=True)
    i1 = jnp.min(jnp.where(el == m1, lane_f, big), axis=-1, keepdims=True)
    el2 = jnp.where(lane_f == i1, neg, el)
    m2 = jnp.max(el2, axis=-1, keepdims=True)
    i2 = jnp.min(jnp.where(el2 == m2, lane_f, big), axis=-1, keepdims=True)
    r2 = jnp.exp(m2 - m1)
    gate1 = p_group / (1.0 + r2)
    gate2 = p_group * r2 / (1.0 + r2)

    hot1 = lane_f == i1
    hot2 = lane_f == i2
    cnt = (hot1 | hot2).astype(F32)
    tr = lax.broadcasted_iota(jnp.int32, (TT, TT), 0)
    tc = lax.broadcasted_iota(jnp.int32, (TT, TT), 1)
    strict = (tc < tr).astype(BF16)
    base = carry_scr[...] + jnp.dot(strict, cnt.astype(BF16), preferred_element_type=F32)
    rank1 = jnp.sum(jnp.where(hot1, base, 0.0), axis=-1, keepdims=True)
    rank2 = jnp.sum(jnp.where(hot2, base, 0.0), axis=-1, keepdims=True)
    total = carry_scr[...] + jnp.sum(cnt, axis=0, keepdims=True)
    carry_scr[...] = total
    counts_ref[...] = total

    route = jnp.where(lane_i == 0, i1 - ROUTE_BASE, 0.0)
    route = jnp.where(lane_i == 1, i2 - ROUTE_BASE, route)
    route = jnp.where(lane_i == 2, gate1, route)
    route = jnp.where(lane_i == 3, gate2, route)
    route = jnp.where(lane_i == 4, rank1, route)
    route = jnp.where(lane_i == 5, rank2, route)
    route_ref[...] = route
    meta_ref[...] = route.T[0:META_ROWS, :].astype(jnp.int32)


def _dispatch_kernel(dest_ref, x1s_ref, xs_ref, zbuf, sems):
    i = pl.program_id(0)
    tile_rows = TT * SLAB

    @pl.when(i == 0)
    def _():
        zbuf[...] = jnp.zeros_like(zbuf)
        fill = pltpu.make_async_copy(zbuf, xs_ref.at[pl.ds(xs_ref.shape[0] - BLK * SLAB, BLK * SLAB), :], sems.at[1])
        fill.start()
        fill.wait()

    for r in range(TT):
        for k in range(TOP_K):
            dst = pl.multiple_of(dest_ref[0, k, r] * SLAB, SLAB)
            pltpu.make_async_copy(x1s_ref.at[pl.ds(r * SLAB, SLAB), :], xs_ref.at[pl.ds(dst, SLAB), :],
                                  sems.at[0]).start(priority=k)

    for k in range(TOP_K):
        pltpu.make_async_copy(x1s_ref, xs_ref.at[pl.ds(0, tile_rows), :], sems.at[0]).wait()


def _expert_kernel(be_ref, nv_ref, row0_ref, ord_ref, nblk_ref, xs_ref, wg_ref, wu_ref, wd_ref, yb_ref,
                   xbuf, wg_f, wu_f, wd_f, wg_s, wu_s, wd_s, sems, wsems):
    i = pl.program_id(0)
    nv = nv_ref[0]
    blk_rows = BLK * SLAB

    def block_copy(b, slot):
        r = pl.multiple_of(row0_ref[b] * SLAB, SLAB)
        return pltpu.make_async_copy(xs_ref.at[pl.ds(r, blk_rows), :],
                                     xbuf.at[pl.ds(slot * blk_rows, blk_rows), :], sems.at[slot])

    def weight_copies(e, ws):
        return [pltpu.make_async_copy(src.at[e], dst.at[ws], wsems.at[ws])
                for src, dst in ((wg_ref, wg_f), (wu_ref, wu_f), (wd_ref, wd_f))]

    @pl.when(i < nv)
    def _():
        slot = i % 2
        e = be_ref[i]
        ws = ord_ref[i] % 2

        @pl.when(i == 0)
        def _():
            block_copy(0, 0).start()
            for c in weight_copies(e, ws):
                c.start()

        @pl.when(i + 1 < nv)
        def _():
            block_copy(i + 1, 1 - slot).start()

        prev = be_ref[jnp.maximum(i - 1, 0)]

        @pl.when((i == 0) | (e != prev))
        def _():
            j = i + nblk_ref[e]

            @pl.when(j < nv)
            def _():
                for c in weight_copies(be_ref[jnp.minimum(j, nv - 1)], 1 - ws):
                    c.start()

            for c in weight_copies(e, ws):
                c.wait()
            wg_s[...] = wg_f[ws].astype(BF16)
            wu_s[...] = wu_f[ws].astype(BF16)
            wd_s[...] = wd_f[ws].astype(BF16)

        block_copy(i, slot).wait()
        base = pl.multiple_of(slot * blk_rows, blk_rows)
        xb = jnp.concatenate(
            [xbuf[pl.ds(base + s, BLK, stride=SLAB), :].astype(BF16) for s in range(SLAB)], axis=1)
        a = jnp.dot(xb, wg_s[...], preferred_element_type=F32)
        u = jnp.dot(xb, wu_s[...], preferred_element_type=F32)
        h = (a * jax.nn.sigmoid(a)) * u
        y = jnp.dot(h.astype(BF16), wd_s[...], preferred_element_type=F32)
        for s in range(SLAB):
            yb_ref[pl.ds(s, BLK, stride=SLAB), :] = y[:, s * LANES:(s + 1) * LANES]

    @pl.when(i >= nv)
    def _():
        yb_ref[...] = jnp.zeros_like(yb_ref)


def _combine_kernel(dest_ref, dest_next_ref, x1s_ref, route_ref, yb_ref, g_ref, b_ref, out_ref,
                    ybuf_a, ybuf_b, sems):
    i = pl.program_id(0)
    n_steps = pl.num_programs(0)
    half_rows = TD * SLAB

    def start_gather(d_ref, buf, sem):
        for r in range(TD):
            for k in range(TOP_K):
                src = pl.multiple_of(d_ref[0, k, r] * SLAB, SLAB)
                pltpu.make_async_copy(yb_ref.at[pl.ds(src, SLAB), :], buf.at[pl.ds((k * TD + r) * SLAB, SLAB), :],
                                      sem).start(priority=k)

    def wait_gather(buf, sem):
        pltpu.make_async_copy(yb_ref.at[pl.ds(0, TOP_K * half_rows), :], buf, sem).wait()

    def step(cur, nxt, sem_cur, sem_nxt):
        @pl.when(i == 0)
        def _():
            start_gather(dest_ref, cur, sem_cur)

        wait_gather(cur, sem_cur)
        start_gather(dest_next_ref, nxt, sem_nxt)

        g1 = route_ref[:, 2:3]
        g2 = route_ref[:, 3:4]
        zs = []
        for s in range(SLAB):
            y1 = cur[pl.ds(s, TD, stride=SLAB), :]
            y2 = cur[pl.ds(half_rows + s, TD, stride=SLAB), :]
            zs.append(DEEPNORM_ALPHA * x1s_ref[pl.ds(s, TD, stride=SLAB), :] + (g1 * y1 + g2 * y2))
        z = jnp.concatenate(zs, axis=1)
        out_ref[...] = _layer_norm(z, g_ref[...], b_ref[...])

        @pl.when(i + 1 >= n_steps)
        def _():
            wait_gather(nxt, sem_nxt)

    @pl.when(i % 2 == 0)
    def _():
        step(ybuf_a, ybuf_b, sems.at[0], sems.at[1])

    @pl.when(i % 2 == 1)
    def _():
        step(ybuf_b, ybuf_a, sems.at[1], sems.at[0])


def _const_spec(shape):
    nd = len(shape)
    return pl.BlockSpec(shape, lambda *_: (0,) * nd, pipeline_mode=pl.Buffered(1))


def kernel(x, w_in, w_a2, b_a, lb_logits, norm_h, norm_g, w_out, ln1_g, ln1_b, w_group_router, w_expert_router,
           w_gate, w_up, w_down, ln2_g, ln2_b):
    B, T, D = x.shape
    n_tok = B * T
    assert D == D_MODEL == SLAB * LANES and T % TT == 0 and TD == TT
    n_blocks = n_tok * TOP_K // BLK + N_EXPERTS
    n_slots = n_blocks * BLK

    parts = jnp.split(w_in[0], [int(c) for c in np.cumsum(IN_SPLITS)[:-1]], axis=-1)
    hq, hf, hi, hg, gq, gk, gv, ga, gg = parts
    win = jnp.concatenate([hq, hf, hi, hg, gq, gk, gv, gg, ga], axis=-1)
    win = jnp.pad(win, ((0, 0), (0, PROJ_COLS - win.shape[1]))).astype(BF16)
    wa2 = jnp.pad(w_a2[0], ((0, LANES - GATE_RANK), (0, 0))).astype(BF16)
    ba = b_a[0].reshape(1, GL_QK).astype(F32)
    gain = jnp.concatenate([norm_h[0], norm_g[0]]).reshape(1, D).astype(F32)
    wout = w_out[0].astype(BF16)
    wr = jnp.concatenate([w_group_router[0], w_expert_router[0]], axis=-1).astype(F32)
    wr = jnp.pad(wr, ((0, 0), (0, LANES - wr.shape[1])))
    wrh = wr.astype(BF16)
    wrl = (wr - wrh.astype(F32)).astype(BF16)
    wstk = jnp.asarray(_decay_matrices(), dtype=BF16)
    lev = jnp.asarray(_level_map())
    n_w = wstk.shape[0]

    n_steps = n_tok // TT
    x1s, route, meta, counts = pl.pallas_call(
        _mixer_kernel,
        grid=(B, T // TT),
        in_specs=[
            pl.BlockSpec((None, TT, D), lambda b, j: (b, j, 0)),
            _const_spec((D, PROJ_COLS)),
            _const_spec((LANES, GL_QK)),
            _const_spec((1, GL_QK)),
            _const_spec((DEPTH + 1, HG_WIDTH)),
            _const_spec((1, D)),
            _const_spec((D, D)),
            _const_spec((1, D)),
            _const_spec((1, D)),
            _const_spec((D, LANES)),
            _const_spec((D, LANES)),
            _const_spec((n_w, 3 * SUB)),
            _const_spec((SUB, SUB)),
        ],
        out_specs=[
            pl.BlockSpec((TT * SLAB, LANES), lambda b, j: (b * (T // TT) + j, 0)),
            pl.BlockSpec((TT, LANES), lambda b, j: (b * (T // TT) + j, 0)),
            pl.BlockSpec((META_ROWS, TT), lambda b, j: (b * (T // TT) + j, 0)),
            pl.BlockSpec((1, LANES), lambda b, j: (0, 0)),
        ],
        out_shape=[
            jax.ShapeDtypeStruct((n_tok * SLAB, LANES), F32),
            jax.ShapeDtypeStruct((n_tok, LANES), F32),
            jax.ShapeDtypeStruct((n_steps * META_ROWS, TT), jnp.int32),
            jax.ShapeDtypeStruct((1, LANES), F32),
        ],
        scratch_shapes=[
            pltpu.VMEM((TT, PROJ_COLS), F32),
            pltpu.VMEM((TT, D), F32),
            pltpu.VMEM((SUB, HG_WIDTH + GL_QK), F32),
            pltpu.VMEM((HG_HEADS + GL_HEADS, LANES, LANES), F32),
            pltpu.VMEM((1, LANES), F32),
        ],
        compiler_params=pltpu.CompilerParams(
            dimension_semantics=("arbitrary", "arbitrary"), vmem_limit_bytes=VMEM_LIMIT),
        name="mixer",
    )(x, win, wa2, ba, lb_logits.astype(F32), gain, wout, ln1_g[0].reshape(1, D), ln1_b[0].reshape(1, D),
      wrh, wrl, wstk, lev)

    cnt = counts[0, ROUTE_BASE:ROUTE_BASE + N_EXPERTS].astype(jnp.int32)
    padded = (cnt + BLK - 1) // BLK * BLK
    ends = jnp.cumsum(padded)
    starts = ends - padded
    n_valid = (ends[-1] // BLK).astype(jnp.int32)
    blk_ids = jnp.minimum(jnp.arange(n_blocks, dtype=jnp.int32), n_valid - 1)
    block_expert = jnp.sum((ends[None, :] <= (blk_ids * BLK)[:, None]).astype(jnp.int32), axis=1)
    block_expert = jnp.minimum(block_expert, N_EXPERTS - 1)
    blk_ord = jnp.cumsum(jnp.concatenate(
        [jnp.zeros((1,), jnp.int32), (block_expert[1:] != block_expert[:-1]).astype(jnp.int32)]))
    dstarts = jnp.cumsum(cnt) - cnt
    first_blk = starts // BLK
    pick = block_expert[None, :] == jnp.arange(N_EXPERTS, dtype=jnp.int32)[:, None]
    row0 = jnp.sum(jnp.where(pick, (dstarts - first_blk * BLK)[:, None], 0), axis=0) + blk_ids * BLK
    meta3 = meta.reshape(n_steps, META_ROWS, TT)
    hit = meta3[None, :, 0:TOP_K, :] == jnp.arange(N_EXPERTS, dtype=jnp.int32)[:, None, None, None]
    rank = meta3[:, 4:4 + TOP_K, :]
    dest_x = rank + jnp.sum(jnp.where(hit, dstarts[:, None, None, None], 0), axis=0)
    dest = rank + jnp.sum(jnp.where(hit, starts[:, None, None, None], 0), axis=0)

    xs_rows = (n_tok * TOP_K + BLK) * SLAB
    xs = pl.pallas_call(
        _dispatch_kernel,
        grid=(n_steps,),
        in_specs=[
            pl.BlockSpec((1, TOP_K, TT), lambda i: (i, 0, 0), memory_space=pltpu.SMEM),
            pl.BlockSpec((TT * SLAB, LANES), lambda i: (i, 0)),
        ],
        out_specs=pl.BlockSpec(memory_space=pl.ANY),
        out_shape=jax.ShapeDtypeStruct((xs_rows, LANES), F32),
        scratch_shapes=[pltpu.VMEM((BLK * SLAB, LANES), F32), pltpu.SemaphoreType.DMA((2,))],
        compiler_params=pltpu.CompilerParams(dimension_semantics=("arbitrary",)),
        name="dispatch",
    )(dest_x, x1s)

    yb = pl.pallas_call(
        _expert_kernel,
        grid_spec=pltpu.PrefetchScalarGridSpec(
            num_scalar_prefetch=5,
            grid=(n_blocks,),
            in_specs=[pl.BlockSpec(memory_space=pl.ANY)] * 4,
            out_specs=pl.BlockSpec((BLK * SLAB, LANES), lambda i, *_: (i, 0)),
            scratch_shapes=[
                pltpu.VMEM((2 * BLK * SLAB, LANES), F32),
                pltpu.VMEM((2, D, D_EXPERT), F32),
                pltpu.VMEM((2, D, D_EXPERT), F32),
                pltpu.VMEM((2, D_EXPERT, D), F32),
                pltpu.VMEM((D, D_EXPERT), BF16),
                pltpu.VMEM((D, D_EXPERT), BF16),
                pltpu.VMEM((D_EXPERT, D), BF16),
                pltpu.SemaphoreType.DMA((2,)),
                pltpu.SemaphoreType.DMA((2,)),
            ],
        ),
        out_shape=jax.ShapeDtypeStruct((n_slots * SLAB, LANES), F32),
        compiler_params=pltpu.CompilerParams(
            dimension_semantics=("arbitrary",), vmem_limit_bytes=VMEM_LIMIT),
        name="experts",
    )(block_expert, n_valid.reshape(1), row0, blk_ord, padded // BLK, xs, w_gate[0], w_up[0], w_down[0])

    last = n_tok // TD - 1
    out = pl.pallas_call(
        _combine_kernel,
        grid=(n_tok // TD,),
        in_specs=[
            pl.BlockSpec((1, TOP_K, TD), lambda i: (i, 0, 0), memory_space=pltpu.SMEM),
            pl.BlockSpec((1, TOP_K, TD), lambda i: (jnp.minimum(i + 1, last), 0, 0), memory_space=pltpu.SMEM),
            pl.BlockSpec((TD * SLAB, LANES), lambda i: (i, 0)),
            pl.BlockSpec((TD, LANES), lambda i: (i, 0)),
            pl.BlockSpec(memory_space=pl.ANY),
            pl.BlockSpec((1, D), lambda i: (0, 0)),
            pl.BlockSpec((1, D), lambda i: (0, 0)),
        ],
        out_specs=pl.BlockSpec((TD, D), lambda i: (i, 0)),
        scratch_shapes=[pltpu.VMEM((TOP_K * TD * SLAB, LANES), F32), pltpu.VMEM((TOP_K * TD * SLAB, LANES), F32),
                        pltpu.SemaphoreType.DMA((2,))],
        out_shape=jax.ShapeDtypeStruct((n_tok, D), F32),
        compiler_params=pltpu.CompilerParams(dimension_semantics=("arbitrary",)),
        name="combine",
    )(dest, dest, x1s, route, yb, ln2_g[0].reshape(1, D), ln2_b[0].reshape(1, D))

    return out.reshape(B, T, D).astype(x.dtype)
```

```python
import functools

import numpy as np
import jax
import jax.numpy as jnp
from jax import lax
from jax.experimental import pallas as pl
from jax.experimental.pallas import tpu as pltpu

D_MODEL = 1024
HG_WIDTH = 512
HG_HEAD_DIM = 128
HG_HEADS = 4
GL_WIDTH = 512
GL_HEADS = 4
GL_DV = 128
GL_DK = 64
GL_QK = 256
GATE_RANK = 16
GATE_TAU = 16.0
N_GROUPS = 8
EXPERTS_PER_GROUP = 8
N_EXPERTS = 64
TOP_K = 2
D_EXPERT = 512
DEPTH = 1
DEEPNORM_ALPHA = (2.0 * DEPTH) ** 0.25
LN_EPS = 1e-5
LOG2E = 1.4426950408889634
IN_SPLITS = (HG_WIDTH, HG_WIDTH, HG_WIDTH, HG_WIDTH, GL_QK, GL_QK, GL_WIDTH, GATE_RANK, GL_WIDTH)

LANES = 128
SUB = 128
N_LEVELS = 7
MXU_LEVELS = 3
TT = 256
TD = 512
BLK = 256
PROJ_COLS = 3712
ROUTE_BASE = 8
SLAB = 8
META_ROWS = 8
VMEM_LIMIT = 56 * 1024 * 1024

F32 = jnp.float32
BF16 = jnp.bfloat16
NT_DIMS = (((1,), (1,)), ((), ()))
TN_DIMS = (((0,), (0,)), ((), ()))


def _decay_matrices():
    t = np.arange(SUB)[:, None]
    j = np.arange(SUB)[None, :]
    mats = [(j <= t)]
    for lvl in range(1, MXU_LEVELS + 1):
        blk = 1 << lvl
        half = blk >> 1
        r = (t // blk) * blk + half - 1
        right = (t % blk) >= half
        mats.append(np.where(right, (j > r) & (j <= t), (j > t) & (j <= r)))
    w = np.concatenate(mats, axis=0).astype(np.float32)
    return np.concatenate([w, w, w], axis=1)


def _level_map():
    t = np.arange(SUB)[:, None]
    s = np.arange(SUB)[None, :]
    x = np.bitwise_xor(t, s)
    lvl = np.floor(np.log2(np.maximum(x, 1))).astype(np.int32) + 1
    lvl = np.where(x == 0, 0, lvl)
    return np.where(s > t, -1, lvl).astype(np.int32)


def _layer_norm(y, g, b):
    mu = jnp.mean(y, axis=-1, keepdims=True)
    yc = y - mu
    var = jnp.mean(yc * yc, axis=-1, keepdims=True)
    return yc * lax.rsqrt(var + LN_EPS) * g + b


def _mixer_kernel(x_ref, win_ref, wa2_ref, ba_ref, lbl_ref, gain_ref, wout_ref, ln1g_ref, ln1b_ref,
                  wrh_ref, wrl_ref, wstk_ref, lev_ref,
                  x1s_ref, route_ref, meta_ref, counts_ref,
                  p_scr, o_scr, b_scr, st_scr, carry_scr):
    bi = pl.program_id(0)
    ji = pl.program_id(1)

    @pl.when(ji == 0)
    def _():
        st_scr[...] = jnp.zeros_like(st_scr)

    @pl.when((bi == 0) & (ji == 0))
    def _():
        carry_scr[...] = jnp.zeros_like(carry_scr)

    xt = x_ref[...]
    p_scr[...] = jnp.dot(xt.astype(BF16), win_ref[...], preferred_element_type=F32)

    lbl = lbl_ref[...]
    lbe = jnp.exp(lbl - jnp.max(lbl, axis=0, keepdims=True))
    lb = lbe[0:1, :] / jnp.sum(lbe, axis=0, keepdims=True)

    lev = lev_ref[...]
    tcol = lax.broadcasted_iota(jnp.int32, (SUB, 1), 0)
    qrow = [(tcol & ((1 << l) - 1)) >= (1 << (l - 1)) for l in range(1, N_LEVELS + 1)]
    lane = lax.broadcasted_iota(jnp.int32, (1, LANES), 1)
    half_mask = [(lane < GL_DK), (lane >= GL_DK)]

    def by_halves(l, left_fn, right_fn):
        blk = 1 << l
        half = blk >> 1
        parts = []
        for j in range(SUB // blk):
            lo = j * blk
            parts += [left_fn(lo, lo + half), right_fn(lo + half, lo + blk)]
        return jnp.concatenate(parts, axis=0)

    def subtile(s, carry):
        r0 = pl.multiple_of(s * SUB, SUB)
        rows = pl.ds(r0, SUB)
        hq = p_scr[rows, 0:512]
        hf = p_scr[rows, 512:1024]
        hv = p_scr[rows, 1024:1536]
        hgate = p_scr[rows, 1536:2048]
        gq = p_scr[rows, 2048:2304]
        gk = p_scr[rows, 2304:2560]
        gv = p_scr[rows, 2560:3072]
        ggate = p_scr[rows, 3072:3584]
        ga = p_scr[rows, 3584:3712]

        f = lb + (1.0 - lb) * jax.nn.sigmoid(hf)
        kh = 1.0 - f
        lgh = jnp.log(f)
        z = jnp.dot(ga.astype(BF16), wa2_ref[...], preferred_element_type=F32) + ba_ref[...]
        lga = (jnp.minimum(z, 0.0) - jnp.log1p(jnp.exp(-jnp.abs(z)))) * (1.0 / GATE_TAU)
        lg = jnp.concatenate([lgh, lga], axis=1) * LOG2E

        l_hi = lg.astype(BF16)
        rem = lg - l_hi.astype(F32)
        l_mid = rem.astype(BF16)
        l_lo = (rem - l_mid.astype(F32)).astype(BF16)
        sums = jnp.dot(wstk_ref[...], jnp.concatenate([l_hi, l_mid, l_lo], axis=0),
                       preferred_element_type=F32)
        bcum = sums[0:SUB]
        b_scr[...] = bcum
        expos = [bcum, b_scr[SUB - 1:SUB, :] - bcum]
        for l in range(1, N_LEVELS + 1):
            if l <= MXU_LEVELS:
                expos.append(sums[l * SUB:(l + 1) * SUB])
            else:
                expos.append(by_halves(l, lambda lo, mid: b_scr[mid - 1:mid, :] - bcum[lo:mid],
                                       lambda mid, hi: bcum[mid:hi] - b_scr[mid - 1:mid, :]))
        decs = [jnp.exp2(e) for e in expos]

        def group_terms(q, k, c0):
            e_pre = decs[0][:, c0:c0 + LANES]
            e_suf = decs[1][:, c0:c0 + LANES]
            us = []
            for l in range(1, N_LEVELS + 1):
                e_l = decs[1 + l][:, c0:c0 + LANES]
                if l <= MXU_LEVELS:
                    sel = jnp.where(qrow[l - 1], q, k)
                else:
                    sel = by_halves(l, lambda lo, mid: k[lo:mid], lambda mid, hi: q[mid:hi])
                u = sel * e_l
                us.append((u, u.astype(BF16)))
            return us, (q * e_pre).astype(BF16), k * e_suf, e_pre[SUB - 1:SUB, :]

        def head(hidx, us, qd, kd, qk, v, g_last, mask):
            sc = jnp.where(lev == 0, jnp.sum(qk, axis=-1, keepdims=True), 0.0)
            for l in range(1, N_LEVELS + 1):
                u, ub = us[l - 1]
                if l <= MXU_LEVELS:
                    lhs = ub if mask is None else jnp.where(mask, ub, jnp.zeros_like(ub))
                    g = lax.dot_general(lhs, ub, NT_DIMS, preferred_element_type=F32)
                    sc = jnp.where(lev == l, g, sc)
                else:
                    blk = 1 << l
                    half = blk >> 1
                    mids = [j * blk + half for j in range(SUB // blk)]
                    uq = jnp.concatenate([u[m:m + half] for m in mids], axis=0)
                    if mask is not None:
                        uq = jnp.where(mask, uq, 0.0)
                    g = lax.dot_general(uq.astype(BF16), ub, NT_DIMS, preferred_element_type=F32)
                    parts = []
                    for n, m in enumerate(mids):
                        parts.append(sc[m - half:m])
                        parts.append(jnp.where(lev[m:m + half] == l, g[n * half:(n + 1) * half], sc[m:m + half]))
                    sc = jnp.concatenate(parts, axis=0)
            vb = v.astype(BF16)
            st = st_scr[hidx]
            o = jnp.dot(sc.astype(BF16), vb, preferred_element_type=F32)
            o = o + lax.dot_general(qd, st.astype(BF16), NT_DIMS, preferred_element_type=F32)
            ut = lax.dot_general(vb, kd.astype(BF16), TN_DIMS, preferred_element_type=F32)
            st_scr[hidx] = st * g_last + ut
            return o

        def finish(o, gate, c0):
            o = o * lax.rsqrt(jnp.mean(o * o, axis=-1, keepdims=True) + LN_EPS)
            o = o * gain_ref[:, c0:c0 + LANES] * (gate * jax.nn.sigmoid(gate))
            o_scr[rows, c0:c0 + LANES] = o

        for g in range(HG_HEADS):
            c0 = g * LANES
            q = hq[:, c0:c0 + LANES] * (HG_HEAD_DIM ** -0.5)
            k = kh[:, c0:c0 + LANES]
            us, qd, kd, g_last = group_terms(q, k, c0)
            o = head(g, us, qd, kd, q * k, hv[:, c0:c0 + LANES], g_last, None)
            finish(o, hgate[:, c0:c0 + LANES], c0)

        for p in range(GL_HEADS // 2):
            c0 = p * LANES
            q = gq[:, c0:c0 + LANES] * (GL_DK ** -0.5)
            k = gk[:, c0:c0 + LANES]
            us, qd, kd, g_last = group_terms(q, k, HG_WIDTH + c0)
            for hh in range(2):
                h = 2 * p + hh
                m = half_mask[hh]
                o = head(HG_HEADS + h, us, qd, jnp.where(m, kd, 0.0), jnp.where(m, q * k, 0.0),
                         gv[:, h * LANES:(h + 1) * LANES], g_last, m)
                finish(o, ggate[:, h * LANES:(h + 1) * LANES], HG_WIDTH + h * LANES)
        return carry

    lax.fori_loop(0, TT // SUB, subtile, 0)

    hmix = jnp.dot(o_scr[...].astype(BF16), wout_ref[...], preferred_element_type=F32)
    x1 = _layer_norm(DEEPNORM_ALPHA * xt + hmix, ln1g_ref[...], ln1b_ref[...])
    for s in range(SLAB):
        x1s_ref[pl.ds(s, TT, stride=SLAB), :] = x1[:, s * LANES:(s + 1) * LANES]

    x1h = x1.astype(BF16)
    x1l = (x1 - x1h.astype(F32)).astype(BF16)
    logits = (jnp.dot(x1h, wrh_ref[...], preferred_element_type=F32)
              + jnp.dot(x1l, wrh_ref[...], preferred_element_type=F32)
              + jnp.dot(x1h, wrl_ref[...], preferred_element_type=F32))

    lane_i = lax.broadcasted_iota(jnp.int32, (TT, LANES), 1)
    lane_f = lane_i.astype(F32)
    neg = jnp.float32(-jnp.inf)
    big = jnp.float32(1e9)

    gl = jnp.where(lane_i < N_GROUPS, logits, neg)
    gmax = jnp.max(gl, axis=-1, keepdims=True)
    gidx = jnp.min(jnp.where(gl == gmax, lane_f, big), axis=-1, keepdims=True)
    p_group = 1.0 / jnp.sum(jnp.exp(gl - gmax), axis=-1, keepdims=True)

    e_lo = ROUTE_BASE + EXPERTS_PER_GROUP * gidx
    el = jnp.where((lane_f >= e_lo) & (lane_f < e_lo + EXPERTS_PER_GROUP), logits, neg)
    m1 = jnp.max(el, axis=-1, keepdims=True)
    i1 = jnp.min(jnp.where(el == m1, lane_f, big), axis=-1, keepdims=True)
    el2 = jnp.where(lane_f == i1, neg, el)
    m2 = jnp.max(el2, axis=-1, keepdims=True)
    i2 = jnp.min(jnp.where(el2 == m2, lane_f, big), axis=-1, keepdims=True)
    r2 = jnp.exp(m2 - m1)
    gate1 = p_group / (1.0 + r2)
    gate2 = p_group * r2 / (1.0 + r2)

    hot1 = lane_f == i1
    hot2 = lane_f == i2
    cnt = (hot1 | hot2).astype(F32)
    tr = lax.broadcasted_iota(jnp.int32, (TT, TT), 0)
    tc = lax.broadcasted_iota(jnp.int32, (TT, TT), 1)
    strict = (tc < tr).astype(BF16)
    base = carry_scr[...] + jnp.dot(strict, cnt.astype(BF16), preferred_element_type=F32)
    rank1 = jnp.sum(jnp.where(hot1, base, 0.0), axis=-1, keepdims=True)
    rank2 = jnp.sum(jnp.where(hot2, base, 0.0), axis=-1, keepdims=True)
    total = carry_scr[...] + jnp.sum(cnt, axis=0, keepdims=True)
    carry_scr[...] = total
    counts_ref[...] = total

    route = jnp.where(lane_i == 0, i1 - ROUTE_BASE, 0.0)
    route = jnp.where(lane_i == 1, i2 - ROUTE_BASE, route)
    route = jnp.where(lane_i == 2, gate1, route)
    route = jnp.where(lane_i == 3, gate2, route)
    route = jnp.where(lane_i == 4, rank1, route)
    route = jnp.where(lane_i == 5, rank2, route)
    route_ref[...] = route
    meta_ref[...] = route.T[0:META_ROWS, :].astype(jnp.int32)


def _dispatch_kernel(dest_ref, x1s_ref, xs_ref, zbuf, sems):
    i = pl.program_id(0)
    tile_rows = TD * SLAB

    @pl.when(i == 0)
    def _():
        zbuf[...] = jnp.zeros_like(zbuf)
        fill = pltpu.make_async_copy(zbuf, xs_ref.at[pl.ds(xs_ref.shape[0] - BLK * SLAB, BLK * SLAB), :], sems.at[1])
        fill.start()
        fill.wait()

    for r in range(TD):
        for k in range(TOP_K):
            dst = pl.multiple_of(dest_ref[r // TT, k, r % TT] * SLAB, SLAB)
            pltpu.make_async_copy(x1s_ref.at[pl.ds(r * SLAB, SLAB), :], xs_ref.at[pl.ds(dst, SLAB), :],
                                  sems.at[0]).start(priority=k)

    for k in range(TOP_K):
        pltpu.make_async_copy(x1s_ref, xs_ref.at[pl.ds(0, tile_rows), :], sems.at[0]).wait()


def _expert_kernel(be_ref, nv_ref, row0_ref, ord_ref, nblk_ref, xs_ref, wg_ref, wu_ref, wd_ref, yb_ref,
                   xbuf, wg_f, wu_f, wd_f, wg_s, wu_s, wd_s, sems, wsems):
    i = pl.program_id(0)
    nv = nv_ref[0]
    blk_rows = BLK * SLAB

    def block_copy(b, slot):
        r = pl.multiple_of(row0_ref[b] * SLAB, SLAB)
        return pltpu.make_async_copy(xs_ref.at[pl.ds(r, blk_rows), :],
                                     xbuf.at[pl.ds(slot * blk_rows, blk_rows), :], sems.at[slot])

    def weight_copies(e, ws):
        return [pltpu.make_async_copy(src.at[e], dst.at[ws], wsems.at[ws])
                for src, dst in ((wg_ref, wg_f), (wu_ref, wu_f), (wd_ref, wd_f))]

    @pl.when(i < nv)
    def _():
        slot = i % 2
        e = be_ref[i]
        ws = ord_ref[i] % 2

        @pl.when(i == 0)
        def _():
            block_copy(0, 0).start()
            for c in weight_copies(e, ws):
                c.start()

        @pl.when(i + 1 < nv)
        def _():
            block_copy(i + 1, 1 - slot).start()

        prev = be_ref[jnp.maximum(i - 1, 0)]

        @pl.when((i == 0) | (e != prev))
        def _():
            j = i + nblk_ref[e]

            @pl.when(j < nv)
            def _():
                for c in weight_copies(be_ref[jnp.minimum(j, nv - 1)], 1 - ws):
                    c.start()

            for c in weight_copies(e, ws):
                c.wait()
            wg_s[...] = wg_f[ws].astype(BF16)
            wu_s[...] = wu_f[ws].astype(BF16)
            wd_s[...] = wd_f[ws].astype(BF16)

        block_copy(i, slot).wait()
        base = pl.multiple_of(slot * blk_rows, blk_rows)
        xb = jnp.concatenate(
            [xbuf[pl.ds(base + s, BLK, stride=SLAB), :].astype(BF16) for s in range(SLAB)], axis=1)
        a = jnp.dot(xb, wg_s[...], preferred_element_type=F32)
        u = jnp.dot(xb, wu_s[...], preferred_element_type=F32)
        h = (a * jax.nn.sigmoid(a)) * u
        y = jnp.dot(h.astype(BF16), wd_s[...], preferred_element_type=F32)
        for s in range(SLAB):
            yb_ref[pl.ds(s, BLK, stride=SLAB), :] = y[:, s * LANES:(s + 1) * LANES]

    @pl.when(i >= nv)
    def _():
        yb_ref[...] = jnp.zeros_like(yb_ref)


def _combine_kernel(dest_ref, dest_next_ref, x1s_ref, route_ref, yb_ref, g_ref, b_ref, out_ref, ybuf, sems):
    i = pl.program_id(0)
    n_steps = pl.num_programs(0)
    half_rows = TD * SLAB
    par_rows = TOP_K * half_rows

    def start_gather(d_ref, par):
        for r in range(TD):
            for k in range(TOP_K):
                src = pl.multiple_of(d_ref[r // TT, k, r % TT] * SLAB, SLAB)
                dst = pl.multiple_of(par * par_rows + (k * TD + r) * SLAB, SLAB)
                pltpu.make_async_copy(yb_ref.at[pl.ds(src, SLAB), :], ybuf.at[pl.ds(dst, SLAB), :],
                                      sems.at[par]).start(priority=k)

    def wait_gather(par):
        pltpu.make_async_copy(yb_ref.at[pl.ds(0, par_rows), :],
                              ybuf.at[pl.ds(par * par_rows, par_rows), :], sems.at[par]).wait()

    par = i % 2

    @pl.when(i == 0)
    def _():
        start_gather(dest_ref, 0)

    wait_gather(par)
    start_gather(dest_next_ref, 1 - par)

    base = pl.multiple_of(par * par_rows, par_rows)
    g1 = route_ref[:, 2:3]
    g2 = route_ref[:, 3:4]
    zs = []
    for s in range(SLAB):
        y1 = ybuf[pl.ds(base + s, TD, stride=SLAB), :]
        y2 = ybuf[pl.ds(base + half_rows + s, TD, stride=SLAB), :]
        zs.append(DEEPNORM_ALPHA * x1s_ref[pl.ds(s, TD, stride=SLAB), :] + (g1 * y1 + g2 * y2))
    z = jnp.concatenate(zs, axis=1)
    out_ref[...] = _layer_norm(z, g_ref[...], b_ref[...])

    @pl.when(i + 1 >= n_steps)
    def _():
        wait_gather(1 - par)


def _const_spec(shape):
    nd = len(shape)
    return pl.BlockSpec(shape, lambda *_: (0,) * nd, pipeline_mode=pl.Buffered(1))


def kernel(x, w_in, w_a2, b_a, lb_logits, norm_h, norm_g, w_out, ln1_g, ln1_b, w_group_router, w_expert_router,
           w_gate, w_up, w_down, ln2_g, ln2_b):
    B, T, D = x.shape
    n_tok = B * T
    assert D == D_MODEL == SLAB * LANES and T % TT == 0 and TD % TT == 0 and n_tok % TD == 0
    n_blocks = n_tok * TOP_K // BLK + N_EXPERTS
    n_slots = n_blocks * BLK

    parts = jnp.split(w_in[0], [int(c) for c in np.cumsum(IN_SPLITS)[:-1]], axis=-1)
    hq, hf, hi, hg, gq, gk, gv, ga, gg = parts
    win = jnp.concatenate([hq, hf, hi, hg, gq, gk, gv, gg, ga], axis=-1)
    win = jnp.pad(win, ((0, 0), (0, PROJ_COLS - win.shape[1]))).astype(BF16)
    wa2 = jnp.pad(w_a2[0], ((0, LANES - GATE_RANK), (0, 0))).astype(BF16)
    ba = b_a[0].reshape(1, GL_QK).astype(F32)
    gain = jnp.concatenate([norm_h[0], norm_g[0]]).reshape(1, D).astype(F32)
    wout = w_out[0].astype(BF16)
    wr = jnp.concatenate([w_group_router[0], w_expert_router[0]], axis=-1).astype(F32)
    wr = jnp.pad(wr, ((0, 0), (0, LANES - wr.shape[1])))
    wrh = wr.astype(BF16)
    wrl = (wr - wrh.astype(F32)).astype(BF16)
    wstk = jnp.asarray(_decay_matrices(), dtype=BF16)
    lev = jnp.asarray(_level_map())
    n_w = wstk.shape[0]

    n_steps = n_tok // TT
    x1s, route, meta, counts = pl.pallas_call(
        _mixer_kernel,
        grid=(B, T // TT),
        in_specs=[
            pl.BlockSpec((None, TT, D), lambda b, j: (b, j, 0)),
            _const_spec((D, PROJ_COLS)),
            _const_spec((LANES, GL_QK)),
            _const_spec((1, GL_QK)),
            _const_spec((DEPTH + 1, HG_WIDTH)),
            _const_spec((1, D)),
            _const_spec((D, D)),
            _const_spec((1, D)),
            _const_spec((1, D)),
            _const_spec((D, LANES)),
            _const_spec((D, LANES)),
            _const_spec((n_w, 3 * SUB)),
            _const_spec((SUB, SUB)),
        ],
        out_specs=[
            pl.BlockSpec((TT * SLAB, LANES), lambda b, j: (b * (T // TT) + j, 0)),
            pl.BlockSpec((TT, LANES), lambda b, j: (b * (T // TT) + j, 0)),
            pl.BlockSpec((META_ROWS, TT), lambda b, j: (b * (T // TT) + j, 0)),
            pl.BlockSpec((1, LANES), lambda b, j: (0, 0)),
        ],
        out_shape=[
            jax.ShapeDtypeStruct((n_tok * SLAB, LANES), F32),
            jax.ShapeDtypeStruct((n_tok, LANES), F32),
            jax.ShapeDtypeStruct((n_steps * META_ROWS, TT), jnp.int32),
            jax.ShapeDtypeStruct((1, LANES), F32),
        ],
        scratch_shapes=[
            pltpu.VMEM((TT, PROJ_COLS), F32),
            pltpu.VMEM((TT, D), F32),
            pltpu.VMEM((SUB, HG_WIDTH + GL_QK), F32),
            pltpu.VMEM((HG_HEADS + GL_HEADS, LANES, LANES), F32),
            pltpu.VMEM((1, LANES), F32),
        ],
        compiler_params=pltpu.CompilerParams(
            dimension_semantics=("arbitrary", "arbitrary"), vmem_limit_bytes=VMEM_LIMIT),
        name="mixer",
    )(x, win, wa2, ba, lb_logits.astype(F32), gain, wout, ln1_g[0].reshape(1, D), ln1_b[0].reshape(1, D),
      wrh, wrl, wstk, lev)

    cnt = counts[0, ROUTE_BASE:ROUTE_BASE + N_EXPERTS].astype(jnp.int32)
    padded = (cnt + BLK - 1) // BLK * BLK
    ends = jnp.cumsum(padded)
    starts = ends - padded
    n_valid = (ends[-1] // BLK).astype(jnp.int32)
    blk_ids = jnp.minimum(jnp.arange(n_blocks, dtype=jnp.int32), n_valid - 1)
    block_expert = jnp.sum((ends[None, :] <= (blk_ids * BLK)[:, None]).astype(jnp.int32), axis=1)
    block_expert = jnp.minimum(block_expert, N_EXPERTS - 1)
    blk_ord = jnp.cumsum(jnp.concatenate(
        [jnp.zeros((1,), jnp.int32), (block_expert[1:] != block_expert[:-1]).astype(jnp.int32)]))
    dstarts = jnp.cumsum(cnt) - cnt
    first_blk = starts // BLK
    pick = block_expert[None, :] == jnp.arange(N_EXPERTS, dtype=jnp.int32)[:, None]
    row0 = jnp.sum(jnp.where(pick, (dstarts - first_blk * BLK)[:, None], 0), axis=0) + blk_ids * BLK
    meta3 = meta.reshape(n_steps, META_ROWS, TT)
    hit = meta3[None, :, 0:TOP_K, :] == jnp.arange(N_EXPERTS, dtype=jnp.int32)[:, None, None, None]
    rank = meta3[:, 4:4 + TOP_K, :]
    dest_x = rank + jnp.sum(jnp.where(hit, dstarts[:, None, None, None], 0), axis=0)
    dest = rank + jnp.sum(jnp.where(hit, starts[:, None, None, None], 0), axis=0)

    xs_rows = (n_tok * TOP_K + BLK) * SLAB
    xs = pl.pallas_call(
        _dispatch_kernel,
        grid=(n_tok // TD,),
        in_specs=[
            pl.BlockSpec((TD // TT, TOP_K, TT), lambda i: (i, 0, 0), memory_space=pltpu.SMEM),
            pl.BlockSpec((TD * SLAB, LANES), lambda i: (i, 0)),
        ],
        out_specs=pl.BlockSpec(memory_space=pl.ANY),
        out_shape=jax.ShapeDtypeStruct((xs_rows, LANES), F32),
        scratch_shapes=[pltpu.VMEM((BLK * SLAB, LANES), F32), pltpu.SemaphoreType.DMA((2,))],
        compiler_params=pltpu.CompilerParams(dimension_semantics=("arbitrary",)),
        name="dispatch",
    )(dest_x, x1s)

    yb = pl.pallas_call(
        _expert_kernel,
        grid_spec=pltpu.PrefetchScalarGridSpec(
            num_scalar_prefetch=5,
            grid=(n_blocks,),
            in_specs=[pl.BlockSpec(memory_space=pl.ANY)] * 4,
            out_specs=pl.BlockSpec((BLK * SLAB, LANES), lambda i, *_: (i, 0)),
            scratch_shapes=[
                pltpu.VMEM((2 * BLK * SLAB, LANES), F32),
                pltpu.VMEM((2, D, D_EXPERT), F32),
                pltpu.VMEM((2, D, D_EXPERT), F32),
                pltpu.VMEM((2, D_EXPERT, D), F32),
                pltpu.VMEM((D, D_EXPERT), BF16),
                pltpu.VMEM((D, D_EXPERT), BF16),
                pltpu.VMEM((D_EXPERT, D), BF16),
                pltpu.SemaphoreType.DMA((2,)),
                pltpu.SemaphoreType.DMA((2,)),
            ],
        ),
        out_shape=jax.ShapeDtypeStruct((n_slots * SLAB, LANES), F32),
        compiler_params=pltpu.CompilerParams(
            dimension_semantics=("arbitrary",), vmem_limit_bytes=VMEM_LIMIT),
        name="experts",
    )(block_expert, n_valid.reshape(1), row0, blk_ord, padded // BLK, xs, w_gate[0], w_up[0], w_down[0])

    last = n_tok // TD - 1
    out = pl.pallas_call(
        _combine_kernel,
        grid=(n_tok // TD,),
        in_specs=[
            pl.BlockSpec((TD // TT, TOP_K, TT), lambda i: (i, 0, 0), memory_space=pltpu.SMEM),
            pl.BlockSpec((TD // TT, TOP_K, TT), lambda i: (jnp.minimum(i + 1, last), 0, 0), memory_space=pltpu.SMEM),
            pl.BlockSpec((TD * SLAB, LANES), lambda i: (i, 0)),
            pl.BlockSpec((TD, LANES), lambda i: (i, 0)),
            pl.BlockSpec(memory_space=pl.ANY),
            pl.BlockSpec((1, D), lambda i: (0, 0)),
            pl.BlockSpec((1, D), lambda i: (0, 0)),
        ],
        out_specs=pl.BlockSpec((TD, D), lambda i: (i, 0)),
        scratch_shapes=[pltpu.VMEM((2 * TOP_K * TD * SLAB, LANES), F32), pltpu.SemaphoreType.DMA((2,))],
        out_shape=jax.ShapeDtypeStruct((n_tok, D), F32),
        compiler_params=pltpu.CompilerParams(dimension_semantics=("arbitrary",)),
        name="combine",
    )(dest, dest, x1s, route, yb, ln2_g[0].reshape(1, D), ln2_b[0].reshape(1, D))

    return out.reshape(B, T, D).astype(x.dtype)
```

```python
import functools

import numpy as np
import jax
import jax.numpy as jnp
from jax import lax
from jax.experimental import pallas as pl
from jax.experimental.pallas import tpu as pltpu

D_MODEL = 1024
HG_WIDTH = 512
HG_HEAD_DIM = 128
HG_HEADS = 4
GL_WIDTH = 512
GL_HEADS = 4
GL_DV = 128
GL_DK = 64
GL_QK = 256
GATE_RANK = 16
GATE_TAU = 16.0
N_GROUPS = 8
EXPERTS_PER_GROUP = 8
N_EXPERTS = 64
TOP_K = 2
D_EXPERT = 512
DEPTH = 1
DEEPNORM_ALPHA = (2.0 * DEPTH) ** 0.25
LN_EPS = 1e-5
LOG2E = 1.4426950408889634
IN_SPLITS = (HG_WIDTH, HG_WIDTH, HG_WIDTH, HG_WIDTH, GL_QK, GL_QK, GL_WIDTH, GATE_RANK, GL_WIDTH)

LANES = 128
SUB = 128
N_LEVELS = 7
MXU_LEVELS = 3
TT = 256
TD = 512
BLK = 256
PROJ_COLS = 3712
ROUTE_BASE = 8
SLAB = 8
META_ROWS = 8
VMEM_LIMIT = 56 * 1024 * 1024

F32 = jnp.float32
BF16 = jnp.bfloat16
NT_DIMS = (((1,), (1,)), ((), ()))
TN_DIMS = (((0,), (0,)), ((), ()))


def _decay_matrices():
    t = np.arange(SUB)[:, None]
    j = np.arange(SUB)[None, :]
    mats = [(j <= t)]
    for lvl in range(1, MXU_LEVELS + 1):
        blk = 1 << lvl
        half = blk >> 1
        r = (t // blk) * blk + half - 1
        right = (t % blk) >= half
        mats.append(np.where(right, (j > r) & (j <= t), (j > t) & (j <= r)))
    w = np.concatenate(mats, axis=0).astype(np.float32)
    return np.concatenate([w, w, w], axis=1)


def _level_map():
    t = np.arange(SUB)[:, None]
    s = np.arange(SUB)[None, :]
    x = np.bitwise_xor(t, s)
    lvl = np.floor(np.log2(np.maximum(x, 1))).astype(np.int32) + 1
    lvl = np.where(x == 0, 0, lvl)
    return np.where(s > t, -1, lvl).astype(np.int32)


def _layer_norm(y, g, b):
    mu = jnp.mean(y, axis=-1, keepdims=True)
    yc = y - mu
    var = jnp.mean(yc * yc, axis=-1, keepdims=True)
    return yc * lax.rsqrt(var + LN_EPS) * g + b


def _mixer_kernel(x_ref, win_ref, wa2_ref, ba_ref, lbl_ref, gain_ref, wout_ref, ln1g_ref, ln1b_ref,
                  wrh_ref, wrl_ref, wstk_ref, lev_ref,
                  x1s_ref, route_ref, meta_ref, counts_ref,
                  p_scr, o_scr, b_scr, st_scr, carry_scr):
    bi = pl.program_id(0)
    ji = pl.program_id(1)

    @pl.when(ji == 0)
    def _():
        st_scr[...] = jnp.zeros_like(st_scr)

    @pl.when((bi == 0) & (ji == 0))
    def _():
        carry_scr[...] = jnp.zeros_like(carry_scr)

    xt = x_ref[...]
    p_scr[...] = jnp.dot(xt.astype(BF16), win_ref[...], preferred_element_type=F32)

    lbl = lbl_ref[...]
    lbe = jnp.exp(lbl - jnp.max(lbl, axis=0, keepdims=True))
    lb = lbe[0:1, :] / jnp.sum(lbe, axis=0, keepdims=True)

    lev = lev_ref[...]
    tcol = lax.broadcasted_iota(jnp.int32, (SUB, 1), 0)
    qrow = [(tcol & ((1 << l) - 1)) >= (1 << (l - 1)) for l in range(1, N_LEVELS + 1)]
    lane = lax.broadcasted_iota(jnp.int32, (1, LANES), 1)
    half_mask = [(lane < GL_DK), (lane >= GL_DK)]

    def by_halves(l, left_fn, right_fn):
        blk = 1 << l
        half = blk >> 1
        parts = []
        for j in range(SUB // blk):
            lo = j * blk
            parts += [left_fn(lo, lo + half), right_fn(lo + half, lo + blk)]
        return jnp.concatenate(parts, axis=0)

    def subtile(s, carry):
        r0 = pl.multiple_of(s * SUB, SUB)
        rows = pl.ds(r0, SUB)
        hq = p_scr[rows, 0:512]
        hf = p_scr[rows, 512:1024]
        hv = p_scr[rows, 1024:1536]
        hgate = p_scr[rows, 1536:2048]
        gq = p_scr[rows, 2048:2304]
        gk = p_scr[rows, 2304:2560]
        gv = p_scr[rows, 2560:3072]
        ggate = p_scr[rows, 3072:3584]
        ga = p_scr[rows, 3584:3712]

        f = lb + (1.0 - lb) * jax.nn.sigmoid(hf)
        kh = 1.0 - f
        lgh = jnp.log(f)
        z = jnp.dot(ga.astype(BF16), wa2_ref[...], preferred_element_type=F32) + ba_ref[...]
        lga = (jnp.minimum(z, 0.0) - jnp.log1p(jnp.exp(-jnp.abs(z)))) * (1.0 / GATE_TAU)
        lg = jnp.concatenate([lgh, lga], axis=1) * LOG2E

        l_hi = lg.astype(BF16)
        rem = lg - l_hi.astype(F32)
        l_mid = rem.astype(BF16)
        l_lo = (rem - l_mid.astype(F32)).astype(BF16)
        sums = jnp.dot(wstk_ref[...], jnp.concatenate([l_hi, l_mid, l_lo], axis=0),
                       preferred_element_type=F32)
        bcum = sums[0:SUB]
        b_scr[...] = bcum
        expos = [bcum, b_scr[SUB - 1:SUB, :] - bcum]
        for l in range(1, N_LEVELS + 1):
            if l <= MXU_LEVELS:
                expos.append(sums[l * SUB:(l + 1) * SUB])
            else:
                expos.append(by_halves(l, lambda lo, mid: b_scr[mid - 1:mid, :] - bcum[lo:mid],
                                       lambda mid, hi: bcum[mid:hi] - b_scr[mid - 1:mid, :]))
        decs = [jnp.exp2(e) for e in expos]

        def group_terms(q, k, c0):
            e_pre = decs[0][:, c0:c0 + LANES]
            e_suf = decs[1][:, c0:c0 + LANES]
            us = []
            for l in range(1, N_LEVELS + 1):
                e_l = decs[1 + l][:, c0:c0 + LANES]
                if l <= MXU_LEVELS:
                    sel = jnp.where(qrow[l - 1], q, k)
                else:
                    sel = by_halves(l, lambda lo, mid: k[lo:mid], lambda mid, hi: q[mid:hi])
                u = sel * e_l
                us.append((u, u.astype(BF16)))
            return us, (q * e_pre).astype(BF16), k * e_suf, e_pre[SUB - 1:SUB, :]

        def head(hidx, us, qd, kd, qk, v, g_last, mask):
            sc = jnp.where(lev == 0, jnp.sum(qk, axis=-1, keepdims=True), 0.0)
            for l in range(1, N_LEVELS + 1):
                u, ub = us[l - 1]
                if l <= MXU_LEVELS:
                    lhs = ub if mask is None else jnp.where(mask, ub, jnp.zeros_like(ub))
                    g = lax.dot_general(lhs, ub, NT_DIMS, preferred_element_type=F32)
                    sc = jnp.where(lev == l, g, sc)
                else:
                    blk = 1 << l
                    half = blk >> 1
                    mids = [j * blk + half for j in range(SUB // blk)]
                    uq = jnp.concatenate([u[m:m + half] for m in mids], axis=0)
                    if mask is not None:
                        uq = jnp.where(mask, uq, 0.0)
                    g = lax.dot_general(uq.astype(BF16), ub, NT_DIMS, preferred_element_type=F32)
                    parts = []
                    for n, m in enumerate(mids):
                        parts.append(sc[m - half:m])
                        parts.append(jnp.where(lev[m:m + half] == l, g[n * half:(n + 1) * half], sc[m:m + half]))
                    sc = jnp.concatenate(parts, axis=0)
            vb = v.astype(BF16)
            st = st_scr[hidx]
            o = jnp.dot(sc.astype(BF16), vb, preferred_element_type=F32)
            o = o + lax.dot_general(qd, st.astype(BF16), NT_DIMS, preferred_element_type=F32)
            ut = lax.dot_general(vb, kd.astype(BF16), TN_DIMS, preferred_element_type=F32)
            st_scr[hidx] = st * g_last + ut
            return o

        def finish(o, gate, c0):
            o = o * lax.rsqrt(jnp.mean(o * o, axis=-1, keepdims=True) + LN_EPS)
            o = o * gain_ref[:, c0:c0 + LANES] * (gate * jax.nn.sigmoid(gate))
            o_scr[rows, c0:c0 + LANES] = o

        for g in range(HG_HEADS):
            c0 = g * LANES
            q = hq[:, c0:c0 + LANES] * (HG_HEAD_DIM ** -0.5)
            k = kh[:, c0:c0 + LANES]
            us, qd, kd, g_last = group_terms(q, k, c0)
            o = head(g, us, qd, kd, q * k, hv[:, c0:c0 + LANES], g_last, None)
            finish(o, hgate[:, c0:c0 + LANES], c0)

        for p in range(GL_HEADS // 2):
            c0 = p * LANES
            q = gq[:, c0:c0 + LANES] * (GL_DK ** -0.5)
            k = gk[:, c0:c0 + LANES]
            us, qd, kd, g_last = group_terms(q, k, HG_WIDTH + c0)
            for hh in range(2):
                h = 2 * p + hh
                m = half_mask[hh]
                o = head(HG_HEADS + h, us, qd, jnp.where(m, kd, 0.0), jnp.where(m, q * k, 0.0),
                         gv[:, h * LANES:(h + 1) * LANES], g_last, m)
                finish(o, ggate[:, h * LANES:(h + 1) * LANES], HG_WIDTH + h * LANES)
        return carry

    lax.fori_loop(0, TT // SUB, subtile, 0)

    hmix = jnp.dot(o_scr[...].astype(BF16), wout_ref[...], preferred_element_type=F32)
    x1 = _layer_norm(DEEPNORM_ALPHA * xt + hmix, ln1g_ref[...], ln1b_ref[...])
    for s in range(SLAB):
        x1s_ref[pl.ds(s, TT, stride=SLAB), :] = x1[:, s * LANES:(s + 1) * LANES]

    x1h = x1.astype(BF16)
    x1l = (x1 - x1h.astype(F32)).astype(BF16)
    logits = (jnp.dot(x1h, wrh_ref[...], preferred_element_type=F32)
              + jnp.dot(x1l, wrh_ref[...], preferred_element_type=F32)
              + jnp.dot(x1h, wrl_ref[...], preferred_element_type=F32))

    lane_i = lax.broadcasted_iota(jnp.int32, (TT, LANES), 1)
    lane_f = lane_i.astype(F32)
    neg = jnp.float32(-jnp.inf)
    big = jnp.float32(1e9)

    gl = jnp.where(lane_i < N_GROUPS, logits, neg)
    gmax = jnp.max(gl, axis=-1, keepdims=True)
    gidx = jnp.min(jnp.where(gl == gmax, lane_f, big), axis=-1, keepdims=True)
    p_group = 1.0 / jnp.sum(jnp.exp(gl - gmax), axis=-1, keepdims=True)

    e_lo = ROUTE_BASE + EXPERTS_PER_GROUP * gidx
    el = jnp.where((lane_f >= e_lo) & (lane_f < e_lo + EXPERTS_PER_GROUP), logits, neg)
    m1 = jnp.max(el, axis=-1, keepdims=True)
    i1 = jnp.min(jnp.where(el == m1, lane_f, big), axis=-1, keepdims=True)
    el2 = jnp.where(lane_f == i1, neg, el)
    m2 = jnp.max(el2, axis=-1, keepdims=True)
    i2 = jnp.min(jnp.where(el2 == m2, lane_f, big), axis=-1, keepdims=True)
    r2 = jnp.exp(m2 - m1)
    gate1 = p_group / (1.0 + r2)
    gate2 = p_group * r2 / (1.0 + r2)

    hot1 = lane_f == i1
    hot2 = lane_f == i2
    cnt = (hot1 | hot2).astype(F32)
    tr = lax.broadcasted_iota(jnp.int32, (TT, TT), 0)
    tc = lax.broadcasted_iota(jnp.int32, (TT, TT), 1)
    strict = (tc < tr).astype(BF16)
    base = carry_scr[...] + jnp.dot(strict, cnt.astype(BF16), preferred_element_type=F32)
    rank1 = jnp.sum(jnp.where(hot1, base, 0.0), axis=-1, keepdims=True)
    rank2 = jnp.sum(jnp.where(hot2, base, 0.0), axis=-1, keepdims=True)
    total = carry_scr[...] + jnp.sum(cnt, axis=0, keepdims=True)
    carry_scr[...] = total
    counts_ref[...] = total

    route = jnp.where(lane_i == 0, i1 - ROUTE_BASE, 0.0)
    route = jnp.where(lane_i == 1, i2 - ROUTE_BASE, route)
    route = jnp.where(lane_i == 2, gate1, route)
    route = jnp.where(lane_i == 3, gate2, route)
    route = jnp.where(lane_i == 4, rank1, route)
    route = jnp.where(lane_i == 5, rank2, route)
    route_ref[...] = route
    meta_ref[...] = route.T[0:META_ROWS, :].astype(jnp.int32)


def _dispatch_kernel(dest_ref, x1s_ref, xs_ref, xin, zbuf, in_sems, out_sems, fill_sem):
    i = pl.program_id(0)
    n_steps = pl.num_programs(0)
    tile_rows = TD * SLAB

    def tile_load(t, slot):
        return pltpu.make_async_copy(x1s_ref.at[pl.ds(pl.multiple_of(t * tile_rows, tile_rows), tile_rows), :],
                                     xin.at[slot], in_sems.at[slot])

    def wait_rows(slot, sem):
        for k in range(TOP_K):
            pltpu.make_async_copy(xin.at[slot], xs_ref.at[pl.ds(0, tile_rows), :], sem).wait()

    @pl.when(i == 0)
    def _():
        tile_load(0, 0).start()
        zbuf[...] = jnp.zeros_like(zbuf)
        fill = pltpu.make_async_copy(zbuf, xs_ref.at[pl.ds(xs_ref.shape[0] - BLK * SLAB, BLK * SLAB), :], fill_sem)
        fill.start()
        fill.wait()

    slot = i % 3

    @pl.when(i + 1 < n_steps)
    def _():
        tile_load(i + 1, (i + 1) % 3).start()

    tile_load(i, slot).wait()
    for r in range(TD):
        for k in range(TOP_K):
            dst = pl.multiple_of(dest_ref[r // TT, k, r % TT] * SLAB, SLAB)
            pltpu.make_async_copy(xin.at[slot, pl.ds(r * SLAB, SLAB), :], xs_ref.at[pl.ds(dst, SLAB), :],
                                  out_sems.at[i % 2]).start(priority=k)

    @pl.when(i > 0)
    def _():
        wait_rows((i + 2) % 3, out_sems.at[(i + 1) % 2])

    @pl.when(i + 1 >= n_steps)
    def _():
        wait_rows(slot, out_sems.at[i % 2])


def _expert_kernel(be_ref, nv_ref, row0_ref, ord_ref, nblk_ref, xs_ref, wg_ref, wu_ref, wd_ref, yb_ref,
                   xbuf, wg_f, wu_f, wd_f, wg_s, wu_s, wd_s, sems, wsems):
    i = pl.program_id(0)
    nv = nv_ref[0]
    blk_rows = BLK * SLAB

    def block_copy(b, slot):
        r = pl.multiple_of(row0_ref[b] * SLAB, SLAB)
        return pltpu.make_async_copy(xs_ref.at[pl.ds(r, blk_rows), :],
                                     xbuf.at[pl.ds(slot * blk_rows, blk_rows), :], sems.at[slot])

    def weight_copies(e, ws):
        return [pltpu.make_async_copy(src.at[e], dst.at[ws], wsems.at[ws])
                for src, dst in ((wg_ref, wg_f), (wu_ref, wu_f), (wd_ref, wd_f))]

    @pl.when(i < nv)
    def _():
        slot = i % 2
        e = be_ref[i]
        ws = ord_ref[i] % 2

        @pl.when(i == 0)
        def _():
            block_copy(0, 0).start()
            for c in weight_copies(e, ws):
                c.start()

        @pl.when(i + 1 < nv)
        def _():
            block_copy(i + 1, 1 - slot).start()

        prev = be_ref[jnp.maximum(i - 1, 0)]

        @pl.when((i == 0) | (e != prev))
        def _():
            j = i + nblk_ref[e]

            @pl.when(j < nv)
            def _():
                for c in weight_copies(be_ref[jnp.minimum(j, nv - 1)], 1 - ws):
                    c.start()

            for c in weight_copies(e, ws):
                c.wait()
            wg_s[...] = wg_f[ws].astype(BF16)
            wu_s[...] = wu_f[ws].astype(BF16)
            wd_s[...] = wd_f[ws].astype(BF16)

        block_copy(i, slot).wait()
        base = pl.multiple_of(slot * blk_rows, blk_rows)
        xb = jnp.concatenate(
            [xbuf[pl.ds(base + s, BLK, stride=SLAB), :].astype(BF16) for s in range(SLAB)], axis=1)
        a = jnp.dot(xb, wg_s[...], preferred_element_type=F32)
        u = jnp.dot(xb, wu_s[...], preferred_element_type=F32)
        h = (a * jax.nn.sigmoid(a)) * u
        y = jnp.dot(h.astype(BF16), wd_s[...], preferred_element_type=F32)
        for s in range(SLAB):
            yb_ref[pl.ds(s, BLK, stride=SLAB), :] = y[:, s * LANES:(s + 1) * LANES]

    @pl.when(i >= nv)
    def _():
        yb_ref[...] = jnp.zeros_like(yb_ref)


def _combine_kernel(dest_ref, dest_next_ref, x1s_ref, route_ref, yb_ref, g_ref, b_ref, out_ref, ybuf, sems):
    i = pl.program_id(0)
    n_steps = pl.num_programs(0)
    half_rows = TD * SLAB
    par_rows = TOP_K * half_rows

    def start_gather(d_ref, par):
        for r in range(TD):
            for k in range(TOP_K):
                src = pl.multiple_of(d_ref[r // TT, k, r % TT] * SLAB, SLAB)
                dst = pl.multiple_of(par * par_rows + (k * TD + r) * SLAB, SLAB)
                pltpu.make_async_copy(yb_ref.at[pl.ds(src, SLAB), :], ybuf.at[pl.ds(dst, SLAB), :],
                                      sems.at[par]).start(priority=k)

    def wait_gather(par):
        pltpu.make_async_copy(yb_ref.at[pl.ds(0, par_rows), :],
                              ybuf.at[pl.ds(par * par_rows, par_rows), :], sems.at[par]).wait()

    par = i % 2

    @pl.when(i == 0)
    def _():
        start_gather(dest_ref, 0)

    wait_gather(par)
    start_gather(dest_next_ref, 1 - par)

    base = pl.multiple_of(par * par_rows, par_rows)
    g1 = route_ref[:, 2:3]
    g2 = route_ref[:, 3:4]
    zs = []
    for s in range(SLAB):
        y1 = ybuf[pl.ds(base + s, TD, stride=SLAB), :]
        y2 = ybuf[pl.ds(base + half_rows + s, TD, stride=SLAB), :]
        zs.append(DEEPNORM_ALPHA * x1s_ref[pl.ds(s, TD, stride=SLAB), :] + (g1 * y1 + g2 * y2))
    z = jnp.concatenate(zs, axis=1)
    out_ref[...] = _layer_norm(z, g_ref[...], b_ref[...])

    @pl.when(i + 1 >= n_steps)
    def _():
        wait_gather(1 - par)


def _const_spec(shape):
    nd = len(shape)
    return pl.BlockSpec(shape, lambda *_: (0,) * nd, pipeline_mode=pl.Buffered(1))


def kernel(x, w_in, w_a2, b_a, lb_logits, norm_h, norm_g, w_out, ln1_g, ln1_b, w_group_router, w_expert_router,
           w_gate, w_up, w_down, ln2_g, ln2_b):
    B, T, D = x.shape
    n_tok = B * T
    assert D == D_MODEL == SLAB * LANES and T % TT == 0 and TD % TT == 0 and n_tok % TD == 0
    n_blocks = n_tok * TOP_K // BLK + N_EXPERTS
    n_slots = n_blocks * BLK

    parts = jnp.split(w_in[0], [int(c) for c in np.cumsum(IN_SPLITS)[:-1]], axis=-1)
    hq, hf, hi, hg, gq, gk, gv, ga, gg = parts
    win = jnp.concatenate([hq, hf, hi, hg, gq, gk, gv, gg, ga], axis=-1)
    win = jnp.pad(win, ((0, 0), (0, PROJ_COLS - win.shape[1]))).astype(BF16)
    wa2 = jnp.pad(w_a2[0], ((0, LANES - GATE_RANK), (0, 0))).astype(BF16)
    ba = b_a[0].reshape(1, GL_QK).astype(F32)
    gain = jnp.concatenate([norm_h[0], norm_g[0]]).reshape(1, D).astype(F32)
    wout = w_out[0].astype(BF16)
    wr = jnp.concatenate([w_group_router[0], w_expert_router[0]], axis=-1).astype(F32)
    wr = jnp.pad(wr, ((0, 0), (0, LANES - wr.shape[1])))
    wrh = wr.astype(BF16)
    wrl = (wr - wrh.astype(F32)).astype(BF16)
    wstk = jnp.asarray(_decay_matrices(), dtype=BF16)
    lev = jnp.asarray(_level_map())
    n_w = wstk.shape[0]

    n_steps = n_tok // TT
    x1s, route, meta, counts = pl.pallas_call(
        _mixer_kernel,
        grid=(B, T // TT),
        in_specs=[
            pl.BlockSpec((None, TT, D), lambda b, j: (b, j, 0)),
            _const_spec((D, PROJ_COLS)),
            _const_spec((LANES, GL_QK)),
            _const_spec((1, GL_QK)),
            _const_spec((DEPTH + 1, HG_WIDTH)),
            _const_spec((1, D)),
            _const_spec((D, D)),
            _const_spec((1, D)),
            _const_spec((1, D)),
            _const_spec((D, LANES)),
            _const_spec((D, LANES)),
            _const_spec((n_w, 3 * SUB)),
            _const_spec((SUB, SUB)),
        ],
        out_specs=[
            pl.BlockSpec((TT * SLAB, LANES), lambda b, j: (b * (T // TT) + j, 0)),
            pl.BlockSpec((TT, LANES), lambda b, j: (b * (T // TT) + j, 0)),
            pl.BlockSpec((META_ROWS, TT), lambda b, j: (b * (T // TT) + j, 0)),
            pl.BlockSpec((1, LANES), lambda b, j: (0, 0)),
        ],
        out_shape=[
            jax.ShapeDtypeStruct((n_tok * SLAB, LANES), F32),
            jax.ShapeDtypeStruct((n_tok, LANES), F32),
            jax.ShapeDtypeStruct((n_steps * META_ROWS, TT), jnp.int32),
            jax.ShapeDtypeStruct((1, LANES), F32),
        ],
        scratch_shapes=[
            pltpu.VMEM((TT, PROJ_COLS), F32),
            pltpu.VMEM((TT, D), F32),
            pltpu.VMEM((SUB, HG_WIDTH + GL_QK), F32),
            pltpu.VMEM((HG_HEADS + GL_HEADS, LANES, LANES), F32),
            pltpu.VMEM((1, LANES), F32),
        ],
        compiler_params=pltpu.CompilerParams(
            dimension_semantics=("arbitrary", "arbitrary"), vmem_limit_bytes=VMEM_LIMIT),
        name="mixer",
    )(x, win, wa2, ba, lb_logits.astype(F32), gain, wout, ln1_g[0].reshape(1, D), ln1_b[0].reshape(1, D),
      wrh, wrl, wstk, lev)

    cnt = counts[0, ROUTE_BASE:ROUTE_BASE + N_EXPERTS].astype(jnp.int32)
    padded = (cnt + BLK - 1) // BLK * BLK
    ends = jnp.cumsum(padded)
    starts = ends - padded
    n_valid = (ends[-1] // BLK).astype(jnp.int32)
    blk_ids = jnp.minimum(jnp.arange(n_blocks, dtype=jnp.int32), n_valid - 1)
    block_expert = jnp.sum((ends[None, :] <= (blk_ids * BLK)[:, None]).astype(jnp.int32), axis=1)
    block_expert = jnp.minimum(block_expert, N_EXPERTS - 1)
    blk_ord = jnp.cumsum(jnp.concatenate(
        [jnp.zeros((1,), jnp.int32), (block_expert[1:] != block_expert[:-1]).astype(jnp.int32)]))
    dstarts = jnp.cumsum(cnt) - cnt
    first_blk = starts // BLK
    pick = block_expert[None, :] == jnp.arange(N_EXPERTS, dtype=jnp.int32)[:, None]
    row0 = jnp.sum(jnp.where(pick, (dstarts - first_blk * BLK)[:, None], 0), axis=0) + blk_ids * BLK
    meta3 = meta.reshape(n_steps, META_ROWS, TT)
    hit = meta3[None, :, 0:TOP_K, :] == jnp.arange(N_EXPERTS, dtype=jnp.int32)[:, None, None, None]
    rank = meta3[:, 4:4 + TOP_K, :]
    dest_x = rank + jnp.sum(jnp.where(hit, dstarts[:, None, None, None], 0), axis=0)
    dest = rank + jnp.sum(jnp.where(hit, starts[:, None, None, None], 0), axis=0)

    xs_rows = (n_tok * TOP_K + BLK) * SLAB
    xs = pl.pallas_call(
        _dispatch_kernel,
        grid=(n_tok // TD,),
        in_specs=[
            pl.BlockSpec((TD // TT, TOP_K, TT), lambda i: (i, 0, 0), memory_space=pltpu.SMEM),
            pl.BlockSpec(memory_space=pl.ANY),
        ],
        out_specs=pl.BlockSpec(memory_space=pl.ANY),
        out_shape=jax.ShapeDtypeStruct((xs_rows, LANES), F32),
        scratch_shapes=[pltpu.VMEM((3, TD * SLAB, LANES), F32), pltpu.VMEM((BLK * SLAB, LANES), F32),
                        pltpu.SemaphoreType.DMA((3,)), pltpu.SemaphoreType.DMA((2,)), pltpu.SemaphoreType.DMA],
        compiler_params=pltpu.CompilerParams(dimension_semantics=("arbitrary",)),
        name="dispatch",
    )(dest_x, x1s)

    yb = pl.pallas_call(
        _expert_kernel,
        grid_spec=pltpu.PrefetchScalarGridSpec(
            num_scalar_prefetch=5,
            grid=(n_blocks,),
            in_specs=[pl.BlockSpec(memory_space=pl.ANY)] * 4,
            out_specs=pl.BlockSpec((BLK * SLAB, LANES), lambda i, *_: (i, 0)),
            scratch_shapes=[
                pltpu.VMEM((2 * BLK * SLAB, LANES), F32),
                pltpu.VMEM((2, D, D_EXPERT), F32),
                pltpu.VMEM((2, D, D_EXPERT), F32),
                pltpu.VMEM((2, D_EXPERT, D), F32),
                pltpu.VMEM((D, D_EXPERT), BF16),
                pltpu.VMEM((D, D_EXPERT), BF16),
                pltpu.VMEM((D_EXPERT, D), BF16),
                pltpu.SemaphoreType.DMA((2,)),
                pltpu.SemaphoreType.DMA((2,)),
            ],
        ),
        out_shape=jax.ShapeDtypeStruct((n_slots * SLAB, LANES), F32),
        compiler_params=pltpu.CompilerParams(
            dimension_semantics=("arbitrary",), vmem_limit_bytes=VMEM_LIMIT),
        name="experts",
    )(block_expert, n_valid.reshape(1), row0, blk_ord, padded // BLK, xs, w_gate[0], w_up[0], w_down[0])

    last = n_tok // TD - 1
    out = pl.pallas_call(
        _combine_kernel,
        grid=(n_tok // TD,),
        in_specs=[
            pl.BlockSpec((TD // TT, TOP_K, TT), lambda i: (i, 0, 0), memory_space=pltpu.SMEM),
            pl.BlockSpec((TD // TT, TOP_K, TT), lambda i: (jnp.minimum(i + 1, last), 0, 0), memory_space=pltpu.SMEM),
            pl.BlockSpec((TD * SLAB, LANES), lambda i: (i, 0)),
            pl.BlockSpec((TD, LANES), lambda i: (i, 0)),
            pl.BlockSpec(memory_space=pl.ANY),
            pl.BlockSpec((1, D), lambda i: (0, 0)),
            pl.BlockSpec((1, D), lambda i: (0, 0)),
        ],
        out_specs=pl.BlockSpec((TD, D), lambda i: (i, 0)),
        scratch_shapes=[pltpu.VMEM((2 * TOP_K * TD * SLAB, LANES), F32), pltpu.SemaphoreType.DMA((2,))],
        out_shape=jax.ShapeDtypeStruct((n_tok, D), F32),
        compiler_params=pltpu.CompilerParams(dimension_semantics=("arbitrary",)),
        name="combine",
    )(dest, dest, x1s, route, yb, ln2_g[0].reshape(1, D), ln2_b[0].reshape(1, D))

    return out.reshape(B, T, D).astype(x.dtype)
```

```python
import functools

import numpy as np
import jax
import jax.numpy as jnp
from jax import lax
from jax.experimental import pallas as pl
from jax.experimental.pallas import tpu as pltpu

D_MODEL = 1024
HG_WIDTH = 512
HG_HEAD_DIM = 128
HG_HEADS = 4
GL_WIDTH = 512
GL_HEADS = 4
GL_DV = 128
GL_DK = 64
GL_QK = 256
GATE_RANK = 16
GATE_TAU = 16.0
N_GROUPS = 8
EXPERTS_PER_GROUP = 8
N_EXPERTS = 64
TOP_K = 2
D_EXPERT = 512
DEPTH = 1
DEEPNORM_ALPHA = (2.0 * DEPTH) ** 0.25
LN_EPS = 1e-5
LOG2E = 1.4426950408889634
IN_SPLITS = (HG_WIDTH, HG_WIDTH, HG_WIDTH, HG_WIDTH, GL_QK, GL_QK, GL_WIDTH, GATE_RANK, GL_WIDTH)

LANES = 128
SUB = 128
N_LEVELS = 7
MXU_LEVELS = 3
TT = 256
TD = 512
BLK = 256
PROJ_COLS = 3712
ROUTE_BASE = 8
SLAB = 8
META_ROWS = 8
VMEM_LIMIT = 56 * 1024 * 1024

F32 = jnp.float32
BF16 = jnp.bfloat16
NT_DIMS = (((1,), (1,)), ((), ()))
TN_DIMS = (((0,), (0,)), ((), ()))


def _decay_matrices():
    t = np.arange(SUB)[:, None]
    j = np.arange(SUB)[None, :]
    mats = [(j <= t)]
    for lvl in range(1, MXU_LEVELS + 1):
        blk = 1 << lvl
        half = blk >> 1
        r = (t // blk) * blk + half - 1
        right = (t % blk) >= half
        mats.append(np.where(right, (j > r) & (j <= t), (j > t) & (j <= r)))
    w = np.concatenate(mats, axis=0).astype(np.float32)
    return np.concatenate([w, w, w], axis=1)


def _level_map():
    t = np.arange(SUB)[:, None]
    s = np.arange(SUB)[None, :]
    x = np.bitwise_xor(t, s)
    lvl = np.floor(np.log2(np.maximum(x, 1))).astype(np.int32) + 1
    lvl = np.where(x == 0, 0, lvl)
    return np.where(s > t, -1, lvl).astype(np.int32)


def _layer_norm(y, g, b):
    mu = jnp.mean(y, axis=-1, keepdims=True)
    yc = y - mu
    var = jnp.mean(yc * yc, axis=-1, keepdims=True)
    return yc * lax.rsqrt(var + LN_EPS) * g + b


def _mixer_kernel(x_ref, win_ref, wa2_ref, ba_ref, lbl_ref, gain_ref, wout_ref, ln1g_ref, ln1b_ref,
                  wrh_ref, wrl_ref, wstk_ref, lev_ref,
                  x1s_ref, route_ref, meta_ref, counts_ref,
                  p_scr, o_scr, b_scr, st_scr, carry_scr):
    bi = pl.program_id(0)
    ji = pl.program_id(1)

    @pl.when(ji == 0)
    def _():
        st_scr[...] = jnp.zeros_like(st_scr)

    @pl.when((bi == 0) & (ji == 0))
    def _():
        carry_scr[...] = jnp.zeros_like(carry_scr)

    xt = x_ref[...]
    p_scr[...] = jnp.dot(xt.astype(BF16), win_ref[...], preferred_element_type=F32)

    lbl = lbl_ref[...]
    lbe = jnp.exp(lbl - jnp.max(lbl, axis=0, keepdims=True))
    lb = lbe[0:1, :] / jnp.sum(lbe, axis=0, keepdims=True)

    lev = lev_ref[...]
    tcol = lax.broadcasted_iota(jnp.int32, (SUB, 1), 0)
    qrow = [(tcol & ((1 << l) - 1)) >= (1 << (l - 1)) for l in range(1, N_LEVELS + 1)]
    lane = lax.broadcasted_iota(jnp.int32, (1, LANES), 1)
    half_mask = [(lane < GL_DK), (lane >= GL_DK)]

    def by_halves(l, left_fn, right_fn):
        blk = 1 << l
        half = blk >> 1
        parts = []
        for j in range(SUB // blk):
            lo = j * blk
            parts += [left_fn(lo, lo + half), right_fn(lo + half, lo + blk)]
        return jnp.concatenate(parts, axis=0)

    def subtile(s, carry):
        r0 = pl.multiple_of(s * SUB, SUB)
        rows = pl.ds(r0, SUB)
        hq = p_scr[rows, 0:512]
        hf = p_scr[rows, 512:1024]
        hv = p_scr[rows, 1024:1536]
        hgate = p_scr[rows, 1536:2048]
        gq = p_scr[rows, 2048:2304]
        gk = p_scr[rows, 2304:2560]
        gv = p_scr[rows, 2560:3072]
        ggate = p_scr[rows, 3072:3584]
        ga = p_scr[rows, 3584:3712]

        f = lb + (1.0 - lb) * jax.nn.sigmoid(hf)
        kh = 1.0 - f
        lgh = jnp.log(f)
        z = jnp.dot(ga.astype(BF16), wa2_ref[...], preferred_element_type=F32) + ba_ref[...]
        lga = (jnp.minimum(z, 0.0) - jnp.log1p(jnp.exp(-jnp.abs(z)))) * (1.0 / GATE_TAU)
        lg = jnp.concatenate([lgh, lga], axis=1) * LOG2E

        l_hi = lg.astype(BF16)
        rem = lg - l_hi.astype(F32)
        l_mid = rem.astype(BF16)
        l_lo = (rem - l_mid.astype(F32)).astype(BF16)
        sums = jnp.dot(wstk_ref[...], jnp.concatenate([l_hi, l_mid, l_lo], axis=0),
                       preferred_element_type=F32)
        bcum = sums[0:SUB]
        b_scr[...] = bcum
        expos = [bcum, b_scr[SUB - 1:SUB, :] - bcum]
        for l in range(1, N_LEVELS + 1):
            if l <= MXU_LEVELS:
                expos.append(sums[l * SUB:(l + 1) * SUB])
            else:
                expos.append(by_halves(l, lambda lo, mid: b_scr[mid - 1:mid, :] - bcum[lo:mid],
                                       lambda mid, hi: bcum[mid:hi] - b_scr[mid - 1:mid, :]))
        decs = [jnp.exp2(e) for e in expos]

        def group_terms(q, k, c0):
            e_pre = decs[0][:, c0:c0 + LANES]
            e_suf = decs[1][:, c0:c0 + LANES]
            us = []
            for l in range(1, N_LEVELS + 1):
                e_l = decs[1 + l][:, c0:c0 + LANES]
                if l <= MXU_LEVELS:
                    sel = jnp.where(qrow[l - 1], q, k)
                else:
                    sel = by_halves(l, lambda lo, mid: k[lo:mid], lambda mid, hi: q[mid:hi])
                u = sel * e_l
                us.append((u, u.astype(BF16)))
            return us, (q * e_pre).astype(BF16), k * e_suf, e_pre[SUB - 1:SUB, :]

        def head(hidx, us, qd, kd, qk, v, g_last, mask):
            sc = jnp.where(lev == 0, jnp.sum(qk, axis=-1, keepdims=True), 0.0)
            for l in range(1, N_LEVELS + 1):
                u, ub = us[l - 1]
                if l <= MXU_LEVELS:
                    lhs = ub if mask is None else jnp.where(mask, ub, jnp.zeros_like(ub))
                    g = lax.dot_general(lhs, ub, NT_DIMS, preferred_element_type=F32)
                    sc = jnp.where(lev == l, g, sc)
                else:
                    blk = 1 << l
                    half = blk >> 1
                    mids = [j * blk + half for j in range(SUB // blk)]
                    uq = jnp.concatenate([u[m:m + half] for m in mids], axis=0)
                    if mask is not None:
                        uq = jnp.where(mask, uq, 0.0)
                    g = lax.dot_general(uq.astype(BF16), ub, NT_DIMS, preferred_element_type=F32)
                    parts = []
                    for n, m in enumerate(mids):
                        parts.append(sc[m - half:m])
                        parts.append(jnp.where(lev[m:m + half] == l, g[n * half:(n + 1) * half], sc[m:m + half]))
                    sc = jnp.concatenate(parts, axis=0)
            vb = v.astype(BF16)
            st = st_scr[hidx]
            o = jnp.dot(sc.astype(BF16), vb, preferred_element_type=F32)
            o = o + lax.dot_general(qd, st.astype(BF16), NT_DIMS, preferred_element_type=F32)
            ut = lax.dot_general(vb, kd.astype(BF16), TN_DIMS, preferred_element_type=F32)
            st_scr[hidx] = st * g_last + ut
            return o

        def finish(o, gate, c0):
            o = o * lax.rsqrt(jnp.mean(o * o, axis=-1, keepdims=True) + LN_EPS)
            o = o * gain_ref[:, c0:c0 + LANES] * (gate * jax.nn.sigmoid(gate))
            o_scr[rows, c0:c0 + LANES] = o

        for g in range(HG_HEADS):
            c0 = g * LANES
            q = hq[:, c0:c0 + LANES] * (HG_HEAD_DIM ** -0.5)
            k = kh[:, c0:c0 + LANES]
            us, qd, kd, g_last = group_terms(q, k, c0)
            o = head(g, us, qd, kd, q * k, hv[:, c0:c0 + LANES], g_last, None)
            finish(o, hgate[:, c0:c0 + LANES], c0)

        for p in range(GL_HEADS // 2):
            c0 = p * LANES
            q = gq[:, c0:c0 + LANES] * (GL_DK ** -0.5)
            k = gk[:, c0:c0 + LANES]
            us, qd, kd, g_last = group_terms(q, k, HG_WIDTH + c0)
            for hh in range(2):
                h = 2 * p + hh
                m = half_mask[hh]
                o = head(HG_HEADS + h, us, qd, jnp.where(m, kd, 0.0), jnp.where(m, q * k, 0.0),
                         gv[:, h * LANES:(h + 1) * LANES], g_last, m)
                finish(o, ggate[:, h * LANES:(h + 1) * LANES], HG_WIDTH + h * LANES)
        return carry

    lax.fori_loop(0, TT // SUB, subtile, 0, unroll=True)

    hmix = jnp.dot(o_scr[...].astype(BF16), wout_ref[...], preferred_element_type=F32)
    x1 = _layer_norm(DEEPNORM_ALPHA * xt + hmix, ln1g_ref[...], ln1b_ref[...])
    for s in range(SLAB):
        x1s_ref[pl.ds(s, TT, stride=SLAB), :] = x1[:, s * LANES:(s + 1) * LANES]

    x1h = x1.astype(BF16)
    x1l = (x1 - x1h.astype(F32)).astype(BF16)
    logits = (jnp.dot(x1h, wrh_ref[...], preferred_element_type=F32)
              + jnp.dot(x1l, wrh_ref[...], preferred_element_type=F32)
              + jnp.dot(x1h, wrl_ref[...], preferred_element_type=F32))

    lane_i = lax.broadcasted_iota(jnp.int32, (TT, LANES), 1)
    lane_f = lane_i.astype(F32)
    neg = jnp.float32(-jnp.inf)
    big = jnp.float32(1e9)

    gl = jnp.where(lane_i < N_GROUPS, logits, neg)
    gmax = jnp.max(gl, axis=-1, keepdims=True)
    gidx = jnp.min(jnp.where(gl == gmax, lane_f, big), axis=-1, keepdims=True)
    p_group = 1.0 / jnp.sum(jnp.exp(gl - gmax), axis=-1, keepdims=True)

    e_lo = ROUTE_BASE + EXPERTS_PER_GROUP * gidx
    el = jnp.where((lane_f >= e_lo) & (lane_f < e_lo + EXPERTS_PER_GROUP), logits, neg)
    m1 = jnp.max(el, axis=-1, keepdims=True)
    i1 = jnp.min(jnp.where(el == m1, lane_f, big), axis=-1, keepdims=True)
    el2 = jnp.where(lane_f == i1, neg, el)
    m2 = jnp.max(el2, axis=-1, keepdims=True)
    i2 = jnp.min(jnp.where(el2 == m2, lane_f, big), axis=-1, keepdims=True)
    r2 = jnp.exp(m2 - m1)
    gate1 = p_group / (1.0 + r2)
    gate2 = p_group * r2 / (1.0 + r2)

    hot1 = lane_f == i1
    hot2 = lane_f == i2
    cnt = (hot1 | hot2).astype(F32)
    tr = lax.broadcasted_iota(jnp.int32, (TT, TT), 0)
    tc = lax.broadcasted_iota(jnp.int32, (TT, TT), 1)
    strict = (tc < tr).astype(BF16)
    base = carry_scr[...] + jnp.dot(strict, cnt.astype(BF16), preferred_element_type=F32)
    rank1 = jnp.sum(jnp.where(hot1, base, 0.0), axis=-1, keepdims=True)
    rank2 = jnp.sum(jnp.where(hot2, base, 0.0), axis=-1, keepdims=True)
    total = carry_scr[...] + jnp.sum(cnt, axis=0, keepdims=True)
    carry_scr[...] = total
    counts_ref[...] = total

    route = jnp.where(lane_i == 0, i1 - ROUTE_BASE, 0.0)
    route = jnp.where(lane_i == 1, i2 - ROUTE_BASE, route)
    route = jnp.where(lane_i == 2, gate1, route)
    route = jnp.where(lane_i == 3, gate2, route)
    route = jnp.where(lane_i == 4, rank1, route)
    route = jnp.where(lane_i == 5, rank2, route)
    route_ref[...] = route
    meta_ref[...] = route.T[0:META_ROWS, :].astype(jnp.int32)


def _dispatch_kernel(dest_ref, x1s_ref, xs_ref, xin, zbuf, in_sems, out_sems, fill_sem):
    i = pl.program_id(0)
    n_steps = pl.num_programs(0)
    tile_rows = TD * SLAB

    def tile_load(t, slot):
        return pltpu.make_async_copy(x1s_ref.at[pl.ds(pl.multiple_of(t * tile_rows, tile_rows), tile_rows), :],
                                     xin.at[slot], in_sems.at[slot])

    def wait_rows(slot, sem):
        for k in range(TOP_K):
            pltpu.make_async_copy(xin.at[slot], xs_ref.at[pl.ds(0, tile_rows), :], sem).wait()

    @pl.when(i == 0)
    def _():
        tile_load(0, 0).start()
        zbuf[...] = jnp.zeros_like(zbuf)
        fill = pltpu.make_async_copy(zbuf, xs_ref.at[pl.ds(xs_ref.shape[0] - BLK * SLAB, BLK * SLAB), :], fill_sem)
        fill.start()
        fill.wait()

    slot = i % 3

    @pl.when(i + 1 < n_steps)
    def _():
        tile_load(i + 1, (i + 1) % 3).start()

    tile_load(i, slot).wait()
    for r in range(TD):
        for k in range(TOP_K):
            dst = pl.multiple_of(dest_ref[r // TT, k, r % TT] * SLAB, SLAB)
            pltpu.make_async_copy(xin.at[slot, pl.ds(r * SLAB, SLAB), :], xs_ref.at[pl.ds(dst, SLAB), :],
                                  out_sems.at[i % 2]).start(priority=k)

    @pl.when(i > 0)
    def _():
        wait_rows((i + 2) % 3, out_sems.at[(i + 1) % 2])

    @pl.when(i + 1 >= n_steps)
    def _():
        wait_rows(slot, out_sems.at[i % 2])


def _expert_kernel(be_ref, nv_ref, row0_ref, ord_ref, nblk_ref, xs_ref, wg_ref, wu_ref, wd_ref, yb_ref,
                   xbuf, wg_f, wu_f, wd_f, wg_s, wu_s, wd_s, sems, wsems):
    i = pl.program_id(0)
    nv = nv_ref[0]
    blk_rows = BLK * SLAB

    def block_copy(b, slot):
        r = pl.multiple_of(row0_ref[b] * SLAB, SLAB)
        return pltpu.make_async_copy(xs_ref.at[pl.ds(r, blk_rows), :],
                                     xbuf.at[pl.ds(slot * blk_rows, blk_rows), :], sems.at[slot])

    def weight_copies(e, ws):
        return [pltpu.make_async_copy(src.at[e], dst.at[ws], wsems.at[ws])
                for src, dst in ((wg_ref, wg_f), (wu_ref, wu_f), (wd_ref, wd_f))]

    @pl.when(i < nv)
    def _():
        slot = i % 2
        e = be_ref[i]
        ws = ord_ref[i] % 2

        @pl.when(i == 0)
        def _():
            block_copy(0, 0).start()
            for c in weight_copies(e, ws):
                c.start()

        @pl.when(i + 1 < nv)
        def _():
            block_copy(i + 1, 1 - slot).start()

        prev = be_ref[jnp.maximum(i - 1, 0)]

        @pl.when((i == 0) | (e != prev))
        def _():
            j = i + nblk_ref[e]

            @pl.when(j < nv)
            def _():
                for c in weight_copies(be_ref[jnp.minimum(j, nv - 1)], 1 - ws):
                    c.start()

            for c in weight_copies(e, ws):
                c.wait()
            wg_s[...] = wg_f[ws].astype(BF16)
            wu_s[...] = wu_f[ws].astype(BF16)
            wd_s[...] = wd_f[ws].astype(BF16)

        block_copy(i, slot).wait()
        base = pl.multiple_of(slot * blk_rows, blk_rows)
        xb = jnp.concatenate(
            [xbuf[pl.ds(base + s, BLK, stride=SLAB), :].astype(BF16) for s in range(SLAB)], axis=1)
        a = jnp.dot(xb, wg_s[...], preferred_element_type=F32)
        u = jnp.dot(xb, wu_s[...], preferred_element_type=F32)
        h = (a * jax.nn.sigmoid(a)) * u
        y = jnp.dot(h.astype(BF16), wd_s[...], preferred_element_type=F32)
        for s in range(SLAB):
            yb_ref[pl.ds(s, BLK, stride=SLAB), :] = y[:, s * LANES:(s + 1) * LANES]

    @pl.when(i >= nv)
    def _():
        yb_ref[...] = jnp.zeros_like(yb_ref)


def _combine_kernel(dest_ref, dest_next_ref, x1s_ref, route_ref, yb_ref, g_ref, b_ref, out_ref, ybuf, sems):
    i = pl.program_id(0)
    n_steps = pl.num_programs(0)
    half_rows = TD * SLAB
    par_rows = TOP_K * half_rows

    def start_gather(d_ref, par):
        for r in range(TD):
            for k in range(TOP_K):
                src = pl.multiple_of(d_ref[r // TT, k, r % TT] * SLAB, SLAB)
                dst = pl.multiple_of(par * par_rows + (k * TD + r) * SLAB, SLAB)
                pltpu.make_async_copy(yb_ref.at[pl.ds(src, SLAB), :], ybuf.at[pl.ds(dst, SLAB), :],
                                      sems.at[par]).start(priority=k)

    def wait_gather(par):
        pltpu.make_async_copy(yb_ref.at[pl.ds(0, par_rows), :],
                              ybuf.at[pl.ds(par * par_rows, par_rows), :], sems.at[par]).wait()

    par = i % 2

    @pl.when(i == 0)
    def _():
        start_gather(dest_ref, 0)

    wait_gather(par)
    start_gather(dest_next_ref, 1 - par)

    base = pl.multiple_of(par * par_rows, par_rows)
    g1 = route_ref[:, 2:3]
    g2 = route_ref[:, 3:4]
    zs = []
    for s in range(SLAB):
        y1 = ybuf[pl.ds(base + s, TD, stride=SLAB), :]
        y2 = ybuf[pl.ds(base + half_rows + s, TD, stride=SLAB), :]
        zs.append(DEEPNORM_ALPHA * x1s_ref[pl.ds(s, TD, stride=SLAB), :] + (g1 * y1 + g2 * y2))
    z = jnp.concatenate(zs, axis=1)
    out_ref[...] = _layer_norm(z, g_ref[...], b_ref[...])

    @pl.when(i + 1 >= n_steps)
    def _():
        wait_gather(1 - par)


def _const_spec(shape):
    nd = len(shape)
    return pl.BlockSpec(shape, lambda *_: (0,) * nd, pipeline_mode=pl.Buffered(1))


def kernel(x, w_in, w_a2, b_a, lb_logits, norm_h, norm_g, w_out, ln1_g, ln1_b, w_group_router, w_expert_router,
           w_gate, w_up, w_down, ln2_g, ln2_b):
    B, T, D = x.shape
    n_tok = B * T
    assert D == D_MODEL == SLAB * LANES and T % TT == 0 and TD % TT == 0 and n_tok % TD == 0
    n_blocks = n_tok * TOP_K // BLK + N_EXPERTS
    n_slots = n_blocks * BLK

    parts = jnp.split(w_in[0], [int(c) for c in np.cumsum(IN_SPLITS)[:-1]], axis=-1)
    hq, hf, hi, hg, gq, gk, gv, ga, gg = parts
    win = jnp.concatenate([hq, hf, hi, hg, gq, gk, gv, gg, ga], axis=-1)
    win = jnp.pad(win, ((0, 0), (0, PROJ_COLS - win.shape[1]))).astype(BF16)
    wa2 = jnp.pad(w_a2[0], ((0, LANES - GATE_RANK), (0, 0))).astype(BF16)
    ba = b_a[0].reshape(1, GL_QK).astype(F32)
    gain = jnp.concatenate([norm_h[0], norm_g[0]]).reshape(1, D).astype(F32)
    wout = w_out[0].astype(BF16)
    wr = jnp.concatenate([w_group_router[0], w_expert_router[0]], axis=-1).astype(F32)
    wr = jnp.pad(wr, ((0, 0), (0, LANES - wr.shape[1])))
    wrh = wr.astype(BF16)
    wrl = (wr - wrh.astype(F32)).astype(BF16)
    wstk = jnp.asarray(_decay_matrices(), dtype=BF16)
    lev = jnp.asarray(_level_map())
    n_w = wstk.shape[0]

    n_steps = n_tok // TT
    x1s, route, meta, counts = pl.pallas_call(
        _mixer_kernel,
        grid=(B, T // TT),
        in_specs=[
            pl.BlockSpec((None, TT, D), lambda b, j: (b, j, 0)),
            _const_spec((D, PROJ_COLS)),
            _const_spec((LANES, GL_QK)),
            _const_spec((1, GL_QK)),
            _const_spec((DEPTH + 1, HG_WIDTH)),
            _const_spec((1, D)),
            _const_spec((D, D)),
            _const_spec((1, D)),
            _const_spec((1, D)),
            _const_spec((D, LANES)),
            _const_spec((D, LANES)),
            _const_spec((n_w, 3 * SUB)),
            _const_spec((SUB, SUB)),
        ],
        out_specs=[
            pl.BlockSpec((TT * SLAB, LANES), lambda b, j: (b * (T // TT) + j, 0)),
            pl.BlockSpec((TT, LANES), lambda b, j: (b * (T // TT) + j, 0)),
            pl.BlockSpec((META_ROWS, TT), lambda b, j: (b * (T // TT) + j, 0)),
            pl.BlockSpec((1, LANES), lambda b, j: (0, 0)),
        ],
        out_shape=[
            jax.ShapeDtypeStruct((n_tok * SLAB, LANES), F32),
            jax.ShapeDtypeStruct((n_tok, LANES), F32),
            jax.ShapeDtypeStruct((n_steps * META_ROWS, TT), jnp.int32),
            jax.ShapeDtypeStruct((1, LANES), F32),
        ],
        scratch_shapes=[
            pltpu.VMEM((TT, PROJ_COLS), F32),
            pltpu.VMEM((TT, D), F32),
            pltpu.VMEM((SUB, HG_WIDTH + GL_QK), F32),
            pltpu.VMEM((HG_HEADS + GL_HEADS, LANES, LANES), F32),
            pltpu.VMEM((1, LANES), F32),
        ],
        compiler_params=pltpu.CompilerParams(
            dimension_semantics=("arbitrary", "arbitrary"), vmem_limit_bytes=VMEM_LIMIT),
        name="mixer",
    )(x, win, wa2, ba, lb_logits.astype(F32), gain, wout, ln1_g[0].reshape(1, D), ln1_b[0].reshape(1, D),
      wrh, wrl, wstk, lev)

    cnt = counts[0, ROUTE_BASE:ROUTE_BASE + N_EXPERTS].astype(jnp.int32)
    padded = (cnt + BLK - 1) // BLK * BLK
    ends = jnp.cumsum(padded)
    starts = ends - padded
    n_valid = (ends[-1] // BLK).astype(jnp.int32)
    blk_ids = jnp.minimum(jnp.arange(n_blocks, dtype=jnp.int32), n_valid - 1)
    block_expert = jnp.sum((ends[None, :] <= (blk_ids * BLK)[:, None]).astype(jnp.int32), axis=1)
    block_expert = jnp.minimum(block_expert, N_EXPERTS - 1)
    blk_ord = jnp.cumsum(jnp.concatenate(
        [jnp.zeros((1,), jnp.int32), (block_expert[1:] != block_expert[:-1]).astype(jnp.int32)]))
    dstarts = jnp.cumsum(cnt) - cnt
    first_blk = starts // BLK
    pick = block_expert[None, :] == jnp.arange(N_EXPERTS, dtype=jnp.int32)[:, None]
    row0 = jnp.sum(jnp.where(pick, (dstarts - first_blk * BLK)[:, None], 0), axis=0) + blk_ids * BLK
    meta3 = meta.reshape(n_steps, META_ROWS, TT)
    hit = meta3[None, :, 0:TOP_K, :] == jnp.arange(N_EXPERTS, dtype=jnp.int32)[:, None, None, None]
    rank = meta3[:, 4:4 + TOP_K, :]
    dest_x = rank + jnp.sum(jnp.where(hit, dstarts[:, None, None, None], 0), axis=0)
    dest = rank + jnp.sum(jnp.where(hit, starts[:, None, None, None], 0), axis=0)

    xs_rows = (n_tok * TOP_K + BLK) * SLAB
    xs = pl.pallas_call(
        _dispatch_kernel,
        grid=(n_tok // TD,),
        in_specs=[
            pl.BlockSpec((TD // TT, TOP_K, TT), lambda i: (i, 0, 0), memory_space=pltpu.SMEM),
            pl.BlockSpec(memory_space=pl.ANY),
        ],
        out_specs=pl.BlockSpec(memory_space=pl.ANY),
        out_shape=jax.ShapeDtypeStruct((xs_rows, LANES), F32),
        scratch_shapes=[pltpu.VMEM((3, TD * SLAB, LANES), F32), pltpu.VMEM((BLK * SLAB, LANES), F32),
                        pltpu.SemaphoreType.DMA((3,)), pltpu.SemaphoreType.DMA((2,)), pltpu.SemaphoreType.DMA],
        compiler_params=pltpu.CompilerParams(dimension_semantics=("arbitrary",)),
        name="dispatch",
    )(dest_x, x1s)

    yb = pl.pallas_call(
        _expert_kernel,
        grid_spec=pltpu.PrefetchScalarGridSpec(
            num_scalar_prefetch=5,
            grid=(n_blocks,),
            in_specs=[pl.BlockSpec(memory_space=pl.ANY)] * 4,
            out_specs=pl.BlockSpec((BLK * SLAB, LANES), lambda i, *_: (i, 0)),
            scratch_shapes=[
                pltpu.VMEM((2 * BLK * SLAB, LANES), F32),
                pltpu.VMEM((2, D, D_EXPERT), F32),
                pltpu.VMEM((2, D, D_EXPERT), F32),
                pltpu.VMEM((2, D_EXPERT, D), F32),
                pltpu.VMEM((D, D_EXPERT), BF16),
                pltpu.VMEM((D, D_EXPERT), BF16),
                pltpu.VMEM((D_EXPERT, D), BF16),
                pltpu.SemaphoreType.DMA((2,)),
                pltpu.SemaphoreType.DMA((2,)),
            ],
        ),
        out_shape=jax.ShapeDtypeStruct((n_slots * SLAB, LANES), F32),
        compiler_params=pltpu.CompilerParams(
            dimension_semantics=("arbitrary",), vmem_limit_bytes=VMEM_LIMIT),
        name="experts",
    )(block_expert, n_valid.reshape(1), row0, blk_ord, padded // BLK, xs, w_gate[0], w_up[0], w_down[0])

    last = n_tok // TD - 1
    out = pl.pallas_call(
        _combine_kernel,
        grid=(n_tok // TD,),
        in_specs=[
            pl.BlockSpec((TD // TT, TOP_K, TT), lambda i: (i, 0, 0), memory_space=pltpu.SMEM),
            pl.BlockSpec((TD // TT, TOP_K, TT), lambda i: (jnp.minimum(i + 1, last), 0, 0), memory_space=pltpu.SMEM),
            pl.BlockSpec((TD * SLAB, LANES), lambda i: (i, 0)),
            pl.BlockSpec((TD, LANES), lambda i: (i, 0)),
            pl.BlockSpec(memory_space=pl.ANY),
            pl.BlockSpec((1, D), lambda i: (0, 0)),
            pl.BlockSpec((1, D), lambda i: (0, 0)),
        ],
        out_specs=pl.BlockSpec((TD, D), lambda i: (i, 0)),
        scratch_shapes=[pltpu.VMEM((2 * TOP_K * TD * SLAB, LANES), F32), pltpu.SemaphoreType.DMA((2,))],
        out_shape=jax.ShapeDtypeStruct((n_tok, D), F32),
        compiler_params=pltpu.CompilerParams(dimension_semantics=("arbitrary",)),
        name="combine",
    )(dest, dest, x1s, route, yb, ln2_g[0].reshape(1, D), ln2_b[0].reshape(1, D))

    return out.reshape(B, T, D).astype(x.dtype)
```

```python
import functools

import numpy as np
import jax
import jax.numpy as jnp
from jax import lax
from jax.experimental import pallas as pl
from jax.experimental.pallas import tpu as pltpu

D_MODEL = 1024
HG_WIDTH = 512
HG_HEAD_DIM = 128
HG_HEADS = 4
GL_WIDTH = 512
GL_HEADS = 4
GL_DV = 128
GL_DK = 64
GL_QK = 256
GATE_RANK = 16
GATE_TAU = 16.0
N_GROUPS = 8
EXPERTS_PER_GROUP = 8
N_EXPERTS = 64
TOP_K = 2
D_EXPERT = 512
DEPTH = 1
DEEPNORM_ALPHA = (2.0 * DEPTH) ** 0.25
LN_EPS = 1e-5
LOG2E = 1.4426950408889634
IN_SPLITS = (HG_WIDTH, HG_WIDTH, HG_WIDTH, HG_WIDTH, GL_QK, GL_QK, GL_WIDTH, GATE_RANK, GL_WIDTH)

LANES = 128
SUB = 128
N_LEVELS = 7
MXU_LEVELS = 3
TT = 256
TD = 512
BLK = 256
PROJ_COLS = 3712
ROUTE_BASE = 8
SLAB = 8
META_ROWS = 8
VMEM_LIMIT = 56 * 1024 * 1024

F32 = jnp.float32
BF16 = jnp.bfloat16
NT_DIMS = (((1,), (1,)), ((), ()))
TN_DIMS = (((0,), (0,)), ((), ()))


def _decay_matrices():
    t = np.arange(SUB)[:, None]
    j = np.arange(SUB)[None, :]
    mats = [(j <= t)]
    for lvl in range(1, MXU_LEVELS + 1):
        blk = 1 << lvl
        half = blk >> 1
        r = (t // blk) * blk + half - 1
        right = (t % blk) >= half
        mats.append(np.where(right, (j > r) & (j <= t), (j > t) & (j <= r)))
    w = np.concatenate(mats, axis=0).astype(np.float32)
    return np.concatenate([w, w, w], axis=1)


def _level_map():
    t = np.arange(SUB)[:, None]
    s = np.arange(SUB)[None, :]
    x = np.bitwise_xor(t, s)
    lvl = np.floor(np.log2(np.maximum(x, 1))).astype(np.int32) + 1
    lvl = np.where(x == 0, 0, lvl)
    return np.where(s > t, -1, lvl).astype(np.int32)


def _layer_norm(y, g, b):
    mu = jnp.mean(y, axis=-1, keepdims=True)
    yc = y - mu
    var = jnp.mean(yc * yc, axis=-1, keepdims=True)
    return yc * lax.rsqrt(var + LN_EPS) * g + b


def _mixer_kernel(x_ref, win_ref, wa2_ref, ba_ref, lbl_ref, gain_ref, wout_ref, ln1g_ref, ln1b_ref,
                  wr_ref, wstk_ref, lev_ref,
                  x1s_ref, route_ref, meta_ref, counts_ref,
                  p_scr, o_scr, b_scr, st_scr, carry_scr):
    bi = pl.program_id(0)
    ji = pl.program_id(1)

    @pl.when(ji == 0)
    def _():
        st_scr[...] = jnp.zeros_like(st_scr)

    @pl.when((bi == 0) & (ji == 0))
    def _():
        carry_scr[...] = jnp.zeros_like(carry_scr)

    xt = x_ref[...]
    p_scr[...] = jnp.dot(xt.astype(BF16), win_ref[...], preferred_element_type=F32)

    lbl = lbl_ref[...]
    lbe = jnp.exp(lbl - jnp.max(lbl, axis=0, keepdims=True))
    lb = lbe[0:1, :] / jnp.sum(lbe, axis=0, keepdims=True)

    lev = lev_ref[...]
    tcol = lax.broadcasted_iota(jnp.int32, (SUB, 1), 0)
    qrow = [(tcol & ((1 << l) - 1)) >= (1 << (l - 1)) for l in range(1, N_LEVELS + 1)]
    lane = lax.broadcasted_iota(jnp.int32, (1, LANES), 1)
    half_mask = [(lane < GL_DK), (lane >= GL_DK)]

    def by_halves(l, left_fn, right_fn):
        blk = 1 << l
        half = blk >> 1
        parts = []
        for j in range(SUB // blk):
            lo = j * blk
            parts += [left_fn(lo, lo + half), right_fn(lo + half, lo + blk)]
        return jnp.concatenate(parts, axis=0)

    def subtile(s, carry):
        r0 = pl.multiple_of(s * SUB, SUB)
        rows = pl.ds(r0, SUB)
        hq = p_scr[rows, 0:512]
        hf = p_scr[rows, 512:1024]
        hv = p_scr[rows, 1024:1536]
        hgate = p_scr[rows, 1536:2048]
        gq = p_scr[rows, 2048:2304]
        gk = p_scr[rows, 2304:2560]
        gv = p_scr[rows, 2560:3072]
        ggate = p_scr[rows, 3072:3584]
        ga = p_scr[rows, 3584:3712]

        f = lb + (1.0 - lb) * jax.nn.sigmoid(hf)
        kh = 1.0 - f
        lgh = jnp.log(f)
        z = jnp.dot(ga.astype(BF16), wa2_ref[...], preferred_element_type=F32) + ba_ref[...]
        lga = (jnp.minimum(z, 0.0) - jnp.log1p(jnp.exp(-jnp.abs(z)))) * (1.0 / GATE_TAU)
        lg = jnp.concatenate([lgh, lga], axis=1) * LOG2E

        l_hi = lg.astype(BF16)
        rem = lg - l_hi.astype(F32)
        l_mid = rem.astype(BF16)
        l_lo = (rem - l_mid.astype(F32)).astype(BF16)
        sums = jnp.dot(wstk_ref[...], jnp.concatenate([l_hi, l_mid, l_lo], axis=0),
                       preferred_element_type=F32)
        bcum = sums[0:SUB]
        b_scr[...] = bcum
        expos = [bcum, b_scr[SUB - 1:SUB, :] - bcum]
        for l in range(1, N_LEVELS + 1):
            if l <= MXU_LEVELS:
                expos.append(sums[l * SUB:(l + 1) * SUB])
            else:
                expos.append(by_halves(l, lambda lo, mid: b_scr[mid - 1:mid, :] - bcum[lo:mid],
                                       lambda mid, hi: bcum[mid:hi] - b_scr[mid - 1:mid, :]))
        decs = [jnp.exp2(e) for e in expos]

        def group_terms(q, k, c0):
            e_pre = decs[0][:, c0:c0 + LANES]
            e_suf = decs[1][:, c0:c0 + LANES]
            us = []
            for l in range(1, N_LEVELS + 1):
                e_l = decs[1 + l][:, c0:c0 + LANES]
                if l <= MXU_LEVELS:
                    sel = jnp.where(qrow[l - 1], q, k)
                else:
                    sel = by_halves(l, lambda lo, mid: k[lo:mid], lambda mid, hi: q[mid:hi])
                u = sel * e_l
                us.append((u, u.astype(BF16)))
            return us, (q * e_pre).astype(BF16), k * e_suf, e_pre[SUB - 1:SUB, :]

        def scores(us_ab, qks_ab, masks):
            nm = len(masks)
            zero_b = jnp.zeros((SUB, LANES), BF16)
            scs = [[jnp.where(lev == 0, jnp.sum(qk, axis=-1, keepdims=True), 0.0) for qk in qks] for qks in qks_ab]
            for l in range(1, N_LEVELS + 1):
                (ua, uba), (ub_, ubb) = us_ab[0][l - 1], us_ab[1][l - 1]
                rhs = jnp.concatenate([jnp.concatenate([uba, zero_b], axis=1),
                                       jnp.concatenate([zero_b, ubb], axis=1)], axis=0)
                if l <= MXU_LEVELS:
                    rows_l = SUB
                    lhs = jnp.concatenate(
                        [jnp.concatenate([x if m is None else jnp.where(m, x, jnp.zeros_like(x)) for x in (uba, ubb)],
                                         axis=1) for m in masks], axis=0)
                    g = lax.dot_general(lhs, rhs, NT_DIMS, preferred_element_type=F32)
                    for gi in range(2):
                        for n in range(nm):
                            gn = g[n * rows_l:(n + 1) * rows_l, gi * LANES:(gi + 1) * LANES]
                            scs[gi][n] = jnp.where(lev == l, gn, scs[gi][n])
                else:
                    blk = 1 << l
                    half = blk >> 1
                    mids = [j * blk + half for j in range(SUB // blk)]
                    rows_l = SUB // 2
                    uqs = [jnp.concatenate([x[m:m + half] for m in mids], axis=0) for x in (ua, ub_)]
                    lhs = jnp.concatenate(
                        [jnp.concatenate([x if m is None else jnp.where(m, x, 0.0) for x in uqs], axis=1)
                         for m in masks], axis=0)
                    g = lax.dot_general(lhs.astype(BF16), rhs, NT_DIMS, preferred_element_type=F32)
                    for gi in range(2):
                        for n in range(nm):
                            parts = []
                            for j, m in enumerate(mids):
                                gj = g[n * rows_l + j * half:n * rows_l + (j + 1) * half, gi * LANES:(gi + 1) * LANES]
                                parts.append(scs[gi][n][m - half:m])
                                parts.append(jnp.where(lev[m:m + half] == l, gj, scs[gi][n][m:m + half]))
                            scs[gi][n] = jnp.concatenate(parts, axis=0)
            return scs

        def block_diag(a, b):
            za = jnp.zeros_like(a)
            return jnp.concatenate([jnp.concatenate([a, za], axis=1), jnp.concatenate([za, b], axis=1)], axis=0)

        def head_pair(hidx, scs2, qds, kds, vs, g_lasts):
            vb = [v.astype(BF16) for v in vs]
            sts = [st_scr[h] for h in hidx]
            o = jnp.dot(jnp.concatenate([s.astype(BF16) for s in scs2], axis=1), block_diag(*vb),
                        preferred_element_type=F32)
            o = o + lax.dot_general(jnp.concatenate(qds, axis=1), block_diag(*[s.astype(BF16) for s in sts]),
                                    NT_DIMS, preferred_element_type=F32)
            ut = lax.dot_general(jnp.concatenate(vb, axis=1), jnp.concatenate([k.astype(BF16) for k in kds], axis=1),
                                 TN_DIMS, preferred_element_type=F32)
            for n, h in enumerate(hidx):
                st_scr[h] = sts[n] * g_lasts[n] + ut[n * LANES:(n + 1) * LANES, n * LANES:(n + 1) * LANES]
            return o[:, :LANES], o[:, LANES:]

        def finish(o, gate, c0):
            o = o * lax.rsqrt(jnp.mean(o * o, axis=-1, keepdims=True) + LN_EPS)
            o = o * gain_ref[:, c0:c0 + LANES] * (gate * jax.nn.sigmoid(gate))
            o_scr[rows, c0:c0 + LANES] = o

        for g0 in range(0, HG_HEADS, 2):
            terms = []
            for g in (g0, g0 + 1):
                c0 = g * LANES
                q = hq[:, c0:c0 + LANES] * (HG_HEAD_DIM ** -0.5)
                k = kh[:, c0:c0 + LANES]
                terms.append((q * k,) + group_terms(q, k, c0))
            scs = scores([t[1] for t in terms], [[t[0]] for t in terms], [None])
            cols = [slice(g * LANES, (g + 1) * LANES) for g in (g0, g0 + 1)]
            outs = head_pair((g0, g0 + 1), [scs[0][0], scs[1][0]], [t[2] for t in terms], [t[3] for t in terms],
                             [hv[:, c] for c in cols], [t[4] for t in terms])
            for o, c in zip(outs, cols):
                finish(o, hgate[:, c], c.start)

        terms = []
        for p in range(GL_HEADS // 2):
            c0 = p * LANES
            q = gq[:, c0:c0 + LANES] * (GL_DK ** -0.5)
            k = gk[:, c0:c0 + LANES]
            terms.append((q * k,) + group_terms(q, k, HG_WIDTH + c0))
        scs = scores([t[1] for t in terms], [[jnp.where(m, t[0], 0.0) for m in half_mask] for t in terms], half_mask)
        for p in range(GL_HEADS // 2):
            _, _, qd, kd, g_last = terms[p]
            cols = [slice(h * LANES, (h + 1) * LANES) for h in (2 * p, 2 * p + 1)]
            outs = head_pair((HG_HEADS + 2 * p, HG_HEADS + 2 * p + 1), scs[p], [qd, qd],
                             [jnp.where(m, kd, 0.0) for m in half_mask], [gv[:, c] for c in cols], [g_last, g_last])
            for o, c in zip(outs, cols):
                finish(o, ggate[:, c], HG_WIDTH + c.start)
        return carry

    lax.fori_loop(0, TT // SUB, subtile, 0, unroll=True)

    hmix = jnp.dot(o_scr[...].astype(BF16), wout_ref[...], preferred_element_type=F32)
    x1 = _layer_norm(DEEPNORM_ALPHA * xt + hmix, ln1g_ref[...], ln1b_ref[...])
    for s in range(SLAB):
        x1s_ref[pl.ds(s, TT, stride=SLAB), :] = x1[:, s * LANES:(s + 1) * LANES]

    x1h = x1.astype(BF16)
    x1l = (x1 - x1h.astype(F32)).astype(BF16)
    prod = jnp.dot(jnp.concatenate([x1h, x1l], axis=0), wr_ref[...], preferred_element_type=F32)
    logits = prod[:TT, :LANES] + prod[TT:, :LANES] + prod[:TT, LANES:]

    lane_i = lax.broadcasted_iota(jnp.int32, (TT, LANES), 1)
    lane_f = lane_i.astype(F32)
    neg = jnp.float32(-jnp.inf)
    big = jnp.float32(1e9)

    gl = jnp.where(lane_i < N_GROUPS, logits, neg)
    gmax = jnp.max(gl, axis=-1, keepdims=True)
    gidx = jnp.min(jnp.where(gl == gmax, lane_f, big), axis=-1, keepdims=True)
    p_group = 1.0 / jnp.sum(jnp.exp(gl - gmax), axis=-1, keepdims=True)

    e_lo = ROUTE_BASE + EXPERTS_PER_GROUP * gidx
    el = jnp.where((lane_f >= e_lo) & (lane_f < e_lo + EXPERTS_PER_GROUP), logits, neg)
    m1 = jnp.max(el, axis=-1, keepdims=True)
    i1 = jnp.min(jnp.where(el == m1, lane_f, big), axis=-1, keepdims=True)
    el2 = jnp.where(lane_f == i1, neg, el)
    m2 = jnp.max(el2, axis=-1, keepdims=True)
    i2 = jnp.min(jnp.where(el2 == m2, lane_f, big), axis=-1, keepdims=True)
    r2 = jnp.exp(m2 - m1)
    gate1 = p_group / (1.0 + r2)
    gate2 = p_group * r2 / (1.0 + r2)

    hot1 = lane_f == i1
    hot2 = lane_f == i2
    cnt = (hot1 | hot2).astype(F32)
    tr = lax.broadcasted_iota(jnp.int32, (TT, TT), 0)
    tc = lax.broadcasted_iota(jnp.int32, (TT, TT), 1)
    strict = (tc < tr).astype(BF16)
    base = carry_scr[...] + jnp.dot(strict, cnt.astype(BF16), preferred_element_type=F32)
    rank1 = jnp.sum(jnp.where(hot1, base, 0.0), axis=-1, keepdims=True)
    rank2 = jnp.sum(jnp.where(hot2, base, 0.0), axis=-1, keepdims=True)
    total = carry_scr[...] + jnp.sum(cnt, axis=0, keepdims=True)
    carry_scr[...] = total
    counts_ref[...] = total

    route = jnp.where(lane_i == 0, i1 - ROUTE_BASE, 0.0)
    route = jnp.where(lane_i == 1, i2 - ROUTE_BASE, route)
    route = jnp.where(lane_i == 2, gate1, route)
    route = jnp.where(lane_i == 3, gate2, route)
    route = jnp.where(lane_i == 4, rank1, route)
    route = jnp.where(lane_i == 5, rank2, route)
    route_ref[...] = route
    meta_ref[...] = route.T[0:META_ROWS, :].astype(jnp.int32)


def _dispatch_kernel(dest_ref, x1s_ref, xs_ref, xin, zbuf, in_sems, out_sems, fill_sem):
    i = pl.program_id(0)
    n_steps = pl.num_programs(0)
    tile_rows = TD * SLAB

    def tile_load(t, slot):
        return pltpu.make_async_copy(x1s_ref.at[pl.ds(pl.multiple_of(t * tile_rows, tile_rows), tile_rows), :],
                                     xin.at[slot], in_sems.at[slot])

    def wait_rows(slot, sem):
        for k in range(TOP_K):
            pltpu.make_async_copy(xin.at[slot], xs_ref.at[pl.ds(0, tile_rows), :], sem).wait()

    @pl.when(i == 0)
    def _():
        tile_load(0, 0).start()
        zbuf[...] = jnp.zeros_like(zbuf)
        fill = pltpu.make_async_copy(zbuf, xs_ref.at[pl.ds(xs_ref.shape[0] - BLK * SLAB, BLK * SLAB), :], fill_sem)
        fill.start()
        fill.wait()

    slot = i % 3

    @pl.when(i + 1 < n_steps)
    def _():
        tile_load(i + 1, (i + 1) % 3).start()

    tile_load(i, slot).wait()
    for r in range(TD):
        for k in range(TOP_K):
            dst = pl.multiple_of(dest_ref[r // TT, k, r % TT] * SLAB, SLAB)
            pltpu.make_async_copy(xin.at[slot, pl.ds(r * SLAB, SLAB), :], xs_ref.at[pl.ds(dst, SLAB), :],
                                  out_sems.at[i % 2]).start(priority=k)

    @pl.when(i > 0)
    def _():
        wait_rows((i + 2) % 3, out_sems.at[(i + 1) % 2])

    @pl.when(i + 1 >= n_steps)
    def _():
        wait_rows(slot, out_sems.at[i % 2])


def _expert_kernel(be_ref, nv_ref, row0_ref, ord_ref, nblk_ref, xs_ref, wg_ref, wu_ref, wd_ref, yb_ref,
                   xbuf, wg_f, wu_f, wd_f, wg_s, wu_s, wd_s, sems, wsems):
    i = pl.program_id(0)
    nv = nv_ref[0]
    blk_rows = BLK * SLAB

    def block_copy(b, slot):
        r = pl.multiple_of(row0_ref[b] * SLAB, SLAB)
        return pltpu.make_async_copy(xs_ref.at[pl.ds(r, blk_rows), :],
                                     xbuf.at[pl.ds(slot * blk_rows, blk_rows), :], sems.at[slot])

    def weight_copies(e, ws):
        return [pltpu.make_async_copy(src.at[e], dst.at[ws], wsems.at[ws])
                for src, dst in ((wg_ref, wg_f), (wu_ref, wu_f), (wd_ref, wd_f))]

    @pl.when(i < nv)
    def _():
        slot = i % 2
        e = be_ref[i]
        ws = ord_ref[i] % 2

        @pl.when(i == 0)
        def _():
            block_copy(0, 0).start()
            for c in weight_copies(e, ws):
                c.start()

        @pl.when(i + 1 < nv)
        def _():
            block_copy(i + 1, 1 - slot).start()

        prev = be_ref[jnp.maximum(i - 1, 0)]

        @pl.when((i == 0) | (e != prev))
        def _():
            j = i + nblk_ref[e]

            @pl.when(j < nv)
            def _():
                for c in weight_copies(be_ref[jnp.minimum(j, nv - 1)], 1 - ws):
                    c.start()

            for c in weight_copies(e, ws):
                c.wait()
            wg_s[...] = wg_f[ws].astype(BF16)
            wu_s[...] = wu_f[ws].astype(BF16)
            wd_s[...] = wd_f[ws].astype(BF16)

        block_copy(i, slot).wait()
        base = pl.multiple_of(slot * blk_rows, blk_rows)
        xb = jnp.concatenate(
            [xbuf[pl.ds(base + s, BLK, stride=SLAB), :].astype(BF16) for s in range(SLAB)], axis=1)
        a = jnp.dot(xb, wg_s[...], preferred_element_type=F32)
        u = jnp.dot(xb, wu_s[...], preferred_element_type=F32)
        h = (a * jax.nn.sigmoid(a)) * u
        y = jnp.dot(h.astype(BF16), wd_s[...], preferred_element_type=F32)
        for s in range(SLAB):
            yb_ref[pl.ds(s, BLK, stride=SLAB), :] = y[:, s * LANES:(s + 1) * LANES]

    @pl.when(i >= nv)
    def _():
        yb_ref[...] = jnp.zeros_like(yb_ref)


def _combine_kernel(dest_ref, dest_next_ref, x1s_ref, route_ref, yb_ref, g_ref, b_ref, out_ref, ybuf, sems):
    i = pl.program_id(0)
    n_steps = pl.num_programs(0)
    half_rows = TD * SLAB
    par_rows = TOP_K * half_rows

    def start_gather(d_ref, par):
        for r in range(TD):
            for k in range(TOP_K):
                src = pl.multiple_of(d_ref[r // TT, k, r % TT] * SLAB, SLAB)
                dst = pl.multiple_of(par * par_rows + (k * TD + r) * SLAB, SLAB)
                pltpu.make_async_copy(yb_ref.at[pl.ds(src, SLAB), :], ybuf.at[pl.ds(dst, SLAB), :],
                                      sems.at[par]).start(priority=k)

    def wait_gather(par):
        pltpu.make_async_copy(yb_ref.at[pl.ds(0, par_rows), :],
                              ybuf.at[pl.ds(par * par_rows, par_rows), :], sems.at[par]).wait()

    par = i % 2

    @pl.when(i == 0)
    def _():
        start_gather(dest_ref, 0)

    wait_gather(par)
    start_gather(dest_next_ref, 1 - par)

    base = pl.multiple_of(par * par_rows, par_rows)
    g1 = route_ref[:, 2:3]
    g2 = route_ref[:, 3:4]
    zs = []
    for s in range(SLAB):
        y1 = ybuf[pl.ds(base + s, TD, stride=SLAB), :]
        y2 = ybuf[pl.ds(base + half_rows + s, TD, stride=SLAB), :]
        zs.append(DEEPNORM_ALPHA * x1s_ref[pl.ds(s, TD, stride=SLAB), :] + (g1 * y1 + g2 * y2))
    z = jnp.concatenate(zs, axis=1)
    out_ref[...] = _layer_norm(z, g_ref[...], b_ref[...])

    @pl.when(i + 1 >= n_steps)
    def _():
        wait_gather(1 - par)


def _const_spec(shape):
    nd = len(shape)
    return pl.BlockSpec(shape, lambda *_: (0,) * nd, pipeline_mode=pl.Buffered(1))


def kernel(x, w_in, w_a2, b_a, lb_logits, norm_h, norm_g, w_out, ln1_g, ln1_b, w_group_router, w_expert_router,
           w_gate, w_up, w_down, ln2_g, ln2_b):
    B, T, D = x.shape
    n_tok = B * T
    assert D == D_MODEL == SLAB * LANES and T % TT == 0 and TD % TT == 0 and n_tok % TD == 0
    n_blocks = n_tok * TOP_K // BLK + N_EXPERTS
    n_slots = n_blocks * BLK

    parts = jnp.split(w_in[0], [int(c) for c in np.cumsum(IN_SPLITS)[:-1]], axis=-1)
    hq, hf, hi, hg, gq, gk, gv, ga, gg = parts
    win = jnp.concatenate([hq, hf, hi, hg, gq, gk, gv, gg, ga], axis=-1)
    win = jnp.pad(win, ((0, 0), (0, PROJ_COLS - win.shape[1]))).astype(BF16)
    wa2 = jnp.pad(w_a2[0], ((0, LANES - GATE_RANK), (0, 0))).astype(BF16)
    ba = b_a[0].reshape(1, GL_QK).astype(F32)
    gain = jnp.concatenate([norm_h[0], norm_g[0]]).reshape(1, D).astype(F32)
    wout = w_out[0].astype(BF16)
    wr = jnp.concatenate([w_group_router[0], w_expert_router[0]], axis=-1).astype(F32)
    wr = jnp.pad(wr, ((0, 0), (0, LANES - wr.shape[1])))
    wrh = wr.astype(BF16)
    wr2 = jnp.concatenate([wrh, (wr - wrh.astype(F32)).astype(BF16)], axis=1)
    wstk = jnp.asarray(_decay_matrices(), dtype=BF16)
    lev = jnp.asarray(_level_map())
    n_w = wstk.shape[0]

    n_steps = n_tok // TT
    x1s, route, meta, counts = pl.pallas_call(
        _mixer_kernel,
        grid=(B, T // TT),
        in_specs=[
            pl.BlockSpec((None, TT, D), lambda b, j: (b, j, 0)),
            _const_spec((D, PROJ_COLS)),
            _const_spec((LANES, GL_QK)),
            _const_spec((1, GL_QK)),
            _const_spec((DEPTH + 1, HG_WIDTH)),
            _const_spec((1, D)),
            _const_spec((D, D)),
            _const_spec((1, D)),
            _const_spec((1, D)),
            _const_spec((D, 2 * LANES)),
            _const_spec((n_w, 3 * SUB)),
            _const_spec((SUB, SUB)),
        ],
        out_specs=[
            pl.BlockSpec((TT * SLAB, LANES), lambda b, j: (b * (T // TT) + j, 0)),
            pl.BlockSpec((TT, LANES), lambda b, j: (b * (T // TT) + j, 0)),
            pl.BlockSpec((META_ROWS, TT), lambda b, j: (b * (T // TT) + j, 0)),
            pl.BlockSpec((1, LANES), lambda b, j: (0, 0)),
        ],
        out_shape=[
            jax.ShapeDtypeStruct((n_tok * SLAB, LANES), F32),
            jax.ShapeDtypeStruct((n_tok, LANES), F32),
            jax.ShapeDtypeStruct((n_steps * META_ROWS, TT), jnp.int32),
            jax.ShapeDtypeStruct((1, LANES), F32),
        ],
        scratch_shapes=[
            pltpu.VMEM((TT, PROJ_COLS), F32),
            pltpu.VMEM((TT, D), F32),
            pltpu.VMEM((SUB, HG_WIDTH + GL_QK), F32),
            pltpu.VMEM((HG_HEADS + GL_HEADS, LANES, LANES), F32),
            pltpu.VMEM((1, LANES), F32),
        ],
        compiler_params=pltpu.CompilerParams(
            dimension_semantics=("arbitrary", "arbitrary"), vmem_limit_bytes=VMEM_LIMIT),
        name="mixer",
    )(x, win, wa2, ba, lb_logits.astype(F32), gain, wout, ln1_g[0].reshape(1, D), ln1_b[0].reshape(1, D),
      wr2, wstk, lev)

    cnt = counts[0, ROUTE_BASE:ROUTE_BASE + N_EXPERTS].astype(jnp.int32)
    padded = (cnt + BLK - 1) // BLK * BLK
    ends = jnp.cumsum(padded)
    starts = ends - padded
    n_valid = (ends[-1] // BLK).astype(jnp.int32)
    blk_ids = jnp.minimum(jnp.arange(n_blocks, dtype=jnp.int32), n_valid - 1)
    block_expert = jnp.sum((ends[None, :] <= (blk_ids * BLK)[:, None]).astype(jnp.int32), axis=1)
    block_expert = jnp.minimum(block_expert, N_EXPERTS - 1)
    blk_ord = jnp.cumsum(jnp.concatenate(
        [jnp.zeros((1,), jnp.int32), (block_expert[1:] != block_expert[:-1]).astype(jnp.int32)]))
    dstarts = jnp.cumsum(cnt) - cnt
    first_blk = starts // BLK
    pick = block_expert[None, :] == jnp.arange(N_EXPERTS, dtype=jnp.int32)[:, None]
    row0 = jnp.sum(jnp.where(pick, (dstarts - first_blk * BLK)[:, None], 0), axis=0) + blk_ids * BLK
    meta3 = meta.reshape(n_steps, META_ROWS, TT)
    hit = meta3[None, :, 0:TOP_K, :] == jnp.arange(N_EXPERTS, dtype=jnp.int32)[:, None, None, None]
    rank = meta3[:, 4:4 + TOP_K, :]
    dest_x = rank + jnp.sum(jnp.where(hit, dstarts[:, None, None, None], 0), axis=0)
    dest = rank + jnp.sum(jnp.where(hit, starts[:, None, None, None], 0), axis=0)

    xs_rows = (n_tok * TOP_K + BLK) * SLAB
    xs = pl.pallas_call(
        _dispatch_kernel,
        grid=(n_tok // TD,),
        in_specs=[
            pl.BlockSpec((TD // TT, TOP_K, TT), lambda i: (i, 0, 0), memory_space=pltpu.SMEM),
            pl.BlockSpec(memory_space=pl.ANY),
        ],
        out_specs=pl.BlockSpec(memory_space=pl.ANY),
        out_shape=jax.ShapeDtypeStruct((xs_rows, LANES), F32),
        scratch_shapes=[pltpu.VMEM((3, TD * SLAB, LANES), F32), pltpu.VMEM((BLK * SLAB, LANES), F32),
                        pltpu.SemaphoreType.DMA((3,)), pltpu.SemaphoreType.DMA((2,)), pltpu.SemaphoreType.DMA],
        compiler_params=pltpu.CompilerParams(dimension_semantics=("arbitrary",)),
        name="dispatch",
    )(dest_x, x1s)

    yb = pl.pallas_call(
        _expert_kernel,
        grid_spec=pltpu.PrefetchScalarGridSpec(
            num_scalar_prefetch=5,
            grid=(n_blocks,),
            in_specs=[pl.BlockSpec(memory_space=pl.ANY)] * 4,
            out_specs=pl.BlockSpec((BLK * SLAB, LANES), lambda i, *_: (i, 0)),
            scratch_shapes=[
                pltpu.VMEM((2 * BLK * SLAB, LANES), F32),
                pltpu.VMEM((2, D, D_EXPERT), F32),
                pltpu.VMEM((2, D, D_EXPERT), F32),
                pltpu.VMEM((2, D_EXPERT, D), F32),
                pltpu.VMEM((D, D_EXPERT), BF16),
                pltpu.VMEM((D, D_EXPERT), BF16),
                pltpu.VMEM((D_EXPERT, D), BF16),
                pltpu.SemaphoreType.DMA((2,)),
                pltpu.SemaphoreType.DMA((2,)),
            ],
        ),
        out_shape=jax.ShapeDtypeStruct((n_slots * SLAB, LANES), F32),
        compiler_params=pltpu.CompilerParams(
            dimension_semantics=("arbitrary",), vmem_limit_bytes=VMEM_LIMIT),
        name="experts",
    )(block_expert, n_valid.reshape(1), row0, blk_ord, padded // BLK, xs, w_gate[0], w_up[0], w_down[0])

    last = n_tok // TD - 1
    out = pl.pallas_call(
        _combine_kernel,
        grid=(n_tok // TD,),
        in_specs=[
            pl.BlockSpec((TD // TT, TOP_K, TT), lambda i: (i, 0, 0), memory_space=pltpu.SMEM),
            pl.BlockSpec((TD // TT, TOP_K, TT), lambda i: (jnp.minimum(i + 1, last), 0, 0), memory_space=pltpu.SMEM),
            pl.BlockSpec((TD * SLAB, LANES), lambda i: (i, 0)),
            pl.BlockSpec((TD, LANES), lambda i: (i, 0)),
            pl.BlockSpec(memory_space=pl.ANY),
            pl.BlockSpec((1, D), lambda i: (0, 0)),
            pl.BlockSpec((1, D), lambda i: (0, 0)),
        ],
        out_specs=pl.BlockSpec((TD, D), lambda i: (i, 0)),
        scratch_shapes=[pltpu.VMEM((2 * TOP_K * TD * SLAB, LANES), F32), pltpu.SemaphoreType.DMA((2,))],
        out_shape=jax.ShapeDtypeStruct((n_tok, D), F32),
        compiler_params=pltpu.CompilerParams(dimension_semantics=("arbitrary",)),
        name="combine",
    )(dest, dest, x1s, route, yb, ln2_g[0].reshape(1, D), ln2_b[0].reshape(1, D))

    return out.reshape(B, T, D).astype(x.dtype)
```

```python
import functools

import numpy as np
import jax
import jax.numpy as jnp
from jax import lax
from jax.experimental import pallas as pl
from jax.experimental.pallas import tpu as pltpu

D_MODEL = 1024
HG_WIDTH = 512
HG_HEAD_DIM = 128
HG_HEADS = 4
GL_WIDTH = 512
GL_HEADS = 4
GL_DV = 128
GL_DK = 64
GL_QK = 256
GATE_RANK = 16
GATE_TAU = 16.0
N_GROUPS = 8
EXPERTS_PER_GROUP = 8
N_EXPERTS = 64
TOP_K = 2
D_EXPERT = 512
DEPTH = 1
DEEPNORM_ALPHA = (2.0 * DEPTH) ** 0.25
LN_EPS = 1e-5
LOG2E = 1.4426950408889634
IN_SPLITS = (HG_WIDTH, HG_WIDTH, HG_WIDTH, HG_WIDTH, GL_QK, GL_QK, GL_WIDTH, GATE_RANK, GL_WIDTH)

LANES = 128
SUB = 128
N_LEVELS = 7
MXU_LEVELS = 3
TT = 256
TD = 512
BLK = 256
PROJ_COLS = 3712
ROUTE_BASE = 8
SLAB = 8
META_ROWS = 8
VMEM_LIMIT = 56 * 1024 * 1024

F32 = jnp.float32
BF16 = jnp.bfloat16
NT_DIMS = (((1,), (1,)), ((), ()))
TN_DIMS = (((0,), (0,)), ((), ()))


def _decay_matrices():
    t = np.arange(SUB)[:, None]
    j = np.arange(SUB)[None, :]
    mats = [(j <= t)]
    for lvl in range(1, MXU_LEVELS + 1):
        blk = 1 << lvl
        half = blk >> 1
        r = (t // blk) * blk + half - 1
        right = (t % blk) >= half
        mats.append(np.where(right, (j > r) & (j <= t), (j > t) & (j <= r)))
    w = np.concatenate(mats, axis=0).astype(np.float32)
    return np.concatenate([w, w, w], axis=1)


def _level_map():
    t = np.arange(SUB)[:, None]
    s = np.arange(SUB)[None, :]
    x = np.bitwise_xor(t, s)
    lvl = np.floor(np.log2(np.maximum(x, 1))).astype(np.int32) + 1
    lvl = np.where(x == 0, 0, lvl)
    return np.where(s > t, -1, lvl).astype(np.int32)


def _layer_norm(y, g, b):
    mu = jnp.mean(y, axis=-1, keepdims=True)
    yc = y - mu
    var = jnp.mean(yc * yc, axis=-1, keepdims=True)
    return yc * lax.rsqrt(var + LN_EPS) * g + b


def _prefix_rows(x):
    n = x.shape[0]
    row = lax.broadcasted_iota(jnp.int32, (n, 1), 0)
    inc = x
    sh = 1
    while sh < n:
        inc = inc + jnp.where(row >= sh, pltpu.roll(inc, sh, axis=0), 0.0)
        sh *= 2
    return inc - x


def _mixer_kernel(x_ref, win_ref, wa2_ref, ba_ref, lbl_ref, gain_ref, wout_ref, ln1g_ref, ln1b_ref,
                  wr_ref, wstk_ref, lev_ref,
                  x1s_ref, route_ref, meta_ref, counts_ref,
                  p_scr, o_scr, b_scr, lg_scr, st_scr, carry_scr, *, steps_per_seq):
    si = pl.program_id(0)
    n_tiles = pl.num_programs(0) - 1

    @pl.when(si == 0)
    def _():
        carry_scr[...] = jnp.zeros_like(carry_scr)
        lg_scr[...] = jnp.zeros_like(lg_scr)

    @pl.when(si % steps_per_seq == 0)
    def _():
        st_scr[...] = jnp.zeros_like(st_scr)

    xt = x_ref[...]
    p_scr[...] = jnp.dot(xt.astype(BF16), win_ref[...], preferred_element_type=F32)
    _route_tile(si, lg_scr[...], carry_scr, route_ref, meta_ref, counts_ref)

    lbl = lbl_ref[...]
    lbe = jnp.exp(lbl - jnp.max(lbl, axis=0, keepdims=True))
    lb = lbe[0:1, :] / jnp.sum(lbe, axis=0, keepdims=True)

    lev = lev_ref[...]
    tcol = lax.broadcasted_iota(jnp.int32, (SUB, 1), 0)
    qrow = [(tcol & ((1 << l) - 1)) >= (1 << (l - 1)) for l in range(1, N_LEVELS + 1)]
    lane = lax.broadcasted_iota(jnp.int32, (1, LANES), 1)
    half_mask = [(lane < GL_DK), (lane >= GL_DK)]

    def by_halves(l, left_fn, right_fn):
        blk = 1 << l
        half = blk >> 1
        parts = []
        for j in range(SUB // blk):
            lo = j * blk
            parts += [left_fn(lo, lo + half), right_fn(lo + half, lo + blk)]
        return jnp.concatenate(parts, axis=0)

    def subtile(s, carry):
        r0 = pl.multiple_of(s * SUB, SUB)
        rows = pl.ds(r0, SUB)
        hq = p_scr[rows, 0:512]
        hf = p_scr[rows, 512:1024]
        hv = p_scr[rows, 1024:1536]
        hgate = p_scr[rows, 1536:2048]
        gq = p_scr[rows, 2048:2304]
        gk = p_scr[rows, 2304:2560]
        gv = p_scr[rows, 2560:3072]
        ggate = p_scr[rows, 3072:3584]
        ga = p_scr[rows, 3584:3712]

        f = lb + (1.0 - lb) * jax.nn.sigmoid(hf)
        kh = 1.0 - f
        lgh = jnp.log(f)
        z = jnp.dot(ga.astype(BF16), wa2_ref[...], preferred_element_type=F32) + ba_ref[...]
        lga = (jnp.minimum(z, 0.0) - jnp.log1p(jnp.exp(-jnp.abs(z)))) * (1.0 / GATE_TAU)
        lg = jnp.concatenate([lgh, lga], axis=1) * LOG2E

        l_hi = lg.astype(BF16)
        rem = lg - l_hi.astype(F32)
        l_mid = rem.astype(BF16)
        l_lo = (rem - l_mid.astype(F32)).astype(BF16)
        sums = jnp.dot(wstk_ref[...], jnp.concatenate([l_hi, l_mid, l_lo], axis=0),
                       preferred_element_type=F32)
        bcum = sums[0:SUB]
        b_scr[...] = bcum
        expos = [bcum, b_scr[SUB - 1:SUB, :] - bcum]
        for l in range(1, N_LEVELS + 1):
            if l <= MXU_LEVELS:
                expos.append(sums[l * SUB:(l + 1) * SUB])
            else:
                expos.append(by_halves(l, lambda lo, mid: b_scr[mid - 1:mid, :] - bcum[lo:mid],
                                       lambda mid, hi: bcum[mid:hi] - b_scr[mid - 1:mid, :]))
        decs = [jnp.exp2(e) for e in expos]

        def group_terms(q, k, c0):
            e_pre = decs[0][:, c0:c0 + LANES]
            e_suf = decs[1][:, c0:c0 + LANES]
            us = []
            for l in range(1, N_LEVELS + 1):
                e_l = decs[1 + l][:, c0:c0 + LANES]
                if l <= MXU_LEVELS:
                    sel = jnp.where(qrow[l - 1], q, k)
                else:
                    sel = by_halves(l, lambda lo, mid: k[lo:mid], lambda mid, hi: q[mid:hi])
                u = sel * e_l
                us.append((u, u.astype(BF16)))
            return us, (q * e_pre).astype(BF16), k * e_suf, e_pre[SUB - 1:SUB, :]

        def scores(us_ab, qks_ab, masks):
            nm = len(masks)
            zero_b = jnp.zeros((SUB, LANES), BF16)
            scs = [[jnp.where(lev == 0, jnp.sum(qk, axis=-1, keepdims=True), 0.0) for qk in qks] for qks in qks_ab]
            for l in range(1, N_LEVELS + 1):
                (ua, uba), (ub_, ubb) = us_ab[0][l - 1], us_ab[1][l - 1]
                rhs = jnp.concatenate([jnp.concatenate([uba, zero_b], axis=1),
                                       jnp.concatenate([zero_b, ubb], axis=1)], axis=0)
                if l <= MXU_LEVELS:
                    rows_l = SUB
                    lhs = jnp.concatenate(
                        [jnp.concatenate([x if m is None else jnp.where(m, x, jnp.zeros_like(x)) for x in (uba, ubb)],
                                         axis=1) for m in masks], axis=0)
                    g = lax.dot_general(lhs, rhs, NT_DIMS, preferred_element_type=F32)
                    for gi in range(2):
                        for n in range(nm):
                            gn = g[n * rows_l:(n + 1) * rows_l, gi * LANES:(gi + 1) * LANES]
                            scs[gi][n] = jnp.where(lev == l, gn, scs[gi][n])
                else:
                    blk = 1 << l
                    half = blk >> 1
                    mids = [j * blk + half for j in range(SUB // blk)]
                    rows_l = SUB // 2
                    uqs = [jnp.concatenate([x[m:m + half] for m in mids], axis=0) for x in (ua, ub_)]
                    lhs = jnp.concatenate(
                        [jnp.concatenate([x if m is None else jnp.where(m, x, 0.0) for x in uqs], axis=1)
                         for m in masks], axis=0)
                    g = lax.dot_general(lhs.astype(BF16), rhs, NT_DIMS, preferred_element_type=F32)
                    for gi in range(2):
                        for n in range(nm):
                            parts = []
                            for j, m in enumerate(mids):
                                gj = g[n * rows_l + j * half:n * rows_l + (j + 1) * half, gi * LANES:(gi + 1) * LANES]
                                parts.append(scs[gi][n][m - half:m])
                                parts.append(jnp.where(lev[m:m + half] == l, gj, scs[gi][n][m:m + half]))
                            scs[gi][n] = jnp.concatenate(parts, axis=0)
            return scs

        def block_diag(a, b):
            za = jnp.zeros_like(a)
            return jnp.concatenate([jnp.concatenate([a, za], axis=1), jnp.concatenate([za, b], axis=1)], axis=0)

        def head_pair(hidx, scs2, qds, kds, vs, g_lasts):
            vb = [v.astype(BF16) for v in vs]
            sts = [st_scr[h] for h in hidx]
            o = jnp.dot(jnp.concatenate([s.astype(BF16) for s in scs2], axis=1), block_diag(*vb),
                        preferred_element_type=F32)
            o = o + lax.dot_general(jnp.concatenate(qds, axis=1), block_diag(*[s.astype(BF16) for s in sts]),
                                    NT_DIMS, preferred_element_type=F32)
            ut = lax.dot_general(jnp.concatenate(vb, axis=1), jnp.concatenate([k.astype(BF16) for k in kds], axis=1),
                                 TN_DIMS, preferred_element_type=F32)
            for n, h in enumerate(hidx):
                st_scr[h] = sts[n] * g_lasts[n] + ut[n * LANES:(n + 1) * LANES, n * LANES:(n + 1) * LANES]
            return o[:, :LANES], o[:, LANES:]

        def finish(o, gate, c0):
            o = o * lax.rsqrt(jnp.mean(o * o, axis=-1, keepdims=True) + LN_EPS)
            o = o * gain_ref[:, c0:c0 + LANES] * (gate * jax.nn.sigmoid(gate))
            o_scr[rows, c0:c0 + LANES] = o

        for g0 in range(0, HG_HEADS, 2):
            terms = []
            for g in (g0, g0 + 1):
                c0 = g * LANES
                q = hq[:, c0:c0 + LANES] * (HG_HEAD_DIM ** -0.5)
                k = kh[:, c0:c0 + LANES]
                terms.append((q * k,) + group_terms(q, k, c0))
            scs = scores([t[1] for t in terms], [[t[0]] for t in terms], [None])
            cols = [slice(g * LANES, (g + 1) * LANES) for g in (g0, g0 + 1)]
            outs = head_pair((g0, g0 + 1), [scs[0][0], scs[1][0]], [t[2] for t in terms], [t[3] for t in terms],
                             [hv[:, c] for c in cols], [t[4] for t in terms])
            for o, c in zip(outs, cols):
                finish(o, hgate[:, c], c.start)

        terms = []
        for p in range(GL_HEADS // 2):
            c0 = p * LANES
            q = gq[:, c0:c0 + LANES] * (GL_DK ** -0.5)
            k = gk[:, c0:c0 + LANES]
            terms.append((q * k,) + group_terms(q, k, HG_WIDTH + c0))
        scs = scores([t[1] for t in terms], [[jnp.where(m, t[0], 0.0) for m in half_mask] for t in terms], half_mask)
        for p in range(GL_HEADS // 2):
            _, _, qd, kd, g_last = terms[p]
            cols = [slice(h * LANES, (h + 1) * LANES) for h in (2 * p, 2 * p + 1)]
            outs = head_pair((HG_HEADS + 2 * p, HG_HEADS + 2 * p + 1), scs[p], [qd, qd],
                             [jnp.where(m, kd, 0.0) for m in half_mask], [gv[:, c] for c in cols], [g_last, g_last])
            for o, c in zip(outs, cols):
                finish(o, ggate[:, c], HG_WIDTH + c.start)
        return carry

    lax.fori_loop(0, TT // SUB, subtile, 0, unroll=True)

    hmix = jnp.dot(o_scr[...].astype(BF16), wout_ref[...], preferred_element_type=F32)
    x1 = _layer_norm(DEEPNORM_ALPHA * xt + hmix, ln1g_ref[...], ln1b_ref[...])
    x1h = x1.astype(BF16)
    x1l = (x1 - x1h.astype(F32)).astype(BF16)
    prod = jnp.dot(jnp.concatenate([x1h, x1l], axis=0), wr_ref[...], preferred_element_type=F32)
    logits = prod[:TT, :LANES] + prod[TT:, :LANES] + prod[:TT, LANES:]

    @pl.when(si < n_tiles)
    def _():
        for s in range(SLAB):
            x1s_ref[pl.ds(s, TT, stride=SLAB), :] = x1[:, s * LANES:(s + 1) * LANES]
        lg_scr[...] = logits


def _route_tile(si, logits, carry_scr, route_ref, meta_ref, counts_ref):
    lane_i = lax.broadcasted_iota(jnp.int32, (TT, LANES), 1)
    lane_f = lane_i.astype(F32)
    neg = jnp.float32(-jnp.inf)
    big = jnp.float32(1e9)

    gl = jnp.where(lane_i < N_GROUPS, logits, neg)
    gmax = jnp.max(gl, axis=-1, keepdims=True)
    gidx = jnp.min(jnp.where(gl == gmax, lane_f, big), axis=-1, keepdims=True)
    p_group = 1.0 / jnp.sum(jnp.exp(gl - gmax), axis=-1, keepdims=True)

    e_lo = ROUTE_BASE + EXPERTS_PER_GROUP * gidx
    el = jnp.where((lane_f >= e_lo) & (lane_f < e_lo + EXPERTS_PER_GROUP), logits, neg)
    m1 = jnp.max(el, axis=-1, keepdims=True)
    i1 = jnp.min(jnp.where(el == m1, lane_f, big), axis=-1, keepdims=True)
    el2 = jnp.where(lane_f == i1, neg, el)
    m2 = jnp.max(el2, axis=-1, keepdims=True)
    i2 = jnp.min(jnp.where(el2 == m2, lane_f, big), axis=-1, keepdims=True)
    r2 = jnp.exp(m2 - m1)
    gate1 = p_group / (1.0 + r2)
    gate2 = p_group * r2 / (1.0 + r2)

    hot1 = lane_f == i1
    hot2 = lane_f == i2
    cnt = jnp.where(si > 0, (hot1 | hot2).astype(F32), 0.0)
    base = carry_scr[...] + _prefix_rows(cnt)
    rank1 = jnp.sum(jnp.where(hot1, base, 0.0), axis=-1, keepdims=True)
    rank2 = jnp.sum(jnp.where(hot2, base, 0.0), axis=-1, keepdims=True)
    total = carry_scr[...] + jnp.sum(cnt, axis=0, keepdims=True)
    carry_scr[...] = total
    counts_ref[...] = total

    route = jnp.where(lane_i == 0, i1 - ROUTE_BASE, 0.0)
    route = jnp.where(lane_i == 1, i2 - ROUTE_BASE, route)
    route = jnp.where(lane_i == 2, gate1, route)
    route = jnp.where(lane_i == 3, gate2, route)
    route = jnp.where(lane_i == 4, rank1, route)
    route = jnp.where(lane_i == 5, rank2, route)
    route_ref[...] = route
    meta_ref[...] = route.T[0:META_ROWS, :].astype(jnp.int32)


def _dispatch_kernel(dest_ref, x1s_ref, xs_ref, xin, zbuf, in_sems, out_sems, fill_sem):
    i = pl.program_id(0)
    n_steps = pl.num_programs(0)
    tile_rows = TD * SLAB

    def tile_load(t, slot):
        return pltpu.make_async_copy(x1s_ref.at[pl.ds(pl.multiple_of(t * tile_rows, tile_rows), tile_rows), :],
                                     xin.at[slot], in_sems.at[slot])

    def wait_rows(slot, sem):
        for k in range(TOP_K):
            pltpu.make_async_copy(xin.at[slot], xs_ref.at[pl.ds(0, tile_rows), :], sem).wait()

    @pl.when(i == 0)
    def _():
        tile_load(0, 0).start()
        zbuf[...] = jnp.zeros_like(zbuf)
        fill = pltpu.make_async_copy(zbuf, xs_ref.at[pl.ds(xs_ref.shape[0] - BLK * SLAB, BLK * SLAB), :], fill_sem)
        fill.start()
        fill.wait()

    slot = i % 3

    @pl.when(i + 1 < n_steps)
    def _():
        tile_load(i + 1, (i + 1) % 3).start()

    tile_load(i, slot).wait()
    for r in range(TD):
        for k in range(TOP_K):
            dst = pl.multiple_of(dest_ref[r // TT, k, r % TT] * SLAB, SLAB)
            pltpu.make_async_copy(xin.at[slot, pl.ds(r * SLAB, SLAB), :], xs_ref.at[pl.ds(dst, SLAB), :],
                                  out_sems.at[i % 2]).start(priority=k)

    @pl.when(i > 0)
    def _():
        wait_rows((i + 2) % 3, out_sems.at[(i + 1) % 2])

    @pl.when(i + 1 >= n_steps)
    def _():
        wait_rows(slot, out_sems.at[i % 2])


def _expert_kernel(be_ref, nv_ref, row0_ref, ord_ref, nblk_ref, xs_ref, wg_ref, wu_ref, wd_ref, yb_ref,
                   xbuf, wg_f, wu_f, wd_f, wg_s, wu_s, wd_s, sems, wsems):
    i = pl.program_id(0)
    nv = nv_ref[0]
    blk_rows = BLK * SLAB

    def block_copy(b, slot):
        r = pl.multiple_of(row0_ref[b] * SLAB, SLAB)
        return pltpu.make_async_copy(xs_ref.at[pl.ds(r, blk_rows), :],
                                     xbuf.at[pl.ds(slot * blk_rows, blk_rows), :], sems.at[slot])

    def weight_copies(e, ws):
        return [pltpu.make_async_copy(src.at[e], dst.at[ws], wsems.at[ws])
                for src, dst in ((wg_ref, wg_f), (wu_ref, wu_f), (wd_ref, wd_f))]

    @pl.when(i < nv)
    def _():
        slot = i % 2
        e = be_ref[i]
        ws = ord_ref[i] % 2

        @pl.when(i == 0)
        def _():
            block_copy(0, 0).start()
            for c in weight_copies(e, ws):
                c.start()

        @pl.when(i + 1 < nv)
        def _():
            block_copy(i + 1, 1 - slot).start()

        prev = be_ref[jnp.maximum(i - 1, 0)]

        @pl.when((i == 0) | (e != prev))
        def _():
            j = i + nblk_ref[e]

            @pl.when(j < nv)
            def _():
                for c in weight_copies(be_ref[jnp.minimum(j, nv - 1)], 1 - ws):
                    c.start()

            for c in weight_copies(e, ws):
                c.wait()
            wg_s[...] = wg_f[ws].astype(BF16)
            wu_s[...] = wu_f[ws].astype(BF16)
            wd_s[...] = wd_f[ws].astype(BF16)

        block_copy(i, slot).wait()
        base = pl.multiple_of(slot * blk_rows, blk_rows)
        xb = jnp.concatenate(
            [xbuf[pl.ds(base + s, BLK, stride=SLAB), :].astype(BF16) for s in range(SLAB)], axis=1)
        a = jnp.dot(xb, wg_s[...], preferred_element_type=F32)
        u = jnp.dot(xb, wu_s[...], preferred_element_type=F32)
        h = (a * jax.nn.sigmoid(a)) * u
        y = jnp.dot(h.astype(BF16), wd_s[...], preferred_element_type=F32)
        for s in range(SLAB):
            yb_ref[pl.ds(s, BLK, stride=SLAB), :] = y[:, s * LANES:(s + 1) * LANES]

    @pl.when(i >= nv)
    def _():
        yb_ref[...] = jnp.zeros_like(yb_ref)


def _combine_kernel(dest_ref, dest_next_ref, x1s_ref, route_ref, yb_ref, g_ref, b_ref, out_ref, ybuf, sems):
    i = pl.program_id(0)
    n_steps = pl.num_programs(0)
    half_rows = TD * SLAB
    par_rows = TOP_K * half_rows

    def start_gather(d_ref, par):
        for r in range(TD):
            for k in range(TOP_K):
                src = pl.multiple_of(d_ref[r // TT, k, r % TT] * SLAB, SLAB)
                dst = pl.multiple_of(par * par_rows + (k * TD + r) * SLAB, SLAB)
                pltpu.make_async_copy(yb_ref.at[pl.ds(src, SLAB), :], ybuf.at[pl.ds(dst, SLAB), :],
                                      sems.at[par]).start(priority=k)

    def wait_gather(par):
        pltpu.make_async_copy(yb_ref.at[pl.ds(0, par_rows), :],
                              ybuf.at[pl.ds(par * par_rows, par_rows), :], sems.at[par]).wait()

    par = i % 2

    @pl.when(i == 0)
    def _():
        start_gather(dest_ref, 0)

    wait_gather(par)
    start_gather(dest_next_ref, 1 - par)

    base = pl.multiple_of(par * par_rows, par_rows)
    g1 = route_ref[:, 2:3]
    g2 = route_ref[:, 3:4]
    zs = []
    for s in range(SLAB):
        y1 = ybuf[pl.ds(base + s, TD, stride=SLAB), :]
        y2 = ybuf[pl.ds(base + half_rows + s, TD, stride=SLAB), :]
        zs.append(DEEPNORM_ALPHA * x1s_ref[pl.ds(s, TD, stride=SLAB), :] + (g1 * y1 + g2 * y2))
    z = jnp.concatenate(zs, axis=1)
    out_ref[...] = _layer_norm(z, g_ref[...], b_ref[...])

    @pl.when(i + 1 >= n_steps)
    def _():
        wait_gather(1 - par)


def _const_spec(shape):
    nd = len(shape)
    return pl.BlockSpec(shape, lambda *_: (0,) * nd, pipeline_mode=pl.Buffered(1))


def kernel(x, w_in, w_a2, b_a, lb_logits, norm_h, norm_g, w_out, ln1_g, ln1_b, w_group_router, w_expert_router,
           w_gate, w_up, w_down, ln2_g, ln2_b):
    B, T, D = x.shape
    n_tok = B * T
    assert D == D_MODEL == SLAB * LANES and T % TT == 0 and TD % TT == 0 and n_tok % TD == 0
    n_blocks = n_tok * TOP_K // BLK + N_EXPERTS
    n_slots = n_blocks * BLK

    parts = jnp.split(w_in[0], [int(c) for c in np.cumsum(IN_SPLITS)[:-1]], axis=-1)
    hq, hf, hi, hg, gq, gk, gv, ga, gg = parts
    win = jnp.concatenate([hq, hf, hi, hg, gq, gk, gv, gg, ga], axis=-1)
    win = jnp.pad(win, ((0, 0), (0, PROJ_COLS - win.shape[1]))).astype(BF16)
    wa2 = jnp.pad(w_a2[0], ((0, LANES - GATE_RANK), (0, 0))).astype(BF16)
    ba = b_a[0].reshape(1, GL_QK).astype(F32)
    gain = jnp.concatenate([norm_h[0], norm_g[0]]).reshape(1, D).astype(F32)
    wout = w_out[0].astype(BF16)
    wr = jnp.concatenate([w_group_router[0], w_expert_router[0]], axis=-1).astype(F32)
    wr = jnp.pad(wr, ((0, 0), (0, LANES - wr.shape[1])))
    wrh = wr.astype(BF16)
    wr2 = jnp.concatenate([wrh, (wr - wrh.astype(F32)).astype(BF16)], axis=1)
    wstk = jnp.asarray(_decay_matrices(), dtype=BF16)
    lev = jnp.asarray(_level_map())
    n_w = wstk.shape[0]

    n_steps = n_tok // TT
    x1s, route, meta, counts = pl.pallas_call(
        functools.partial(_mixer_kernel, steps_per_seq=T // TT),
        grid=(n_steps + 1,),
        in_specs=[
            pl.BlockSpec((TT, D), lambda i: (jnp.minimum(i, n_steps - 1), 0)),
            _const_spec((D, PROJ_COLS)),
            _const_spec((LANES, GL_QK)),
            _const_spec((1, GL_QK)),
            _const_spec((DEPTH + 1, HG_WIDTH)),
            _const_spec((1, D)),
            _const_spec((D, D)),
            _const_spec((1, D)),
            _const_spec((1, D)),
            _const_spec((D, 2 * LANES)),
            _const_spec((n_w, 3 * SUB)),
            _const_spec((SUB, SUB)),
        ],
        out_specs=[
            pl.BlockSpec((TT * SLAB, LANES), lambda i: (jnp.minimum(i, n_steps - 1), 0)),
            pl.BlockSpec((TT, LANES), lambda i: (jnp.maximum(i - 1, 0), 0)),
            pl.BlockSpec((META_ROWS, TT), lambda i: (jnp.maximum(i - 1, 0), 0)),
            pl.BlockSpec((1, LANES), lambda i: (0, 0)),
        ],
        out_shape=[
            jax.ShapeDtypeStruct((n_tok * SLAB, LANES), F32),
            jax.ShapeDtypeStruct((n_tok, LANES), F32),
            jax.ShapeDtypeStruct((n_steps * META_ROWS, TT), jnp.int32),
            jax.ShapeDtypeStruct((1, LANES), F32),
        ],
        scratch_shapes=[
            pltpu.VMEM((TT, PROJ_COLS), F32),
            pltpu.VMEM((TT, D), F32),
            pltpu.VMEM((SUB, HG_WIDTH + GL_QK), F32),
            pltpu.VMEM((TT, LANES), F32),
            pltpu.VMEM((HG_HEADS + GL_HEADS, LANES, LANES), F32),
            pltpu.VMEM((1, LANES), F32),
        ],
        compiler_params=pltpu.CompilerParams(
            dimension_semantics=("arbitrary",), vmem_limit_bytes=VMEM_LIMIT),
        name="mixer",
    )(x.reshape(n_tok, D), win, wa2, ba, lb_logits.astype(F32), gain, wout, ln1_g[0].reshape(1, D),
      ln1_b[0].reshape(1, D), wr2, wstk, lev)

    cnt = counts[0, ROUTE_BASE:ROUTE_BASE + N_EXPERTS].astype(jnp.int32)
    padded = (cnt + BLK - 1) // BLK * BLK
    ends = jnp.cumsum(padded)
    starts = ends - padded
    n_valid = (ends[-1] // BLK).astype(jnp.int32)
    blk_ids = jnp.minimum(jnp.arange(n_blocks, dtype=jnp.int32), n_valid - 1)
    block_expert = jnp.sum((ends[None, :] <= (blk_ids * BLK)[:, None]).astype(jnp.int32), axis=1)
    block_expert = jnp.minimum(block_expert, N_EXPERTS - 1)
    blk_ord = jnp.cumsum(jnp.concatenate(
        [jnp.zeros((1,), jnp.int32), (block_expert[1:] != block_expert[:-1]).astype(jnp.int32)]))
    dstarts = jnp.cumsum(cnt) - cnt
    first_blk = starts // BLK
    pick = block_expert[None, :] == jnp.arange(N_EXPERTS, dtype=jnp.int32)[:, None]
    row0 = jnp.sum(jnp.where(pick, (dstarts - first_blk * BLK)[:, None], 0), axis=0) + blk_ids * BLK
    meta3 = meta.reshape(n_steps, META_ROWS, TT)
    hit = meta3[None, :, 0:TOP_K, :] == jnp.arange(N_EXPERTS, dtype=jnp.int32)[:, None, None, None]
    rank = meta3[:, 4:4 + TOP_K, :]
    dest_x = rank + jnp.sum(jnp.where(hit, dstarts[:, None, None, None], 0), axis=0)
    dest = rank + jnp.sum(jnp.where(hit, starts[:, None, None, None], 0), axis=0)

    xs_rows = (n_tok * TOP_K + BLK) * SLAB
    xs = pl.pallas_call(
        _dispatch_kernel,
        grid=(n_tok // TD,),
        in_specs=[
            pl.BlockSpec((TD // TT, TOP_K, TT), lambda i: (i, 0, 0), memory_space=pltpu.SMEM),
            pl.BlockSpec(memory_space=pl.ANY),
        ],
        out_specs=pl.BlockSpec(memory_space=pl.ANY),
        out_shape=jax.ShapeDtypeStruct((xs_rows, LANES), F32),
        scratch_shapes=[pltpu.VMEM((3, TD * SLAB, LANES), F32), pltpu.VMEM((BLK * SLAB, LANES), F32),
                        pltpu.SemaphoreType.DMA((3,)), pltpu.SemaphoreType.DMA((2,)), pltpu.SemaphoreType.DMA],
        compiler_params=pltpu.CompilerParams(dimension_semantics=("arbitrary",)),
        name="dispatch",
    )(dest_x, x1s)

    yb = pl.pallas_call(
        _expert_kernel,
        grid_spec=pltpu.PrefetchScalarGridSpec(
            num_scalar_prefetch=5,
            grid=(n_blocks,),
            in_specs=[pl.BlockSpec(memory_space=pl.ANY)] * 4,
            out_specs=pl.BlockSpec((BLK * SLAB, LANES), lambda i, *_: (i, 0)),
            scratch_shapes=[
                pltpu.VMEM((2 * BLK * SLAB, LANES), F32),
                pltpu.VMEM((2, D, D_EXPERT), F32),
                pltpu.VMEM((2, D, D_EXPERT), F32),
                pltpu.VMEM((2, D_EXPERT, D), F32),
                pltpu.VMEM((D, D_EXPERT), BF16),
                pltpu.VMEM((D, D_EXPERT), BF16),
                pltpu.VMEM((D_EXPERT, D), BF16),
                pltpu.SemaphoreType.DMA((2,)),
                pltpu.SemaphoreType.DMA((2,)),
            ],
        ),
        out_shape=jax.ShapeDtypeStruct((n_slots * SLAB, LANES), F32),
        compiler_params=pltpu.CompilerParams(
            dimension_semantics=("arbitrary",), vmem_limit_bytes=VMEM_LIMIT),
        name="experts",
    )(block_expert, n_valid.reshape(1), row0, blk_ord, padded // BLK, xs, w_gate[0], w_up[0], w_down[0])

    last = n_tok // TD - 1
    out = pl.pallas_call(
        _combine_kernel,
        grid=(n_tok // TD,),
        in_specs=[
            pl.BlockSpec((TD // TT, TOP_K, TT), lambda i: (i, 0, 0), memory_space=pltpu.SMEM),
            pl.BlockSpec((TD // TT, TOP_K, TT), lambda i: (jnp.minimum(i + 1, last), 0, 0), memory_space=pltpu.SMEM),
            pl.BlockSpec((TD * SLAB, LANES), lambda i: (i, 0)),
            pl.BlockSpec((TD, LANES), lambda i: (i, 0)),
            pl.BlockSpec(memory_space=pl.ANY),
            pl.BlockSpec((1, D), lambda i: (0, 0)),
            pl.BlockSpec((1, D), lambda i: (0, 0)),
        ],
        out_specs=pl.BlockSpec((TD, D), lambda i: (i, 0)),
        scratch_shapes=[pltpu.VMEM((2 * TOP_K * TD * SLAB, LANES), F32), pltpu.SemaphoreType.DMA((2,))],
        out_shape=jax.ShapeDtypeStruct((n_tok, D), F32),
        compiler_params=pltpu.CompilerParams(dimension_semantics=("arbitrary",)),
        name="combine",
    )(dest, dest, x1s, route, yb, ln2_g[0].reshape(1, D), ln2_b[0].reshape(1, D))

    return out.reshape(B, T, D).astype(x.dtype)
```

```python
import functools

import numpy as np
import jax
import jax.numpy as jnp
from jax import lax
from jax.experimental import pallas as pl
from jax.experimental.pallas import tpu as pltpu

D_MODEL = 1024
HG_WIDTH = 512
HG_HEAD_DIM = 128
HG_HEADS = 4
GL_WIDTH = 512
GL_HEADS = 4
GL_DV = 128
GL_DK = 64
GL_QK = 256
GATE_RANK = 16
GATE_TAU = 16.0
N_GROUPS = 8
EXPERTS_PER_GROUP = 8
N_EXPERTS = 64
TOP_K = 2
D_EXPERT = 512
DEPTH = 1
DEEPNORM_ALPHA = (2.0 * DEPTH) ** 0.25
LN_EPS = 1e-5
LOG2E = 1.4426950408889634
IN_SPLITS = (HG_WIDTH, HG_WIDTH, HG_WIDTH, HG_WIDTH, GL_QK, GL_QK, GL_WIDTH, GATE_RANK, GL_WIDTH)

LANES = 128
SUB = 128
N_LEVELS = 7
MXU_LEVELS = 3
TT = 256
TD = 512
BLK = 256
BLOCKS_PER_STEP = 2
PROJ_WIDTHS = (HG_WIDTH, HG_WIDTH, HG_WIDTH, HG_WIDTH, GL_QK, GL_QK, GL_WIDTH, GL_WIDTH, LANES)
PROJ_SLICES = tuple(slice(sum(PROJ_WIDTHS[:n]), sum(PROJ_WIDTHS[:n + 1])) for n in range(len(PROJ_WIDTHS)))
PROJ_COLS = sum(PROJ_WIDTHS)
ROUTE_BASE = N_GROUPS
SLAB = 8
COL_EXPERT, COL_GATE, COL_RANK = 0, 2, 4
META_ROWS = 8
V7X_VMEM_BYTES = 64 * 1024 * 1024
VMEM_LIMIT = V7X_VMEM_BYTES * 7 // 8
assert DEPTH == 1 and D_MODEL == SLAB * LANES and GL_DK * 2 == LANES and SUB == 1 << N_LEVELS

F32 = jnp.float32
BF16 = jnp.bfloat16
NT_DIMS = (((1,), (1,)), ((), ()))
TN_DIMS = (((0,), (0,)), ((), ()))


def _decay_matrices():
    t = np.arange(SUB)[:, None]
    j = np.arange(SUB)[None, :]
    mats = [(j <= t)]
    for lvl in range(1, MXU_LEVELS + 1):
        blk = 1 << lvl
        half = blk >> 1
        r = (t // blk) * blk + half - 1
        right = (t % blk) >= half
        mats.append(np.where(right, (j > r) & (j <= t), (j > t) & (j <= r)))
    w = np.concatenate(mats, axis=0).astype(np.float32)
    return np.concatenate([w, w, w], axis=1)


def _level_map():
    t = np.arange(SUB)[:, None]
    s = np.arange(SUB)[None, :]
    x = np.bitwise_xor(t, s)
    lvl = np.floor(np.log2(np.maximum(x, 1))).astype(np.int32) + 1
    lvl = np.where(x == 0, 0, lvl)
    return np.where(s > t, -1, lvl).astype(np.int32)


def _layer_norm(y, g, b):
    mu = jnp.mean(y, axis=-1, keepdims=True)
    yc = y - mu
    var = jnp.mean(yc * yc, axis=-1, keepdims=True)
    return yc * lax.rsqrt(var + LN_EPS) * g + b


def _prefix_rows(x):
    n = x.shape[0]
    row = lax.broadcasted_iota(jnp.int32, (n, 1), 0)
    inc = x
    sh = 1
    while sh < n:
        inc = inc + jnp.where(row >= sh, pltpu.roll(inc, sh, axis=0), 0.0)
        sh *= 2
    return inc - x


def _mixer_kernel(x_ref, win_ref, wa2_ref, ba_ref, lbl_ref, gain_ref, wout_ref, ln1g_ref, ln1b_ref,
                  wr_ref, wstk_ref, lev_ref,
                  x1s_ref, route_ref, meta_ref, counts_ref,
                  p_scr, o_scr, b_scr, lg_scr, st_scr, carry_scr, *, steps_per_seq):
    si = pl.program_id(0)
    n_tiles = pl.num_programs(0) - 1

    @pl.when(si == 0)
    def _():
        carry_scr[...] = jnp.zeros_like(carry_scr)
        lg_scr[...] = jnp.zeros_like(lg_scr)

    @pl.when(si % steps_per_seq == 0)
    def _():
        st_scr[...] = jnp.zeros_like(st_scr)

    xt = x_ref[...]
    p_scr[...] = jnp.dot(xt.astype(BF16), win_ref[...], preferred_element_type=F32)
    _route_tile(si, lg_scr[...], carry_scr, route_ref, meta_ref, counts_ref)

    lbl = lbl_ref[...]
    lbe = jnp.exp(lbl - jnp.max(lbl, axis=0, keepdims=True))
    lb = lbe[0:1, :] / jnp.sum(lbe, axis=0, keepdims=True)

    lev = lev_ref[...]
    tcol = lax.broadcasted_iota(jnp.int32, (SUB, 1), 0)
    qrow = [(tcol & ((1 << l) - 1)) >= (1 << (l - 1)) for l in range(1, N_LEVELS + 1)]
    lane = lax.broadcasted_iota(jnp.int32, (1, LANES), 1)
    half_mask = [(lane < GL_DK), (lane >= GL_DK)]

    def by_halves(l, left_fn, right_fn):
        blk = 1 << l
        half = blk >> 1
        parts = []
        for j in range(SUB // blk):
            lo = j * blk
            parts += [left_fn(lo, lo + half), right_fn(lo + half, lo + blk)]
        return jnp.concatenate(parts, axis=0)

    def subtile(s, carry):
        r0 = pl.multiple_of(s * SUB, SUB)
        rows = pl.ds(r0, SUB)
        hq, hf, hv, hgate, gq, gk, gv, ggate, ga = [p_scr[rows, c] for c in PROJ_SLICES]

        f = lb + (1.0 - lb) * jax.nn.sigmoid(hf)
        kh = 1.0 - f
        lgh = jnp.log(f)
        z = jnp.dot(ga.astype(BF16), wa2_ref[...], preferred_element_type=F32) + ba_ref[...]
        lga = (jnp.minimum(z, 0.0) - jnp.log1p(jnp.exp(-jnp.abs(z)))) * (1.0 / GATE_TAU)
        lg = jnp.concatenate([lgh, lga], axis=1) * LOG2E

        l_hi = lg.astype(BF16)
        rem = lg - l_hi.astype(F32)
        l_mid = rem.astype(BF16)
        l_lo = (rem - l_mid.astype(F32)).astype(BF16)
        sums = jnp.dot(wstk_ref[...], jnp.concatenate([l_hi, l_mid, l_lo], axis=0),
                       preferred_element_type=F32)
        bcum = sums[0:SUB]
        b_scr[...] = bcum
        expos = [bcum, b_scr[SUB - 1:SUB, :] - bcum]
        for l in range(1, N_LEVELS + 1):
            if l <= MXU_LEVELS:
                expos.append(sums[l * SUB:(l + 1) * SUB])
            else:
                expos.append(by_halves(l, lambda lo, mid: b_scr[mid - 1:mid, :] - bcum[lo:mid],
                                       lambda mid, hi: bcum[mid:hi] - b_scr[mid - 1:mid, :]))
        decs = [jnp.exp2(e) for e in expos]

        def group_terms(q, k, c0):
            e_pre = decs[0][:, c0:c0 + LANES]
            e_suf = decs[1][:, c0:c0 + LANES]
            us = []
            for l in range(1, N_LEVELS + 1):
                e_l = decs[1 + l][:, c0:c0 + LANES]
                if l <= MXU_LEVELS:
                    sel = jnp.where(qrow[l - 1], q, k)
                else:
                    sel = by_halves(l, lambda lo, mid: k[lo:mid], lambda mid, hi: q[mid:hi])
                u = sel * e_l
                us.append((u, u.astype(BF16)))
            return us, (q * e_pre).astype(BF16), k * e_suf, e_pre[SUB - 1:SUB, :]

        def scores(us_ab, qks_ab, masks):
            nm = len(masks)
            zero_b = jnp.zeros((SUB, LANES), BF16)
            scs = [[jnp.where(lev == 0, jnp.sum(qk, axis=-1, keepdims=True), 0.0) for qk in qks] for qks in qks_ab]
            for l in range(1, N_LEVELS + 1):
                (ua, uba), (ub_, ubb) = us_ab[0][l - 1], us_ab[1][l - 1]
                rhs = jnp.concatenate([jnp.concatenate([uba, zero_b], axis=1),
                                       jnp.concatenate([zero_b, ubb], axis=1)], axis=0)
                if l <= MXU_LEVELS:
                    rows_l = SUB
                    lhs = jnp.concatenate(
                        [jnp.concatenate([x if m is None else jnp.where(m, x, jnp.zeros_like(x)) for x in (uba, ubb)],
                                         axis=1) for m in masks], axis=0)
                    g = lax.dot_general(lhs, rhs, NT_DIMS, preferred_element_type=F32)
                    for gi in range(2):
                        for n in range(nm):
                            gn = g[n * rows_l:(n + 1) * rows_l, gi * LANES:(gi + 1) * LANES]
                            scs[gi][n] = jnp.where(lev == l, gn, scs[gi][n])
                else:
                    blk = 1 << l
                    half = blk >> 1
                    mids = [j * blk + half for j in range(SUB // blk)]
                    rows_l = SUB // 2
                    uqs = [jnp.concatenate([x[m:m + half] for m in mids], axis=0) for x in (ua, ub_)]
                    lhs = jnp.concatenate(
                        [jnp.concatenate([x if m is None else jnp.where(m, x, 0.0) for x in uqs], axis=1)
                         for m in masks], axis=0)
                    g = lax.dot_general(lhs.astype(BF16), rhs, NT_DIMS, preferred_element_type=F32)
                    for gi in range(2):
                        for n in range(nm):
                            parts = []
                            for j, m in enumerate(mids):
                                gj = g[n * rows_l + j * half:n * rows_l + (j + 1) * half, gi * LANES:(gi + 1) * LANES]
                                parts.append(scs[gi][n][m - half:m])
                                parts.append(jnp.where(lev[m:m + half] == l, gj, scs[gi][n][m:m + half]))
                            scs[gi][n] = jnp.concatenate(parts, axis=0)
            return scs

        def block_diag(a, b):
            za = jnp.zeros_like(a)
            return jnp.concatenate([jnp.concatenate([a, za], axis=1), jnp.concatenate([za, b], axis=1)], axis=0)

        def head_pair(hidx, scs2, qds, kds, vs, g_lasts):
            vb = [v.astype(BF16) for v in vs]
            sts = [st_scr[h] for h in hidx]
            o = jnp.dot(jnp.concatenate([s.astype(BF16) for s in scs2], axis=1), block_diag(*vb),
                        preferred_element_type=F32)
            o = o + lax.dot_general(jnp.concatenate(qds, axis=1), block_diag(*[s.astype(BF16) for s in sts]),
                                    NT_DIMS, preferred_element_type=F32)
            ut = lax.dot_general(jnp.concatenate(vb, axis=1), jnp.concatenate([k.astype(BF16) for k in kds], axis=1),
                                 TN_DIMS, preferred_element_type=F32)
            for n, h in enumerate(hidx):
                st_scr[h] = sts[n] * g_lasts[n] + ut[n * LANES:(n + 1) * LANES, n * LANES:(n + 1) * LANES]
            return o[:, :LANES], o[:, LANES:]

        def finish(o, gate, c0):
            o = o * lax.rsqrt(jnp.mean(o * o, axis=-1, keepdims=True) + LN_EPS)
            o = o * gain_ref[:, c0:c0 + LANES] * (gate * jax.nn.sigmoid(gate))
            o_scr[rows, c0:c0 + LANES] = o

        for g0 in range(0, HG_HEADS, 2):
            terms = []
            for g in (g0, g0 + 1):
                c0 = g * LANES
                q = hq[:, c0:c0 + LANES] * (HG_HEAD_DIM ** -0.5)
                k = kh[:, c0:c0 + LANES]
                terms.append((q * k,) + group_terms(q, k, c0))
            scs = scores([t[1] for t in terms], [[t[0]] for t in terms], [None])
            cols = [slice(g * LANES, (g + 1) * LANES) for g in (g0, g0 + 1)]
            outs = head_pair((g0, g0 + 1), [scs[0][0], scs[1][0]], [t[2] for t in terms], [t[3] for t in terms],
                             [hv[:, c] for c in cols], [t[4] for t in terms])
            for o, c in zip(outs, cols):
                finish(o, hgate[:, c], c.start)

        terms = []
        for p in range(GL_HEADS // 2):
            c0 = p * LANES
            q = gq[:, c0:c0 + LANES] * (GL_DK ** -0.5)
            k = gk[:, c0:c0 + LANES]
            terms.append((q * k,) + group_terms(q, k, HG_WIDTH + c0))
        scs = scores([t[1] for t in terms], [[jnp.where(m, t[0], 0.0) for m in half_mask] for t in terms], half_mask)
        for p in range(GL_HEADS // 2):
            _, _, qd, kd, g_last = terms[p]
            cols = [slice(h * LANES, (h + 1) * LANES) for h in (2 * p, 2 * p + 1)]
            outs = head_pair((HG_HEADS + 2 * p, HG_HEADS + 2 * p + 1), scs[p], [qd, qd],
                             [jnp.where(m, kd, 0.0) for m in half_mask], [gv[:, c] for c in cols], [g_last, g_last])
            for o, c in zip(outs, cols):
                finish(o, ggate[:, c], HG_WIDTH + c.start)
        return carry

    lax.fori_loop(0, TT // SUB, subtile, 0, unroll=True)

    hmix = jnp.dot(o_scr[...].astype(BF16), wout_ref[...], preferred_element_type=F32)
    x1 = _layer_norm(DEEPNORM_ALPHA * xt + hmix, ln1g_ref[...], ln1b_ref[...])
    x1h = x1.astype(BF16)
    x1l = (x1 - x1h.astype(F32)).astype(BF16)
    prod = jnp.dot(jnp.concatenate([x1h, x1l], axis=0), wr_ref[...], preferred_element_type=F32)
    logits = prod[:TT, :LANES] + prod[TT:, :LANES] + prod[:TT, LANES:]

    @pl.when(si < n_tiles)
    def _():
        for s in range(SLAB):
            x1s_ref[pl.ds(s, TT, stride=SLAB), :] = x1[:, s * LANES:(s + 1) * LANES]
        lg_scr[...] = logits


def _route_tile(si, logits, carry_scr, route_ref, meta_ref, counts_ref):
    lane_i = lax.broadcasted_iota(jnp.int32, (TT, LANES), 1)
    lane_f = lane_i.astype(F32)
    neg = jnp.float32(-jnp.inf)
    big = jnp.float32(1e9)

    gl = jnp.where(lane_i < N_GROUPS, logits, neg)
    gmax = jnp.max(gl, axis=-1, keepdims=True)
    gidx = jnp.min(jnp.where(gl == gmax, lane_f, big), axis=-1, keepdims=True)
    p_group = 1.0 / jnp.sum(jnp.exp(gl - gmax), axis=-1, keepdims=True)

    e_lo = ROUTE_BASE + EXPERTS_PER_GROUP * gidx
    el = jnp.where((lane_f >= e_lo) & (lane_f < e_lo + EXPERTS_PER_GROUP), logits, neg)
    m1 = jnp.max(el, axis=-1, keepdims=True)
    i1 = jnp.min(jnp.where(el == m1, lane_f, big), axis=-1, keepdims=True)
    el2 = jnp.where(lane_f == i1, neg, el)
    m2 = jnp.max(el2, axis=-1, keepdims=True)
    i2 = jnp.min(jnp.where(el2 == m2, lane_f, big), axis=-1, keepdims=True)
    r2 = jnp.exp(m2 - m1)
    gate1 = p_group / (1.0 + r2)
    gate2 = p_group * r2 / (1.0 + r2)

    hot1 = lane_f == i1
    hot2 = lane_f == i2
    cnt = jnp.where(si > 0, (hot1 | hot2).astype(F32), 0.0)
    base = carry_scr[...] + _prefix_rows(cnt)
    rank1 = jnp.sum(jnp.where(hot1, base, 0.0), axis=-1, keepdims=True)
    rank2 = jnp.sum(jnp.where(hot2, base, 0.0), axis=-1, keepdims=True)
    total = carry_scr[...] + jnp.sum(cnt, axis=0, keepdims=True)
    carry_scr[...] = total
    counts_ref[...] = total

    route = jnp.zeros((TT, LANES), F32)
    for lane0, vals in ((COL_EXPERT, (i1 - ROUTE_BASE, i2 - ROUTE_BASE)), (COL_GATE, (gate1, gate2)),
                        (COL_RANK, (rank1, rank2))):
        for k, v in enumerate(vals):
            route = jnp.where(lane_i == lane0 + k, v, route)
    route_ref[...] = route
    meta_ref[...] = route.T[0:META_ROWS, :].astype(jnp.int32)


def _dispatch_kernel(dest_ref, x1s_ref, xs_ref, xin, zbuf, in_sems, out_sems, fill_sem):
    i = pl.program_id(0)
    n_steps = pl.num_programs(0)
    tile_rows = TD * SLAB

    def tile_load(t, slot):
        return pltpu.make_async_copy(x1s_ref.at[pl.ds(pl.multiple_of(t * tile_rows, tile_rows), tile_rows), :],
                                     xin.at[slot], in_sems.at[slot])

    def wait_rows(slot, sem):
        for k in range(TOP_K):
            pltpu.make_async_copy(xin.at[slot], xs_ref.at[pl.ds(0, tile_rows), :], sem).wait()

    @pl.when(i == 0)
    def _():
        tile_load(0, 0).start()
        zbuf[...] = jnp.zeros_like(zbuf)
        fill = pltpu.make_async_copy(zbuf, xs_ref.at[pl.ds(xs_ref.shape[0] - BLK * SLAB, BLK * SLAB), :], fill_sem)
        fill.start()
        fill.wait()

    slot = i % 3

    @pl.when(i + 1 < n_steps)
    def _():
        tile_load(i + 1, (i + 1) % 3).start()

    tile_load(i, slot).wait()
    for r in range(TD):
        for k in range(TOP_K):
            dst = pl.multiple_of(dest_ref[r // TT, k, r % TT] * SLAB, SLAB)
            pltpu.make_async_copy(xin.at[slot, pl.ds(r * SLAB, SLAB), :], xs_ref.at[pl.ds(dst, SLAB), :],
                                  out_sems.at[i % 2]).start(priority=k)

    @pl.when(i > 0)
    def _():
        wait_rows((i + 2) % 3, out_sems.at[(i + 1) % 2])

    @pl.when(i + 1 >= n_steps)
    def _():
        wait_rows(slot, out_sems.at[i % 2])


def _expert_kernel(be_ref, nv_ref, row0_ref, ord_ref, nblk_ref, xs_ref, wg_ref, wu_ref, wd_ref, yb_ref,
                   xbuf, wg_f, wu_f, wd_f, wg_s, wu_s, wd_s, sems, wsems):
    nv = nv_ref[0]
    blk_rows = BLK * SLAB

    def block_copy(b, slot):
        r = pl.multiple_of(row0_ref[b] * SLAB, SLAB)
        return pltpu.make_async_copy(xs_ref.at[pl.ds(r, blk_rows), :],
                                     xbuf.at[pl.ds(slot * blk_rows, blk_rows), :], sems.at[slot])

    def weight_copies(e, ws):
        return [pltpu.make_async_copy(src.at[e], dst.at[ws], wsems.at[ws])
                for src, dst in ((wg_ref, wg_f), (wu_ref, wu_f), (wd_ref, wd_f))]

    def block(b, slot):
        out0 = slot * blk_rows

        @pl.when(b < nv)
        def _():
            e = be_ref[b]
            ws = ord_ref[b] % 2

            @pl.when(b == 0)
            def _():
                block_copy(0, 0).start()
                for c in weight_copies(e, ws):
                    c.start()

            @pl.when(b + 1 < nv)
            def _():
                block_copy(b + 1, 1 - slot).start()

            prev = be_ref[jnp.maximum(b - 1, 0)]

            @pl.when((b == 0) | (e != prev))
            def _():
                j = b + nblk_ref[e]

                @pl.when(j < nv)
                def _():
                    for c in weight_copies(be_ref[jnp.minimum(j, nv - 1)], 1 - ws):
                        c.start()

                for c in weight_copies(e, ws):
                    c.wait()
                wg_s[...] = wg_f[ws].astype(BF16)
                wu_s[...] = wu_f[ws].astype(BF16)
                wd_s[...] = wd_f[ws].astype(BF16)

            block_copy(b, slot).wait()
            xb = jnp.concatenate(
                [xbuf[pl.ds(slot * blk_rows + s, BLK, stride=SLAB), :].astype(BF16) for s in range(SLAB)],
                axis=1)
            a = jnp.dot(xb, wg_s[...], preferred_element_type=F32)
            u = jnp.dot(xb, wu_s[...], preferred_element_type=F32)
            h = (a * jax.nn.sigmoid(a)) * u
            y = jnp.dot(h.astype(BF16), wd_s[...], preferred_element_type=F32)
            for s in range(SLAB):
                yb_ref[pl.ds(out0 + s, BLK, stride=SLAB), :] = y[:, s * LANES:(s + 1) * LANES]

        @pl.when(b >= nv)
        def _():
            yb_ref[pl.ds(out0, blk_rows), :] = jnp.zeros((blk_rows, LANES), F32)

    for sub in range(BLOCKS_PER_STEP):
        block(pl.program_id(0) * BLOCKS_PER_STEP + sub, sub)


def _combine_kernel(dest_ref, dest_next_ref, x1s_ref, route_ref, yb_ref, g_ref, b_ref, out_ref, ybuf, sems):
    i = pl.program_id(0)
    n_steps = pl.num_programs(0)
    half_rows = TD * SLAB
    par_rows = TOP_K * half_rows

    def start_gather(d_ref, par):
        for r in range(TD):
            for k in range(TOP_K):
                src = pl.multiple_of(d_ref[r // TT, k, r % TT] * SLAB, SLAB)
                dst = pl.multiple_of(par * par_rows + (k * TD + r) * SLAB, SLAB)
                pltpu.make_async_copy(yb_ref.at[pl.ds(src, SLAB), :], ybuf.at[pl.ds(dst, SLAB), :],
                                      sems.at[par]).start(priority=k)

    def wait_gather(par):
        pltpu.make_async_copy(yb_ref.at[pl.ds(0, par_rows), :],
                              ybuf.at[pl.ds(par * par_rows, par_rows), :], sems.at[par]).wait()

    par = i % 2

    @pl.when(i == 0)
    def _():
        start_gather(dest_ref, 0)

    wait_gather(par)
    start_gather(dest_next_ref, 1 - par)

    base = pl.multiple_of(par * par_rows, par_rows)
    g1 = route_ref[:, COL_GATE:COL_GATE + 1]
    g2 = route_ref[:, COL_GATE + 1:COL_GATE + 2]
    zs = []
    for s in range(SLAB):
        y1 = ybuf[pl.ds(base + s, TD, stride=SLAB), :]
        y2 = ybuf[pl.ds(base + half_rows + s, TD, stride=SLAB), :]
        zs.append(DEEPNORM_ALPHA * x1s_ref[pl.ds(s, TD, stride=SLAB), :] + (g1 * y1 + g2 * y2))
    z = jnp.concatenate(zs, axis=1)
    out_ref[...] = _layer_norm(z, g_ref[...], b_ref[...])

    @pl.when(i + 1 >= n_steps)
    def _():
        wait_gather(1 - par)


def _const_spec(shape):
    nd = len(shape)
    return pl.BlockSpec(shape, lambda *_: (0,) * nd, pipeline_mode=pl.Buffered(1))


def kernel(x, w_in, w_a2, b_a, lb_logits, norm_h, norm_g, w_out, ln1_g, ln1_b, w_group_router, w_expert_router,
           w_gate, w_up, w_down, ln2_g, ln2_b):
    B, T, D = x.shape
    n_tok = B * T
    assert D == D_MODEL == SLAB * LANES and T % TT == 0 and TD % TT == 0 and n_tok % TD == 0
    n_blocks = n_tok * TOP_K // BLK + N_EXPERTS
    assert n_blocks % BLOCKS_PER_STEP == 0 and BLOCKS_PER_STEP == 2
    n_slots = n_blocks * BLK

    parts = jnp.split(w_in[0], [int(c) for c in np.cumsum(IN_SPLITS)[:-1]], axis=-1)
    hq, hf, hi, hg, gq, gk, gv, ga, gg = parts
    win = jnp.concatenate([hq, hf, hi, hg, gq, gk, gv, gg, ga], axis=-1)
    win = jnp.pad(win, ((0, 0), (0, PROJ_COLS - win.shape[1]))).astype(BF16)
    wa2 = jnp.pad(w_a2[0], ((0, LANES - GATE_RANK), (0, 0))).astype(BF16)
    ba = b_a[0].reshape(1, GL_QK).astype(F32)
    gain = jnp.concatenate([norm_h[0], norm_g[0]]).reshape(1, D).astype(F32)
    wout = w_out[0].astype(BF16)
    wr = jnp.concatenate([w_group_router[0], w_expert_router[0]], axis=-1).astype(F32)
    wr = jnp.pad(wr, ((0, 0), (0, LANES - wr.shape[1])))
    wrh = wr.astype(BF16)
    wr2 = jnp.concatenate([wrh, (wr - wrh.astype(F32)).astype(BF16)], axis=1)
    wstk = jnp.asarray(_decay_matrices(), dtype=BF16)
    lev = jnp.asarray(_level_map())
    n_w = wstk.shape[0]

    n_steps = n_tok // TT
    x1s, route, meta, counts = pl.pallas_call(
        functools.partial(_mixer_kernel, steps_per_seq=T // TT),
        grid=(n_steps + 1,),
        in_specs=[
            pl.BlockSpec((TT, D), lambda i: (jnp.minimum(i, n_steps - 1), 0)),
            _const_spec((D, PROJ_COLS)),
            _const_spec((LANES, GL_QK)),
            _const_spec((1, GL_QK)),
            _const_spec((DEPTH + 1, HG_WIDTH)),
            _const_spec((1, D)),
            _const_spec((D, D)),
            _const_spec((1, D)),
            _const_spec((1, D)),
            _const_spec((D, 2 * LANES)),
            _const_spec((n_w, 3 * SUB)),
            _const_spec((SUB, SUB)),
        ],
        out_specs=[
            pl.BlockSpec((TT * SLAB, LANES), lambda i: (jnp.minimum(i, n_steps - 1), 0)),
            pl.BlockSpec((TT, LANES), lambda i: (jnp.maximum(i - 1, 0), 0)),
            pl.BlockSpec((META_ROWS, TT), lambda i: (jnp.maximum(i - 1, 0), 0)),
            pl.BlockSpec((1, LANES), lambda i: (0, 0)),
        ],
        out_shape=[
            jax.ShapeDtypeStruct((n_tok * SLAB, LANES), F32),
            jax.ShapeDtypeStruct((n_tok, LANES), F32),
            jax.ShapeDtypeStruct((n_steps * META_ROWS, TT), jnp.int32),
            jax.ShapeDtypeStruct((1, LANES), F32),
        ],
        scratch_shapes=[
            pltpu.VMEM((TT, PROJ_COLS), F32),
            pltpu.VMEM((TT, D), F32),
            pltpu.VMEM((SUB, HG_WIDTH + GL_QK), F32),
            pltpu.VMEM((TT, LANES), F32),
            pltpu.VMEM((HG_HEADS + GL_HEADS, LANES, LANES), F32),
            pltpu.VMEM((1, LANES), F32),
        ],
        compiler_params=pltpu.CompilerParams(
            dimension_semantics=("arbitrary",), vmem_limit_bytes=VMEM_LIMIT),
        name="mixer",
    )(x.reshape(n_tok, D), win, wa2, ba, lb_logits.astype(F32), gain, wout, ln1_g[0].reshape(1, D),
      ln1_b[0].reshape(1, D), wr2, wstk, lev)

    cnt = counts[0, ROUTE_BASE:ROUTE_BASE + N_EXPERTS].astype(jnp.int32)
    padded = (cnt + BLK - 1) // BLK * BLK
    ends = jnp.cumsum(padded)
    starts = ends - padded
    n_valid = (ends[-1] // BLK).astype(jnp.int32)
    blk_ids = jnp.minimum(jnp.arange(n_blocks, dtype=jnp.int32), n_valid - 1)
    block_expert = jnp.sum((ends[None, :] <= (blk_ids * BLK)[:, None]).astype(jnp.int32), axis=1)
    block_expert = jnp.minimum(block_expert, N_EXPERTS - 1)
    blk_ord = jnp.cumsum(jnp.concatenate(
        [jnp.zeros((1,), jnp.int32), (block_expert[1:] != block_expert[:-1]).astype(jnp.int32)]))
    dstarts = jnp.cumsum(cnt) - cnt
    first_blk = starts // BLK
    pick = block_expert[None, :] == jnp.arange(N_EXPERTS, dtype=jnp.int32)[:, None]
    row0 = jnp.sum(jnp.where(pick, (dstarts - first_blk * BLK)[:, None], 0), axis=0) + blk_ids * BLK
    meta3 = meta.reshape(n_steps, META_ROWS, TT)
    hit = (meta3[None, :, COL_EXPERT:COL_EXPERT + TOP_K, :]
           == jnp.arange(N_EXPERTS, dtype=jnp.int32)[:, None, None, None])
    rank = meta3[:, COL_RANK:COL_RANK + TOP_K, :]
    dest_x = rank + jnp.sum(jnp.where(hit, dstarts[:, None, None, None], 0), axis=0)
    dest = rank + jnp.sum(jnp.where(hit, starts[:, None, None, None], 0), axis=0)

    xs_rows = (n_tok * TOP_K + BLK) * SLAB
    xs = pl.pallas_call(
        _dispatch_kernel,
        grid=(n_tok // TD,),
        in_specs=[
            pl.BlockSpec((TD // TT, TOP_K, TT), lambda i: (i, 0, 0), memory_space=pltpu.SMEM),
            pl.BlockSpec(memory_space=pl.ANY),
        ],
        out_specs=pl.BlockSpec(memory_space=pl.ANY),
        out_shape=jax.ShapeDtypeStruct((xs_rows, LANES), F32),
        scratch_shapes=[pltpu.VMEM((3, TD * SLAB, LANES), F32), pltpu.VMEM((BLK * SLAB, LANES), F32),
                        pltpu.SemaphoreType.DMA((3,)), pltpu.SemaphoreType.DMA((2,)), pltpu.SemaphoreType.DMA],
        compiler_params=pltpu.CompilerParams(dimension_semantics=("arbitrary",)),
        name="dispatch",
    )(dest_x, x1s)

    yb = pl.pallas_call(
        _expert_kernel,
        grid_spec=pltpu.PrefetchScalarGridSpec(
            num_scalar_prefetch=5,
            grid=(n_blocks // BLOCKS_PER_STEP,),
            in_specs=[pl.BlockSpec(memory_space=pl.ANY)] * 4,
            out_specs=pl.BlockSpec((BLOCKS_PER_STEP * BLK * SLAB, LANES), lambda i, *_: (i, 0)),
            scratch_shapes=[
                pltpu.VMEM((BLOCKS_PER_STEP * BLK * SLAB, LANES), F32),
                pltpu.VMEM((2, D, D_EXPERT), F32),
                pltpu.VMEM((2, D, D_EXPERT), F32),
                pltpu.VMEM((2, D_EXPERT, D), F32),
                pltpu.VMEM((D, D_EXPERT), BF16),
                pltpu.VMEM((D, D_EXPERT), BF16),
                pltpu.VMEM((D_EXPERT, D), BF16),
                pltpu.SemaphoreType.DMA((2,)),
                pltpu.SemaphoreType.DMA((2,)),
            ],
        ),
        out_shape=jax.ShapeDtypeStruct((n_slots * SLAB, LANES), F32),
        compiler_params=pltpu.CompilerParams(
            dimension_semantics=("arbitrary",), vmem_limit_bytes=VMEM_LIMIT),
        name="experts",
    )(block_expert, n_valid.reshape(1), row0, blk_ord, padded // BLK, xs, w_gate[0], w_up[0], w_down[0])

    last = n_tok // TD - 1
    out = pl.pallas_call(
        _combine_kernel,
        grid=(n_tok // TD,),
        in_specs=[
            pl.BlockSpec((TD // TT, TOP_K, TT), lambda i: (i, 0, 0), memory_space=pltpu.SMEM),
            pl.BlockSpec((TD // TT, TOP_K, TT), lambda i: (jnp.minimum(i + 1, last), 0, 0), memory_space=pltpu.SMEM),
            pl.BlockSpec((TD * SLAB, LANES), lambda i: (i, 0)),
            pl.BlockSpec((TD, LANES), lambda i: (i, 0)),
            pl.BlockSpec(memory_space=pl.ANY),
            pl.BlockSpec((1, D), lambda i: (0, 0)),
            pl.BlockSpec((1, D), lambda i: (0, 0)),
        ],
        out_specs=pl.BlockSpec((TD, D), lambda i: (i, 0)),
        scratch_shapes=[pltpu.VMEM((2 * TOP_K * TD * SLAB, LANES), F32), pltpu.SemaphoreType.DMA((2,))],
        out_shape=jax.ShapeDtypeStruct((n_tok, D), F32),
        compiler_params=pltpu.CompilerParams(dimension_semantics=("arbitrary",)),
        name="combine",
    )(dest, dest, x1s, route, yb, ln2_g[0].reshape(1, D), ln2_b[0].reshape(1, D))

    return out.reshape(B, T, D).astype(x.dtype)
```

```python
import functools

import numpy as np
import jax
import jax.numpy as jnp
from jax import lax
from jax.experimental import pallas as pl
from jax.experimental.pallas import tpu as pltpu

D_MODEL = 1024
HG_WIDTH = 512
HG_HEAD_DIM = 128
HG_HEADS = 4
GL_WIDTH = 512
GL_HEADS = 4
GL_DV = 128
GL_DK = 64
GL_QK = 256
GATE_RANK = 16
GATE_TAU = 16.0
N_GROUPS = 8
EXPERTS_PER_GROUP = 8
N_EXPERTS = 64
TOP_K = 2
D_EXPERT = 512
DEPTH = 1
DEEPNORM_ALPHA = (2.0 * DEPTH) ** 0.25
LN_EPS = 1e-5
LOG2E = 1.4426950408889634
IN_SPLITS = (HG_WIDTH, HG_WIDTH, HG_WIDTH, HG_WIDTH, GL_QK, GL_QK, GL_WIDTH, GATE_RANK, GL_WIDTH)

LANES = 128
SUB = 128
N_LEVELS = 7
MXU_LEVELS = 3
TT = 256
TD = 1024
BLK = 256
BLOCKS_PER_STEP = 4
PROJ_WIDTHS = (HG_WIDTH, HG_WIDTH, HG_WIDTH, HG_WIDTH, GL_QK, GL_QK, GL_WIDTH, GL_WIDTH, LANES)
PROJ_SLICES = tuple(slice(sum(PROJ_WIDTHS[:n]), sum(PROJ_WIDTHS[:n + 1])) for n in range(len(PROJ_WIDTHS)))
PROJ_COLS = sum(PROJ_WIDTHS)
ROUTE_BASE = N_GROUPS
SLAB = 8
COL_EXPERT, COL_GATE, COL_RANK = 0, 2, 4
META_ROWS = 8
V7X_VMEM_BYTES = 64 * 1024 * 1024
VMEM_LIMIT = V7X_VMEM_BYTES * 7 // 8
assert DEPTH == 1 and D_MODEL == SLAB * LANES and GL_DK * 2 == LANES and SUB == 1 << N_LEVELS

F32 = jnp.float32
BF16 = jnp.bfloat16
NT_DIMS = (((1,), (1,)), ((), ()))
TN_DIMS = (((0,), (0,)), ((), ()))


def _decay_matrices():
    t = np.arange(SUB)[:, None]
    j = np.arange(SUB)[None, :]
    mats = [(j <= t)]
    for lvl in range(1, MXU_LEVELS + 1):
        blk = 1 << lvl
        half = blk >> 1
        r = (t // blk) * blk + half - 1
        right = (t % blk) >= half
        mats.append(np.where(right, (j > r) & (j <= t), (j > t) & (j <= r)))
    w = np.concatenate(mats, axis=0).astype(np.float32)
    return np.concatenate([w, w, w], axis=1)


def _level_map():
    t = np.arange(SUB)[:, None]
    s = np.arange(SUB)[None, :]
    x = np.bitwise_xor(t, s)
    lvl = np.floor(np.log2(np.maximum(x, 1))).astype(np.int32) + 1
    lvl = np.where(x == 0, 0, lvl)
    return np.where(s > t, -1, lvl).astype(np.int32)


def _layer_norm(y, g, b):
    mu = jnp.mean(y, axis=-1, keepdims=True)
    yc = y - mu
    var = jnp.mean(yc * yc, axis=-1, keepdims=True)
    return yc * lax.rsqrt(var + LN_EPS) * g + b


def _prefix_rows(x):
    n = x.shape[0]
    row = lax.broadcasted_iota(jnp.int32, (n, 1), 0)
    inc = x
    sh = 1
    while sh < n:
        inc = inc + jnp.where(row >= sh, pltpu.roll(inc, sh, axis=0), 0.0)
        sh *= 2
    return inc - x


def _mixer_kernel(x_ref, win_ref, wa2_ref, ba_ref, lbl_ref, gain_ref, wout_ref, ln1g_ref, ln1b_ref,
                  wr_ref, wstk_ref, lev_ref,
                  x1s_ref, route_ref, meta_ref, counts_ref,
                  p_scr, o_scr, b_scr, lg_scr, st_scr, carry_scr, *, steps_per_seq):
    si = pl.program_id(0)
    n_tiles = pl.num_programs(0) - 1

    @pl.when(si == 0)
    def _():
        carry_scr[...] = jnp.zeros_like(carry_scr)
        lg_scr[...] = jnp.zeros_like(lg_scr)

    @pl.when(si % steps_per_seq == 0)
    def _():
        st_scr[...] = jnp.zeros_like(st_scr)

    xt = x_ref[...]
    p_scr[...] = jnp.dot(xt.astype(BF16), win_ref[...], preferred_element_type=F32)
    _route_tile(si, lg_scr[...], carry_scr, route_ref, meta_ref, counts_ref)

    lbl = lbl_ref[...]
    lbe = jnp.exp(lbl - jnp.max(lbl, axis=0, keepdims=True))
    lb = lbe[0:1, :] / jnp.sum(lbe, axis=0, keepdims=True)

    lev = lev_ref[...]
    tcol = lax.broadcasted_iota(jnp.int32, (SUB, 1), 0)
    qrow = [(tcol & ((1 << l) - 1)) >= (1 << (l - 1)) for l in range(1, N_LEVELS + 1)]
    lane = lax.broadcasted_iota(jnp.int32, (1, LANES), 1)
    half_mask = [(lane < GL_DK), (lane >= GL_DK)]

    def by_halves(l, left_fn, right_fn):
        blk = 1 << l
        half = blk >> 1
        parts = []
        for j in range(SUB // blk):
            lo = j * blk
            parts += [left_fn(lo, lo + half), right_fn(lo + half, lo + blk)]
        return jnp.concatenate(parts, axis=0)

    def subtile(s, carry):
        r0 = pl.multiple_of(s * SUB, SUB)
        rows = pl.ds(r0, SUB)
        hq, hf, hv, hgate, gq, gk, gv, ggate, ga = [p_scr[rows, c] for c in PROJ_SLICES]

        f = lb + (1.0 - lb) * jax.nn.sigmoid(hf)
        kh = 1.0 - f
        lgh = jnp.log(f)
        z = jnp.dot(ga.astype(BF16), wa2_ref[...], preferred_element_type=F32) + ba_ref[...]
        lga = (jnp.minimum(z, 0.0) - jnp.log1p(jnp.exp(-jnp.abs(z)))) * (1.0 / GATE_TAU)
        lg = jnp.concatenate([lgh, lga], axis=1) * LOG2E

        l_hi = lg.astype(BF16)
        rem = lg - l_hi.astype(F32)
        l_mid = rem.astype(BF16)
        l_lo = (rem - l_mid.astype(F32)).astype(BF16)
        sums = jnp.dot(wstk_ref[...], jnp.concatenate([l_hi, l_mid, l_lo], axis=0),
                       preferred_element_type=F32)
        bcum = sums[0:SUB]
        b_scr[...] = bcum
        expos = [bcum, b_scr[SUB - 1:SUB, :] - bcum]
        for l in range(1, N_LEVELS + 1):
            if l <= MXU_LEVELS:
                expos.append(sums[l * SUB:(l + 1) * SUB])
            else:
                expos.append(by_halves(l, lambda lo, mid: b_scr[mid - 1:mid, :] - bcum[lo:mid],
                                       lambda mid, hi: bcum[mid:hi] - b_scr[mid - 1:mid, :]))
        decs = [jnp.exp2(e) for e in expos]

        def group_terms(q, k, c0):
            e_pre = decs[0][:, c0:c0 + LANES]
            e_suf = decs[1][:, c0:c0 + LANES]
            us = []
            for l in range(1, N_LEVELS + 1):
                e_l = decs[1 + l][:, c0:c0 + LANES]
                if l <= MXU_LEVELS:
                    sel = jnp.where(qrow[l - 1], q, k)
                else:
                    sel = by_halves(l, lambda lo, mid: k[lo:mid], lambda mid, hi: q[mid:hi])
                u = sel * e_l
                us.append((u, u.astype(BF16)))
            return us, (q * e_pre).astype(BF16), k * e_suf, e_pre[SUB - 1:SUB, :]

        def scores(us_ab, qks_ab, masks):
            nm = len(masks)
            zero_b = jnp.zeros((SUB, LANES), BF16)
            scs = [[jnp.where(lev == 0, jnp.sum(qk, axis=-1, keepdims=True), 0.0) for qk in qks] for qks in qks_ab]
            for l in range(1, N_LEVELS + 1):
                (ua, uba), (ub_, ubb) = us_ab[0][l - 1], us_ab[1][l - 1]
                rhs = jnp.concatenate([jnp.concatenate([uba, zero_b], axis=1),
                                       jnp.concatenate([zero_b, ubb], axis=1)], axis=0)
                if l <= MXU_LEVELS:
                    rows_l = SUB
                    lhs = jnp.concatenate(
                        [jnp.concatenate([x if m is None else jnp.where(m, x, jnp.zeros_like(x)) for x in (uba, ubb)],
                                         axis=1) for m in masks], axis=0)
                    g = lax.dot_general(lhs, rhs, NT_DIMS, preferred_element_type=F32)
                    for gi in range(2):
                        for n in range(nm):
                            gn = g[n * rows_l:(n + 1) * rows_l, gi * LANES:(gi + 1) * LANES]
                            scs[gi][n] = jnp.where(lev == l, gn, scs[gi][n])
                else:
                    blk = 1 << l
                    half = blk >> 1
                    mids = [j * blk + half for j in range(SUB // blk)]
                    rows_l = SUB // 2
                    uqs = [jnp.concatenate([x[m:m + half] for m in mids], axis=0) for x in (ua, ub_)]
                    lhs = jnp.concatenate(
                        [jnp.concatenate([x if m is None else jnp.where(m, x, 0.0) for x in uqs], axis=1)
                         for m in masks], axis=0)
                    g = lax.dot_general(lhs.astype(BF16), rhs, NT_DIMS, preferred_element_type=F32)
                    for gi in range(2):
                        for n in range(nm):
                            parts = []
                            for j, m in enumerate(mids):
                                gj = g[n * rows_l + j * half:n * rows_l + (j + 1) * half, gi * LANES:(gi + 1) * LANES]
                                parts.append(scs[gi][n][m - half:m])
                                parts.append(jnp.where(lev[m:m + half] == l, gj, scs[gi][n][m:m + half]))
                            scs[gi][n] = jnp.concatenate(parts, axis=0)
            return scs

        def block_diag(a, b):
            za = jnp.zeros_like(a)
            return jnp.concatenate([jnp.concatenate([a, za], axis=1), jnp.concatenate([za, b], axis=1)], axis=0)

        def head_pair(hidx, scs2, qds, kds, vs, g_lasts):
            vb = [v.astype(BF16) for v in vs]
            sts = [st_scr[h] for h in hidx]
            o = jnp.dot(jnp.concatenate([s.astype(BF16) for s in scs2], axis=1), block_diag(*vb),
                        preferred_element_type=F32)
            o = o + lax.dot_general(jnp.concatenate(qds, axis=1), block_diag(*[s.astype(BF16) for s in sts]),
                                    NT_DIMS, preferred_element_type=F32)
            ut = lax.dot_general(jnp.concatenate(vb, axis=1), jnp.concatenate([k.astype(BF16) for k in kds], axis=1),
                                 TN_DIMS, preferred_element_type=F32)
            for n, h in enumerate(hidx):
                st_scr[h] = sts[n] * g_lasts[n] + ut[n * LANES:(n + 1) * LANES, n * LANES:(n + 1) * LANES]
            return o[:, :LANES], o[:, LANES:]

        def finish(o, gate, c0):
            o = o * lax.rsqrt(jnp.mean(o * o, axis=-1, keepdims=True) + LN_EPS)
            o = o * gain_ref[:, c0:c0 + LANES] * (gate * jax.nn.sigmoid(gate))
            o_scr[rows, c0:c0 + LANES] = o

        for g0 in range(0, HG_HEADS, 2):
            terms = []
            for g in (g0, g0 + 1):
                c0 = g * LANES
                q = hq[:, c0:c0 + LANES] * (HG_HEAD_DIM ** -0.5)
                k = kh[:, c0:c0 + LANES]
                terms.append((q * k,) + group_terms(q, k, c0))
            scs = scores([t[1] for t in terms], [[t[0]] for t in terms], [None])
            cols = [slice(g * LANES, (g + 1) * LANES) for g in (g0, g0 + 1)]
            outs = head_pair((g0, g0 + 1), [scs[0][0], scs[1][0]], [t[2] for t in terms], [t[3] for t in terms],
                             [hv[:, c] for c in cols], [t[4] for t in terms])
            for o, c in zip(outs, cols):
                finish(o, hgate[:, c], c.start)

        terms = []
        for p in range(GL_HEADS // 2):
            c0 = p * LANES
            q = gq[:, c0:c0 + LANES] * (GL_DK ** -0.5)
            k = gk[:, c0:c0 + LANES]
            terms.append((q * k,) + group_terms(q, k, HG_WIDTH + c0))
        scs = scores([t[1] for t in terms], [[jnp.where(m, t[0], 0.0) for m in half_mask] for t in terms], half_mask)
        for p in range(GL_HEADS // 2):
            _, _, qd, kd, g_last = terms[p]
            cols = [slice(h * LANES, (h + 1) * LANES) for h in (2 * p, 2 * p + 1)]
            outs = head_pair((HG_HEADS + 2 * p, HG_HEADS + 2 * p + 1), scs[p], [qd, qd],
                             [jnp.where(m, kd, 0.0) for m in half_mask], [gv[:, c] for c in cols], [g_last, g_last])
            for o, c in zip(outs, cols):
                finish(o, ggate[:, c], HG_WIDTH + c.start)
        return carry

    lax.fori_loop(0, TT // SUB, subtile, 0, unroll=True)

    hmix = jnp.dot(o_scr[...].astype(BF16), wout_ref[...], preferred_element_type=F32)
    x1 = _layer_norm(DEEPNORM_ALPHA * xt + hmix, ln1g_ref[...], ln1b_ref[...])
    x1h = x1.astype(BF16)
    x1l = (x1 - x1h.astype(F32)).astype(BF16)
    prod = jnp.dot(jnp.concatenate([x1h, x1l], axis=0), wr_ref[...], preferred_element_type=F32)
    logits = prod[:TT, :LANES] + prod[TT:, :LANES] + prod[:TT, LANES:]

    @pl.when(si < n_tiles)
    def _():
        for s in range(SLAB):
            x1s_ref[pl.ds(s, TT, stride=SLAB), :] = x1[:, s * LANES:(s + 1) * LANES]
        lg_scr[...] = logits


def _route_tile(si, logits, carry_scr, route_ref, meta_ref, counts_ref):
    lane_i = lax.broadcasted_iota(jnp.int32, (TT, LANES), 1)
    lane_f = lane_i.astype(F32)
    neg = jnp.float32(-jnp.inf)
    big = jnp.float32(1e9)

    gl = jnp.where(lane_i < N_GROUPS, logits, neg)
    gmax = jnp.max(gl, axis=-1, keepdims=True)
    gidx = jnp.min(jnp.where(gl == gmax, lane_f, big), axis=-1, keepdims=True)
    p_group = 1.0 / jnp.sum(jnp.exp(gl - gmax), axis=-1, keepdims=True)

    e_lo = ROUTE_BASE + EXPERTS_PER_GROUP * gidx
    el = jnp.where((lane_f >= e_lo) & (lane_f < e_lo + EXPERTS_PER_GROUP), logits, neg)
    m1 = jnp.max(el, axis=-1, keepdims=True)
    i1 = jnp.min(jnp.where(el == m1, lane_f, big), axis=-1, keepdims=True)
    el2 = jnp.where(lane_f == i1, neg, el)
    m2 = jnp.max(el2, axis=-1, keepdims=True)
    i2 = jnp.min(jnp.where(el2 == m2, lane_f, big), axis=-1, keepdims=True)
    r2 = jnp.exp(m2 - m1)
    gate1 = p_group / (1.0 + r2)
    gate2 = p_group * r2 / (1.0 + r2)

    hot1 = lane_f == i1
    hot2 = lane_f == i2
    cnt = jnp.where(si > 0, (hot1 | hot2).astype(F32), 0.0)
    base = carry_scr[...] + _prefix_rows(cnt)
    rank1 = jnp.sum(jnp.where(hot1, base, 0.0), axis=-1, keepdims=True)
    rank2 = jnp.sum(jnp.where(hot2, base, 0.0), axis=-1, keepdims=True)
    total = carry_scr[...] + jnp.sum(cnt, axis=0, keepdims=True)
    carry_scr[...] = total
    counts_ref[...] = total

    route = jnp.zeros((TT, LANES), F32)
    for lane0, vals in ((COL_EXPERT, (i1 - ROUTE_BASE, i2 - ROUTE_BASE)), (COL_GATE, (gate1, gate2)),
                        (COL_RANK, (rank1, rank2))):
        for k, v in enumerate(vals):
            route = jnp.where(lane_i == lane0 + k, v, route)
    route_ref[...] = route
    meta_ref[...] = route.T[0:META_ROWS, :].astype(jnp.int32)


def _dispatch_kernel(dest_ref, x1s_ref, xs_ref, xin, zbuf, in_sems, out_sems, fill_sem):
    i = pl.program_id(0)
    n_steps = pl.num_programs(0)
    tile_rows = TD * SLAB

    def tile_load(t, slot):
        return pltpu.make_async_copy(x1s_ref.at[pl.ds(pl.multiple_of(t * tile_rows, tile_rows), tile_rows), :],
                                     xin.at[slot], in_sems.at[slot])

    def wait_rows(slot, sem):
        for k in range(TOP_K):
            pltpu.make_async_copy(xin.at[slot], xs_ref.at[pl.ds(0, tile_rows), :], sem).wait()

    @pl.when(i == 0)
    def _():
        tile_load(0, 0).start()
        zbuf[...] = jnp.zeros_like(zbuf)
        fill = pltpu.make_async_copy(zbuf, xs_ref.at[pl.ds(xs_ref.shape[0] - BLK * SLAB, BLK * SLAB), :], fill_sem)
        fill.start()
        fill.wait()

    slot = i % 3

    @pl.when(i + 1 < n_steps)
    def _():
        tile_load(i + 1, (i + 1) % 3).start()

    tile_load(i, slot).wait()
    for r in range(TD):
        for k in range(TOP_K):
            dst = pl.multiple_of(dest_ref[r // TT, k, r % TT] * SLAB, SLAB)
            pltpu.make_async_copy(xin.at[slot, pl.ds(r * SLAB, SLAB), :], xs_ref.at[pl.ds(dst, SLAB), :],
                                  out_sems.at[i % 2]).start(priority=k)

    @pl.when(i > 0)
    def _():
        wait_rows((i + 2) % 3, out_sems.at[(i + 1) % 2])

    @pl.when(i + 1 >= n_steps)
    def _():
        wait_rows(slot, out_sems.at[i % 2])


def _expert_kernel(be_ref, nv_ref, row0_ref, ord_ref, nblk_ref, xs_ref, wg_ref, wu_ref, wd_ref, yb_ref,
                   xbuf, wg_f, wu_f, wd_f, wg_s, wu_s, wd_s, sems, wsems):
    nv = nv_ref[0]
    blk_rows = BLK * SLAB

    def block_copy(b, slot):
        r = pl.multiple_of(row0_ref[b] * SLAB, SLAB)
        return pltpu.make_async_copy(xs_ref.at[pl.ds(r, blk_rows), :],
                                     xbuf.at[pl.ds(slot * blk_rows, blk_rows), :], sems.at[slot])

    def weight_copies(e, ws):
        return [pltpu.make_async_copy(src.at[e], dst.at[ws], wsems.at[ws])
                for src, dst in ((wg_ref, wg_f), (wu_ref, wu_f), (wd_ref, wd_f))]

    def block(b, sub):
        slot = sub % 2
        out0 = sub * blk_rows

        @pl.when(b < nv)
        def _():
            e = be_ref[b]
            ws = ord_ref[b] % 2

            @pl.when(b == 0)
            def _():
                block_copy(0, 0).start()
                for c in weight_copies(e, ws):
                    c.start()

            @pl.when(b + 1 < nv)
            def _():
                block_copy(b + 1, 1 - slot).start()

            prev = be_ref[jnp.maximum(b - 1, 0)]

            @pl.when((b == 0) | (e != prev))
            def _():
                j = b + nblk_ref[e]

                @pl.when(j < nv)
                def _():
                    for c in weight_copies(be_ref[jnp.minimum(j, nv - 1)], 1 - ws):
                        c.start()

                for c in weight_copies(e, ws):
                    c.wait()
                wg_s[...] = wg_f[ws].astype(BF16)
                wu_s[...] = wu_f[ws].astype(BF16)
                wd_s[...] = wd_f[ws].astype(BF16)

            block_copy(b, slot).wait()
            xb = jnp.concatenate(
                [xbuf[pl.ds(slot * blk_rows + s, BLK, stride=SLAB), :].astype(BF16) for s in range(SLAB)],
                axis=1)
            a = jnp.dot(xb, wg_s[...], preferred_element_type=F32)
            u = jnp.dot(xb, wu_s[...], preferred_element_type=F32)
            h = (a * jax.nn.sigmoid(a)) * u
            y = jnp.dot(h.astype(BF16), wd_s[...], preferred_element_type=F32)
            for s in range(SLAB):
                yb_ref[pl.ds(out0 + s, BLK, stride=SLAB), :] = y[:, s * LANES:(s + 1) * LANES]

        @pl.when(b >= nv)
        def _():
            yb_ref[pl.ds(out0, blk_rows), :] = jnp.zeros((blk_rows, LANES), F32)

    for sub in range(BLOCKS_PER_STEP):
        block(pl.program_id(0) * BLOCKS_PER_STEP + sub, sub)


def _combine_kernel(dest_ref, dest_next_ref, x1s_ref, route_ref, yb_ref, g_ref, b_ref, out_ref, ybuf, sems):
    i = pl.program_id(0)
    n_steps = pl.num_programs(0)
    half_rows = TD * SLAB
    par_rows = TOP_K * half_rows

    def start_gather(d_ref, par):
        for r in range(TD):
            for k in range(TOP_K):
                src = pl.multiple_of(d_ref[r // TT, k, r % TT] * SLAB, SLAB)
                dst = pl.multiple_of(par * par_rows + (k * TD + r) * SLAB, SLAB)
                pltpu.make_async_copy(yb_ref.at[pl.ds(src, SLAB), :], ybuf.at[pl.ds(dst, SLAB), :],
                                      sems.at[par]).start(priority=k)

    def wait_gather(par):
        pltpu.make_async_copy(yb_ref.at[pl.ds(0, par_rows), :],
                              ybuf.at[pl.ds(par * par_rows, par_rows), :], sems.at[par]).wait()

    par = i % 2

    @pl.when(i == 0)
    def _():
        start_gather(dest_ref, 0)

    wait_gather(par)
    start_gather(dest_next_ref, 1 - par)

    base = pl.multiple_of(par * par_rows, par_rows)
    g1 = route_ref[:, COL_GATE:COL_GATE + 1]
    g2 = route_ref[:, COL_GATE + 1:COL_GATE + 2]
    zs = []
    for s in range(SLAB):
        y1 = ybuf[pl.ds(base + s, TD, stride=SLAB), :]
        y2 = ybuf[pl.ds(base + half_rows + s, TD, stride=SLAB), :]
        zs.append(DEEPNORM_ALPHA * x1s_ref[pl.ds(s, TD, stride=SLAB), :] + (g1 * y1 + g2 * y2))
    z = jnp.concatenate(zs, axis=1)
    out_ref[...] = _layer_norm(z, g_ref[...], b_ref[...])

    @pl.when(i + 1 >= n_steps)
    def _():
        wait_gather(1 - par)


def _const_spec(shape):
    nd = len(shape)
    return pl.BlockSpec(shape, lambda *_: (0,) * nd, pipeline_mode=pl.Buffered(1))


def kernel(x, w_in, w_a2, b_a, lb_logits, norm_h, norm_g, w_out, ln1_g, ln1_b, w_group_router, w_expert_router,
           w_gate, w_up, w_down, ln2_g, ln2_b):
    B, T, D = x.shape
    n_tok = B * T
    assert D == D_MODEL == SLAB * LANES and T % TT == 0 and TD % TT == 0 and n_tok % TD == 0
    n_blocks = n_tok * TOP_K // BLK + N_EXPERTS
    assert n_blocks % BLOCKS_PER_STEP == 0 and BLOCKS_PER_STEP % 2 == 0
    n_slots = n_blocks * BLK

    parts = jnp.split(w_in[0], [int(c) for c in np.cumsum(IN_SPLITS)[:-1]], axis=-1)
    hq, hf, hi, hg, gq, gk, gv, ga, gg = parts
    win = jnp.concatenate([hq, hf, hi, hg, gq, gk, gv, gg, ga], axis=-1)
    win = jnp.pad(win, ((0, 0), (0, PROJ_COLS - win.shape[1]))).astype(BF16)
    wa2 = jnp.pad(w_a2[0], ((0, LANES - GATE_RANK), (0, 0))).astype(BF16)
    ba = b_a[0].reshape(1, GL_QK).astype(F32)
    gain = jnp.concatenate([norm_h[0], norm_g[0]]).reshape(1, D).astype(F32)
    wout = w_out[0].astype(BF16)
    wr = jnp.concatenate([w_group_router[0], w_expert_router[0]], axis=-1).astype(F32)
    wr = jnp.pad(wr, ((0, 0), (0, LANES - wr.shape[1])))
    wrh = wr.astype(BF16)
    wr2 = jnp.concatenate([wrh, (wr - wrh.astype(F32)).astype(BF16)], axis=1)
    wstk = jnp.asarray(_decay_matrices(), dtype=BF16)
    lev = jnp.asarray(_level_map())
    n_w = wstk.shape[0]

    n_steps = n_tok // TT
    x1s, route, meta, counts = pl.pallas_call(
        functools.partial(_mixer_kernel, steps_per_seq=T // TT),
        grid=(n_steps + 1,),
        in_specs=[
            pl.BlockSpec((TT, D), lambda i: (jnp.minimum(i, n_steps - 1), 0)),
            _const_spec((D, PROJ_COLS)),
            _const_spec((LANES, GL_QK)),
            _const_spec((1, GL_QK)),
            _const_spec((DEPTH + 1, HG_WIDTH)),
            _const_spec((1, D)),
            _const_spec((D, D)),
            _const_spec((1, D)),
            _const_spec((1, D)),
            _const_spec((D, 2 * LANES)),
            _const_spec((n_w, 3 * SUB)),
            _const_spec((SUB, SUB)),
        ],
        out_specs=[
            pl.BlockSpec((TT * SLAB, LANES), lambda i: (jnp.minimum(i, n_steps - 1), 0)),
            pl.BlockSpec((TT, LANES), lambda i: (jnp.maximum(i - 1, 0), 0)),
            pl.BlockSpec((META_ROWS, TT), lambda i: (jnp.maximum(i - 1, 0), 0)),
            pl.BlockSpec((1, LANES), lambda i: (0, 0)),
        ],
        out_shape=[
            jax.ShapeDtypeStruct((n_tok * SLAB, LANES), F32),
            jax.ShapeDtypeStruct((n_tok, LANES), F32),
            jax.ShapeDtypeStruct((n_steps * META_ROWS, TT), jnp.int32),
            jax.ShapeDtypeStruct((1, LANES), F32),
        ],
        scratch_shapes=[
            pltpu.VMEM((TT, PROJ_COLS), F32),
            pltpu.VMEM((TT, D), F32),
            pltpu.VMEM((SUB, HG_WIDTH + GL_QK), F32),
            pltpu.VMEM((TT, LANES), F32),
            pltpu.VMEM((HG_HEADS + GL_HEADS, LANES, LANES), F32),
            pltpu.VMEM((1, LANES), F32),
        ],
        compiler_params=pltpu.CompilerParams(
            dimension_semantics=("arbitrary",), vmem_limit_bytes=VMEM_LIMIT),
        name="mixer",
    )(x.reshape(n_tok, D), win, wa2, ba, lb_logits.astype(F32), gain, wout, ln1_g[0].reshape(1, D),
      ln1_b[0].reshape(1, D), wr2, wstk, lev)

    cnt = counts[0, ROUTE_BASE:ROUTE_BASE + N_EXPERTS].astype(jnp.int32)
    padded = (cnt + BLK - 1) // BLK * BLK
    ends = jnp.cumsum(padded)
    starts = ends - padded
    n_valid = (ends[-1] // BLK).astype(jnp.int32)
    blk_ids = jnp.minimum(jnp.arange(n_blocks, dtype=jnp.int32), n_valid - 1)
    block_expert = jnp.sum((ends[None, :] <= (blk_ids * BLK)[:, None]).astype(jnp.int32), axis=1)
    block_expert = jnp.minimum(block_expert, N_EXPERTS - 1)
    blk_ord = jnp.cumsum(jnp.concatenate(
        [jnp.zeros((1,), jnp.int32), (block_expert[1:] != block_expert[:-1]).astype(jnp.int32)]))
    dstarts = jnp.cumsum(cnt) - cnt
    first_blk = starts // BLK
    pick = block_expert[None, :] == jnp.arange(N_EXPERTS, dtype=jnp.int32)[:, None]
    row0 = jnp.sum(jnp.where(pick, (dstarts - first_blk * BLK)[:, None], 0), axis=0) + blk_ids * BLK
    meta3 = meta.reshape(n_steps, META_ROWS, TT)
    hit = (meta3[None, :, COL_EXPERT:COL_EXPERT + TOP_K, :]
           == jnp.arange(N_EXPERTS, dtype=jnp.int32)[:, None, None, None])
    rank = meta3[:, COL_RANK:COL_RANK + TOP_K, :]
    dest_x = rank + jnp.sum(jnp.where(hit, dstarts[:, None, None, None], 0), axis=0)
    dest = rank + jnp.sum(jnp.where(hit, starts[:, None, None, None], 0), axis=0)

    xs_rows = (n_tok * TOP_K + BLK) * SLAB
    xs = pl.pallas_call(
        _dispatch_kernel,
        grid=(n_tok // TD,),
        in_specs=[
            pl.BlockSpec((TD // TT, TOP_K, TT), lambda i: (i, 0, 0), memory_space=pltpu.SMEM),
            pl.BlockSpec(memory_space=pl.ANY),
        ],
        out_specs=pl.BlockSpec(memory_space=pl.ANY),
        out_shape=jax.ShapeDtypeStruct((xs_rows, LANES), F32),
        scratch_shapes=[pltpu.VMEM((3, TD * SLAB, LANES), F32), pltpu.VMEM((BLK * SLAB, LANES), F32),
                        pltpu.SemaphoreType.DMA((3,)), pltpu.SemaphoreType.DMA((2,)), pltpu.SemaphoreType.DMA],
        compiler_params=pltpu.CompilerParams(dimension_semantics=("arbitrary",)),
        name="dispatch",
    )(dest_x, x1s)

    yb = pl.pallas_call(
        _expert_kernel,
        grid_spec=pltpu.PrefetchScalarGridSpec(
            num_scalar_prefetch=5,
            grid=(n_blocks // BLOCKS_PER_STEP,),
            in_specs=[pl.BlockSpec(memory_space=pl.ANY)] * 4,
            out_specs=pl.BlockSpec((BLOCKS_PER_STEP * BLK * SLAB, LANES), lambda i, *_: (i, 0)),
            scratch_shapes=[
                pltpu.VMEM((2 * BLK * SLAB, LANES), F32),
                pltpu.VMEM((2, D, D_EXPERT), F32),
                pltpu.VMEM((2, D, D_EXPERT), F32),
                pltpu.VMEM((2, D_EXPERT, D), F32),
                pltpu.VMEM((D, D_EXPERT), BF16),
                pltpu.VMEM((D, D_EXPERT), BF16),
                pltpu.VMEM((D_EXPERT, D), BF16),
                pltpu.SemaphoreType.DMA((2,)),
                pltpu.SemaphoreType.DMA((2,)),
            ],
        ),
        out_shape=jax.ShapeDtypeStruct((n_slots * SLAB, LANES), F32),
        compiler_params=pltpu.CompilerParams(
            dimension_semantics=("arbitrary",), vmem_limit_bytes=VMEM_LIMIT),
        name="experts",
    )(block_expert, n_valid.reshape(1), row0, blk_ord, padded // BLK, xs, w_gate[0], w_up[0], w_down[0])

    last = n_tok // TD - 1
    out = pl.pallas_call(
        _combine_kernel,
        grid=(n_tok // TD,),
        in_specs=[
            pl.BlockSpec((TD // TT, TOP_K, TT), lambda i: (i, 0, 0), memory_space=pltpu.SMEM),
            pl.BlockSpec((TD // TT, TOP_K, TT), lambda i: (jnp.minimum(i + 1, last), 0, 0), memory_space=pltpu.SMEM),
            pl.BlockSpec((TD * SLAB, LANES), lambda i: (i, 0)),
            pl.BlockSpec((TD, LANES), lambda i: (i, 0)),
            pl.BlockSpec(memory_space=pl.ANY),
            pl.BlockSpec((1, D), lambda i: (0, 0)),
            pl.BlockSpec((1, D), lambda i: (0, 0)),
        ],
        out_specs=pl.BlockSpec((TD, D), lambda i: (i, 0)),
        scratch_shapes=[pltpu.VMEM((2 * TOP_K * TD * SLAB, LANES), F32), pltpu.SemaphoreType.DMA((2,))],
        out_shape=jax.ShapeDtypeStruct((n_tok, D), F32),
        compiler_params=pltpu.CompilerParams(dimension_semantics=("arbitrary",)),
        name="combine",
    )(dest, dest, x1s, route, yb, ln2_g[0].reshape(1, D), ln2_b[0].reshape(1, D))

    return out.reshape(B, T, D).astype(x.dtype)
```

```python
import functools

import numpy as np
import jax
import jax.numpy as jnp
from jax import lax
from jax.experimental import pallas as pl
from jax.experimental.pallas import tpu as pltpu

D_MODEL = 1024
HG_WIDTH = 512
HG_HEAD_DIM = 128
HG_HEADS = 4
GL_WIDTH = 512
GL_HEADS = 4
GL_DV = 128
GL_DK = 64
GL_QK = 256
GATE_RANK = 16
GATE_TAU = 16.0
N_GROUPS = 8
EXPERTS_PER_GROUP = 8
N_EXPERTS = 64
TOP_K = 2
D_EXPERT = 512
DEPTH = 1
DEEPNORM_ALPHA = (2.0 * DEPTH) ** 0.25
LN_EPS = 1e-5
LOG2E = 1.4426950408889634
IN_SPLITS = (HG_WIDTH, HG_WIDTH, HG_WIDTH, HG_WIDTH, GL_QK, GL_QK, GL_WIDTH, GATE_RANK, GL_WIDTH)

LANES = 128
SUB = 128
N_LEVELS = 7
MXU_LEVELS = 3
TT = 256
TD = 512
BLK = 256
BLOCKS_PER_STEP = 2
PROJ_WIDTHS = (HG_WIDTH, HG_WIDTH, HG_WIDTH, HG_WIDTH, GL_QK, GL_QK, GL_WIDTH, GL_WIDTH, LANES)
PROJ_SLICES = tuple(slice(sum(PROJ_WIDTHS[:n]), sum(PROJ_WIDTHS[:n + 1])) for n in range(len(PROJ_WIDTHS)))
PROJ_COLS = sum(PROJ_WIDTHS)
ROUTE_BASE = N_GROUPS
SLAB = 8
COL_EXPERT, COL_GATE, COL_RANK = 0, 2, 4
META_ROWS = 8
V7X_VMEM_BYTES = 64 * 1024 * 1024
VMEM_LIMIT = V7X_VMEM_BYTES * 7 // 8
assert DEPTH == 1 and D_MODEL == SLAB * LANES and GL_DK * 2 == LANES and SUB == 1 << N_LEVELS

F32 = jnp.float32
BF16 = jnp.bfloat16
NT_DIMS = (((1,), (1,)), ((), ()))
TN_DIMS = (((0,), (0,)), ((), ()))


def _decay_matrices():
    t = np.arange(SUB)[:, None]
    j = np.arange(SUB)[None, :]
    mats = [(j <= t)]
    for lvl in range(1, MXU_LEVELS + 1):
        blk = 1 << lvl
        half = blk >> 1
        r = (t // blk) * blk + half - 1
        right = (t % blk) >= half
        mats.append(np.where(right, (j > r) & (j <= t), (j > t) & (j <= r)))
    w = np.concatenate(mats, axis=0).astype(np.float32)
    return np.concatenate([w, w, w], axis=1)


def _level_map():
    t = np.arange(SUB)[:, None]
    s = np.arange(SUB)[None, :]
    x = np.bitwise_xor(t, s)
    lvl = np.floor(np.log2(np.maximum(x, 1))).astype(np.int32) + 1
    lvl = np.where(x == 0, 0, lvl)
    return np.where(s > t, -1, lvl).astype(np.int32)


def _layer_norm(y, g, b):
    mu = jnp.mean(y, axis=-1, keepdims=True)
    yc = y - mu
    var = jnp.mean(yc * yc, axis=-1, keepdims=True)
    return yc * lax.rsqrt(var + LN_EPS) * g + b


def _prefix_rows(x):
    n = x.shape[0]
    row = lax.broadcasted_iota(jnp.int32, (n, 1), 0)
    inc = x
    sh = 1
    while sh < n:
        inc = inc + jnp.where(row >= sh, pltpu.roll(inc, sh, axis=0), 0.0)
        sh *= 2
    return inc - x


def _mixer_kernel(x_ref, winf_ref, wa2_ref, ba_ref, lbl_ref, gain_ref, woutf_ref, ln1g_ref, ln1b_ref,
                  wr_ref, wstk_ref, lev_ref,
                  x1s_ref, route_ref, meta_ref, counts_ref,
                  win_ref, wout_ref, p_scr, o_scr, b_scr, lg_scr, st_scr, carry_scr, *, steps_per_seq):
    si = pl.program_id(0)
    n_tiles = pl.num_programs(0) - 1

    @pl.when(si == 0)
    def _():
        carry_scr[...] = jnp.zeros_like(carry_scr)
        lg_scr[...] = jnp.zeros_like(lg_scr)
        keep = sum(IN_SPLITS[:7])
        wout_ref[...] = woutf_ref[...].astype(BF16)
        win_ref[:, keep + GL_WIDTH:] = jnp.zeros((D_MODEL, PROJ_COLS - keep - GL_WIDTH), BF16)

        def rows(c, carry):
            r = pl.ds(pl.multiple_of(c * SUB, SUB), SUB)
            win_ref[r, 0:keep] = winf_ref[r, 0:keep].astype(BF16)
            win_ref[r, keep:keep + GL_WIDTH] = winf_ref[r, keep + GATE_RANK:keep + GATE_RANK + GL_WIDTH].astype(BF16)
            win_ref[r, keep + GL_WIDTH:keep + GL_WIDTH + GATE_RANK] = winf_ref[r, keep:keep + GATE_RANK].astype(BF16)
            return carry
        lax.fori_loop(0, D_MODEL // SUB, rows, 0)

    @pl.when(si % steps_per_seq == 0)
    def _():
        st_scr[...] = jnp.zeros_like(st_scr)

    xt = x_ref[...]
    p_scr[...] = jnp.dot(xt.astype(BF16), win_ref[...], preferred_element_type=F32)
    _route_tile(si, lg_scr[...], carry_scr, route_ref, meta_ref, counts_ref)

    lbl = lbl_ref[...]
    lbe = jnp.exp(lbl - jnp.max(lbl, axis=0, keepdims=True))
    lb = lbe[0:1, :] / jnp.sum(lbe, axis=0, keepdims=True)

    lev = lev_ref[...]
    tcol = lax.broadcasted_iota(jnp.int32, (SUB, 1), 0)
    qrow = [(tcol & ((1 << l) - 1)) >= (1 << (l - 1)) for l in range(1, N_LEVELS + 1)]
    lane = lax.broadcasted_iota(jnp.int32, (1, LANES), 1)
    half_mask = [(lane < GL_DK), (lane >= GL_DK)]

    def by_halves(l, left_fn, right_fn):
        blk = 1 << l
        half = blk >> 1
        parts = []
        for j in range(SUB // blk):
            lo = j * blk
            parts += [left_fn(lo, lo + half), right_fn(lo + half, lo + blk)]
        return jnp.concatenate(parts, axis=0)

    def subtile(s, carry):
        r0 = pl.multiple_of(s * SUB, SUB)
        rows = pl.ds(r0, SUB)
        hq, hf, hv, hgate, gq, gk, gv, ggate, ga = [p_scr[rows, c] for c in PROJ_SLICES]

        f = lb + (1.0 - lb) * jax.nn.sigmoid(hf)
        kh = 1.0 - f
        lgh = jnp.log(f)
        z = jnp.dot(ga.astype(BF16), wa2_ref[...], preferred_element_type=F32) + ba_ref[...]
        lga = (jnp.minimum(z, 0.0) - jnp.log1p(jnp.exp(-jnp.abs(z)))) * (1.0 / GATE_TAU)
        lg = jnp.concatenate([lgh, lga], axis=1) * LOG2E

        l_hi = lg.astype(BF16)
        rem = lg - l_hi.astype(F32)
        l_mid = rem.astype(BF16)
        l_lo = (rem - l_mid.astype(F32)).astype(BF16)
        sums = jnp.dot(wstk_ref[...], jnp.concatenate([l_hi, l_mid, l_lo], axis=0),
                       preferred_element_type=F32)
        bcum = sums[0:SUB]
        b_scr[...] = bcum
        expos = [bcum, b_scr[SUB - 1:SUB, :] - bcum]
        for l in range(1, N_LEVELS + 1):
            if l <= MXU_LEVELS:
                expos.append(sums[l * SUB:(l + 1) * SUB])
            else:
                expos.append(by_halves(l, lambda lo, mid: b_scr[mid - 1:mid, :] - bcum[lo:mid],
                                       lambda mid, hi: bcum[mid:hi] - b_scr[mid - 1:mid, :]))
        decs = [jnp.exp2(e) for e in expos]

        def group_terms(q, k, c0):
            e_pre = decs[0][:, c0:c0 + LANES]
            e_suf = decs[1][:, c0:c0 + LANES]
            us = []
            for l in range(1, N_LEVELS + 1):
                e_l = decs[1 + l][:, c0:c0 + LANES]
                if l <= MXU_LEVELS:
                    sel = jnp.where(qrow[l - 1], q, k)
                else:
                    sel = by_halves(l, lambda lo, mid: k[lo:mid], lambda mid, hi: q[mid:hi])
                u = sel * e_l
                us.append((u, u.astype(BF16)))
            return us, (q * e_pre).astype(BF16), k * e_suf, e_pre[SUB - 1:SUB, :]

        def scores(us_ab, qks_ab, masks):
            nm = len(masks)
            zero_b = jnp.zeros((SUB, LANES), BF16)
            scs = [[jnp.where(lev == 0, jnp.sum(qk, axis=-1, keepdims=True), 0.0) for qk in qks] for qks in qks_ab]
            for l in range(1, N_LEVELS + 1):
                (ua, uba), (ub_, ubb) = us_ab[0][l - 1], us_ab[1][l - 1]
                rhs = jnp.concatenate([jnp.concatenate([uba, zero_b], axis=1),
                                       jnp.concatenate([zero_b, ubb], axis=1)], axis=0)
                if l <= MXU_LEVELS:
                    rows_l = SUB
                    lhs = jnp.concatenate(
                        [jnp.concatenate([x if m is None else jnp.where(m, x, jnp.zeros_like(x)) for x in (uba, ubb)],
                                         axis=1) for m in masks], axis=0)
                    g = lax.dot_general(lhs, rhs, NT_DIMS, preferred_element_type=F32)
                    for gi in range(2):
                        for n in range(nm):
                            gn = g[n * rows_l:(n + 1) * rows_l, gi * LANES:(gi + 1) * LANES]
                            scs[gi][n] = jnp.where(lev == l, gn, scs[gi][n])
                else:
                    blk = 1 << l
                    half = blk >> 1
                    mids = [j * blk + half for j in range(SUB // blk)]
                    rows_l = SUB // 2
                    uqs = [jnp.concatenate([x[m:m + half] for m in mids], axis=0) for x in (ua, ub_)]
                    lhs = jnp.concatenate(
                        [jnp.concatenate([x if m is None else jnp.where(m, x, 0.0) for x in uqs], axis=1)
                         for m in masks], axis=0)
                    g = lax.dot_general(lhs.astype(BF16), rhs, NT_DIMS, preferred_element_type=F32)
                    for gi in range(2):
                        for n in range(nm):
                            parts = []
                            for j, m in enumerate(mids):
                                gj = g[n * rows_l + j * half:n * rows_l + (j + 1) * half, gi * LANES:(gi + 1) * LANES]
                                parts.append(scs[gi][n][m - half:m])
                                parts.append(jnp.where(lev[m:m + half] == l, gj, scs[gi][n][m:m + half]))
                            scs[gi][n] = jnp.concatenate(parts, axis=0)
            return scs

        def block_diag(a, b):
            za = jnp.zeros_like(a)
            return jnp.concatenate([jnp.concatenate([a, za], axis=1), jnp.concatenate([za, b], axis=1)], axis=0)

        def head_pair(hidx, scs2, qds, kds, vs, g_lasts):
            vb = [v.astype(BF16) for v in vs]
            sts = [st_scr[h] for h in hidx]
            o = jnp.dot(jnp.concatenate([s.astype(BF16) for s in scs2], axis=1), block_diag(*vb),
                        preferred_element_type=F32)
            o = o + lax.dot_general(jnp.concatenate(qds, axis=1), block_diag(*[s.astype(BF16) for s in sts]),
                                    NT_DIMS, preferred_element_type=F32)
            ut = lax.dot_general(jnp.concatenate(vb, axis=1), jnp.concatenate([k.astype(BF16) for k in kds], axis=1),
                                 TN_DIMS, preferred_element_type=F32)
            for n, h in enumerate(hidx):
                st_scr[h] = sts[n] * g_lasts[n] + ut[n * LANES:(n + 1) * LANES, n * LANES:(n + 1) * LANES]
            return o[:, :LANES], o[:, LANES:]

        def finish(o, gate, c0):
            o = o * lax.rsqrt(jnp.mean(o * o, axis=-1, keepdims=True) + LN_EPS)
            o = o * gain_ref[:, c0:c0 + LANES] * (gate * jax.nn.sigmoid(gate))
            o_scr[rows, c0:c0 + LANES] = o

        for g0 in range(0, HG_HEADS, 2):
            terms = []
            for g in (g0, g0 + 1):
                c0 = g * LANES
                q = hq[:, c0:c0 + LANES] * (HG_HEAD_DIM ** -0.5)
                k = kh[:, c0:c0 + LANES]
                terms.append((q * k,) + group_terms(q, k, c0))
            scs = scores([t[1] for t in terms], [[t[0]] for t in terms], [None])
            cols = [slice(g * LANES, (g + 1) * LANES) for g in (g0, g0 + 1)]
            outs = head_pair((g0, g0 + 1), [scs[0][0], scs[1][0]], [t[2] for t in terms], [t[3] for t in terms],
                             [hv[:, c] for c in cols], [t[4] for t in terms])
            for o, c in zip(outs, cols):
                finish(o, hgate[:, c], c.start)

        terms = []
        for p in range(GL_HEADS // 2):
            c0 = p * LANES
            q = gq[:, c0:c0 + LANES] * (GL_DK ** -0.5)
            k = gk[:, c0:c0 + LANES]
            terms.append((q * k,) + group_terms(q, k, HG_WIDTH + c0))
        scs = scores([t[1] for t in terms], [[jnp.where(m, t[0], 0.0) for m in half_mask] for t in terms], half_mask)
        for p in range(GL_HEADS // 2):
            _, _, qd, kd, g_last = terms[p]
            cols = [slice(h * LANES, (h + 1) * LANES) for h in (2 * p, 2 * p + 1)]
            outs = head_pair((HG_HEADS + 2 * p, HG_HEADS + 2 * p + 1), scs[p], [qd, qd],
                             [jnp.where(m, kd, 0.0) for m in half_mask], [gv[:, c] for c in cols], [g_last, g_last])
            for o, c in zip(outs, cols):
                finish(o, ggate[:, c], HG_WIDTH + c.start)
        return carry

    lax.fori_loop(0, TT // SUB, subtile, 0, unroll=True)

    hmix = jnp.dot(o_scr[...].astype(BF16), wout_ref[...], preferred_element_type=F32)
    x1 = _layer_norm(DEEPNORM_ALPHA * xt + hmix, ln1g_ref[...], ln1b_ref[...])
    x1h = x1.astype(BF16)
    x1l = (x1 - x1h.astype(F32)).astype(BF16)
    prod = jnp.dot(jnp.concatenate([x1h, x1l], axis=0), wr_ref[...], preferred_element_type=F32)
    logits = prod[:TT, :LANES] + prod[TT:, :LANES] + prod[:TT, LANES:]

    @pl.when(si < n_tiles)
    def _():
        for s in range(SLAB):
            x1s_ref[pl.ds(s, TT, stride=SLAB), :] = x1[:, s * LANES:(s + 1) * LANES]
        lg_scr[...] = logits


def _route_tile(si, logits, carry_scr, route_ref, meta_ref, counts_ref):
    lane_i = lax.broadcasted_iota(jnp.int32, (TT, LANES), 1)
    lane_f = lane_i.astype(F32)
    neg = jnp.float32(-jnp.inf)
    big = jnp.float32(1e9)

    gl = jnp.where(lane_i < N_GROUPS, logits, neg)
    gmax = jnp.max(gl, axis=-1, keepdims=True)
    gidx = jnp.min(jnp.where(gl == gmax, lane_f, big), axis=-1, keepdims=True)
    p_group = 1.0 / jnp.sum(jnp.exp(gl - gmax), axis=-1, keepdims=True)

    e_lo = ROUTE_BASE + EXPERTS_PER_GROUP * gidx
    el = jnp.where((lane_f >= e_lo) & (lane_f < e_lo + EXPERTS_PER_GROUP), logits, neg)
    m1 = jnp.max(el, axis=-1, keepdims=True)
    i1 = jnp.min(jnp.where(el == m1, lane_f, big), axis=-1, keepdims=True)
    el2 = jnp.where(lane_f == i1, neg, el)
    m2 = jnp.max(el2, axis=-1, keepdims=True)
    i2 = jnp.min(jnp.where(el2 == m2, lane_f, big), axis=-1, keepdims=True)
    r2 = jnp.exp(m2 - m1)
    gate1 = p_group / (1.0 + r2)
    gate2 = p_group * r2 / (1.0 + r2)

    hot1 = lane_f == i1
    hot2 = lane_f == i2
    cnt = jnp.where(si > 0, (hot1 | hot2).astype(F32), 0.0)
    base = carry_scr[...] + _prefix_rows(cnt)
    rank1 = jnp.sum(jnp.where(hot1, base, 0.0), axis=-1, keepdims=True)
    rank2 = jnp.sum(jnp.where(hot2, base, 0.0), axis=-1, keepdims=True)
    total = carry_scr[...] + jnp.sum(cnt, axis=0, keepdims=True)
    carry_scr[...] = total
    counts_ref[...] = total

    route = jnp.zeros((TT, LANES), F32)
    for lane0, vals in ((COL_EXPERT, (i1 - ROUTE_BASE, i2 - ROUTE_BASE)), (COL_GATE, (gate1, gate2)),
                        (COL_RANK, (rank1, rank2))):
        for k, v in enumerate(vals):
            route = jnp.where(lane_i == lane0 + k, v, route)
    route_ref[...] = route
    meta_ref[...] = route.T[0:META_ROWS, :].astype(jnp.int32)


def _dispatch_kernel(dest_ref, x1s_ref, xs_ref, xin, zbuf, in_sems, out_sems, fill_sem):
    i = pl.program_id(0)
    n_steps = pl.num_programs(0)
    tile_rows = TD * SLAB

    def tile_load(t, slot):
        return pltpu.make_async_copy(x1s_ref.at[pl.ds(pl.multiple_of(t * tile_rows, tile_rows), tile_rows), :],
                                     xin.at[slot], in_sems.at[slot])

    def wait_rows(slot, sem):
        for k in range(TOP_K):
            pltpu.make_async_copy(xin.at[slot], xs_ref.at[pl.ds(0, tile_rows), :], sem).wait()

    @pl.when(i == 0)
    def _():
        tile_load(0, 0).start()
        zbuf[...] = jnp.zeros_like(zbuf)
        fill = pltpu.make_async_copy(zbuf, xs_ref.at[pl.ds(xs_ref.shape[0] - BLK * SLAB, BLK * SLAB), :], fill_sem)
        fill.start()
        fill.wait()

    slot = i % 3

    @pl.when(i + 1 < n_steps)
    def _():
        tile_load(i + 1, (i + 1) % 3).start()

    tile_load(i, slot).wait()
    for r in range(TD):
        for k in range(TOP_K):
            dst = pl.multiple_of(dest_ref[r // TT, k, r % TT] * SLAB, SLAB)
            pltpu.make_async_copy(xin.at[slot, pl.ds(r * SLAB, SLAB), :], xs_ref.at[pl.ds(dst, SLAB), :],
                                  out_sems.at[i % 2]).start(priority=k)

    @pl.when(i > 0)
    def _():
        wait_rows((i + 2) % 3, out_sems.at[(i + 1) % 2])

    @pl.when(i + 1 >= n_steps)
    def _():
        wait_rows(slot, out_sems.at[i % 2])


def _expert_kernel(be_ref, nv_ref, row0_ref, ord_ref, nblk_ref, xs_ref, wg_ref, wu_ref, wd_ref, yb_ref,
                   xbuf, wg_f, wu_f, wd_f, wg_s, wu_s, wd_s, sems, wsems):
    nv = nv_ref[0]
    blk_rows = BLK * SLAB

    def block_copy(b, slot):
        r = pl.multiple_of(row0_ref[b] * SLAB, SLAB)
        return pltpu.make_async_copy(xs_ref.at[pl.ds(r, blk_rows), :],
                                     xbuf.at[pl.ds(slot * blk_rows, blk_rows), :], sems.at[slot])

    def weight_copies(e, ws):
        return [pltpu.make_async_copy(src.at[e], dst.at[ws], wsems.at[ws])
                for src, dst in ((wg_ref, wg_f), (wu_ref, wu_f), (wd_ref, wd_f))]

    def block(b, sub):
        slot = sub % 2
        out0 = sub * blk_rows

        @pl.when(b < nv)
        def _():
            e = be_ref[b]
            ws = ord_ref[b] % 2

            @pl.when(b == 0)
            def _():
                block_copy(0, 0).start()
                for c in weight_copies(e, ws):
                    c.start()

            @pl.when(b + 1 < nv)
            def _():
                block_copy(b + 1, 1 - slot).start()

            prev = be_ref[jnp.maximum(b - 1, 0)]

            @pl.when((b == 0) | (e != prev))
            def _():
                j = b + nblk_ref[e]

                @pl.when(j < nv)
                def _():
                    for c in weight_copies(be_ref[jnp.minimum(j, nv - 1)], 1 - ws):
                        c.start()

                for c in weight_copies(e, ws):
                    c.wait()
                wg_s[...] = wg_f[ws].astype(BF16)
                wu_s[...] = wu_f[ws].astype(BF16)
                wd_s[...] = wd_f[ws].astype(BF16)

            block_copy(b, slot).wait()
            xb = jnp.concatenate(
                [xbuf[pl.ds(slot * blk_rows + s, BLK, stride=SLAB), :].astype(BF16) for s in range(SLAB)],
                axis=1)
            a = jnp.dot(xb, wg_s[...], preferred_element_type=F32)
            u = jnp.dot(xb, wu_s[...], preferred_element_type=F32)
            h = (a * jax.nn.sigmoid(a)) * u
            y = jnp.dot(h.astype(BF16), wd_s[...], preferred_element_type=F32)
            for s in range(SLAB):
                yb_ref[pl.ds(out0 + s, BLK, stride=SLAB), :] = y[:, s * LANES:(s + 1) * LANES]

        @pl.when(b >= nv)
        def _():
            yb_ref[pl.ds(out0, blk_rows), :] = jnp.zeros((blk_rows, LANES), F32)

    for sub in range(BLOCKS_PER_STEP):
        block(pl.program_id(0) * BLOCKS_PER_STEP + sub, sub)


def _combine_kernel(dest_ref, dest_next_ref, x1s_ref, route_ref, yb_ref, g_ref, b_ref, out_ref, ybuf, sems):
    i = pl.program_id(0)
    n_steps = pl.num_programs(0)
    half_rows = TD * SLAB
    par_rows = TOP_K * half_rows

    def start_gather(d_ref, par):
        for r in range(TD):
            for k in range(TOP_K):
                src = pl.multiple_of(d_ref[r // TT, k, r % TT] * SLAB, SLAB)
                dst = pl.multiple_of(par * par_rows + (k * TD + r) * SLAB, SLAB)
                pltpu.make_async_copy(yb_ref.at[pl.ds(src, SLAB), :], ybuf.at[pl.ds(dst, SLAB), :],
                                      sems.at[par]).start(priority=k)

    def wait_gather(par):
        pltpu.make_async_copy(yb_ref.at[pl.ds(0, par_rows), :],
                              ybuf.at[pl.ds(par * par_rows, par_rows), :], sems.at[par]).wait()

    par = i % 2

    @pl.when(i == 0)
    def _():
        start_gather(dest_ref, 0)

    wait_gather(par)
    start_gather(dest_next_ref, 1 - par)

    base = pl.multiple_of(par * par_rows, par_rows)
    g1 = route_ref[:, COL_GATE:COL_GATE + 1]
    g2 = route_ref[:, COL_GATE + 1:COL_GATE + 2]
    zs = []
    for s in range(SLAB):
        y1 = ybuf[pl.ds(base + s, TD, stride=SLAB), :]
        y2 = ybuf[pl.ds(base + half_rows + s, TD, stride=SLAB), :]
        zs.append(DEEPNORM_ALPHA * x1s_ref[pl.ds(s, TD, stride=SLAB), :] + (g1 * y1 + g2 * y2))
    z = jnp.concatenate(zs, axis=1)
    out_ref[...] = _layer_norm(z, g_ref[...], b_ref[...])

    @pl.when(i + 1 >= n_steps)
    def _():
        wait_gather(1 - par)


def _const_spec(shape):
    nd = len(shape)
    return pl.BlockSpec(shape, lambda *_: (0,) * nd, pipeline_mode=pl.Buffered(1))


def kernel(x, w_in, w_a2, b_a, lb_logits, norm_h, norm_g, w_out, ln1_g, ln1_b, w_group_router, w_expert_router,
           w_gate, w_up, w_down, ln2_g, ln2_b):
    B, T, D = x.shape
    n_tok = B * T
    assert D == D_MODEL == SLAB * LANES and T % TT == 0 and TD % TT == 0 and n_tok % TD == 0
    n_blocks = n_tok * TOP_K // BLK + N_EXPERTS
    assert n_blocks % BLOCKS_PER_STEP == 0 and BLOCKS_PER_STEP % 2 == 0
    n_slots = n_blocks * BLK

    wa2 =jnp.pad(w_a2[0], ((0, LANES - GATE_RANK), (0, 0))).astype(BF16)
    ba = b_a[0].reshape(1, GL_QK).astype(F32)
    gain = jnp.concatenate([norm_h[0], norm_g[0]]).reshape(1, D).astype(F32)
    wr = jnp.concatenate([w_group_router[0], w_expert_router[0]], axis=-1).astype(F32)
    wr = jnp.pad(wr, ((0, 0), (0, LANES - wr.shape[1])))
    wrh = wr.astype(BF16)
    wr2 = jnp.concatenate([wrh, (wr - wrh.astype(F32)).astype(BF16)], axis=1)
    wstk = jnp.asarray(_decay_matrices(), dtype=BF16)
    lev = jnp.asarray(_level_map())
    n_w = wstk.shape[0]

    n_steps = n_tok // TT
    x1s, route, meta, counts = pl.pallas_call(
        functools.partial(_mixer_kernel, steps_per_seq=T // TT),
        grid=(n_steps + 1,),
        in_specs=[
            pl.BlockSpec((TT, D), lambda i: (jnp.minimum(i, n_steps - 1), 0)),
            _const_spec((D, sum(IN_SPLITS))),
            _const_spec((LANES, GL_QK)),
            _const_spec((1, GL_QK)),
            _const_spec((DEPTH + 1, HG_WIDTH)),
            _const_spec((1, D)),
            _const_spec((D, D)),
            _const_spec((1, D)),
            _const_spec((1, D)),
            _const_spec((D, 2 * LANES)),
            _const_spec((n_w, 3 * SUB)),
            _const_spec((SUB, SUB)),
        ],
        out_specs=[
            pl.BlockSpec((TT * SLAB, LANES), lambda i: (jnp.minimum(i, n_steps - 1), 0)),
            pl.BlockSpec((TT, LANES), lambda i: (jnp.maximum(i - 1, 0), 0)),
            pl.BlockSpec((META_ROWS, TT), lambda i: (jnp.maximum(i - 1, 0), 0)),
            pl.BlockSpec((1, LANES), lambda i: (0, 0)),
        ],
        out_shape=[
            jax.ShapeDtypeStruct((n_tok * SLAB, LANES), F32),
            jax.ShapeDtypeStruct((n_tok, LANES), F32),
            jax.ShapeDtypeStruct((n_steps * META_ROWS, TT), jnp.int32),
            jax.ShapeDtypeStruct((1, LANES), F32),
        ],
        scratch_shapes=[
            pltpu.VMEM((D, PROJ_COLS), BF16),
            pltpu.VMEM((D, D), BF16),
            pltpu.VMEM((TT, PROJ_COLS), F32),
            pltpu.VMEM((TT, D), F32),
            pltpu.VMEM((SUB, HG_WIDTH + GL_QK), F32),
            pltpu.VMEM((TT, LANES), F32),
            pltpu.VMEM((HG_HEADS + GL_HEADS, LANES, LANES), F32),
            pltpu.VMEM((1, LANES), F32),
        ],
        compiler_params=pltpu.CompilerParams(
            dimension_semantics=("arbitrary",), vmem_limit_bytes=VMEM_LIMIT),
        name="mixer",
    )(x.reshape(n_tok, D), w_in[0], wa2, ba, lb_logits.astype(F32), gain, w_out[0], ln1_g[0].reshape(1, D),
      ln1_b[0].reshape(1, D), wr2, wstk, lev)

    cnt = counts[0, ROUTE_BASE:ROUTE_BASE + N_EXPERTS].astype(jnp.int32)
    padded = (cnt + BLK - 1) // BLK * BLK
    ends = jnp.cumsum(padded)
    starts = ends - padded
    n_valid = (ends[-1] // BLK).astype(jnp.int32)
    blk_ids = jnp.minimum(jnp.arange(n_blocks, dtype=jnp.int32), n_valid - 1)
    block_expert = jnp.sum((ends[None, :] <= (blk_ids * BLK)[:, None]).astype(jnp.int32), axis=1)
    block_expert = jnp.minimum(block_expert, N_EXPERTS - 1)
    blk_ord = jnp.cumsum(jnp.concatenate(
        [jnp.zeros((1,), jnp.int32), (block_expert[1:] != block_expert[:-1]).astype(jnp.int32)]))
    dstarts = jnp.cumsum(cnt) - cnt
    first_blk = starts // BLK
    pick = block_expert[None, :] == jnp.arange(N_EXPERTS, dtype=jnp.int32)[:, None]
    row0 = jnp.sum(jnp.where(pick, (dstarts - first_blk * BLK)[:, None], 0), axis=0) + blk_ids * BLK
    meta3 = meta.reshape(n_steps, META_ROWS, TT)
    hit = (meta3[None, :, COL_EXPERT:COL_EXPERT + TOP_K, :]
           == jnp.arange(N_EXPERTS, dtype=jnp.int32)[:, None, None, None])
    rank = meta3[:, COL_RANK:COL_RANK + TOP_K, :]
    dest_x = rank + jnp.sum(jnp.where(hit, dstarts[:, None, None, None], 0), axis=0)
    dest = rank + jnp.sum(jnp.where(hit, starts[:, None, None, None], 0), axis=0)

    xs_rows = (n_tok * TOP_K + BLK) * SLAB
    xs = pl.pallas_call(
        _dispatch_kernel,
        grid=(n_tok // TD,),
        in_specs=[
            pl.BlockSpec((TD // TT, TOP_K, TT), lambda i: (i, 0, 0), memory_space=pltpu.SMEM),
            pl.BlockSpec(memory_space=pl.ANY),
        ],
        out_specs=pl.BlockSpec(memory_space=pl.ANY),
        out_shape=jax.ShapeDtypeStruct((xs_rows, LANES), F32),
        scratch_shapes=[pltpu.VMEM((3, TD * SLAB, LANES), F32), pltpu.VMEM((BLK * SLAB, LANES), F32),
                        pltpu.SemaphoreType.DMA((3,)), pltpu.SemaphoreType.DMA((2,)), pltpu.SemaphoreType.DMA],
        compiler_params=pltpu.CompilerParams(dimension_semantics=("arbitrary",)),
        name="dispatch",
    )(dest_x, x1s)

    yb = pl.pallas_call(
        _expert_kernel,
        grid_spec=pltpu.PrefetchScalarGridSpec(
            num_scalar_prefetch=5,
            grid=(n_blocks // BLOCKS_PER_STEP,),
            in_specs=[pl.BlockSpec(memory_space=pl.ANY)] * 4,
            out_specs=pl.BlockSpec((BLOCKS_PER_STEP * BLK * SLAB, LANES), lambda i, *_: (i, 0)),
            scratch_shapes=[
                pltpu.VMEM((2 * BLK * SLAB, LANES), F32),
                pltpu.VMEM((2, D, D_EXPERT), F32),
                pltpu.VMEM((2, D, D_EXPERT), F32),
                pltpu.VMEM((2, D_EXPERT, D), F32),
                pltpu.VMEM((D, D_EXPERT), BF16),
                pltpu.VMEM((D, D_EXPERT), BF16),
                pltpu.VMEM((D_EXPERT, D), BF16),
                pltpu.SemaphoreType.DMA((2,)),
                pltpu.SemaphoreType.DMA((2,)),
            ],
        ),
        out_shape=jax.ShapeDtypeStruct((n_slots * SLAB, LANES), F32),
        compiler_params=pltpu.CompilerParams(
            dimension_semantics=("arbitrary",), vmem_limit_bytes=VMEM_LIMIT),
        name="experts",
    )(block_expert, n_valid.reshape(1), row0, blk_ord, padded // BLK, xs, w_gate[0], w_up[0], w_down[0])

    last = n_tok // TD - 1
    out = pl.pallas_call(
        _combine_kernel,
        grid=(n_tok // TD,),
        in_specs=[
            pl.BlockSpec((TD // TT, TOP_K, TT), lambda i: (i, 0, 0), memory_space=pltpu.SMEM),
            pl.BlockSpec((TD // TT, TOP_K, TT), lambda i: (jnp.minimum(i + 1, last), 0, 0), memory_space=pltpu.SMEM),
            pl.BlockSpec((TD * SLAB, LANES), lambda i: (i, 0)),
            pl.BlockSpec((TD, LANES), lambda i: (i, 0)),
            pl.BlockSpec(memory_space=pl.ANY),
            pl.BlockSpec((1, D), lambda i: (0, 0)),
            pl.BlockSpec((1, D), lambda i: (0, 0)),
        ],
        out_specs=pl.BlockSpec((TD, D), lambda i: (i, 0)),
        scratch_shapes=[pltpu.VMEM((2 * TOP_K * TD * SLAB, LANES), F32), pltpu.SemaphoreType.DMA((2,))],
        out_shape=jax.ShapeDtypeStruct((n_tok, D), F32),
        compiler_params=pltpu.CompilerParams(dimension_semantics=("arbitrary",)),
        name="combine",
    )(dest, dest, x1s, route, yb, ln2_g[0].reshape(1, D), ln2_b[0].reshape(1, D))

    return out.reshape(B, T, D).astype(x.dtype)
```

```python
import functools

import numpy as np
import jax
import jax.numpy as jnp
from jax import lax
from jax.experimental import pallas as pl
from jax.experimental.pallas import tpu as pltpu

D_MODEL = 1024
HG_WIDTH = 512
HG_HEAD_DIM = 128
HG_HEADS = 4
GL_WIDTH = 512
GL_HEADS = 4
GL_DV = 128
GL_DK = 64
GL_QK = 256
GATE_RANK = 16
GATE_TAU = 16.0
N_GROUPS = 8
EXPERTS_PER_GROUP = 8
N_EXPERTS = 64
TOP_K = 2
D_EXPERT = 512
DEPTH = 1
DEEPNORM_ALPHA = (2.0 * DEPTH) ** 0.25
LN_EPS = 1e-5
LOG2E = 1.4426950408889634
IN_SPLITS = (HG_WIDTH, HG_WIDTH, HG_WIDTH, HG_WIDTH, GL_QK, GL_QK, GL_WIDTH, GATE_RANK, GL_WIDTH)

LANES = 128
SUB = 128
N_LEVELS = 7
MXU_LEVELS = 3
TT = 256
TD = 512
BLK = 256
BLOCKS_PER_STEP = 2
PROJ_WIDTHS = (HG_WIDTH, HG_WIDTH, HG_WIDTH, HG_WIDTH, GL_QK, GL_QK, GL_WIDTH, GL_WIDTH, LANES)
PROJ_SLICES = tuple(slice(sum(PROJ_WIDTHS[:n]), sum(PROJ_WIDTHS[:n + 1])) for n in range(len(PROJ_WIDTHS)))
PROJ_COLS = sum(PROJ_WIDTHS)
ROUTE_BASE = N_GROUPS
SLAB = 8
COL_EXPERT, COL_GATE, COL_RANK = 0, 2, 4
META_ROWS = 8
V7X_VMEM_BYTES = 64 * 1024 * 1024
VMEM_LIMIT = V7X_VMEM_BYTES * 15 // 16
assert DEPTH == 1 and D_MODEL == SLAB * LANES and GL_DK * 2 == LANES and SUB == 1 << N_LEVELS

F32 = jnp.float32
BF16 = jnp.bfloat16
NT_DIMS = (((1,), (1,)), ((), ()))
TN_DIMS = (((0,), (0,)), ((), ()))


def _decay_matrices():
    t = np.arange(SUB)[:, None]
    j = np.arange(SUB)[None, :]
    mats = [(j <= t)]
    for lvl in range(1, MXU_LEVELS + 1):
        blk = 1 << lvl
        half = blk >> 1
        r = (t // blk) * blk + half - 1
        right = (t % blk) >= half
        mats.append(np.where(right, (j > r) & (j <= t), (j > t) & (j <= r)))
    w = np.concatenate(mats, axis=0).astype(np.float32)
    return np.concatenate([w, w, w], axis=1)


def _level_map():
    t = np.arange(SUB)[:, None]
    s = np.arange(SUB)[None, :]
    x = np.bitwise_xor(t, s)
    lvl = np.floor(np.log2(np.maximum(x, 1))).astype(np.int32) + 1
    lvl = np.where(x == 0, 0, lvl)
    return np.where(s > t, -1, lvl).astype(np.int32)


def _layer_norm(y, g, b):
    mu = jnp.mean(y, axis=-1, keepdims=True)
    yc = y - mu
    var = jnp.mean(yc * yc, axis=-1, keepdims=True)
    return yc * lax.rsqrt(var + LN_EPS) * g + b


def _prefix_rows(x):
    n = x.shape[0]
    row = lax.broadcasted_iota(jnp.int32, (n, 1), 0)
    inc = x
    sh = 1
    while sh < n:
        inc = inc + jnp.where(row >= sh, pltpu.roll(inc, sh, axis=0), 0.0)
        sh *= 2
    return inc - x


def _mixer_kernel(x_ref, winf_ref, wa2_ref, ba_ref, lbl_ref, gain_ref, woutf_ref, ln1g_ref, ln1b_ref,
                  wr_ref, wstk_ref, lev_ref, wgf_ref, wuf_ref, wdf_ref,
                  x1s_ref, route_ref, meta_ref, counts_ref, wg16_ref, wu16_ref, wd16_ref,
                  win_ref, wout_ref, p_scr, o_scr, b_scr, lg_scr, st_scr, carry_scr, *, steps_per_seq):
    si = pl.program_id(0)
    n_tiles = pl.num_programs(0) - 1

    @pl.when(si == 0)
    def _():
        carry_scr[...] = jnp.zeros_like(carry_scr)
        lg_scr[...] = jnp.zeros_like(lg_scr)
        keep = sum(IN_SPLITS[:7])
        wout_ref[...] = woutf_ref[...].astype(BF16)
        win_ref[:, keep + GL_WIDTH:] = jnp.zeros((D_MODEL, PROJ_COLS - keep - GL_WIDTH), BF16)

        def rows(c, carry):
            r = pl.ds(pl.multiple_of(c * SUB, SUB), SUB)
            win_ref[r, 0:keep] = winf_ref[r, 0:keep].astype(BF16)
            win_ref[r, keep:keep + GL_WIDTH] = winf_ref[r, keep + GATE_RANK:keep + GATE_RANK + GL_WIDTH].astype(BF16)
            win_ref[r, keep + GL_WIDTH:keep + GL_WIDTH + GATE_RANK] = winf_ref[r, keep:keep + GATE_RANK].astype(BF16)
            return carry
        lax.fori_loop(0, D_MODEL // SUB, rows, 0)

    @pl.when(si % steps_per_seq == 0)
    def _():
        st_scr[...] = jnp.zeros_like(st_scr)

    wg16_ref[...] = wgf_ref[...].astype(BF16)
    wu16_ref[...] = wuf_ref[...].astype(BF16)
    wd16_ref[...] = wdf_ref[...].astype(BF16)

    xt = x_ref[...]
    p_scr[...] = jnp.dot(xt.astype(BF16), win_ref[...], preferred_element_type=F32)
    _route_tile(si, lg_scr[...], carry_scr, route_ref, meta_ref, counts_ref)

    lbl = lbl_ref[...]
    lbe = jnp.exp(lbl - jnp.max(lbl, axis=0, keepdims=True))
    lb = lbe[0:1, :] / jnp.sum(lbe, axis=0, keepdims=True)

    lev = lev_ref[...]
    tcol = lax.broadcasted_iota(jnp.int32, (SUB, 1), 0)
    qrow = [(tcol & ((1 << l) - 1)) >= (1 << (l - 1)) for l in range(1, N_LEVELS + 1)]
    lane = lax.broadcasted_iota(jnp.int32, (1, LANES), 1)
    half_mask = [(lane < GL_DK), (lane >= GL_DK)]

    def by_halves(l, left_fn, right_fn):
        blk = 1 << l
        half = blk >> 1
        parts = []
        for j in range(SUB // blk):
            lo = j * blk
            parts += [left_fn(lo, lo + half), right_fn(lo + half, lo + blk)]
        return jnp.concatenate(parts, axis=0)

    def subtile(s, carry):
        r0 = pl.multiple_of(s * SUB, SUB)
        rows = pl.ds(r0, SUB)
        hq, hf, hv, hgate, gq, gk, gv, ggate, ga = [p_scr[rows, c] for c in PROJ_SLICES]

        f = lb + (1.0 - lb) * jax.nn.sigmoid(hf)
        kh = 1.0 - f
        lgh = jnp.log(f)
        z = jnp.dot(ga.astype(BF16), wa2_ref[...], preferred_element_type=F32) + ba_ref[...]
        lga = (jnp.minimum(z, 0.0) - jnp.log1p(jnp.exp(-jnp.abs(z)))) * (1.0 / GATE_TAU)
        lg = jnp.concatenate([lgh, lga], axis=1) * LOG2E

        l_hi = lg.astype(BF16)
        rem = lg - l_hi.astype(F32)
        l_mid = rem.astype(BF16)
        l_lo = (rem - l_mid.astype(F32)).astype(BF16)
        sums = jnp.dot(wstk_ref[...], jnp.concatenate([l_hi, l_mid, l_lo], axis=0),
                       preferred_element_type=F32)
        bcum = sums[0:SUB]
        b_scr[...] = bcum
        expos = [bcum, b_scr[SUB - 1:SUB, :] - bcum]
        for l in range(1, N_LEVELS + 1):
            if l <= MXU_LEVELS:
                expos.append(sums[l * SUB:(l + 1) * SUB])
            else:
                expos.append(by_halves(l, lambda lo, mid: b_scr[mid - 1:mid, :] - bcum[lo:mid],
                                       lambda mid, hi: bcum[mid:hi] - b_scr[mid - 1:mid, :]))
        decs = [jnp.exp2(e) for e in expos]

        def group_terms(q, k, c0):
            e_pre = decs[0][:, c0:c0 + LANES]
            e_suf = decs[1][:, c0:c0 + LANES]
            us = []
            for l in range(1, N_LEVELS + 1):
                e_l = decs[1 + l][:, c0:c0 + LANES]
                if l <= MXU_LEVELS:
                    sel = jnp.where(qrow[l - 1], q, k)
                else:
                    sel = by_halves(l, lambda lo, mid: k[lo:mid], lambda mid, hi: q[mid:hi])
                u = sel * e_l
                us.append((u, u.astype(BF16)))
            return us, (q * e_pre).astype(BF16), k * e_suf, e_pre[SUB - 1:SUB, :]

        def scores(us_ab, qks_ab, masks):
            nm = len(masks)
            zero_b = jnp.zeros((SUB, LANES), BF16)
            scs = [[jnp.where(lev == 0, jnp.sum(qk, axis=-1, keepdims=True), 0.0) for qk in qks] for qks in qks_ab]
            for l in range(1, N_LEVELS + 1):
                (ua, uba), (ub_, ubb) = us_ab[0][l - 1], us_ab[1][l - 1]
                rhs = jnp.concatenate([jnp.concatenate([uba, zero_b], axis=1),
                                       jnp.concatenate([zero_b, ubb], axis=1)], axis=0)
                if l <= MXU_LEVELS:
                    rows_l = SUB
                    lhs = jnp.concatenate(
                        [jnp.concatenate([x if m is None else jnp.where(m, x, jnp.zeros_like(x)) for x in (uba, ubb)],
                                         axis=1) for m in masks], axis=0)
                    g = lax.dot_general(lhs, rhs, NT_DIMS, preferred_element_type=F32)
                    for gi in range(2):
                        for n in range(nm):
                            gn = g[n * rows_l:(n + 1) * rows_l, gi * LANES:(gi + 1) * LANES]
                            scs[gi][n] = jnp.where(lev == l, gn, scs[gi][n])
                else:
                    blk = 1 << l
                    half = blk >> 1
                    mids = [j * blk + half for j in range(SUB // blk)]
                    rows_l = SUB // 2
                    uqs = [jnp.concatenate([x[m:m + half] for m in mids], axis=0) for x in (ua, ub_)]
                    lhs = jnp.concatenate(
                        [jnp.concatenate([x if m is None else jnp.where(m, x, 0.0) for x in uqs], axis=1)
                         for m in masks], axis=0)
                    g = lax.dot_general(lhs.astype(BF16), rhs, NT_DIMS, preferred_element_type=F32)
                    for gi in range(2):
                        for n in range(nm):
                            parts = []
                            for j, m in enumerate(mids):
                                gj = g[n * rows_l + j * half:n * rows_l + (j + 1) * half, gi * LANES:(gi + 1) * LANES]
                                parts.append(scs[gi][n][m - half:m])
                                parts.append(jnp.where(lev[m:m + half] == l, gj, scs[gi][n][m:m + half]))
                            scs[gi][n] = jnp.concatenate(parts, axis=0)
            return scs

        def block_diag(a, b):
            za = jnp.zeros_like(a)
            return jnp.concatenate([jnp.concatenate([a, za], axis=1), jnp.concatenate([za, b], axis=1)], axis=0)

        def head_pair(hidx, scs2, qds, kds, vs, g_lasts):
            vb = [v.astype(BF16) for v in vs]
            sts = [st_scr[h] for h in hidx]
            o = jnp.dot(jnp.concatenate([s.astype(BF16) for s in scs2], axis=1), block_diag(*vb),
                        preferred_element_type=F32)
            o = o + lax.dot_general(jnp.concatenate(qds, axis=1), block_diag(*[s.astype(BF16) for s in sts]),
                                    NT_DIMS, preferred_element_type=F32)
            ut = lax.dot_general(jnp.concatenate(vb, axis=1), jnp.concatenate([k.astype(BF16) for k in kds], axis=1),
                                 TN_DIMS, preferred_element_type=F32)
            for n, h in enumerate(hidx):
                st_scr[h] = sts[n] * g_lasts[n] + ut[n * LANES:(n + 1) * LANES, n * LANES:(n + 1) * LANES]
            return o[:, :LANES], o[:, LANES:]

        def finish(o, gate, c0):
            o = o * lax.rsqrt(jnp.mean(o * o, axis=-1, keepdims=True) + LN_EPS)
            o = o * gain_ref[:, c0:c0 + LANES] * (gate * jax.nn.sigmoid(gate))
            o_scr[rows, c0:c0 + LANES] = o

        for g0 in range(0, HG_HEADS, 2):
            terms = []
            for g in (g0, g0 + 1):
                c0 = g * LANES
                q = hq[:, c0:c0 + LANES] * (HG_HEAD_DIM ** -0.5)
                k = kh[:, c0:c0 + LANES]
                terms.append((q * k,) + group_terms(q, k, c0))
            scs = scores([t[1] for t in terms], [[t[0]] for t in terms], [None])
            cols = [slice(g * LANES, (g + 1) * LANES) for g in (g0, g0 + 1)]
            outs = head_pair((g0, g0 + 1), [scs[0][0], scs[1][0]], [t[2] for t in terms], [t[3] for t in terms],
                             [hv[:, c] for c in cols], [t[4] for t in terms])
            for o, c in zip(outs, cols):
                finish(o, hgate[:, c], c.start)

        terms = []
        for p in range(GL_HEADS // 2):
            c0 = p * LANES
            q = gq[:, c0:c0 + LANES] * (GL_DK ** -0.5)
            k = gk[:, c0:c0 + LANES]
            terms.append((q * k,) + group_terms(q, k, HG_WIDTH + c0))
        scs = scores([t[1] for t in terms], [[jnp.where(m, t[0], 0.0) for m in half_mask] for t in terms], half_mask)
        for p in range(GL_HEADS // 2):
            _, _, qd, kd, g_last = terms[p]
            cols = [slice(h * LANES, (h + 1) * LANES) for h in (2 * p, 2 * p + 1)]
            outs = head_pair((HG_HEADS + 2 * p, HG_HEADS + 2 * p + 1), scs[p], [qd, qd],
                             [jnp.where(m, kd, 0.0) for m in half_mask], [gv[:, c] for c in cols], [g_last, g_last])
            for o, c in zip(outs, cols):
                finish(o, ggate[:, c], HG_WIDTH + c.start)
        return carry

    lax.fori_loop(0, TT // SUB, subtile, 0, unroll=True)

    hmix = jnp.dot(o_scr[...].astype(BF16), wout_ref[...], preferred_element_type=F32)
    x1 = _layer_norm(DEEPNORM_ALPHA * xt + hmix, ln1g_ref[...], ln1b_ref[...])
    x1h = x1.astype(BF16)
    x1l = (x1 - x1h.astype(F32)).astype(BF16)
    prod = jnp.dot(jnp.concatenate([x1h, x1l], axis=0), wr_ref[...], preferred_element_type=F32)
    logits = prod[:TT, :LANES] + prod[TT:, :LANES] + prod[:TT, LANES:]

    @pl.when(si < n_tiles)
    def _():
        for s in range(SLAB):
            x1s_ref[pl.ds(s, TT, stride=SLAB), :] = x1[:, s * LANES:(s + 1) * LANES]
        lg_scr[...] = logits


def _route_tile(si, logits, carry_scr, route_ref, meta_ref, counts_ref):
    lane_i = lax.broadcasted_iota(jnp.int32, (TT, LANES), 1)
    lane_f = lane_i.astype(F32)
    neg = jnp.float32(-jnp.inf)
    big = jnp.float32(1e9)

    gl = jnp.where(lane_i < N_GROUPS, logits, neg)
    gmax = jnp.max(gl, axis=-1, keepdims=True)
    gidx = jnp.min(jnp.where(gl == gmax, lane_f, big), axis=-1, keepdims=True)
    p_group = 1.0 / jnp.sum(jnp.exp(gl - gmax), axis=-1, keepdims=True)

    e_lo = ROUTE_BASE + EXPERTS_PER_GROUP * gidx
    el = jnp.where((lane_f >= e_lo) & (lane_f < e_lo + EXPERTS_PER_GROUP), logits, neg)
    m1 = jnp.max(el, axis=-1, keepdims=True)
    i1 = jnp.min(jnp.where(el == m1, lane_f, big), axis=-1, keepdims=True)
    el2 = jnp.where(lane_f == i1, neg, el)
    m2 = jnp.max(el2, axis=-1, keepdims=True)
    i2 = jnp.min(jnp.where(el2 == m2, lane_f, big), axis=-1, keepdims=True)
    r2 = jnp.exp(m2 - m1)
    gate1 = p_group / (1.0 + r2)
    gate2 = p_group * r2 / (1.0 + r2)

    hot1 = lane_f == i1
    hot2 = lane_f == i2
    cnt = jnp.where(si > 0, (hot1 | hot2).astype(F32), 0.0)
    base = carry_scr[...] + _prefix_rows(cnt)
    rank1 = jnp.sum(jnp.where(hot1, base, 0.0), axis=-1, keepdims=True)
    rank2 = jnp.sum(jnp.where(hot2, base, 0.0), axis=-1, keepdims=True)
    total = carry_scr[...] + jnp.sum(cnt, axis=0, keepdims=True)
    carry_scr[...] = total
    counts_ref[...] = total

    route = jnp.zeros((TT, LANES), F32)
    for lane0, vals in ((COL_EXPERT, (i1 - ROUTE_BASE, i2 - ROUTE_BASE)), (COL_GATE, (gate1, gate2)),
                        (COL_RANK, (rank1, rank2))):
        for k, v in enumerate(vals):
            route = jnp.where(lane_i == lane0 + k, v, route)
    route_ref[...] = route
    meta_ref[...] = route.T[0:META_ROWS, :].astype(jnp.int32)


def _dispatch_kernel(dest_ref, x1s_ref, xs_ref, xin, zbuf, in_sems, out_sems, fill_sem):
    i = pl.program_id(0)
    n_steps = pl.num_programs(0)
    tile_rows = TD * SLAB

    def tile_load(t, slot):
        return pltpu.make_async_copy(x1s_ref.at[pl.ds(pl.multiple_of(t * tile_rows, tile_rows), tile_rows), :],
                                     xin.at[slot], in_sems.at[slot])

    def wait_rows(slot, sem):
        for k in range(TOP_K):
            pltpu.make_async_copy(xin.at[slot], xs_ref.at[pl.ds(0, tile_rows), :], sem).wait()

    @pl.when(i == 0)
    def _():
        tile_load(0, 0).start()
        zbuf[...] = jnp.zeros_like(zbuf)
        fill = pltpu.make_async_copy(zbuf, xs_ref.at[pl.ds(xs_ref.shape[0] - BLK * SLAB, BLK * SLAB), :], fill_sem)
        fill.start()
        fill.wait()

    slot = i % 3

    @pl.when(i + 1 < n_steps)
    def _():
        tile_load(i + 1, (i + 1) % 3).start()

    tile_load(i, slot).wait()
    for r in range(TD):
        for k in range(TOP_K):
            dst = pl.multiple_of(dest_ref[r // TT, k, r % TT] * SLAB, SLAB)
            pltpu.make_async_copy(xin.at[slot, pl.ds(r * SLAB, SLAB), :], xs_ref.at[pl.ds(dst, SLAB), :],
                                  out_sems.at[i % 2]).start(priority=k)

    @pl.when(i > 0)
    def _():
        wait_rows((i + 2) % 3, out_sems.at[(i + 1) % 2])

    @pl.when(i + 1 >= n_steps)
    def _():
        wait_rows(slot, out_sems.at[i % 2])


def _expert_kernel(be_ref, nv_ref, row0_ref, ord_ref, nblk_ref, xs_ref, wg_ref, wu_ref, wd_ref, yb_ref,
                   xbuf, wg_s, wu_s, wd_s, sems, wsems):
    nv = nv_ref[0]
    blk_rows = BLK * SLAB

    def block_copy(b, slot):
        r = pl.multiple_of(row0_ref[b] * SLAB, SLAB)
        return pltpu.make_async_copy(xs_ref.at[pl.ds(r, blk_rows), :],
                                     xbuf.at[pl.ds(slot * blk_rows, blk_rows), :], sems.at[slot])

    def weight_copies(e, ws):
        return [pltpu.make_async_copy(src.at[e], dst.at[ws], wsems.at[ws])
                for src, dst in ((wg_ref, wg_s), (wu_ref, wu_s), (wd_ref, wd_s))]

    def block(b, sub):
        slot = sub % 2
        out0 = sub * blk_rows

        @pl.when(b < nv)
        def _():
            e = be_ref[b]
            ws = ord_ref[b] % 2

            @pl.when(b == 0)
            def _():
                block_copy(0, 0).start()
                for c in weight_copies(e, ws):
                    c.start()

            @pl.when(b + 1 < nv)
            def _():
                block_copy(b + 1, 1 - slot).start()

            prev = be_ref[jnp.maximum(b - 1, 0)]

            @pl.when((b == 0) | (e != prev))
            def _():
                j = b + nblk_ref[e]

                @pl.when(j < nv)
                def _():
                    for c in weight_copies(be_ref[jnp.minimum(j, nv - 1)], 1 - ws):
                        c.start()

                for c in weight_copies(e, ws):
                    c.wait()

            block_copy(b, slot).wait()
            xb = jnp.concatenate(
                [xbuf[pl.ds(slot * blk_rows + s, BLK, stride=SLAB), :].astype(BF16) for s in range(SLAB)],
                axis=1)
            a = jnp.dot(xb, wg_s[ws], preferred_element_type=F32)
            u = jnp.dot(xb, wu_s[ws], preferred_element_type=F32)
            h = (a * jax.nn.sigmoid(a)) * u
            y = jnp.dot(h.astype(BF16), wd_s[ws], preferred_element_type=F32)
            for s in range(SLAB):
                yb_ref[pl.ds(out0 + s, BLK, stride=SLAB), :] = y[:, s * LANES:(s + 1) * LANES]

        @pl.when(b >= nv)
        def _():
            yb_ref[pl.ds(out0, blk_rows), :] = jnp.zeros((blk_rows, LANES), F32)

    for sub in range(BLOCKS_PER_STEP):
        block(pl.program_id(0) * BLOCKS_PER_STEP + sub, sub)


def _combine_kernel(dest_ref, dest_next_ref, x1s_ref, route_ref, yb_ref, g_ref, b_ref, out_ref, ybuf, sems):
    i = pl.program_id(0)
    n_steps = pl.num_programs(0)
    half_rows = TD * SLAB
    par_rows = TOP_K * half_rows

    def start_gather(d_ref, par):
        for r in range(TD):
            for k in range(TOP_K):
                src = pl.multiple_of(d_ref[r // TT, k, r % TT] * SLAB, SLAB)
                dst = pl.multiple_of(par * par_rows + (k * TD + r) * SLAB, SLAB)
                pltpu.make_async_copy(yb_ref.at[pl.ds(src, SLAB), :], ybuf.at[pl.ds(dst, SLAB), :],
                                      sems.at[par]).start(priority=k)

    def wait_gather(par):
        pltpu.make_async_copy(yb_ref.at[pl.ds(0, par_rows), :],
                              ybuf.at[pl.ds(par * par_rows, par_rows), :], sems.at[par]).wait()

    par = i % 2

    @pl.when(i == 0)
    def _():
        start_gather(dest_ref, 0)

    wait_gather(par)
    start_gather(dest_next_ref, 1 - par)

    base = pl.multiple_of(par * par_rows, par_rows)
    g1 = route_ref[:, COL_GATE:COL_GATE + 1]
    g2 = route_ref[:, COL_GATE + 1:COL_GATE + 2]
    zs = []
    for s in range(SLAB):
        y1 = ybuf[pl.ds(base + s, TD, stride=SLAB), :]
        y2 = ybuf[pl.ds(base + half_rows + s, TD, stride=SLAB), :]
        zs.append(DEEPNORM_ALPHA * x1s_ref[pl.ds(s, TD, stride=SLAB), :] + (g1 * y1 + g2 * y2))
    z = jnp.concatenate(zs, axis=1)
    out_ref[...] = _layer_norm(z, g_ref[...], b_ref[...])

    @pl.when(i + 1 >= n_steps)
    def _():
        wait_gather(1 - par)


def _const_spec(shape):
    nd = len(shape)
    return pl.BlockSpec(shape, lambda *_: (0,) * nd, pipeline_mode=pl.Buffered(1))


def kernel(x, w_in, w_a2, b_a, lb_logits, norm_h, norm_g, w_out, ln1_g, ln1_b, w_group_router, w_expert_router,
           w_gate, w_up, w_down, ln2_g, ln2_b):
    B, T, D = x.shape
    n_tok = B * T
    assert D == D_MODEL == SLAB * LANES and T % TT == 0 and TD % TT == 0 and n_tok % TD == 0
    n_blocks = n_tok * TOP_K // BLK + N_EXPERTS
    assert n_blocks % BLOCKS_PER_STEP == 0 and BLOCKS_PER_STEP % 2 == 0
    n_slots = n_blocks * BLK

    wa2 =jnp.pad(w_a2[0], ((0, LANES - GATE_RANK), (0, 0))).astype(BF16)
    ba = b_a[0].reshape(1, GL_QK).astype(F32)
    gain = jnp.concatenate([norm_h[0], norm_g[0]]).reshape(1, D).astype(F32)
    wr = jnp.concatenate([w_group_router[0], w_expert_router[0]], axis=-1).astype(F32)
    wr = jnp.pad(wr, ((0, 0), (0, LANES - wr.shape[1])))
    wrh = wr.astype(BF16)
    wr2 = jnp.concatenate([wrh, (wr - wrh.astype(F32)).astype(BF16)], axis=1)
    wstk = jnp.asarray(_decay_matrices(), dtype=BF16)
    lev = jnp.asarray(_level_map())
    n_w = wstk.shape[0]

    n_steps = n_tok // TT
    assert n_steps + 1 >= N_EXPERTS
    per_step_expert = lambda i: (jnp.minimum(i, N_EXPERTS - 1), 0, 0)
    x1s, route, meta, counts, wg16, wu16, wd16 = pl.pallas_call(
        functools.partial(_mixer_kernel, steps_per_seq=T // TT),
        grid=(n_steps + 1,),
        in_specs=[
            pl.BlockSpec((TT, D), lambda i: (jnp.minimum(i, n_steps - 1), 0)),
            _const_spec((D, sum(IN_SPLITS))),
            _const_spec((LANES, GL_QK)),
            _const_spec((1, GL_QK)),
            _const_spec((DEPTH + 1, HG_WIDTH)),
            _const_spec((1, D)),
            _const_spec((D, D)),
            _const_spec((1, D)),
            _const_spec((1, D)),
            _const_spec((D, 2 * LANES)),
            _const_spec((n_w, 3 * SUB)),
            _const_spec((SUB, SUB)),
            pl.BlockSpec((None, D, D_EXPERT), per_step_expert),
            pl.BlockSpec((None, D, D_EXPERT), per_step_expert),
            pl.BlockSpec((None, D_EXPERT, D), per_step_expert),
        ],
        out_specs=[
            pl.BlockSpec((TT * SLAB, LANES), lambda i: (jnp.minimum(i, n_steps - 1), 0)),
            pl.BlockSpec((TT, LANES), lambda i: (jnp.maximum(i - 1, 0), 0)),
            pl.BlockSpec((META_ROWS, TT), lambda i: (jnp.maximum(i - 1, 0), 0)),
            pl.BlockSpec((1, LANES), lambda i: (0, 0)),
            pl.BlockSpec((None, D, D_EXPERT), per_step_expert),
            pl.BlockSpec((None, D, D_EXPERT), per_step_expert),
            pl.BlockSpec((None, D_EXPERT, D), per_step_expert),
        ],
        out_shape=[
            jax.ShapeDtypeStruct((n_tok * SLAB, LANES), F32),
            jax.ShapeDtypeStruct((n_tok, LANES), F32),
            jax.ShapeDtypeStruct((n_steps * META_ROWS, TT), jnp.int32),
            jax.ShapeDtypeStruct((1, LANES), F32),
            jax.ShapeDtypeStruct((N_EXPERTS, D, D_EXPERT), BF16),
            jax.ShapeDtypeStruct((N_EXPERTS, D, D_EXPERT), BF16),
            jax.ShapeDtypeStruct((N_EXPERTS, D_EXPERT, D), BF16),
        ],
        scratch_shapes=[
            pltpu.VMEM((D, PROJ_COLS), BF16),
            pltpu.VMEM((D, D), BF16),
            pltpu.VMEM((TT, PROJ_COLS), F32),
            pltpu.VMEM((TT, D), F32),
            pltpu.VMEM((SUB, HG_WIDTH + GL_QK), F32),
            pltpu.VMEM((TT, LANES), F32),
            pltpu.VMEM((HG_HEADS + GL_HEADS, LANES, LANES), F32),
            pltpu.VMEM((1, LANES), F32),
        ],
        compiler_params=pltpu.CompilerParams(
            dimension_semantics=("arbitrary",), vmem_limit_bytes=VMEM_LIMIT),
        name="mixer",
    )(x.reshape(n_tok, D), w_in[0], wa2, ba, lb_logits.astype(F32), gain, w_out[0], ln1_g[0].reshape(1, D),
      ln1_b[0].reshape(1, D), wr2, wstk, lev, w_gate[0], w_up[0], w_down[0])

    cnt = counts[0, ROUTE_BASE:ROUTE_BASE + N_EXPERTS].astype(jnp.int32)
    padded = (cnt + BLK - 1) // BLK * BLK
    ends = jnp.cumsum(padded)
    starts = ends - padded
    n_valid = (ends[-1] // BLK).astype(jnp.int32)
    blk_ids = jnp.minimum(jnp.arange(n_blocks, dtype=jnp.int32), n_valid - 1)
    block_expert = jnp.sum((ends[None, :] <= (blk_ids * BLK)[:, None]).astype(jnp.int32), axis=1)
    block_expert = jnp.minimum(block_expert, N_EXPERTS - 1)
    blk_ord = jnp.cumsum(jnp.concatenate(
        [jnp.zeros((1,), jnp.int32), (block_expert[1:] != block_expert[:-1]).astype(jnp.int32)]))
    dstarts = jnp.cumsum(cnt) - cnt
    first_blk = starts // BLK
    pick = block_expert[None, :] == jnp.arange(N_EXPERTS, dtype=jnp.int32)[:, None]
    row0 = jnp.sum(jnp.where(pick, (dstarts - first_blk * BLK)[:, None], 0), axis=0) + blk_ids * BLK
    meta3 = meta.reshape(n_steps, META_ROWS, TT)
    hit = (meta3[None, :, COL_EXPERT:COL_EXPERT + TOP_K, :]
           == jnp.arange(N_EXPERTS, dtype=jnp.int32)[:, None, None, None])
    rank = meta3[:, COL_RANK:COL_RANK + TOP_K, :]
    dest_x = rank + jnp.sum(jnp.where(hit, dstarts[:, None, None, None], 0), axis=0)
    dest = rank + jnp.sum(jnp.where(hit, starts[:, None, None, None], 0), axis=0)

    xs_rows = (n_tok * TOP_K + BLK) * SLAB
    xs = pl.pallas_call(
        _dispatch_kernel,
        grid=(n_tok // TD,),
        in_specs=[
            pl.BlockSpec((TD // TT, TOP_K, TT), lambda i: (i, 0, 0), memory_space=pltpu.SMEM),
            pl.BlockSpec(memory_space=pl.ANY),
        ],
        out_specs=pl.BlockSpec(memory_space=pl.ANY),
        out_shape=jax.ShapeDtypeStruct((xs_rows, LANES), F32),
        scratch_shapes=[pltpu.VMEM((3, TD * SLAB, LANES), F32), pltpu.VMEM((BLK * SLAB, LANES), F32),
                        pltpu.SemaphoreType.DMA((3,)), pltpu.SemaphoreType.DMA((2,)), pltpu.SemaphoreType.DMA],
        compiler_params=pltpu.CompilerParams(dimension_semantics=("arbitrary",)),
        name="dispatch",
    )(dest_x, x1s)

    yb = pl.pallas_call(
        _expert_kernel,
        grid_spec=pltpu.PrefetchScalarGridSpec(
            num_scalar_prefetch=5,
            grid=(n_blocks // BLOCKS_PER_STEP,),
            in_specs=[pl.BlockSpec(memory_space=pl.ANY)] * 4,
            out_specs=pl.BlockSpec((BLOCKS_PER_STEP * BLK * SLAB, LANES), lambda i, *_: (i, 0)),
            scratch_shapes=[
                pltpu.VMEM((2 * BLK * SLAB, LANES), F32),
                pltpu.VMEM((2, D, D_EXPERT), BF16),
                pltpu.VMEM((2, D, D_EXPERT), BF16),
                pltpu.VMEM((2, D_EXPERT, D), BF16),
                pltpu.SemaphoreType.DMA((2,)),
                pltpu.SemaphoreType.DMA((2,)),
            ],
        ),
        out_shape=jax.ShapeDtypeStruct((n_slots * SLAB, LANES), F32),
        compiler_params=pltpu.CompilerParams(
            dimension_semantics=("arbitrary",), vmem_limit_bytes=VMEM_LIMIT),
        name="experts",
    )(block_expert, n_valid.reshape(1), row0, blk_ord, padded // BLK, xs, wg16, wu16, wd16)

    last = n_tok // TD - 1
    out = pl.pallas_call(
        _combine_kernel,
        grid=(n_tok // TD,),
        in_specs=[
            pl.BlockSpec((TD // TT, TOP_K, TT), lambda i: (i, 0, 0), memory_space=pltpu.SMEM),
            pl.BlockSpec((TD // TT, TOP_K, TT), lambda i: (jnp.minimum(i + 1, last), 0, 0), memory_space=pltpu.SMEM),
            pl.BlockSpec((TD * SLAB, LANES), lambda i: (i, 0)),
            pl.BlockSpec((TD, LANES), lambda i: (i, 0)),
            pl.BlockSpec(memory_space=pl.ANY),
            pl.BlockSpec((1, D), lambda i: (0, 0)),
            pl.BlockSpec((1, D), lambda i: (0, 0)),
        ],
        out_specs=pl.BlockSpec((TD, D), lambda i: (i, 0)),
        scratch_shapes=[pltpu.VMEM((2 * TOP_K * TD * SLAB, LANES), F32), pltpu.SemaphoreType.DMA((2,))],
        out_shape=jax.ShapeDtypeStruct((n_tok, D), F32),
        compiler_params=pltpu.CompilerParams(dimension_semantics=("arbitrary",)),
        name="combine",
    )(dest, dest, x1s, route, yb, ln2_g[0].reshape(1, D), ln2_b[0].reshape(1, D))

    return out.reshape(B, T, D).astype(x.dtype)
```

```python
import functools

import numpy as np
import jax
import jax.numpy as jnp
from jax import lax
from jax.experimental import pallas as pl
from jax.experimental.pallas import tpu as pltpu

D_MODEL = 1024
HG_WIDTH = 512
HG_HEAD_DIM = 128
HG_HEADS = 4
GL_WIDTH = 512
GL_HEADS = 4
GL_DV = 128
GL_DK = 64
GL_QK = 256
GATE_RANK = 16
GATE_TAU = 16.0
N_GROUPS = 8
EXPERTS_PER_GROUP = 8
N_EXPERTS = 64
TOP_K = 2
D_EXPERT = 512
DEPTH = 1
DEEPNORM_ALPHA = (2.0 * DEPTH) ** 0.25
LN_EPS = 1e-5
LOG2E = 1.4426950408889634
IN_SPLITS = (HG_WIDTH, HG_WIDTH, HG_WIDTH, HG_WIDTH, GL_QK, GL_QK, GL_WIDTH, GATE_RANK, GL_WIDTH)

LANES = 128
SUB = 128
N_LEVELS = 7
MXU_LEVELS = 3
TT = 256
TD = 512
BLK = 256
BLOCKS_PER_STEP = 2
PROJ_WIDTHS = (HG_WIDTH, HG_WIDTH, HG_WIDTH, HG_WIDTH, GL_QK, GL_QK, GL_WIDTH, GL_WIDTH, LANES)
PROJ_SLICES = tuple(slice(sum(PROJ_WIDTHS[:n]), sum(PROJ_WIDTHS[:n + 1])) for n in range(len(PROJ_WIDTHS)))
PROJ_COLS = sum(PROJ_WIDTHS)
ROUTE_BASE = N_GROUPS
SLAB = 8
COL_EXPERT, COL_GATE, COL_RANK = 0, 2, 4
META_ROWS = 8
V7X_VMEM_BYTES = 64 * 1024 * 1024
VMEM_LIMIT = V7X_VMEM_BYTES * 7 // 8
assert DEPTH == 1 and D_MODEL == SLAB * LANES and GL_DK * 2 == LANES and SUB == 1 << N_LEVELS

F32 = jnp.float32
BF16 = jnp.bfloat16
NT_DIMS = (((1,), (1,)), ((), ()))
TN_DIMS = (((0,), (0,)), ((), ()))


def _decay_matrices():
    t = np.arange(SUB)[:, None]
    j = np.arange(SUB)[None, :]
    mats = [(j <= t)]
    for lvl in range(1, MXU_LEVELS + 1):
        blk = 1 << lvl
        half = blk >> 1
        r = (t // blk) * blk + half - 1
        right = (t % blk) >= half
        mats.append(np.where(right, (j > r) & (j <= t), (j > t) & (j <= r)))
    w = np.concatenate(mats, axis=0).astype(np.float32)
    return np.concatenate([w, w], axis=1)


def _level_map():
    t = np.arange(SUB)[:, None]
    s = np.arange(SUB)[None, :]
    x = np.bitwise_xor(t, s)
    lvl = np.floor(np.log2(np.maximum(x, 1))).astype(np.int32) + 1
    lvl = np.where(x == 0, 0, lvl)
    return np.where(s > t, -1, lvl).astype(np.int32)


def _layer_norm(y, g, b):
    mu = jnp.mean(y, axis=-1, keepdims=True)
    yc = y - mu
    var = jnp.mean(yc * yc, axis=-1, keepdims=True)
    return yc * lax.rsqrt(var + LN_EPS) * g + b


def _prefix_rows(x):
    n = x.shape[0]
    row = lax.broadcasted_iota(jnp.int32, (n, 1), 0)
    inc = x
    sh = 1
    while sh < n:
        inc = inc + jnp.where(row >= sh, pltpu.roll(inc, sh, axis=0), 0.0)
        sh *= 2
    return inc - x


def _mixer_kernel(x_ref, winf_ref, wa2_ref, ba_ref, lbl_ref, gain_ref, woutf_ref, ln1g_ref, ln1b_ref,
                  wr_ref, wstk_ref, lev_ref,
                  x1s_ref, route_ref, meta_ref, counts_ref,
                  win_ref, wout_ref, p_scr, o_scr, b_scr, lg_scr, st_scr, carry_scr, *, steps_per_seq):
    si = pl.program_id(0)
    n_tiles = pl.num_programs(0) - 1

    @pl.when(si == 0)
    def _():
        carry_scr[...] = jnp.zeros_like(carry_scr)
        lg_scr[...] = jnp.zeros_like(lg_scr)
        keep = sum(IN_SPLITS[:7])
        wout_ref[...] = woutf_ref[...].astype(BF16)
        win_ref[:, keep + GL_WIDTH:] = jnp.zeros((D_MODEL, PROJ_COLS - keep - GL_WIDTH), BF16)

        def rows(c, carry):
            r = pl.ds(pl.multiple_of(c * SUB, SUB), SUB)
            win_ref[r, 0:keep] = winf_ref[r, 0:keep].astype(BF16)
            win_ref[r, keep:keep + GL_WIDTH] = winf_ref[r, keep + GATE_RANK:keep + GATE_RANK + GL_WIDTH].astype(BF16)
            win_ref[r, keep + GL_WIDTH:keep + GL_WIDTH + GATE_RANK] = winf_ref[r, keep:keep + GATE_RANK].astype(BF16)
            return carry
        lax.fori_loop(0, D_MODEL // SUB, rows, 0)

    @pl.when(si % steps_per_seq == 0)
    def _():
        st_scr[...] = jnp.zeros_like(st_scr)

    xt = x_ref[...]
    p_scr[...] = jnp.dot(xt.astype(BF16), win_ref[...], preferred_element_type=F32)
    _route_tile(si, lg_scr[...], carry_scr, route_ref, meta_ref, counts_ref)

    lbl = lbl_ref[...]
    lbe = jnp.exp(lbl - jnp.max(lbl, axis=0, keepdims=True))
    lb = lbe[0:1, :] / jnp.sum(lbe, axis=0, keepdims=True)

    lev = lev_ref[...]
    tcol = lax.broadcasted_iota(jnp.int32, (SUB, 1), 0)
    qrow = [(tcol & ((1 << l) - 1)) >= (1 << (l - 1)) for l in range(1, N_LEVELS + 1)]
    lane = lax.broadcasted_iota(jnp.int32, (1, LANES), 1)
    half_mask = [(lane < GL_DK), (lane >= GL_DK)]

    def by_halves(l, left_fn, right_fn):
        blk = 1 << l
        half = blk >> 1
        parts = []
        for j in range(SUB // blk):
            lo = j * blk
            parts += [left_fn(lo, lo + half), right_fn(lo + half, lo + blk)]
        return jnp.concatenate(parts, axis=0)

    def subtile(s, carry):
        r0 = pl.multiple_of(s * SUB, SUB)
        rows = pl.ds(r0, SUB)
        hq, hf, hv, hgate, gq, gk, gv, ggate, ga = [p_scr[rows, c] for c in PROJ_SLICES]

        f = lb + (1.0 - lb) * jax.nn.sigmoid(hf)
        kh = 1.0 - f
        lgh = jnp.log(f)
        z = jnp.dot(ga.astype(BF16), wa2_ref[...], preferred_element_type=F32) + ba_ref[...]
        lga = (jnp.minimum(z, 0.0) - jnp.log1p(jnp.exp(-jnp.abs(z)))) * (1.0 / GATE_TAU)
        lg = jnp.concatenate([lgh, lga], axis=1) * LOG2E

        l_hi = lg.astype(BF16)
        l_mid = (lg - l_hi.astype(F32)).astype(BF16)
        sums = jnp.dot(wstk_ref[...], jnp.concatenate([l_hi, l_mid], axis=0),
                       preferred_element_type=F32)
        bcum = sums[0:SUB]
        b_scr[...] = bcum
        expos = [bcum, b_scr[SUB - 1:SUB, :] - bcum]
        for l in range(1, N_LEVELS + 1):
            if l <= MXU_LEVELS:
                expos.append(sums[l * SUB:(l + 1) * SUB])
            else:
                expos.append(by_halves(l, lambda lo, mid: b_scr[mid - 1:mid, :] - bcum[lo:mid],
                                       lambda mid, hi: bcum[mid:hi] - b_scr[mid - 1:mid, :]))
        decs = [jnp.exp2(e) for e in expos]

        def group_terms(q, k, c0):
            e_pre = decs[0][:, c0:c0 + LANES]
            e_suf = decs[1][:, c0:c0 + LANES]
            us = []
            for l in range(1, N_LEVELS + 1):
                e_l = decs[1 + l][:, c0:c0 + LANES]
                if l <= MXU_LEVELS:
                    sel = jnp.where(qrow[l - 1], q, k)
                else:
                    sel = by_halves(l, lambda lo, mid: k[lo:mid], lambda mid, hi: q[mid:hi])
                u = sel * e_l
                us.append((u, u.astype(BF16)))
            return us, (q * e_pre).astype(BF16), k * e_suf, e_pre[SUB - 1:SUB, :]

        def scores(us_ab, qks_ab, masks):
            nm = len(masks)
            zero_b = jnp.zeros((SUB, LANES), BF16)
            scs = [[jnp.where(lev == 0, jnp.sum(qk, axis=-1, keepdims=True), 0.0) for qk in qks] for qks in qks_ab]
            for l in range(1, N_LEVELS + 1):
                (ua, uba), (ub_, ubb) = us_ab[0][l - 1], us_ab[1][l - 1]
                rhs = jnp.concatenate([jnp.concatenate([uba, zero_b], axis=1),
                                       jnp.concatenate([zero_b, ubb], axis=1)], axis=0)
                if l <= MXU_LEVELS:
                    rows_l = SUB
                    lhs = jnp.concatenate(
                        [jnp.concatenate([x if m is None else jnp.where(m, x, jnp.zeros_like(x)) for x in (uba, ubb)],
                                         axis=1) for m in masks], axis=0)
                    g = lax.dot_general(lhs, rhs, NT_DIMS, preferred_element_type=F32)
                    for gi in range(2):
                        for n in range(nm):
                            gn = g[n * rows_l:(n + 1) * rows_l, gi * LANES:(gi + 1) * LANES]
                            scs[gi][n] = jnp.where(lev == l, gn, scs[gi][n])
                else:
                    blk = 1 << l
                    half = blk >> 1
                    mids = [j * blk + half for j in range(SUB // blk)]
                    rows_l = SUB // 2
                    uqs = [jnp.concatenate([x[m:m + half] for m in mids], axis=0) for x in (ua, ub_)]
                    lhs = jnp.concatenate(
                        [jnp.concatenate([x if m is None else jnp.where(m, x, 0.0) for x in uqs], axis=1)
                         for m in masks], axis=0)
                    g = lax.dot_general(lhs.astype(BF16), rhs, NT_DIMS, preferred_element_type=F32)
                    for gi in range(2):
                        for n in range(nm):
                            parts = []
                            for j, m in enumerate(mids):
                                gj = g[n * rows_l + j * half:n * rows_l + (j + 1) * half, gi * LANES:(gi + 1) * LANES]
                                parts.append(scs[gi][n][m - half:m])
                                parts.append(jnp.where(lev[m:m + half] == l, gj, scs[gi][n][m:m + half]))
                            scs[gi][n] = jnp.concatenate(parts, axis=0)
            return scs

        def block_diag(a, b):
            za = jnp.zeros_like(a)
            return jnp.concatenate([jnp.concatenate([a, za], axis=1), jnp.concatenate([za, b], axis=1)], axis=0)

        def head_pair(hidx, scs2, qds, kds, vs, g_lasts):
            vb = [v.astype(BF16) for v in vs]
            sts = [st_scr[h] for h in hidx]
            o = jnp.dot(jnp.concatenate([s.astype(BF16) for s in scs2], axis=1), block_diag(*vb),
                        preferred_element_type=F32)
            o = o + lax.dot_general(jnp.concatenate(qds, axis=1), block_diag(*[s.astype(BF16) for s in sts]),
                                    NT_DIMS, preferred_element_type=F32)
            ut = lax.dot_general(jnp.concatenate(vb, axis=1), jnp.concatenate([k.astype(BF16) for k in kds], axis=1),
                                 TN_DIMS, preferred_element_type=F32)
            for n, h in enumerate(hidx):
                st_scr[h] = sts[n] * g_lasts[n] + ut[n * LANES:(n + 1) * LANES, n * LANES:(n + 1) * LANES]
            return o[:, :LANES], o[:, LANES:]

        def finish(o, gate, c0):
            o = o * lax.rsqrt(jnp.mean(o * o, axis=-1, keepdims=True) + LN_EPS)
            o = o * gain_ref[:, c0:c0 + LANES] * (gate * jax.nn.sigmoid(gate))
            o_scr[rows, c0:c0 + LANES] = o

        for g0 in range(0, HG_HEADS, 2):
            terms = []
            for g in (g0, g0 + 1):
                c0 = g * LANES
                q = hq[:, c0:c0 + LANES] * (HG_HEAD_DIM ** -0.5)
                k = kh[:, c0:c0 + LANES]
                terms.append((q * k,) + group_terms(q, k, c0))
            scs = scores([t[1] for t in terms], [[t[0]] for t in terms], [None])
            cols = [slice(g * LANES, (g + 1) * LANES) for g in (g0, g0 + 1)]
            outs = head_pair((g0, g0 + 1), [scs[0][0], scs[1][0]], [t[2] for t in terms], [t[3] for t in terms],
                             [hv[:, c] for c in cols], [t[4] for t in terms])
            for o, c in zip(outs, cols):
                finish(o, hgate[:, c], c.start)

        terms = []
        for p in range(GL_HEADS // 2):
            c0 = p * LANES
            q = gq[:, c0:c0 + LANES] * (GL_DK ** -0.5)
            k = gk[:, c0:c0 + LANES]
            terms.append((q * k,) + group_terms(q, k, HG_WIDTH + c0))
        scs = scores([t[1] for t in terms], [[jnp.where(m, t[0], 0.0) for m in half_mask] for t in terms], half_mask)
        for p in range(GL_HEADS // 2):
            _, _, qd, kd, g_last = terms[p]
            cols = [slice(h * LANES, (h + 1) * LANES) for h in (2 * p, 2 * p + 1)]
            outs = head_pair((HG_HEADS + 2 * p, HG_HEADS + 2 * p + 1), scs[p], [qd, qd],
                             [jnp.where(m, kd, 0.0) for m in half_mask], [gv[:, c] for c in cols], [g_last, g_last])
            for o, c in zip(outs, cols):
                finish(o, ggate[:, c], HG_WIDTH + c.start)
        return carry

    lax.fori_loop(0, TT // SUB, subtile, 0, unroll=True)

    hmix = jnp.dot(o_scr[...].astype(BF16), wout_ref[...], preferred_element_type=F32)
    x1 = _layer_norm(DEEPNORM_ALPHA * xt + hmix, ln1g_ref[...], ln1b_ref[...])
    x1h = x1.astype(BF16)
    x1l = (x1 - x1h.astype(F32)).astype(BF16)
    prod = jnp.dot(jnp.concatenate([x1h, x1l], axis=0), wr_ref[...], preferred_element_type=F32)
    logits = prod[:TT, :LANES] + prod[TT:, :LANES] + prod[:TT, LANES:]

    @pl.when(si < n_tiles)
    def _():
        for s in range(SLAB):
            x1s_ref[pl.ds(s, TT, stride=SLAB), :] = x1[:, s * LANES:(s + 1) * LANES]
        lg_scr[...] = logits


def _route_tile(si, logits, carry_scr, route_ref, meta_ref, counts_ref):
    lane_i = lax.broadcasted_iota(jnp.int32, (TT, LANES), 1)
    lane_f = lane_i.astype(F32)
    neg = jnp.float32(-jnp.inf)
    big = jnp.float32(1e9)

    gl = jnp.where(lane_i < N_GROUPS, logits, neg)
    gmax = jnp.max(gl, axis=-1, keepdims=True)
    gidx = jnp.min(jnp.where(gl == gmax, lane_f, big), axis=-1, keepdims=True)
    p_group = 1.0 / jnp.sum(jnp.exp(gl - gmax), axis=-1, keepdims=True)

    e_lo = ROUTE_BASE + EXPERTS_PER_GROUP * gidx
    el = jnp.where((lane_f >= e_lo) & (lane_f < e_lo + EXPERTS_PER_GROUP), logits, neg)
    m1 = jnp.max(el, axis=-1, keepdims=True)
    i1 = jnp.min(jnp.where(el == m1, lane_f, big), axis=-1, keepdims=True)
    el2 = jnp.where(lane_f == i1, neg, el)
    m2 = jnp.max(el2, axis=-1, keepdims=True)
    i2 = jnp.min(jnp.where(el2 == m2, lane_f, big), axis=-1, keepdims=True)
    r2 = jnp.exp(m2 - m1)
    gate1 = p_group / (1.0 + r2)
    gate2 = p_group * r2 / (1.0 + r2)

    hot1 = lane_f == i1
    hot2 = lane_f == i2
    cnt = jnp.where(si > 0, (hot1 | hot2).astype(F32), 0.0)
    base = carry_scr[...] + _prefix_rows(cnt)
    rank1 = jnp.sum(jnp.where(hot1, base, 0.0), axis=-1, keepdims=True)
    rank2 = jnp.sum(jnp.where(hot2, base, 0.0), axis=-1, keepdims=True)
    total = carry_scr[...] + jnp.sum(cnt, axis=0, keepdims=True)
    carry_scr[...] = total
    counts_ref[...] = total

    route = jnp.zeros((TT, LANES), F32)
    for lane0, vals in ((COL_EXPERT, (i1 - ROUTE_BASE, i2 - ROUTE_BASE)), (COL_GATE, (gate1, gate2)),
                        (COL_RANK, (rank1, rank2))):
        for k, v in enumerate(vals):
            route = jnp.where(lane_i == lane0 + k, v, route)
    route_ref[...] = route
    meta_ref[...] = route.T[0:META_ROWS, :].astype(jnp.int32)


def _dispatch_kernel(dest_ref, x1s_ref, xs_ref, xin, zbuf, in_sems, out_sems, fill_sem):
    i = pl.program_id(0)
    n_steps = pl.num_programs(0)
    tile_rows = TD * SLAB

    def tile_load(t, slot):
        return pltpu.make_async_copy(x1s_ref.at[pl.ds(pl.multiple_of(t * tile_rows, tile_rows), tile_rows), :],
                                     xin.at[slot], in_sems.at[slot])

    def wait_rows(slot, sem):
        for k in range(TOP_K):
            pltpu.make_async_copy(xin.at[slot], xs_ref.at[pl.ds(0, tile_rows), :], sem).wait()

    @pl.when(i == 0)
    def _():
        tile_load(0, 0).start()
        zbuf[...] = jnp.zeros_like(zbuf)
        fill = pltpu.make_async_copy(zbuf, xs_ref.at[pl.ds(xs_ref.shape[0] - BLK * SLAB, BLK * SLAB), :], fill_sem)
        fill.start()
        fill.wait()

    slot = i % 3

    @pl.when(i + 1 < n_steps)
    def _():
        tile_load(i + 1, (i + 1) % 3).start()

    tile_load(i, slot).wait()
    for r in range(TD):
        for k in range(TOP_K):
            dst = pl.multiple_of(dest_ref[r // TT, k, r % TT] * SLAB, SLAB)
            pltpu.make_async_copy(xin.at[slot, pl.ds(r * SLAB, SLAB), :], xs_ref.at[pl.ds(dst, SLAB), :],
                                  out_sems.at[i % 2]).start(priority=k)

    @pl.when(i > 0)
    def _():
        wait_rows((i + 2) % 3, out_sems.at[(i + 1) % 2])

    @pl.when(i + 1 >= n_steps)
    def _():
        wait_rows(slot, out_sems.at[i % 2])


def _expert_kernel(be_ref, nv_ref, row0_ref, ord_ref, nblk_ref, xs_ref, wg_ref, wu_ref, wd_ref, yb_ref,
                   xbuf, wg_f, wu_f, wd_f, wg_s, wu_s, wd_s, sems, wsems):
    nv = nv_ref[0]
    blk_rows = BLK * SLAB

    def block_copy(b, slot):
        r = pl.multiple_of(row0_ref[b] * SLAB, SLAB)
        return pltpu.make_async_copy(xs_ref.at[pl.ds(r, blk_rows), :],
                                     xbuf.at[pl.ds(slot * blk_rows, blk_rows), :], sems.at[slot])

    def weight_copies(e, ws):
        return [pltpu.make_async_copy(src.at[e], dst.at[ws], wsems.at[ws])
                for src, dst in ((wg_ref, wg_f), (wu_ref, wu_f), (wd_ref, wd_f))]

    def block(b, sub):
        slot = sub % 2
        out0 = sub * blk_rows

        @pl.when(b < nv)
        def _():
            e = be_ref[b]
            ws = ord_ref[b] % 2

            @pl.when(b == 0)
            def _():
                block_copy(0, 0).start()
                for c in weight_copies(e, ws):
                    c.start()

            @pl.when(b + 1 < nv)
            def _():
                block_copy(b + 1, 1 - slot).start()

            prev = be_ref[jnp.maximum(b - 1, 0)]

            @pl.when((b == 0) | (e != prev))
            def _():
                j = b + nblk_ref[e]

                @pl.when(j < nv)
                def _():
                    for c in weight_copies(be_ref[jnp.minimum(j, nv - 1)], 1 - ws):
                        c.start()

                for c in weight_copies(e, ws):
                    c.wait()
                wg_s[...] = wg_f[ws].astype(BF16)
                wu_s[...] = wu_f[ws].astype(BF16)
                wd_s[...] = wd_f[ws].astype(BF16)

            block_copy(b, slot).wait()
            xb = jnp.concatenate(
                [xbuf[pl.ds(slot * blk_rows + s, BLK, stride=SLAB), :].astype(BF16) for s in range(SLAB)],
                axis=1)
            a = jnp.dot(xb, wg_s[...], preferred_element_type=F32)
            u = jnp.dot(xb, wu_s[...], preferred_element_type=F32)
            h = (a * jax.nn.sigmoid(a)) * u
            y = jnp.dot(h.astype(BF16), wd_s[...], preferred_element_type=F32)
            for s in range(SLAB):
                yb_ref[pl.ds(out0 + s, BLK, stride=SLAB), :] = y[:, s * LANES:(s + 1) * LANES]

        @pl.when(b >= nv)
        def _():
            yb_ref[pl.ds(out0, blk_rows), :] = jnp.zeros((blk_rows, LANES), F32)

    for sub in range(BLOCKS_PER_STEP):
        block(pl.program_id(0) * BLOCKS_PER_STEP + sub, sub)


def _combine_kernel(dest_ref, dest_next_ref, x1s_ref, route_ref, yb_ref, g_ref, b_ref, out_ref, ybuf, sems):
    i = pl.program_id(0)
    n_steps = pl.num_programs(0)
    half_rows = TD * SLAB
    par_rows = TOP_K * half_rows

    def start_gather(d_ref, par):
        for r in range(TD):
            for k in range(TOP_K):
                src = pl.multiple_of(d_ref[r // TT, k, r % TT] * SLAB, SLAB)
                dst = pl.multiple_of(par * par_rows + (k * TD + r) * SLAB, SLAB)
                pltpu.make_async_copy(yb_ref.at[pl.ds(src, SLAB), :], ybuf.at[pl.ds(dst, SLAB), :],
                                      sems.at[par]).start(priority=k)

    def wait_gather(par):
        pltpu.make_async_copy(yb_ref.at[pl.ds(0, par_rows), :],
                              ybuf.at[pl.ds(par * par_rows, par_rows), :], sems.at[par]).wait()

    par = i % 2

    @pl.when(i == 0)
    def _():
        start_gather(dest_ref, 0)

    wait_gather(par)
    start_gather(dest_next_ref, 1 - par)

    base = pl.multiple_of(par * par_rows, par_rows)
    g1 = route_ref[:, COL_GATE:COL_GATE + 1]
    g2 = route_ref[:, COL_GATE + 1:COL_GATE + 2]
    zs = []
    for s in range(SLAB):
        y1 = ybuf[pl.ds(base + s, TD, stride=SLAB), :]
        y2 = ybuf[pl.ds(base + half_rows + s, TD, stride=SLAB), :]
        zs.append(DEEPNORM_ALPHA * x1s_ref[pl.ds(s, TD, stride=SLAB), :] + (g1 * y1 + g2 * y2))
    z = jnp.concatenate(zs, axis=1)
    out_ref[...] = _layer_norm(z, g_ref[...], b_ref[...])

    @pl.when(i + 1 >= n_steps)
    def _():
        wait_gather(1 - par)


def _const_spec(shape):
    nd = len(shape)
    return pl.BlockSpec(shape, lambda *_: (0,) * nd, pipeline_mode=pl.Buffered(1))


def kernel(x, w_in, w_a2, b_a, lb_logits, norm_h, norm_g, w_out, ln1_g, ln1_b, w_group_router, w_expert_router,
           w_gate, w_up, w_down, ln2_g, ln2_b):
    B, T, D = x.shape
    n_tok = B * T
    assert D == D_MODEL == SLAB * LANES and T % TT == 0 and TD % TT == 0 and n_tok % TD == 0
    n_blocks = n_tok * TOP_K // BLK + N_EXPERTS
    assert n_blocks % BLOCKS_PER_STEP == 0 and BLOCKS_PER_STEP % 2 == 0
    n_slots = n_blocks * BLK

    wa2 =jnp.pad(w_a2[0], ((0, LANES - GATE_RANK), (0, 0))).astype(BF16)
    ba = b_a[0].reshape(1, GL_QK).astype(F32)
    gain = jnp.concatenate([norm_h[0], norm_g[0]]).reshape(1, D).astype(F32)
    wr = jnp.concatenate([w_group_router[0], w_expert_router[0]], axis=-1).astype(F32)
    wr = jnp.pad(wr, ((0, 0), (0, LANES - wr.shape[1])))
    wrh = wr.astype(BF16)
    wr2 = jnp.concatenate([wrh, (wr - wrh.astype(F32)).astype(BF16)], axis=1)
    wstk = jnp.asarray(_decay_matrices(), dtype=BF16)
    lev = jnp.asarray(_level_map())
    n_w = wstk.shape[0]

    n_steps = n_tok // TT
    x1s, route, meta, counts = pl.pallas_call(
        functools.partial(_mixer_kernel, steps_per_seq=T // TT),
        grid=(n_steps + 1,),
        in_specs=[
            pl.BlockSpec((TT, D), lambda i: (jnp.minimum(i, n_steps - 1), 0)),
            _const_spec((D, sum(IN_SPLITS))),
            _const_spec((LANES, GL_QK)),
            _const_spec((1, GL_QK)),
            _const_spec((DEPTH + 1, HG_WIDTH)),
            _const_spec((1, D)),
            _const_spec((D, D)),
            _const_spec((1, D)),
            _const_spec((1, D)),
            _const_spec((D, 2 * LANES)),
            _const_spec((n_w, 2 * SUB)),
            _const_spec((SUB, SUB)),
        ],
        out_specs=[
            pl.BlockSpec((TT * SLAB, LANES), lambda i: (jnp.minimum(i, n_steps - 1), 0)),
            pl.BlockSpec((TT, LANES), lambda i: (jnp.maximum(i - 1, 0), 0)),
            pl.BlockSpec((META_ROWS, TT), lambda i: (jnp.maximum(i - 1, 0), 0)),
            pl.BlockSpec((1, LANES), lambda i: (0, 0)),
        ],
        out_shape=[
            jax.ShapeDtypeStruct((n_tok * SLAB, LANES), F32),
            jax.ShapeDtypeStruct((n_tok, LANES), F32),
            jax.ShapeDtypeStruct((n_steps * META_ROWS, TT), jnp.int32),
            jax.ShapeDtypeStruct((1, LANES), F32),
        ],
        scratch_shapes=[
            pltpu.VMEM((D, PROJ_COLS), BF16),
            pltpu.VMEM((D, D), BF16),
            pltpu.VMEM((TT, PROJ_COLS), F32),
            pltpu.VMEM((TT, D), F32),
            pltpu.VMEM((SUB, HG_WIDTH + GL_QK), F32),
            pltpu.VMEM((TT, LANES), F32),
            pltpu.VMEM((HG_HEADS + GL_HEADS, LANES, LANES), F32),
            pltpu.VMEM((1, LANES), F32),
        ],
        compiler_params=pltpu.CompilerParams(
            dimension_semantics=("arbitrary",), vmem_limit_bytes=VMEM_LIMIT),
        name="mixer",
    )(x.reshape(n_tok, D), w_in[0], wa2, ba, lb_logits.astype(F32), gain, w_out[0], ln1_g[0].reshape(1, D),
      ln1_b[0].reshape(1, D), wr2, wstk, lev)

    cnt = counts[0, ROUTE_BASE:ROUTE_BASE + N_EXPERTS].astype(jnp.int32)
    padded = (cnt + BLK - 1) // BLK * BLK
    ends = jnp.cumsum(padded)
    starts = ends - padded
    n_valid = (ends[-1] // BLK).astype(jnp.int32)
    blk_ids = jnp.minimum(jnp.arange(n_blocks, dtype=jnp.int32), n_valid - 1)
    block_expert = jnp.sum((ends[None, :] <= (blk_ids * BLK)[:, None]).astype(jnp.int32), axis=1)
    block_expert = jnp.minimum(block_expert, N_EXPERTS - 1)
    blk_ord = jnp.cumsum(jnp.concatenate(
        [jnp.zeros((1,), jnp.int32), (block_expert[1:] != block_expert[:-1]).astype(jnp.int32)]))
    dstarts = jnp.cumsum(cnt) - cnt
    first_blk = starts // BLK
    pick = block_expert[None, :] == jnp.arange(N_EXPERTS, dtype=jnp.int32)[:, None]
    row0 = jnp.sum(jnp.where(pick, (dstarts - first_blk * BLK)[:, None], 0), axis=0) + blk_ids * BLK
    meta3 = meta.reshape(n_steps, META_ROWS, TT)
    hit = (meta3[None, :, COL_EXPERT:COL_EXPERT + TOP_K, :]
           == jnp.arange(N_EXPERTS, dtype=jnp.int32)[:, None, None, None])
    rank = meta3[:, COL_RANK:COL_RANK + TOP_K, :]
    dest_x = rank + jnp.sum(jnp.where(hit, dstarts[:, None, None, None], 0), axis=0)
    dest = rank + jnp.sum(jnp.where(hit, starts[:, None, None, None], 0), axis=0)

    xs_rows = (n_tok * TOP_K + BLK) * SLAB
    xs = pl.pallas_call(
        _dispatch_kernel,
        grid=(n_tok // TD,),
        in_specs=[
            pl.BlockSpec((TD // TT, TOP_K, TT), lambda i: (i, 0, 0), memory_space=pltpu.SMEM),
            pl.BlockSpec(memory_space=pl.ANY),
        ],
        out_specs=pl.BlockSpec(memory_space=pl.ANY),
        out_shape=jax.ShapeDtypeStruct((xs_rows, LANES), F32),
        scratch_shapes=[pltpu.VMEM((3, TD * SLAB, LANES), F32), pltpu.VMEM((BLK * SLAB, LANES), F32),
                        pltpu.SemaphoreType.DMA((3,)), pltpu.SemaphoreType.DMA((2,)), pltpu.SemaphoreType.DMA],
        compiler_params=pltpu.CompilerParams(dimension_semantics=("arbitrary",)),
        name="dispatch",
    )(dest_x, x1s)

    yb = pl.pallas_call(
        _expert_kernel,
        grid_spec=pltpu.PrefetchScalarGridSpec(
            num_scalar_prefetch=5,
            grid=(n_blocks // BLOCKS_PER_STEP,),
            in_specs=[pl.BlockSpec(memory_space=pl.ANY)] * 4,
            out_specs=pl.BlockSpec((BLOCKS_PER_STEP * BLK * SLAB, LANES), lambda i, *_: (i, 0)),
            scratch_shapes=[
                pltpu.VMEM((2 * BLK * SLAB, LANES), F32),
                pltpu.VMEM((2, D, D_EXPERT), F32),
                pltpu.VMEM((2, D, D_EXPERT), F32),
                pltpu.VMEM((2, D_EXPERT, D), F32),
                pltpu.VMEM((D, D_EXPERT), BF16),
                pltpu.VMEM((D, D_EXPERT), BF16),
                pltpu.VMEM((D_EXPERT, D), BF16),
                pltpu.SemaphoreType.DMA((2,)),
                pltpu.SemaphoreType.DMA((2,)),
            ],
        ),
        out_shape=jax.ShapeDtypeStruct((n_slots * SLAB, LANES), F32),
        compiler_params=pltpu.CompilerParams(
            dimension_semantics=("arbitrary",), vmem_limit_bytes=VMEM_LIMIT),
        name="experts",
    )(block_expert, n_valid.reshape(1), row0, blk_ord, padded // BLK, xs, w_gate[0], w_up[0], w_down[0])

    last = n_tok // TD - 1
    out = pl.pallas_call(
        _combine_kernel,
        grid=(n_tok // TD,),
        in_specs=[
            pl.BlockSpec((TD // TT, TOP_K, TT), lambda i: (i, 0, 0), memory_space=pltpu.SMEM),
            pl.BlockSpec((TD // TT, TOP_K, TT), lambda i: (jnp.minimum(i + 1, last), 0, 0), memory_space=pltpu.SMEM),
            pl.BlockSpec((TD * SLAB, LANES), lambda i: (i, 0)),
            pl.BlockSpec((TD, LANES), lambda i: (i, 0)),
            pl.BlockSpec(memory_space=pl.ANY),
            pl.BlockSpec((1, D), lambda i: (0, 0)),
            pl.BlockSpec((1, D), lambda i: (0, 0)),
        ],
        out_specs=pl.BlockSpec((TD, D), lambda i: (i, 0)),
        scratch_shapes=[pltpu.VMEM((2 * TOP_K * TD * SLAB, LANES), F32), pltpu.SemaphoreType.DMA((2,))],
        out_shape=jax.ShapeDtypeStruct((n_tok, D), F32),
        compiler_params=pltpu.CompilerParams(dimension_semantics=("arbitrary",)),
        name="combine",
    )(dest, dest, x1s, route, yb, ln2_g[0].reshape(1, D), ln2_b[0].reshape(1, D))

    return out.reshape(B, T, D).astype(x.dtype)
```

```python
import functools

import numpy as np
import jax
import jax.numpy as jnp
from jax import lax
from jax.experimental import pallas as pl
from jax.experimental.pallas import tpu as pltpu

D_MODEL = 1024
HG_WIDTH = 512
HG_HEAD_DIM = 128
HG_HEADS = 4
GL_WIDTH = 512
GL_HEADS = 4
GL_DV = 128
GL_DK = 64
GL_QK = 256
GATE_RANK = 16
GATE_TAU = 16.0
N_GROUPS = 8
EXPERTS_PER_GROUP = 8
N_EXPERTS = 64
TOP_K = 2
D_EXPERT = 512
DEPTH = 1
DEEPNORM_ALPHA = (2.0 * DEPTH) ** 0.25
LN_EPS = 1e-5
LOG2E = 1.4426950408889634
IN_SPLITS = (HG_WIDTH, HG_WIDTH, HG_WIDTH, HG_WIDTH, GL_QK, GL_QK, GL_WIDTH, GATE_RANK, GL_WIDTH)

LANES = 128
SUB = 128
N_LEVELS = 7
MXU_LEVELS = 3
TT = 256
TD = 512
BLK = 256
BLOCKS_PER_STEP = 2
PROJ_WIDTHS = (HG_WIDTH, HG_WIDTH, HG_WIDTH, HG_WIDTH, GL_QK, GL_QK, GL_WIDTH, GL_WIDTH, LANES)
PROJ_SLICES = tuple(slice(sum(PROJ_WIDTHS[:n]), sum(PROJ_WIDTHS[:n + 1])) for n in range(len(PROJ_WIDTHS)))
PROJ_COLS = sum(PROJ_WIDTHS)
ROUTE_BASE = N_GROUPS
SLAB = 8
COL_EXPERT, COL_GATE, COL_RANK = 0, 2, 4
META_ROWS = 8
V7X_VMEM_BYTES = 64 * 1024 * 1024
VMEM_LIMIT = V7X_VMEM_BYTES * 7 // 8
assert DEPTH == 1 and D_MODEL == SLAB * LANES and GL_DK * 2 == LANES and SUB == 1 << N_LEVELS

F32 = jnp.float32
BF16 = jnp.bfloat16
NT_DIMS = (((1,), (1,)), ((), ()))
TN_DIMS = (((0,), (0,)), ((), ()))


def _decay_matrices():
    t = np.arange(SUB)[:, None]
    j = np.arange(SUB)[None, :]
    mats = [(j <= t)]
    for lvl in range(1, MXU_LEVELS + 1):
        blk = 1 << lvl
        half = blk >> 1
        r = (t // blk) * blk + half - 1
        right = (t % blk) >= half
        mats.append(np.where(right, (j > r) & (j <= t), (j > t) & (j <= r)))
    w = np.concatenate(mats, axis=0).astype(np.float32)
    return np.concatenate([w, w], axis=1)


def _level_map():
    t = np.arange(SUB)[:, None]
    s = np.arange(SUB)[None, :]
    x = np.bitwise_xor(t, s)
    lvl = np.floor(np.log2(np.maximum(x, 1))).astype(np.int32) + 1
    lvl = np.where(x == 0, 0, lvl)
    return np.where(s > t, -1, lvl).astype(np.int32)


def _layer_norm(y, g, b):
    mu = jnp.mean(y, axis=-1, keepdims=True)
    yc = y - mu
    var = jnp.mean(yc * yc, axis=-1, keepdims=True)
    return yc * lax.rsqrt(var + LN_EPS) * g + b


def _prefix_rows(x):
    n = x.shape[0]
    row = lax.broadcasted_iota(jnp.int32, (n, 1), 0)
    inc = x
    sh = 1
    while sh < n:
        inc = inc + jnp.where(row >= sh, pltpu.roll(inc, sh, axis=0), 0.0)
        sh *= 2
    return inc - x


def _mixer_kernel(x_ref, winf_ref, wa2_ref, ba_ref, lbl_ref, gain_ref, woutf_ref, ln1g_ref, ln1b_ref,
                  wr_ref, wstk_ref, lev_ref,
                  x1s_ref, route_ref, meta_ref, counts_ref,
                  win_ref, wout_ref, p_scr, o_scr, b_scr, lg_scr, st_scr, carry_scr, *, steps_per_seq):
    si = pl.program_id(0)
    n_tiles = pl.num_programs(0) - 1

    @pl.when(si == 0)
    def _():
        carry_scr[...] = jnp.zeros_like(carry_scr)
        lg_scr[...] = jnp.zeros_like(lg_scr)
        keep = sum(IN_SPLITS[:7])
        wout_ref[...] = woutf_ref[...].astype(BF16)
        win_ref[:, keep + GL_WIDTH:] = jnp.zeros((D_MODEL, PROJ_COLS - keep - GL_WIDTH), BF16)

        def rows(c, carry):
            r = pl.ds(pl.multiple_of(c * SUB, SUB), SUB)
            win_ref[r, 0:keep] = winf_ref[r, 0:keep].astype(BF16)
            win_ref[r, keep:keep + GL_WIDTH] = winf_ref[r, keep + GATE_RANK:keep + GATE_RANK + GL_WIDTH].astype(BF16)
            win_ref[r, keep + GL_WIDTH:keep + GL_WIDTH + GATE_RANK] = winf_ref[r, keep:keep + GATE_RANK].astype(BF16)
            return carry
        lax.fori_loop(0, D_MODEL // SUB, rows, 0)

    @pl.when(si % steps_per_seq == 0)
    def _():
        st_scr[...] = jnp.zeros_like(st_scr)

    xt = x_ref[...]
    p_scr[...] = jnp.dot(xt.astype(BF16), win_ref[...], preferred_element_type=F32)
    _route_tile(si, lg_scr[...], carry_scr, route_ref, meta_ref, counts_ref)

    lbl = lbl_ref[...]
    lbe = jnp.exp(lbl - jnp.max(lbl, axis=0, keepdims=True))
    lb = lbe[0:1, :] / jnp.sum(lbe, axis=0, keepdims=True)

    lev = lev_ref[...]
    tcol = lax.broadcasted_iota(jnp.int32, (SUB, 1), 0)
    qrow = [(tcol & ((1 << l) - 1)) >= (1 << (l - 1)) for l in range(1, N_LEVELS + 1)]
    lane = lax.broadcasted_iota(jnp.int32, (1, LANES), 1)
    half_mask = [(lane < GL_DK), (lane >= GL_DK)]

    def by_halves(l, left_fn, right_fn):
        blk = 1 << l
        half = blk >> 1
        parts = []
        for j in range(SUB // blk):
            lo = j * blk
            parts += [left_fn(lo, lo + half), right_fn(lo + half, lo + blk)]
        return jnp.concatenate(parts, axis=0)

    def subtile(s, carry):
        r0 = pl.multiple_of(s * SUB, SUB)
        rows = pl.ds(r0, SUB)
        hq, hf, hv, hgate, gq, gk, gv, ggate, ga = [p_scr[rows, c] for c in PROJ_SLICES]

        f = lb + (1.0 - lb) * jax.nn.sigmoid(hf)
        kh = 1.0 - f
        lgh = jnp.log(f)
        z = jnp.dot(ga.astype(BF16), wa2_ref[...], preferred_element_type=F32) + ba_ref[...]
        lga = (jnp.minimum(z, 0.0) - jnp.log1p(jnp.exp(-jnp.abs(z)))) * (1.0 / GATE_TAU)
        lg = jnp.concatenate([lgh, lga], axis=1) * LOG2E

        l_hi = lg.astype(BF16)
        l_mid = (lg - l_hi.astype(F32)).astype(BF16)
        sums = jnp.dot(wstk_ref[...], jnp.concatenate([l_hi, l_mid], axis=0),
                       preferred_element_type=F32)
        bcum = sums[0:SUB]
        b_scr[...] = bcum
        expos = [bcum, b_scr[SUB - 1:SUB, :] - bcum]
        for l in range(1, N_LEVELS + 1):
            if l <= MXU_LEVELS:
                expos.append(sums[l * SUB:(l + 1) * SUB])
            else:
                expos.append(by_halves(l, lambda lo, mid: b_scr[mid - 1:mid, :] - bcum[lo:mid],
                                       lambda mid, hi: bcum[mid:hi] - b_scr[mid - 1:mid, :]))
        decs = [jnp.exp2(e) for e in expos]

        def group_terms(q, k, c0):
            e_pre = decs[0][:, c0:c0 + LANES]
            e_suf = decs[1][:, c0:c0 + LANES]
            us = []
            for l in range(1, N_LEVELS + 1):
                e_l = decs[1 + l][:, c0:c0 + LANES]
                if l <= MXU_LEVELS:
                    sel = jnp.where(qrow[l - 1], q, k)
                else:
                    sel = by_halves(l, lambda lo, mid: k[lo:mid], lambda mid, hi: q[mid:hi])
                u = sel * e_l
                us.append((u, u.astype(BF16)))
            return us, (q * e_pre).astype(BF16), k * e_suf, e_pre[SUB - 1:SUB, :]

        def scores(us_ab, qks_ab, masks):
            nm = len(masks)
            zero_b = jnp.zeros((SUB, LANES), BF16)
            scs = [[jnp.where(lev == 0, jnp.sum(qk, axis=-1, keepdims=True), 0.0) for qk in qks] for qks in qks_ab]
            for l in range(1, N_LEVELS + 1):
                (ua, uba), (ub_, ubb) = us_ab[0][l - 1], us_ab[1][l - 1]
                rhs = jnp.concatenate([jnp.concatenate([uba, zero_b], axis=1),
                                       jnp.concatenate([zero_b, ubb], axis=1)], axis=0)
                if l <= MXU_LEVELS:
                    rows_l = SUB
                    lhs = jnp.concatenate(
                        [jnp.concatenate([x if m is None else jnp.where(m, x, jnp.zeros_like(x)) for x in (uba, ubb)],
                                         axis=1) for m in masks], axis=0)
                    g = lax.dot_general(lhs, rhs, NT_DIMS, preferred_element_type=F32)
                    for gi in range(2):
                        for n in range(nm):
                            gn = g[n * rows_l:(n + 1) * rows_l, gi * LANES:(gi + 1) * LANES]
                            scs[gi][n] = jnp.where(lev == l, gn, scs[gi][n])
                else:
                    blk = 1 << l
                    half = blk >> 1
                    mids = [j * blk + half for j in range(SUB // blk)]
                    rows_l = SUB // 2
                    uqs = [jnp.concatenate([x[m:m + half] for m in mids], axis=0) for x in (ua, ub_)]
                    lhs = jnp.concatenate(
                        [jnp.concatenate([x if m is None else jnp.where(m, x, 0.0) for x in uqs], axis=1)
                         for m in masks], axis=0)
                    g = lax.dot_general(lhs.astype(BF16), rhs, NT_DIMS, preferred_element_type=F32)
                    for gi in range(2):
                        for n in range(nm):
                            parts = []
                            for j, m in enumerate(mids):
                                gj = g[n * rows_l + j * half:n * rows_l + (j + 1) * half, gi * LANES:(gi + 1) * LANES]
                                parts.append(scs[gi][n][m - half:m])
                                parts.append(jnp.where(lev[m:m + half] == l, gj, scs[gi][n][m:m + half]))
                            scs[gi][n] = jnp.concatenate(parts, axis=0)
            return scs

        def block_diag(a, b):
            za = jnp.zeros_like(a)
            return jnp.concatenate([jnp.concatenate([a, za], axis=1), jnp.concatenate([za, b], axis=1)], axis=0)

        def head_pair(hidx, scs2, qds, kds, vs, g_lasts):
            vb = [v.astype(BF16) for v in vs]
            sts = [st_scr[h] for h in hidx]
            o = jnp.dot(jnp.concatenate([s.astype(BF16) for s in scs2], axis=1), block_diag(*vb),
                        preferred_element_type=F32)
            o = o + lax.dot_general(jnp.concatenate(qds, axis=1), block_diag(*[s.astype(BF16) for s in sts]),
                                    NT_DIMS, preferred_element_type=F32)
            ut = lax.dot_general(jnp.concatenate(vb, axis=1), jnp.concatenate([k.astype(BF16) for k in kds], axis=1),
                                 TN_DIMS, preferred_element_type=F32)
            for n, h in enumerate(hidx):
                st_scr[h] = sts[n] * g_lasts[n] + ut[n * LANES:(n + 1) * LANES, n * LANES:(n + 1) * LANES]
            return o[:, :LANES], o[:, LANES:]

        def finish(o, gate, c0):
            o = o * lax.rsqrt(jnp.mean(o * o, axis=-1, keepdims=True) + LN_EPS)
            o = o * gain_ref[:, c0:c0 + LANES] * (gate * jax.nn.sigmoid(gate))
            o_scr[rows, c0:c0 + LANES] = o

        for g0 in range(0, HG_HEADS, 2):
            terms = []
            for g in (g0, g0 + 1):
                c0 = g * LANES
                q = hq[:, c0:c0 + LANES] * (HG_HEAD_DIM ** -0.5)
                k = kh[:, c0:c0 + LANES]
                terms.append((q * k,) + group_terms(q, k, c0))
            scs = scores([t[1] for t in terms], [[t[0]] for t in terms], [None])
            cols = [slice(g * LANES, (g + 1) * LANES) for g in (g0, g0 + 1)]
            outs = head_pair((g0, g0 + 1), [scs[0][0], scs[1][0]], [t[2] for t in terms], [t[3] for t in terms],
                             [hv[:, c] for c in cols], [t[4] for t in terms])
            for o, c in zip(outs, cols):
                finish(o, hgate[:, c], c.start)

        terms = []
        for p in range(GL_HEADS // 2):
            c0 = p * LANES
            q = gq[:, c0:c0 + LANES] * (GL_DK ** -0.5)
            k = gk[:, c0:c0 + LANES]
            terms.append((q * k,) + group_terms(q, k, HG_WIDTH + c0))
        scs = scores([t[1] for t in terms], [[jnp.where(m, t[0], 0.0) for m in half_mask] for t in terms], half_mask)
        for p in range(GL_HEADS // 2):
            _, _, qd, kd, g_last = terms[p]
            cols = [slice(h * LANES, (h + 1) * LANES) for h in (2 * p, 2 * p + 1)]
            outs = head_pair((HG_HEADS + 2 * p, HG_HEADS + 2 * p + 1), scs[p], [qd, qd],
                             [jnp.where(m, kd, 0.0) for m in half_mask], [gv[:, c] for c in cols], [g_last, g_last])
            for o, c in zip(outs, cols):
                finish(o, ggate[:, c], HG_WIDTH + c.start)
        return carry

    lax.fori_loop(0, TT // SUB, subtile, 0, unroll=True)

    hmix = jnp.dot(o_scr[...].astype(BF16), wout_ref[...], preferred_element_type=F32)
    x1 = _layer_norm(DEEPNORM_ALPHA * xt + hmix, ln1g_ref[...], ln1b_ref[...])
    x1h = x1.astype(BF16)
    x1l = (x1 - x1h.astype(F32)).astype(BF16)
    prod = jnp.dot(jnp.concatenate([x1h, x1l], axis=0), wr_ref[...], preferred_element_type=F32)
    logits = prod[:TT, :LANES] + prod[TT:, :LANES] + prod[:TT, LANES:]

    @pl.when(si < n_tiles)
    def _():
        for s in range(SLAB):
            x1s_ref[pl.ds(s, TT, stride=SLAB), :] = x1[:, s * LANES:(s + 1) * LANES]
        lg_scr[...] = logits


def _route_tile(si, logits, carry_scr, route_ref, meta_ref, counts_ref):
    lane_i = lax.broadcasted_iota(jnp.int32, (TT, LANES), 1)
    lane_f = lane_i.astype(F32)
    neg = jnp.float32(-jnp.inf)
    big = jnp.float32(1e9)

    gl = jnp.where(lane_i < N_GROUPS, logits, neg)
    gmax = jnp.max(gl, axis=-1, keepdims=True)
    gidx = jnp.min(jnp.where(gl == gmax, lane_f, big), axis=-1, keepdims=True)
    p_group = 1.0 / jnp.sum(jnp.exp(gl - gmax), axis=-1, keepdims=True)

    e_lo = ROUTE_BASE + EXPERTS_PER_GROUP * gidx
    el = jnp.where((lane_f >= e_lo) & (lane_f < e_lo + EXPERTS_PER_GROUP), logits, neg)
    m1 = jnp.max(el, axis=-1, keepdims=True)
    i1 = jnp.min(jnp.where(el == m1, lane_f, big), axis=-1, keepdims=True)
    el2 = jnp.where(lane_f == i1, neg, el)
    m2 = jnp.max(el2, axis=-1, keepdims=True)
    i2 = jnp.min(jnp.where(el2 == m2, lane_f, big), axis=-1, keepdims=True)
    r2 = jnp.exp(m2 - m1)
    gate1 = p_group / (1.0 + r2)
    gate2 = p_group * r2 / (1.0 + r2)

    hot1 = lane_f == i1
    hot2 = lane_f == i2
    cnt = jnp.where(si > 0, (hot1 | hot2).astype(F32), 0.0)
    base = carry_scr[...] + _prefix_rows(cnt)
    rank1 = jnp.sum(jnp.where(hot1, base, 0.0), axis=-1, keepdims=True)
    rank2 = jnp.sum(jnp.where(hot2, base, 0.0), axis=-1, keepdims=True)
    total = carry_scr[...] + jnp.sum(cnt, axis=0, keepdims=True)
    carry_scr[...] = total
    counts_ref[...] = total

    route = jnp.zeros((TT, LANES), F32)
    for lane0, vals in ((COL_EXPERT, (i1 - ROUTE_BASE, i2 - ROUTE_BASE)), (COL_GATE, (gate1, gate2)),
                        (COL_RANK, (rank1, rank2))):
        for k, v in enumerate(vals):
            route = jnp.where(lane_i == lane0 + k, v, route)
    route_ref[...] = route
    meta_ref[...] = route.T[0:META_ROWS, :].astype(jnp.int32)


def _dispatch_kernel(dest_ref, x1s_ref, xs_ref, xin, zbuf, in_sems, out_sems, fill_sem):
    i = pl.program_id(0)
    n_steps = pl.num_programs(0)
    tile_rows = TD * SLAB

    def tile_load(t, slot):
        return pltpu.make_async_copy(x1s_ref.at[pl.ds(pl.multiple_of(t * tile_rows, tile_rows), tile_rows), :],
                                     xin.at[slot], in_sems.at[slot])

    def wait_rows(slot, sem):
        for k in range(TOP_K):
            pltpu.make_async_copy(xin.at[slot], xs_ref.at[pl.ds(0, tile_rows), :], sem).wait()

    @pl.when(i == 0)
    def _():
        tile_load(0, 0).start()
        zbuf[...] = jnp.zeros_like(zbuf)
        fill = pltpu.make_async_copy(zbuf, xs_ref.at[pl.ds(xs_ref.shape[0] - BLK * SLAB, BLK * SLAB), :], fill_sem)
        fill.start()
        fill.wait()

    slot = i % 3

    @pl.when(i + 1 < n_steps)
    def _():
        tile_load(i + 1, (i + 1) % 3).start()

    tile_load(i, slot).wait()
    for r in range(TD):
        for k in range(TOP_K):
            dst = pl.multiple_of(dest_ref[r // TT, k, r % TT] * SLAB, SLAB)
            pltpu.make_async_copy(xin.at[slot, pl.ds(r * SLAB, SLAB), :], xs_ref.at[pl.ds(dst, SLAB), :],
                                  out_sems.at[i % 2]).start(priority=k)

    @pl.when(i > 0)
    def _():
        wait_rows((i + 2) % 3, out_sems.at[(i + 1) % 2])

    @pl.when(i + 1 >= n_steps)
    def _():
        wait_rows(slot, out_sems.at[i % 2])


def _expert_kernel(be_ref, nv_ref, row0_ref, ord_ref, nblk_ref, xs_ref, wg_ref, wu_ref, wd_ref, yb_ref,
                   xbuf, wg_f, wu_f, wd_f, wg_s, wu_s, wd_s, sems, wsems):
    i = pl.program_id(0)
    nv = nv_ref[0]
    blk_rows = BLK * SLAB
    blk_a = i * BLOCKS_PER_STEP
    blk_b = blk_a + 1
    slot0 = (i % 2) * BLOCKS_PER_STEP

    def block_copy(b, slot):
        r = pl.multiple_of(row0_ref[b] * SLAB, SLAB)
        return pltpu.make_async_copy(xs_ref.at[pl.ds(r, blk_rows), :],
                                     xbuf.at[pl.ds(slot * blk_rows, blk_rows), :], sems.at[slot])

    def weight_copies(e, ws):
        return [pltpu.make_async_copy(src.at[e], dst.at[ws], wsems.at[ws])
                for src, dst in ((wg_ref, wg_f), (wu_ref, wu_f), (wd_ref, wd_f))]

    def fetch_step(first_block, first_slot):
        for sub in range(BLOCKS_PER_STEP):
            @pl.when(first_block + sub < nv)
            def _():
                block_copy(first_block + sub, first_slot + sub).start()

    @pl.when(i == 0)
    def _():
        fetch_step(0, 0)
        for c in weight_copies(be_ref[0], 0):
            c.start()

    fetch_step(blk_a + BLOCKS_PER_STEP, BLOCKS_PER_STEP - slot0)

    def take_weights(b):
        @pl.when(b < nv)
        def _():
            e = be_ref[b]
            ws = ord_ref[b] % 2

            @pl.when((b == 0) | (e != be_ref[jnp.maximum(b - 1, 0)]))
            def _():
                j = b + nblk_ref[e]

                @pl.when(j < nv)
                def _():
                    for c in weight_copies(be_ref[jnp.minimum(j, nv - 1)], 1 - ws):
                        c.start()

                for c in weight_copies(e, ws):
                    c.wait()
                wg_s[ws] = wg_f[ws].astype(BF16)
                wu_s[ws] = wu_f[ws].astype(BF16)
                wd_s[ws] = wd_f[ws].astype(BF16)

    take_weights(blk_a)
    take_weights(blk_b)

    def compute(b, sub):
        slot = slot0 + sub
        ws = ord_ref[b] % 2
        block_copy(b, slot).wait()
        xb = jnp.concatenate(
            [xbuf[pl.ds(slot * blk_rows + s, BLK, stride=SLAB), :].astype(BF16) for s in range(SLAB)],
            axis=1)
        a = jnp.dot(xb, wg_s[ws], preferred_element_type=F32)
        u = jnp.dot(xb, wu_s[ws], preferred_element_type=F32)
        h = (a * jax.nn.sigmoid(a)) * u
        y = jnp.dot(h.astype(BF16), wd_s[ws], preferred_element_type=F32)
        for s in range(SLAB):
            yb_ref[pl.ds(sub * blk_rows + s, BLK, stride=SLAB), :] = y[:, s * LANES:(s + 1) * LANES]

    def zero(sub):
        yb_ref[pl.ds(sub * blk_rows, blk_rows), :] = jnp.zeros((blk_rows, LANES), F32)

    @pl.when(blk_b < nv)
    def _():
        compute(blk_a, 0)
        compute(blk_b, 1)

    @pl.when((blk_a < nv) & (blk_b >= nv))
    def _():
        compute(blk_a, 0)
        zero(1)

    @pl.when(blk_a >= nv)
    def _():
        zero(0)
        zero(1)


def _combine_kernel(dest_ref, dest_next_ref, x1s_ref, route_ref, yb_ref, g_ref, b_ref, out_ref, ybuf, sems):
    i = pl.program_id(0)
    n_steps = pl.num_programs(0)
    half_rows = TD * SLAB
    par_rows = TOP_K * half_rows

    def start_gather(d_ref, par):
        for r in range(TD):
            for k in range(TOP_K):
                src = pl.multiple_of(d_ref[r // TT, k, r % TT] * SLAB, SLAB)
                dst = pl.multiple_of(par * par_rows + (k * TD + r) * SLAB, SLAB)
                pltpu.make_async_copy(yb_ref.at[pl.ds(src, SLAB), :], ybuf.at[pl.ds(dst, SLAB), :],
                                      sems.at[par]).start(priority=k)

    def wait_gather(par):
        pltpu.make_async_copy(yb_ref.at[pl.ds(0, par_rows), :],
                              ybuf.at[pl.ds(par * par_rows, par_rows), :], sems.at[par]).wait()

    par = i % 2

    @pl.when(i == 0)
    def _():
        start_gather(dest_ref, 0)

    wait_gather(par)
    start_gather(dest_next_ref, 1 - par)

    base = pl.multiple_of(par * par_rows, par_rows)
    g1 = route_ref[:, COL_GATE:COL_GATE + 1]
    g2 = route_ref[:, COL_GATE + 1:COL_GATE + 2]
    zs = []
    for s in range(SLAB):
        y1 = ybuf[pl.ds(base + s, TD, stride=SLAB), :]
        y2 = ybuf[pl.ds(base + half_rows + s, TD, stride=SLAB), :]
        zs.append(DEEPNORM_ALPHA * x1s_ref[pl.ds(s, TD, stride=SLAB), :] + (g1 * y1 + g2 * y2))
    z = jnp.concatenate(zs, axis=1)
    out_ref[...] = _layer_norm(z, g_ref[...], b_ref[...])

    @pl.when(i + 1 >= n_steps)
    def _():
        wait_gather(1 - par)


def _const_spec(shape):
    nd = len(shape)
    return pl.BlockSpec(shape, lambda *_: (0,) * nd, pipeline_mode=pl.Buffered(1))


def kernel(x, w_in, w_a2, b_a, lb_logits, norm_h, norm_g, w_out, ln1_g, ln1_b, w_group_router, w_expert_router,
           w_gate, w_up, w_down, ln2_g, ln2_b):
    B, T, D = x.shape
    n_tok = B * T
    assert D == D_MODEL == SLAB * LANES and T % TT == 0 and TD % TT == 0 and n_tok % TD == 0
    n_blocks = n_tok * TOP_K // BLK + N_EXPERTS
    assert n_blocks % BLOCKS_PER_STEP == 0 and BLOCKS_PER_STEP == 2
    n_slots = n_blocks * BLK

    wa2 =jnp.pad(w_a2[0], ((0, LANES - GATE_RANK), (0, 0))).astype(BF16)
    ba = b_a[0].reshape(1, GL_QK).astype(F32)
    gain = jnp.concatenate([norm_h[0], norm_g[0]]).reshape(1, D).astype(F32)
    wr = jnp.concatenate([w_group_router[0], w_expert_router[0]], axis=-1).astype(F32)
    wr = jnp.pad(wr, ((0, 0), (0, LANES - wr.shape[1])))
    wrh = wr.astype(BF16)
    wr2 = jnp.concatenate([wrh, (wr - wrh.astype(F32)).astype(BF16)], axis=1)
    wstk = jnp.asarray(_decay_matrices(), dtype=BF16)
    lev = jnp.asarray(_level_map())
    n_w = wstk.shape[0]

    n_steps = n_tok // TT
    x1s, route, meta, counts = pl.pallas_call(
        functools.partial(_mixer_kernel, steps_per_seq=T // TT),
        grid=(n_steps + 1,),
        in_specs=[
            pl.BlockSpec((TT, D), lambda i: (jnp.minimum(i, n_steps - 1), 0)),
            _const_spec((D, sum(IN_SPLITS))),
            _const_spec((LANES, GL_QK)),
            _const_spec((1, GL_QK)),
            _const_spec((DEPTH + 1, HG_WIDTH)),
            _const_spec((1, D)),
            _const_spec((D, D)),
            _const_spec((1, D)),
            _const_spec((1, D)),
            _const_spec((D, 2 * LANES)),
            _const_spec((n_w, 2 * SUB)),
            _const_spec((SUB, SUB)),
        ],
        out_specs=[
            pl.BlockSpec((TT * SLAB, LANES), lambda i: (jnp.minimum(i, n_steps - 1), 0)),
            pl.BlockSpec((TT, LANES), lambda i: (jnp.maximum(i - 1, 0), 0)),
            pl.BlockSpec((META_ROWS, TT), lambda i: (jnp.maximum(i - 1, 0), 0)),
            pl.BlockSpec((1, LANES), lambda i: (0, 0)),
        ],
        out_shape=[
            jax.ShapeDtypeStruct((n_tok * SLAB, LANES), F32),
            jax.ShapeDtypeStruct((n_tok, LANES), F32),
            jax.ShapeDtypeStruct((n_steps * META_ROWS, TT), jnp.int32),
            jax.ShapeDtypeStruct((1, LANES), F32),
        ],
        scratch_shapes=[
            pltpu.VMEM((D, PROJ_COLS), BF16),
            pltpu.VMEM((D, D), BF16),
            pltpu.VMEM((TT, PROJ_COLS), F32),
            pltpu.VMEM((TT, D), F32),
            pltpu.VMEM((SUB, HG_WIDTH + GL_QK), F32),
            pltpu.VMEM((TT, LANES), F32),
            pltpu.VMEM((HG_HEADS + GL_HEADS, LANES, LANES), F32),
            pltpu.VMEM((1, LANES), F32),
        ],
        compiler_params=pltpu.CompilerParams(
            dimension_semantics=("arbitrary",), vmem_limit_bytes=VMEM_LIMIT),
        name="mixer",
    )(x.reshape(n_tok, D), w_in[0], wa2, ba, lb_logits.astype(F32), gain, w_out[0], ln1_g[0].reshape(1, D),
      ln1_b[0].reshape(1, D), wr2, wstk, lev)

    cnt = counts[0, ROUTE_BASE:ROUTE_BASE + N_EXPERTS].astype(jnp.int32)
    padded = (cnt + BLK - 1) // BLK * BLK
    ends = jnp.cumsum(padded)
    starts = ends - padded
    n_valid = (ends[-1] // BLK).astype(jnp.int32)
    blk_ids = jnp.minimum(jnp.arange(n_blocks, dtype=jnp.int32), n_valid - 1)
    block_expert = jnp.sum((ends[None, :] <= (blk_ids * BLK)[:, None]).astype(jnp.int32), axis=1)
    block_expert = jnp.minimum(block_expert, N_EXPERTS - 1)
    blk_ord = jnp.cumsum(jnp.concatenate(
        [jnp.zeros((1,), jnp.int32), (block_expert[1:] != block_expert[:-1]).astype(jnp.int32)]))
    dstarts = jnp.cumsum(cnt) - cnt
    first_blk = starts // BLK
    pick = block_expert[None, :] == jnp.arange(N_EXPERTS, dtype=jnp.int32)[:, None]
    row0 = jnp.sum(jnp.where(pick, (dstarts - first_blk * BLK)[:, None], 0), axis=0) + blk_ids * BLK
    meta3 = meta.reshape(n_steps, META_ROWS, TT)
    hit = (meta3[None, :, COL_EXPERT:COL_EXPERT + TOP_K, :]
           == jnp.arange(N_EXPERTS, dtype=jnp.int32)[:, None, None, None])
    rank = meta3[:, COL_RANK:COL_RANK + TOP_K, :]
    dest_x = rank + jnp.sum(jnp.where(hit, dstarts[:, None, None, None], 0), axis=0)
    dest = rank + jnp.sum(jnp.where(hit, starts[:, None, None, None], 0), axis=0)

    xs_rows = (n_tok * TOP_K + BLK) * SLAB
    xs = pl.pallas_call(
        _dispatch_kernel,
        grid=(n_tok // TD,),
        in_specs=[
            pl.BlockSpec((TD // TT, TOP_K, TT), lambda i: (i, 0, 0), memory_space=pltpu.SMEM),
            pl.BlockSpec(memory_space=pl.ANY),
        ],
        out_specs=pl.BlockSpec(memory_space=pl.ANY),
        out_shape=jax.ShapeDtypeStruct((xs_rows, LANES), F32),
        scratch_shapes=[pltpu.VMEM((3, TD * SLAB, LANES), F32), pltpu.VMEM((BLK * SLAB, LANES), F32),
                        pltpu.SemaphoreType.DMA((3,)), pltpu.SemaphoreType.DMA((2,)), pltpu.SemaphoreType.DMA],
        compiler_params=pltpu.CompilerParams(dimension_semantics=("arbitrary",)),
        name="dispatch",
    )(dest_x, x1s)

    yb = pl.pallas_call(
        _expert_kernel,
        grid_spec=pltpu.PrefetchScalarGridSpec(
            num_scalar_prefetch=5,
            grid=(n_blocks // BLOCKS_PER_STEP,),
            in_specs=[pl.BlockSpec(memory_space=pl.ANY)] * 4,
            out_specs=pl.BlockSpec((BLOCKS_PER_STEP * BLK * SLAB, LANES), lambda i, *_: (i, 0)),
            scratch_shapes=[
                pltpu.VMEM((2 * BLOCKS_PER_STEP * BLK * SLAB, LANES), F32),
                pltpu.VMEM((2, D, D_EXPERT), F32),
                pltpu.VMEM((2, D, D_EXPERT), F32),
                pltpu.VMEM((2, D_EXPERT, D), F32),
                pltpu.VMEM((2, D, D_EXPERT), BF16),
                pltpu.VMEM((2, D, D_EXPERT), BF16),
                pltpu.VMEM((2, D_EXPERT, D), BF16),
                pltpu.SemaphoreType.DMA((2 * BLOCKS_PER_STEP,)),
                pltpu.SemaphoreType.DMA((2,)),
            ],
        ),
        out_shape=jax.ShapeDtypeStruct((n_slots * SLAB, LANES), F32),
        compiler_params=pltpu.CompilerParams(
            dimension_semantics=("arbitrary",), vmem_limit_bytes=VMEM_LIMIT),
        name="experts",
    )(block_expert, n_valid.reshape(1), row0, blk_ord, padded // BLK, xs, w_gate[0], w_up[0], w_down[0])

    last = n_tok // TD - 1
    out = pl.pallas_call(
        _combine_kernel,
        grid=(n_tok // TD,),
        in_specs=[
            pl.BlockSpec((TD // TT, TOP_K, TT), lambda i: (i, 0, 0), memory_space=pltpu.SMEM),
            pl.BlockSpec((TD // TT, TOP_K, TT), lambda i: (jnp.minimum(i + 1, last), 0, 0), memory_space=pltpu.SMEM),
            pl.BlockSpec((TD * SLAB, LANES), lambda i: (i, 0)),
            pl.BlockSpec((TD, LANES), lambda i: (i, 0)),
            pl.BlockSpec(memory_space=pl.ANY),
            pl.BlockSpec((1, D), lambda i: (0, 0)),
            pl.BlockSpec((1, D), lambda i: (0, 0)),
        ],
        out_specs=pl.BlockSpec((TD, D), lambda i: (i, 0)),
        scratch_shapes=[pltpu.VMEM((2 * TOP_K * TD * SLAB, LANES), F32), pltpu.SemaphoreType.DMA((2,))],
        out_shape=jax.ShapeDtypeStruct((n_tok, D), F32),
        compiler_params=pltpu.CompilerParams(dimension_semantics=("arbitrary",)),
        name="combine",
    )(dest, dest, x1s, route, yb, ln2_g[0].reshape(1, D), ln2_b[0].reshape(1, D))

    return out.reshape(B, T, D).astype(x.dtype)
```

```python
import functools

import numpy as np
import jax
import jax.numpy as jnp
from jax import lax
from jax.experimental import pallas as pl
from jax.experimental.pallas import tpu as pltpu

D_MODEL = 1024
HG_WIDTH = 512
HG_HEAD_DIM = 128
HG_HEADS = 4
GL_WIDTH = 512
GL_HEADS = 4
GL_DV = 128
GL_DK = 64
GL_QK = 256
GATE_RANK = 16
GATE_TAU = 16.0
N_GROUPS = 8
EXPERTS_PER_GROUP = 8
N_EXPERTS = 64
TOP_K = 2
D_EXPERT = 512
DEPTH = 1
DEEPNORM_ALPHA = (2.0 * DEPTH) ** 0.25
LN_EPS = 1e-5
LOG2E = 1.4426950408889634
IN_SPLITS = (HG_WIDTH, HG_WIDTH, HG_WIDTH, HG_WIDTH, GL_QK, GL_QK, GL_WIDTH, GATE_RANK, GL_WIDTH)

LANES = 128
SUB = 128
N_LEVELS = 7
MXU_LEVELS = 3
TT = 256
TD = 512
BLK = 256
BLOCKS_PER_STEP = 2
PROJ_WIDTHS = (HG_WIDTH, HG_WIDTH, HG_WIDTH, HG_WIDTH, GL_QK, GL_QK, GL_WIDTH, GL_WIDTH, LANES)
PROJ_SLICES = tuple(slice(sum(PROJ_WIDTHS[:n]), sum(PROJ_WIDTHS[:n + 1])) for n in range(len(PROJ_WIDTHS)))
PROJ_COLS = sum(PROJ_WIDTHS)
ROUTE_BASE = N_GROUPS
SLAB = 8
COL_EXPERT, COL_GATE, COL_RANK = 0, 2, 4
META_ROWS = 8
V7X_VMEM_BYTES = 64 * 1024 * 1024
VMEM_LIMIT = V7X_VMEM_BYTES * 7 // 8
assert DEPTH == 1 and D_MODEL == SLAB * LANES and GL_DK * 2 == LANES and SUB == 1 << N_LEVELS

F32 = jnp.float32
BF16 = jnp.bfloat16
NT_DIMS = (((1,), (1,)), ((), ()))
TN_DIMS = (((0,), (0,)), ((), ()))


def _decay_matrices():
    t = np.arange(SUB)[:, None]
    j = np.arange(SUB)[None, :]
    mats = [(j <= t)]
    for lvl in range(1, MXU_LEVELS + 1):
        blk = 1 << lvl
        half = blk >> 1
        r = (t // blk) * blk + half - 1
        right = (t % blk) >= half
        mats.append(np.where(right, (j > r) & (j <= t), (j > t) & (j <= r)))
    w = np.concatenate(mats, axis=0).astype(np.float32)
    return np.concatenate([w, w], axis=1)


def _level_map():
    t = np.arange(SUB)[:, None]
    s = np.arange(SUB)[None, :]
    x = np.bitwise_xor(t, s)
    lvl = np.floor(np.log2(np.maximum(x, 1))).astype(np.int32) + 1
    lvl = np.where(x == 0, 0, lvl)
    return np.where(s > t, -1, lvl).astype(np.int32)


def _layer_norm(y, g, b):
    mu = jnp.mean(y, axis=-1, keepdims=True)
    yc = y - mu
    var = jnp.mean(yc * yc, axis=-1, keepdims=True)
    return yc * lax.rsqrt(var + LN_EPS) * g + b


def _prefix_rows(x):
    n = x.shape[0]
    row = lax.broadcasted_iota(jnp.int32, (n, 1), 0)
    inc = x
    sh = 1
    while sh < n:
        inc = inc + jnp.where(row >= sh, pltpu.roll(inc, sh, axis=0), 0.0)
        sh *= 2
    return inc - x


def _mixer_kernel(x_ref, winf_ref, wa2_ref, ba_ref, lbl_ref, gain_ref, woutf_ref, ln1g_ref, ln1b_ref,
                  wr_ref, wstk_ref, lev_ref,
                  x1s_ref, route_ref, meta_ref, counts_ref,
                  win_ref, wout_ref, p_scr, o_scr, b_scr, lg_scr, st_scr, carry_scr, *, steps_per_seq):
    si = pl.program_id(0)
    n_tiles = pl.num_programs(0) - 1

    @pl.when(si == 0)
    def _():
        carry_scr[...] = jnp.zeros_like(carry_scr)
        lg_scr[...] = jnp.zeros_like(lg_scr)
        keep = sum(IN_SPLITS[:7])
        wout_ref[...] = woutf_ref[...].astype(BF16)
        win_ref[:, keep + GL_WIDTH:] = jnp.zeros((D_MODEL, PROJ_COLS - keep - GL_WIDTH), BF16)

        def rows(c, carry):
            r = pl.ds(pl.multiple_of(c * SUB, SUB), SUB)
            win_ref[r, 0:keep] = winf_ref[r, 0:keep].astype(BF16)
            win_ref[r, keep:keep + GL_WIDTH] = winf_ref[r, keep + GATE_RANK:keep + GATE_RANK + GL_WIDTH].astype(BF16)
            win_ref[r, keep + GL_WIDTH:keep + GL_WIDTH + GATE_RANK] = winf_ref[r, keep:keep + GATE_RANK].astype(BF16)
            return carry
        lax.fori_loop(0, D_MODEL // SUB, rows, 0)

    @pl.when(si % steps_per_seq == 0)
    def _():
        st_scr[...] = jnp.zeros_like(st_scr)

    xt = x_ref[...]
    p_scr[...] = jnp.dot(xt.astype(BF16), win_ref[...], preferred_element_type=F32)
    _route_tile(si, lg_scr[...], carry_scr, route_ref, meta_ref, counts_ref)

    lbl = lbl_ref[...]
    lbe = jnp.exp(lbl - jnp.max(lbl, axis=0, keepdims=True))
    lb = lbe[0:1, :] / jnp.sum(lbe, axis=0, keepdims=True)

    lev = lev_ref[...]
    tcol = lax.broadcasted_iota(jnp.int32, (SUB, 1), 0)
    qrow = [(tcol & ((1 << l) - 1)) >= (1 << (l - 1)) for l in range(1, N_LEVELS + 1)]
    lane = lax.broadcasted_iota(jnp.int32, (1, LANES), 1)
    half_mask = [(lane < GL_DK), (lane >= GL_DK)]

    def by_halves(l, left_fn, right_fn):
        blk = 1 << l
        half = blk >> 1
        parts = []
        for j in range(SUB // blk):
            lo = j * blk
            parts += [left_fn(lo, lo + half), right_fn(lo + half, lo + blk)]
        return jnp.concatenate(parts, axis=0)

    def subtile(s, carry):
        r0 = pl.multiple_of(s * SUB, SUB)
        rows = pl.ds(r0, SUB)
        hq, hf, hv, hgate, gq, gk, gv, ggate, ga = [p_scr[rows, c] for c in PROJ_SLICES]

        f = lb + (1.0 - lb) * jax.nn.sigmoid(hf)
        kh = 1.0 - f
        lgh = jnp.log(f)
        z = jnp.dot(ga.astype(BF16), wa2_ref[...], preferred_element_type=F32) + ba_ref[...]
        lga = (jnp.minimum(z, 0.0) - jnp.log1p(jnp.exp(-jnp.abs(z)))) * (1.0 / GATE_TAU)
        lg = jnp.concatenate([lgh, lga], axis=1) * LOG2E

        l_hi = lg.astype(BF16)
        l_mid = (lg - l_hi.astype(F32)).astype(BF16)
        sums = jnp.dot(wstk_ref[...], jnp.concatenate([l_hi, l_mid], axis=0),
                       preferred_element_type=F32)
        bcum = sums[0:SUB]
        b_scr[...] = bcum
        expos = [bcum, b_scr[SUB - 1:SUB, :] - bcum]
        for l in range(1, N_LEVELS + 1):
            if l <= MXU_LEVELS:
                expos.append(sums[l * SUB:(l + 1) * SUB])
            else:
                expos.append(by_halves(l, lambda lo, mid: b_scr[mid - 1:mid, :] - bcum[lo:mid],
                                       lambda mid, hi: bcum[mid:hi] - b_scr[mid - 1:mid, :]))
        decs = [jnp.exp2(e) for e in expos]

        def group_terms(q, k, c0):
            e_pre = decs[0][:, c0:c0 + LANES]
            e_suf = decs[1][:, c0:c0 + LANES]
            us = []
            for l in range(1, N_LEVELS + 1):
                e_l = decs[1 + l][:, c0:c0 + LANES]
                if l <= MXU_LEVELS:
                    sel = jnp.where(qrow[l - 1], q, k)
                else:
                    sel = by_halves(l, lambda lo, mid: k[lo:mid], lambda mid, hi: q[mid:hi])
                u = sel * e_l
                us.append((u, u.astype(BF16)))
            return us, (q * e_pre).astype(BF16), k * e_suf, e_pre[SUB - 1:SUB, :]

        def scores(us_ab, qks_ab, masks):
            nm = len(masks)
            zero_b = jnp.zeros((SUB, LANES), BF16)
            scs = [[jnp.where(lev == 0, jnp.sum(qk, axis=-1, keepdims=True), 0.0) for qk in qks] for qks in qks_ab]
            for l in range(1, N_LEVELS + 1):
                (ua, uba), (ub_, ubb) = us_ab[0][l - 1], us_ab[1][l - 1]
                rhs = jnp.concatenate([jnp.concatenate([uba, zero_b], axis=1),
                                       jnp.concatenate([zero_b, ubb], axis=1)], axis=0)
                if l <= MXU_LEVELS:
                    rows_l = SUB
                    lhs = jnp.concatenate(
                        [jnp.concatenate([x if m is None else jnp.where(m, x, jnp.zeros_like(x)) for x in (uba, ubb)],
                                         axis=1) for m in masks], axis=0)
                    g = lax.dot_general(lhs, rhs, NT_DIMS, preferred_element_type=F32)
                    for gi in range(2):
                        for n in range(nm):
                            gn = g[n * rows_l:(n + 1) * rows_l, gi * LANES:(gi + 1) * LANES]
                            scs[gi][n] = jnp.where(lev == l, gn, scs[gi][n])
                else:
                    blk = 1 << l
                    half = blk >> 1
                    mids = [j * blk + half for j in range(SUB // blk)]
                    rows_l = SUB // 2
                    uqs = [jnp.concatenate([x[m:m + half] for m in mids], axis=0) for x in (ua, ub_)]
                    lhs = jnp.concatenate(
                        [jnp.concatenate([x if m is None else jnp.where(m, x, 0.0) for x in uqs], axis=1)
                         for m in masks], axis=0)
                    g = lax.dot_general(lhs.astype(BF16), rhs, NT_DIMS, preferred_element_type=F32)
                    for gi in range(2):
                        for n in range(nm):
                            parts = []
                            for j, m in enumerate(mids):
                                gj = g[n * rows_l + j * half:n * rows_l + (j + 1) * half, gi * LANES:(gi + 1) * LANES]
                                parts.append(scs[gi][n][m - half:m])
                                parts.append(jnp.where(lev[m:m + half] == l, gj, scs[gi][n][m:m + half]))
                            scs[gi][n] = jnp.concatenate(parts, axis=0)
            return scs

        def block_diag(a, b):
            za = jnp.zeros_like(a)
            return jnp.concatenate([jnp.concatenate([a, za], axis=1), jnp.concatenate([za, b], axis=1)], axis=0)

        def head_pair(hidx, scs2, qds, kds, vs, g_lasts):
            vb = [v.astype(BF16) for v in vs]
            sts = [st_scr[h] for h in hidx]
            o = jnp.dot(jnp.concatenate([s.astype(BF16) for s in scs2], axis=1), block_diag(*vb),
                        preferred_element_type=F32)
            o = o + lax.dot_general(jnp.concatenate(qds, axis=1), block_diag(*[s.astype(BF16) for s in sts]),
                                    NT_DIMS, preferred_element_type=F32)
            ut = lax.dot_general(jnp.concatenate(vb, axis=1), jnp.concatenate([k.astype(BF16) for k in kds], axis=1),
                                 TN_DIMS, preferred_element_type=F32)
            for n, h in enumerate(hidx):
                st_scr[h] = sts[n] * g_lasts[n] + ut[n * LANES:(n + 1) * LANES, n * LANES:(n + 1) * LANES]
            return o[:, :LANES], o[:, LANES:]

        def finish(o, gate, c0):
            o = o * lax.rsqrt(jnp.mean(o * o, axis=-1, keepdims=True) + LN_EPS)
            o = o * gain_ref[:, c0:c0 + LANES] * (gate * jax.nn.sigmoid(gate))
            o_scr[rows, c0:c0 + LANES] = o

        for g0 in range(0, HG_HEADS, 2):
            terms = []
            for g in (g0, g0 + 1):
                c0 = g * LANES
                q = hq[:, c0:c0 + LANES] * (HG_HEAD_DIM ** -0.5)
                k = kh[:, c0:c0 + LANES]
                terms.append((q * k,) + group_terms(q, k, c0))
            scs = scores([t[1] for t in terms], [[t[0]] for t in terms], [None])
            cols = [slice(g * LANES, (g + 1) * LANES) for g in (g0, g0 + 1)]
            outs = head_pair((g0, g0 + 1), [scs[0][0], scs[1][0]], [t[2] for t in terms], [t[3] for t in terms],
                             [hv[:, c] for c in cols], [t[4] for t in terms])
            for o, c in zip(outs, cols):
                finish(o, hgate[:, c], c.start)

        terms = []
        for p in range(GL_HEADS // 2):
            c0 = p * LANES
            q = gq[:, c0:c0 + LANES] * (GL_DK ** -0.5)
            k = gk[:, c0:c0 + LANES]
            terms.append((q * k,) + group_terms(q, k, HG_WIDTH + c0))
        scs = scores([t[1] for t in terms], [[jnp.where(m, t[0], 0.0) for m in half_mask] for t in terms], half_mask)
        for p in range(GL_HEADS // 2):
            _, _, qd, kd, g_last = terms[p]
            cols = [slice(h * LANES, (h + 1) * LANES) for h in (2 * p, 2 * p + 1)]
            outs = head_pair((HG_HEADS + 2 * p, HG_HEADS + 2 * p + 1), scs[p], [qd, qd],
                             [jnp.where(m, kd, 0.0) for m in half_mask], [gv[:, c] for c in cols], [g_last, g_last])
            for o, c in zip(outs, cols):
                finish(o, ggate[:, c], HG_WIDTH + c.start)
        return carry

    lax.fori_loop(0, TT // SUB, subtile, 0, unroll=True)

    hmix = jnp.dot(o_scr[...].astype(BF16), wout_ref[...], preferred_element_type=F32)
    x1 = _layer_norm(DEEPNORM_ALPHA * xt + hmix, ln1g_ref[...], ln1b_ref[...])
    x1h = x1.astype(BF16)
    x1l = (x1 - x1h.astype(F32)).astype(BF16)
    prod = jnp.dot(jnp.concatenate([x1h, x1l], axis=0), wr_ref[...], preferred_element_type=F32)
    logits = prod[:TT, :LANES] + prod[TT:, :LANES] + prod[:TT, LANES:]

    @pl.when(si < n_tiles)
    def _():
        for s in range(SLAB):
            x1s_ref[pl.ds(s, TT, stride=SLAB), :] = x1[:, s * LANES:(s + 1) * LANES]
        lg_scr[...] = logits


def _route_tile(si, logits, carry_scr, route_ref, meta_ref, counts_ref):
    lane_i = lax.broadcasted_iota(jnp.int32, (TT, LANES), 1)
    lane_f = lane_i.astype(F32)
    neg = jnp.float32(-jnp.inf)
    big = jnp.float32(1e9)

    gl = jnp.where(lane_i < N_GROUPS, logits, neg)
    gmax = jnp.max(gl, axis=-1, keepdims=True)
    gidx = jnp.min(jnp.where(gl == gmax, lane_f, big), axis=-1, keepdims=True)
    p_group = 1.0 / jnp.sum(jnp.exp(gl - gmax), axis=-1, keepdims=True)

    e_lo = ROUTE_BASE + EXPERTS_PER_GROUP * gidx
    el = jnp.where((lane_f >= e_lo) & (lane_f < e_lo + EXPERTS_PER_GROUP), logits, neg)
    m1 = jnp.max(el, axis=-1, keepdims=True)
    i1 = jnp.min(jnp.where(el == m1, lane_f, big), axis=-1, keepdims=True)
    el2 = jnp.where(lane_f == i1, neg, el)
    m2 = jnp.max(el2, axis=-1, keepdims=True)
    i2 = jnp.min(jnp.where(el2 == m2, lane_f, big), axis=-1, keepdims=True)
    r2 = jnp.exp(m2 - m1)
    gate1 = p_group / (1.0 + r2)
    gate2 = p_group * r2 / (1.0 + r2)

    hot1 = lane_f == i1
    hot2 = lane_f == i2
    cnt = jnp.where(si > 0, (hot1 | hot2).astype(F32), 0.0)
    base = carry_scr[...] + _prefix_rows(cnt)
    rank1 = jnp.sum(jnp.where(hot1, base, 0.0), axis=-1, keepdims=True)
    rank2 = jnp.sum(jnp.where(hot2, base, 0.0), axis=-1, keepdims=True)
    total = carry_scr[...] + jnp.sum(cnt, axis=0, keepdims=True)
    carry_scr[...] = total
    counts_ref[...] = total

    route = jnp.zeros((TT, LANES), F32)
    for lane0, vals in ((COL_EXPERT, (i1 - ROUTE_BASE, i2 - ROUTE_BASE)), (COL_GATE, (gate1, gate2)),
                        (COL_RANK, (rank1, rank2))):
        for k, v in enumerate(vals):
            route = jnp.where(lane_i == lane0 + k, v, route)
    route_ref[...] = route
    meta_ref[...] = route.T[0:META_ROWS, :].astype(jnp.int32)


def _dispatch_kernel(dest_ref, x1s_ref, xs_ref, xin, zbuf, in_sems, out_sems, fill_sem):
    i = pl.program_id(0)
    n_steps = pl.num_programs(0)
    tile_rows = TD * SLAB

    def tile_load(t, slot):
        return pltpu.make_async_copy(x1s_ref.at[pl.ds(pl.multiple_of(t * tile_rows, tile_rows), tile_rows), :],
                                     xin.at[slot], in_sems.at[slot])

    def wait_rows(slot, sem):
        for k in range(TOP_K):
            pltpu.make_async_copy(xin.at[slot], xs_ref.at[pl.ds(0, tile_rows), :], sem).wait()

    @pl.when(i == 0)
    def _():
        tile_load(0, 0).start()
        zbuf[...] = jnp.zeros_like(zbuf)
        fill = pltpu.make_async_copy(zbuf, xs_ref.at[pl.ds(xs_ref.shape[0] - BLK * SLAB, BLK * SLAB), :], fill_sem)
        fill.start()
        fill.wait()

    slot = i % 3

    @pl.when(i + 1 < n_steps)
    def _():
        tile_load(i + 1, (i + 1) % 3).start()

    tile_load(i, slot).wait()
    for r in range(TD):
        for k in range(TOP_K):
            dst = pl.multiple_of(dest_ref[r // TT, k, r % TT] * SLAB, SLAB)
            pltpu.make_async_copy(xin.at[slot, pl.ds(r * SLAB, SLAB), :], xs_ref.at[pl.ds(dst, SLAB), :],
                                  out_sems.at[i % 2]).start(priority=k)

    @pl.when(i > 0)
    def _():
        wait_rows((i + 2) % 3, out_sems.at[(i + 1) % 2])

    @pl.when(i + 1 >= n_steps)
    def _():
        wait_rows(slot, out_sems.at[i % 2])


def _expert_kernel(be_ref, nv_ref, row0_ref, ord_ref, nblk_ref, xs_ref, wg_ref, wu_ref, wd_ref, yb_ref,
                   xbuf, wg_f, wu_f, wd_f, wg_s, wu_s, wd_s, sems, wsems):
    i = pl.program_id(0)
    nv = nv_ref[0]
    blk_rows = BLK * SLAB
    blk_a = i * BLOCKS_PER_STEP
    blk_b = blk_a + 1
    slot0 = (i % 2) * BLOCKS_PER_STEP

    def block_copy(b, slot):
        r = pl.multiple_of(row0_ref[b] * SLAB, SLAB)
        return pltpu.make_async_copy(xs_ref.at[pl.ds(r, blk_rows), :],
                                     xbuf.at[pl.ds(slot * blk_rows, blk_rows), :], sems.at[slot])

    def weight_copies(e, ws):
        return [pltpu.make_async_copy(src.at[e], dst.at[ws], wsems.at[ws])
                for src, dst in ((wg_ref, wg_f), (wu_ref, wu_f), (wd_ref, wd_f))]

    def fetch_step(first_block, first_slot):
        for sub in range(BLOCKS_PER_STEP):
            @pl.when(first_block + sub < nv)
            def _():
                block_copy(first_block + sub, first_slot + sub).start()

    @pl.when(i == 0)
    def _():
        fetch_step(0, 0)
        for c in weight_copies(be_ref[0], 0):
            c.start()

    fetch_step(blk_a + BLOCKS_PER_STEP, BLOCKS_PER_STEP - slot0)

    def take_weights(b):
        @pl.when(b < nv)
        def _():
            e = be_ref[b]
            ws = ord_ref[b] % 2

            @pl.when((b == 0) | (e != be_ref[jnp.maximum(b - 1, 0)]))
            def _():
                j = b + nblk_ref[e]

                @pl.when(j < nv)
                def _():
                    for c in weight_copies(be_ref[jnp.minimum(j, nv - 1)], 1 - ws):
                        c.start()

                for c in weight_copies(e, ws):
                    c.wait()
                wg_s[ws] = wg_f[ws].astype(BF16)
                wu_s[ws] = wu_f[ws].astype(BF16)
                wd_s[ws] = wd_f[ws].astype(BF16)

    take_weights(blk_a)
    take_weights(blk_b)

    def compute(b, sub):
        slot = slot0 + sub
        ws = ord_ref[b] % 2
        block_copy(b, slot).wait()
        xb = jnp.concatenate(
            [xbuf[pl.ds(slot * blk_rows + s, BLK, stride=SLAB), :].astype(BF16) for s in range(SLAB)],
            axis=1)
        a = jnp.dot(xb, wg_s[ws], preferred_element_type=F32)
        u = jnp.dot(xb, wu_s[ws], preferred_element_type=F32)
        h = (a * jax.nn.sigmoid(a)) * u
        y = jnp.dot(h.astype(BF16), wd_s[ws], preferred_element_type=F32)
        for s in range(SLAB):
            yb_ref[pl.ds(sub * blk_rows + s, BLK, stride=SLAB), :] = y[:, s * LANES:(s + 1) * LANES]

    def zero(sub):
        yb_ref[pl.ds(sub * blk_rows, blk_rows), :] = jnp.zeros((blk_rows, LANES), F32)

    @pl.when(blk_b < nv)
    def _():
        compute(blk_a, 0)
        compute(blk_b, 1)

    @pl.when((blk_a < nv) & (blk_b >= nv))
    def _():
        compute(blk_a, 0)
        zero(1)

    @pl.when(blk_a >= nv)
    def _():
        zero(0)
        zero(1)


def _combine_kernel(dest_ref, dest_next_ref, x1s_ref, route_ref, yb_ref, g_ref, b_ref, out_ref, ybuf_a, ybuf_b,
                    sems):
    i = pl.program_id(0)
    n_steps = pl.num_programs(0)
    half_rows = TD * SLAB
    par_rows = TOP_K * half_rows
    bufs = (ybuf_a, ybuf_b)

    def start_gather(d_ref, par):
        for r in range(TD):
            for k in range(TOP_K):
                src = pl.multiple_of(d_ref[r // TT, k, r % TT] * SLAB, SLAB)
                pltpu.make_async_copy(yb_ref.at[pl.ds(src, SLAB), :],
                                      bufs[par].at[pl.ds((k * TD + r) * SLAB, SLAB), :],
                                      sems.at[par]).start(priority=k)

    def wait_gather(par):
        pltpu.make_async_copy(yb_ref.at[pl.ds(0, par_rows), :], bufs[par], sems.at[par]).wait()

    @pl.when(i == 0)
    def _():
        start_gather(dest_ref, 0)

    def step(par):
        ybuf = bufs[par]
        wait_gather(par)
        start_gather(dest_next_ref, 1 - par)
        g1 = route_ref[:, COL_GATE:COL_GATE + 1]
        g2 = route_ref[:, COL_GATE + 1:COL_GATE + 2]
        zs = []
        for s in range(SLAB):
            y1 = ybuf[pl.ds(s, TD, stride=SLAB), :]
            y2 = ybuf[pl.ds(half_rows + s, TD, stride=SLAB), :]
            zs.append(DEEPNORM_ALPHA * x1s_ref[pl.ds(s, TD, stride=SLAB), :] + (g1 * y1 + g2 * y2))
        z = jnp.concatenate(zs, axis=1)
        out_ref[...] = _layer_norm(z, g_ref[...], b_ref[...])

        @pl.when(i + 1 >= n_steps)
        def _():
            wait_gather(1 - par)

    for par in range(2):
        pl.when(i % 2 == par)(functools.partial(step, par))


def _const_spec(shape):
    nd = len(shape)
    return pl.BlockSpec(shape, lambda *_: (0,) * nd, pipeline_mode=pl.Buffered(1))


def kernel(x, w_in, w_a2, b_a, lb_logits, norm_h, norm_g, w_out, ln1_g, ln1_b, w_group_router, w_expert_router,
           w_gate, w_up, w_down, ln2_g, ln2_b):
    B, T, D = x.shape
    n_tok = B * T
    assert D == D_MODEL == SLAB * LANES and T % TT == 0 and TD % TT == 0 and n_tok % TD == 0
    n_blocks = n_tok * TOP_K // BLK + N_EXPERTS
    assert n_blocks % BLOCKS_PER_STEP == 0 and BLOCKS_PER_STEP == 2
    n_slots = n_blocks * BLK

    wa2 =jnp.pad(w_a2[0], ((0, LANES - GATE_RANK), (0, 0))).astype(BF16)
    ba = b_a[0].reshape(1, GL_QK).astype(F32)
    gain = jnp.concatenate([norm_h[0], norm_g[0]]).reshape(1, D).astype(F32)
    wr = jnp.concatenate([w_group_router[0], w_expert_router[0]], axis=-1).astype(F32)
    wr = jnp.pad(wr, ((0, 0), (0, LANES - wr.shape[1])))
    wrh = wr.astype(BF16)
    wr2 = jnp.concatenate([wrh, (wr - wrh.astype(F32)).astype(BF16)], axis=1)
    wstk = jnp.asarray(_decay_matrices(), dtype=BF16)
    lev = jnp.asarray(_level_map())
    n_w = wstk.shape[0]

    n_steps = n_tok // TT
    x1s, route, meta, counts = pl.pallas_call(
        functools.partial(_mixer_kernel, steps_per_seq=T // TT),
        grid=(n_steps + 1,),
        in_specs=[
            pl.BlockSpec((TT, D), lambda i: (jnp.minimum(i, n_steps - 1), 0)),
            _const_spec((D, sum(IN_SPLITS))),
            _const_spec((LANES, GL_QK)),
            _const_spec((1, GL_QK)),
            _const_spec((DEPTH + 1, HG_WIDTH)),
            _const_spec((1, D)),
            _const_spec((D, D)),
            _const_spec((1, D)),
            _const_spec((1, D)),
            _const_spec((D, 2 * LANES)),
            _const_spec((n_w, 2 * SUB)),
            _const_spec((SUB, SUB)),
        ],
        out_specs=[
            pl.BlockSpec((TT * SLAB, LANES), lambda i: (jnp.minimum(i, n_steps - 1), 0)),
            pl.BlockSpec((TT, LANES), lambda i: (jnp.maximum(i - 1, 0), 0)),
            pl.BlockSpec((META_ROWS, TT), lambda i: (jnp.maximum(i - 1, 0), 0)),
            pl.BlockSpec((1, LANES), lambda i: (0, 0)),
        ],
        out_shape=[
            jax.ShapeDtypeStruct((n_tok * SLAB, LANES), F32),
            jax.ShapeDtypeStruct((n_tok, LANES), F32),
            jax.ShapeDtypeStruct((n_steps * META_ROWS, TT), jnp.int32),
            jax.ShapeDtypeStruct((1, LANES), F32),
        ],
        scratch_shapes=[
            pltpu.VMEM((D, PROJ_COLS), BF16),
            pltpu.VMEM((D, D), BF16),
            pltpu.VMEM((TT, PROJ_COLS), F32),
            pltpu.VMEM((TT, D), F32),
            pltpu.VMEM((SUB, HG_WIDTH + GL_QK), F32),
            pltpu.VMEM((TT, LANES), F32),
            pltpu.VMEM((HG_HEADS + GL_HEADS, LANES, LANES), F32),
            pltpu.VMEM((1, LANES), F32),
        ],
        compiler_params=pltpu.CompilerParams(
            dimension_semantics=("arbitrary",), vmem_limit_bytes=VMEM_LIMIT),
        name="mixer",
    )(x.reshape(n_tok, D), w_in[0], wa2, ba, lb_logits.astype(F32), gain, w_out[0], ln1_g[0].reshape(1, D),
      ln1_b[0].reshape(1, D), wr2, wstk, lev)

    cnt = counts[0, ROUTE_BASE:ROUTE_BASE + N_EXPERTS].astype(jnp.int32)
    padded = (cnt + BLK - 1) // BLK * BLK
    ends = jnp.cumsum(padded)
    starts = ends - padded
    n_valid = (ends[-1] // BLK).astype(jnp.int32)
    blk_ids = jnp.minimum(jnp.arange(n_blocks, dtype=jnp.int32), n_valid - 1)
    block_expert = jnp.sum((ends[None, :] <= (blk_ids * BLK)[:, None]).astype(jnp.int32), axis=1)
    block_expert = jnp.minimum(block_expert, N_EXPERTS - 1)
    blk_ord = jnp.cumsum(jnp.concatenate(
        [jnp.zeros((1,), jnp.int32), (block_expert[1:] != block_expert[:-1]).astype(jnp.int32)]))
    dstarts = jnp.cumsum(cnt) - cnt
    first_blk = starts // BLK
    pick = block_expert[None, :] == jnp.arange(N_EXPERTS, dtype=jnp.int32)[:, None]
    row0 = jnp.sum(jnp.where(pick, (dstarts - first_blk * BLK)[:, None], 0), axis=0) + blk_ids * BLK
    meta3 = meta.reshape(n_steps, META_ROWS, TT)
    hit = (meta3[None, :, COL_EXPERT:COL_EXPERT + TOP_K, :]
           == jnp.arange(N_EXPERTS, dtype=jnp.int32)[:, None, None, None])
    rank = meta3[:, COL_RANK:COL_RANK + TOP_K, :]
    dest_x = rank + jnp.sum(jnp.where(hit, dstarts[:, None, None, None], 0), axis=0)
    dest = rank + jnp.sum(jnp.where(hit, starts[:, None, None, None], 0), axis=0)

    xs_rows = (n_tok * TOP_K + BLK) * SLAB
    xs = pl.pallas_call(
        _dispatch_kernel,
        grid=(n_tok // TD,),
        in_specs=[
            pl.BlockSpec((TD // TT, TOP_K, TT), lambda i: (i, 0, 0), memory_space=pltpu.SMEM),
            pl.BlockSpec(memory_space=pl.ANY),
        ],
        out_specs=pl.BlockSpec(memory_space=pl.ANY),
        out_shape=jax.ShapeDtypeStruct((xs_rows, LANES), F32),
        scratch_shapes=[pltpu.VMEM((3, TD * SLAB, LANES), F32), pltpu.VMEM((BLK * SLAB, LANES), F32),
                        pltpu.SemaphoreType.DMA((3,)), pltpu.SemaphoreType.DMA((2,)), pltpu.SemaphoreType.DMA],
        compiler_params=pltpu.CompilerParams(dimension_semantics=("arbitrary",)),
        name="dispatch",
    )(dest_x, x1s)

    yb = pl.pallas_call(
        _expert_kernel,
        grid_spec=pltpu.PrefetchScalarGridSpec(
            num_scalar_prefetch=5,
            grid=(n_blocks // BLOCKS_PER_STEP,),
            in_specs=[pl.BlockSpec(memory_space=pl.ANY)] * 4,
            out_specs=pl.BlockSpec((BLOCKS_PER_STEP * BLK * SLAB, LANES), lambda i, *_: (i, 0)),
            scratch_shapes=[
                pltpu.VMEM((2 * BLOCKS_PER_STEP * BLK * SLAB, LANES), F32),
                pltpu.VMEM((2, D, D_EXPERT), F32),
                pltpu.VMEM((2, D, D_EXPERT), F32),
                pltpu.VMEM((2, D_EXPERT, D), F32),
                pltpu.VMEM((2, D, D_EXPERT), BF16),
                pltpu.VMEM((2, D, D_EXPERT), BF16),
                pltpu.VMEM((2, D_EXPERT, D), BF16),
                pltpu.SemaphoreType.DMA((2 * BLOCKS_PER_STEP,)),
                pltpu.SemaphoreType.DMA((2,)),
            ],
        ),
        out_shape=jax.ShapeDtypeStruct((n_slots * SLAB, LANES), F32),
        compiler_params=pltpu.CompilerParams(
            dimension_semantics=("arbitrary",), vmem_limit_bytes=VMEM_LIMIT),
        name="experts",
    )(block_expert, n_valid.reshape(1), row0, blk_ord, padded // BLK, xs, w_gate[0], w_up[0], w_down[0])

    last = n_tok // TD - 1
    out = pl.pallas_call(
        _combine_kernel,
        grid=(n_tok // TD,),
        in_specs=[
            pl.BlockSpec((TD // TT, TOP_K, TT), lambda i: (i, 0, 0), memory_space=pltpu.SMEM),
            pl.BlockSpec((TD // TT, TOP_K, TT), lambda i: (jnp.minimum(i + 1, last), 0, 0), memory_space=pltpu.SMEM),
            pl.BlockSpec((TD * SLAB, LANES), lambda i: (i, 0)),
            pl.BlockSpec((TD, LANES), lambda i: (i, 0)),
            pl.BlockSpec(memory_space=pl.ANY),
            pl.BlockSpec((1, D), lambda i: (0, 0)),
            pl.BlockSpec((1, D), lambda i: (0, 0)),
        ],
        out_specs=pl.BlockSpec((TD, D), lambda i: (i, 0)),
        scratch_shapes=[pltpu.VMEM((TOP_K * TD * SLAB, LANES), F32), pltpu.VMEM((TOP_K * TD * SLAB, LANES), F32),
                        pltpu.SemaphoreType.DMA((2,))],
        out_shape=jax.ShapeDtypeStruct((n_tok, D), F32),
        compiler_params=pltpu.CompilerParams(dimension_semantics=("arbitrary",)),
        name="combine",
    )(dest, dest, x1s, route, yb, ln2_g[0].reshape(1, D), ln2_b[0].reshape(1, D))

    return out.reshape(B, T, D).astype(x.dtype)
```

```python
import functools

import numpy as np
import jax
import jax.numpy as jnp
from jax import lax
from jax.experimental import pallas as pl
from jax.experimental.pallas import tpu as pltpu

D_MODEL = 1024
HG_WIDTH = 512
HG_HEAD_DIM = 128
HG_HEADS = 4
GL_WIDTH = 512
GL_HEADS = 4
GL_DV = 128
GL_DK = 64
GL_QK = 256
GATE_RANK = 16
GATE_TAU = 16.0
N_GROUPS = 8
EXPERTS_PER_GROUP = 8
N_EXPERTS = 64
TOP_K = 2
D_EXPERT = 512
DEPTH = 1
DEEPNORM_ALPHA = (2.0 * DEPTH) ** 0.25
LN_EPS = 1e-5
LOG2E = 1.4426950408889634
IN_SPLITS = (HG_WIDTH, HG_WIDTH, HG_WIDTH, HG_WIDTH, GL_QK, GL_QK, GL_WIDTH, GATE_RANK, GL_WIDTH)

LANES = 128
SUB = 128
N_LEVELS = 7
MXU_LEVELS = 3
TT = 256
TD = 512
BLK = 256
BLOCKS_PER_STEP = 2
PROJ_WIDTHS = (HG_WIDTH, HG_WIDTH, HG_WIDTH, HG_WIDTH, GL_QK, GL_QK, GL_WIDTH, GL_WIDTH, LANES)
PROJ_SLICES = tuple(slice(sum(PROJ_WIDTHS[:n]), sum(PROJ_WIDTHS[:n + 1])) for n in range(len(PROJ_WIDTHS)))
PROJ_COLS = sum(PROJ_WIDTHS)
ROUTE_BASE = N_GROUPS
SLAB = 8
COL_EXPERT, COL_GATE, COL_RANK = 0, 2, 4
META_ROWS = 8
V7X_VMEM_BYTES = 64 * 1024 * 1024
VMEM_LIMIT = V7X_VMEM_BYTES * 7 // 8
assert DEPTH == 1 and D_MODEL == SLAB * LANES and GL_DK * 2 == LANES and SUB == 1 << N_LEVELS

F32 = jnp.float32
BF16 = jnp.bfloat16
NT_DIMS = (((1,), (1,)), ((), ()))
TN_DIMS = (((0,), (0,)), ((), ()))


def _decay_matrices():
    t = np.arange(SUB)[:, None]
    j = np.arange(SUB)[None, :]
    mats = [(j <= t)]
    for lvl in range(1, MXU_LEVELS + 1):
        blk = 1 << lvl
        half = blk >> 1
        r = (t // blk) * blk + half - 1
        right = (t % blk) >= half
        mats.append(np.where(right, (j > r) & (j <= t), (j > t) & (j <= r)))
    w = np.concatenate(mats, axis=0).astype(np.float32)
    return np.concatenate([w, w], axis=1)


def _level_map():
    t = np.arange(SUB)[:, None]
    s = np.arange(SUB)[None, :]
    x = np.bitwise_xor(t, s)
    lvl = np.floor(np.log2(np.maximum(x, 1))).astype(np.int32) + 1
    lvl = np.where(x == 0, 0, lvl)
    return np.where(s > t, -1, lvl).astype(np.int32)


def _layer_norm(y, g, b):
    mu = jnp.mean(y, axis=-1, keepdims=True)
    yc = y - mu
    var = jnp.mean(yc * yc, axis=-1, keepdims=True)
    return yc * lax.rsqrt(var + LN_EPS) * g + b


def _prefix_rows(x):
    n = x.shape[0]
    row = lax.broadcasted_iota(jnp.int32, (n, 1), 0)
    inc = x
    sh = 1
    while sh < n:
        inc = inc + jnp.where(row >= sh, pltpu.roll(inc, sh, axis=0), 0.0)
        sh *= 2
    return inc - x


def _mixer_kernel(x_ref, winf_ref, wa2_ref, ba_ref, lbl_ref, gain_ref, woutf_ref, ln1g_ref, ln1b_ref,
                  wr_ref, wstk_ref, lev_ref,
                  x1s_ref, route_ref, meta_ref, counts_ref,
                  win_ref, wout_ref, p_scr, o_scr, b_scr, lg_scr, st_scr, carry_scr, *, steps_per_seq):
    si = pl.program_id(0)
    n_tiles = pl.num_programs(0) - 1

    @pl.when(si == 0)
    def _():
        carry_scr[...] = jnp.zeros_like(carry_scr)
        lg_scr[...] = jnp.zeros_like(lg_scr)
        keep = sum(IN_SPLITS[:7])
        wout_ref[...] = woutf_ref[...].astype(BF16)
        win_ref[:, keep + GL_WIDTH:] = jnp.zeros((D_MODEL, PROJ_COLS - keep - GL_WIDTH), BF16)

        def rows(c, carry):
            r = pl.ds(pl.multiple_of(c * SUB, SUB), SUB)
            win_ref[r, 0:keep] = winf_ref[r, 0:keep].astype(BF16)
            win_ref[r, keep:keep + GL_WIDTH] = winf_ref[r, keep + GATE_RANK:keep + GATE_RANK + GL_WIDTH].astype(BF16)
            win_ref[r, keep + GL_WIDTH:keep + GL_WIDTH + GATE_RANK] = winf_ref[r, keep:keep + GATE_RANK].astype(BF16)
            return carry
        lax.fori_loop(0, D_MODEL // SUB, rows, 0)

    @pl.when(si % steps_per_seq == 0)
    def _():
        st_scr[...] = jnp.zeros_like(st_scr)

    xt = x_ref[...]
    p_scr[...] = jnp.dot(xt.astype(BF16), win_ref[...], preferred_element_type=F32)
    _route_tile(si, lg_scr[...], carry_scr, route_ref, meta_ref, counts_ref)

    lbl = lbl_ref[...]
    lbe = jnp.exp(lbl - jnp.max(lbl, axis=0, keepdims=True))
    lb = lbe[0:1, :] / jnp.sum(lbe, axis=0, keepdims=True)

    lev = lev_ref[...]
    tcol = lax.broadcasted_iota(jnp.int32, (SUB, 1), 0)
    qrow = [(tcol & ((1 << l) - 1)) >= (1 << (l - 1)) for l in range(1, N_LEVELS + 1)]
    lane = lax.broadcasted_iota(jnp.int32, (1, LANES), 1)
    half_mask = [(lane < GL_DK), (lane >= GL_DK)]

    def by_halves(l, left_fn, right_fn):
        blk = 1 << l
        half = blk >> 1
        parts = []
        for j in range(SUB // blk):
            lo = j * blk
            parts += [left_fn(lo, lo + half), right_fn(lo + half, lo + blk)]
        return jnp.concatenate(parts, axis=0)

    def subtile(s, carry):
        r0 = pl.multiple_of(s * SUB, SUB)
        rows = pl.ds(r0, SUB)
        hq, hf, hv, hgate, gq, gk, gv, ggate, ga = [p_scr[rows, c] for c in PROJ_SLICES]

        f = lb + (1.0 - lb) * jax.nn.sigmoid(hf)
        kh = 1.0 - f
        lgh = jnp.log(f)
        z = jnp.dot(ga.astype(BF16), wa2_ref[...], preferred_element_type=F32) + ba_ref[...]
        lga = (jnp.minimum(z, 0.0) - jnp.log1p(jnp.exp(-jnp.abs(z)))) * (1.0 / GATE_TAU)
        lg = jnp.concatenate([lgh, lga], axis=1) * LOG2E

        l_hi = lg.astype(BF16)
        l_mid = (lg - l_hi.astype(F32)).astype(BF16)
        sums = jnp.dot(wstk_ref[...], jnp.concatenate([l_hi, l_mid], axis=0),
                       preferred_element_type=F32)
        bcum = sums[0:SUB]
        b_scr[...] = bcum
        expos = [bcum, b_scr[SUB - 1:SUB, :] - bcum]
        for l in range(1, N_LEVELS + 1):
            if l <= MXU_LEVELS:
                expos.append(sums[l * SUB:(l + 1) * SUB])
            else:
                expos.append(by_halves(l, lambda lo, mid: b_scr[mid - 1:mid, :] - bcum[lo:mid],
                                       lambda mid, hi: bcum[mid:hi] - b_scr[mid - 1:mid, :]))
        decs = [jnp.exp2(e) for e in expos]

        def group_terms(q, k, c0):
            e_pre = decs[0][:, c0:c0 + LANES]
            e_suf = decs[1][:, c0:c0 + LANES]
            us = []
            for l in range(1, N_LEVELS + 1):
                e_l = decs[1 + l][:, c0:c0 + LANES]
                if l <= MXU_LEVELS:
                    sel = jnp.where(qrow[l - 1], q, k)
                else:
                    sel = by_halves(l, lambda lo, mid: k[lo:mid], lambda mid, hi: q[mid:hi])
                u = sel * e_l
                us.append((u, u.astype(BF16)))
            return us, (q * e_pre).astype(BF16), k * e_suf, e_pre[SUB - 1:SUB, :]

        def scores(us_ab, qks_ab, masks):
            nm = len(masks)
            zero_b = jnp.zeros((SUB, LANES), BF16)
            scs = [[jnp.where(lev == 0, jnp.sum(qk, axis=-1, keepdims=True), 0.0) for qk in qks] for qks in qks_ab]
            for l in range(1, N_LEVELS + 1):
                (ua, uba), (ub_, ubb) = us_ab[0][l - 1], us_ab[1][l - 1]
                rhs = jnp.concatenate([jnp.concatenate([uba, zero_b], axis=1),
                                       jnp.concatenate([zero_b, ubb], axis=1)], axis=0)
                if l <= MXU_LEVELS:
                    rows_l = SUB
                    lhs = jnp.concatenate(
                        [jnp.concatenate([x if m is None else jnp.where(m, x, jnp.zeros_like(x)) for x in (uba, ubb)],
                                         axis=1) for m in masks], axis=0)
                    g = lax.dot_general(lhs, rhs, NT_DIMS, preferred_element_type=F32)
                    for gi in range(2):
                        for n in range(nm):
                            gn = g[n * rows_l:(n + 1) * rows_l, gi * LANES:(gi + 1) * LANES]
                            scs[gi][n] = jnp.where(lev == l, gn, scs[gi][n])
                else:
                    blk = 1 << l
                    half = blk >> 1
                    mids = [j * blk + half for j in range(SUB // blk)]
                    rows_l = SUB // 2
                    uqs = [jnp.concatenate([x[m:m + half] for m in mids], axis=0) for x in (ua, ub_)]
                    lhs = jnp.concatenate(
                        [jnp.concatenate([x if m is None else jnp.where(m, x, 0.0) for x in uqs], axis=1)
                         for m in masks], axis=0)
                    g = lax.dot_general(lhs.astype(BF16), rhs, NT_DIMS, preferred_element_type=F32)
                    for gi in range(2):
                        for n in range(nm):
                            parts = []
                            for j, m in enumerate(mids):
                                gj = g[n * rows_l + j * half:n * rows_l + (j + 1) * half, gi * LANES:(gi + 1) * LANES]
                                parts.append(scs[gi][n][m - half:m])
                                parts.append(jnp.where(lev[m:m + half] == l, gj, scs[gi][n][m:m + half]))
                            scs[gi][n] = jnp.concatenate(parts, axis=0)
            return scs

        def block_diag(a, b):
            za = jnp.zeros_like(a)
            return jnp.concatenate([jnp.concatenate([a, za], axis=1), jnp.concatenate([za, b], axis=1)], axis=0)

        def head_pair(hidx, scs2, qds, kds, vs, g_lasts):
            vb = [v.astype(BF16) for v in vs]
            sts = [st_scr[h] for h in hidx]
            o = jnp.dot(jnp.concatenate([s.astype(BF16) for s in scs2], axis=1), block_diag(*vb),
                        preferred_element_type=F32)
            o = o + lax.dot_general(jnp.concatenate(qds, axis=1), block_diag(*[s.astype(BF16) for s in sts]),
                                    NT_DIMS, preferred_element_type=F32)
            ut = lax.dot_general(jnp.concatenate(vb, axis=1), jnp.concatenate([k.astype(BF16) for k in kds], axis=1),
                                 TN_DIMS, preferred_element_type=F32)
            for n, h in enumerate(hidx):
                st_scr[h] = sts[n] * g_lasts[n] + ut[n * LANES:(n + 1) * LANES, n * LANES:(n + 1) * LANES]
            return o[:, :LANES], o[:, LANES:]

        def finish(o, gate, c0):
            o = o * lax.rsqrt(jnp.mean(o * o, axis=-1, keepdims=True) + LN_EPS)
            o = o * gain_ref[:, c0:c0 + LANES] * (gate * jax.nn.sigmoid(gate))
            o_scr[rows, c0:c0 + LANES] = o

        for g0 in range(0, HG_HEADS, 2):
            terms = []
            for g in (g0, g0 + 1):
                c0 = g * LANES
                q = hq[:, c0:c0 + LANES] * (HG_HEAD_DIM ** -0.5)
                k = kh[:, c0:c0 + LANES]
                terms.append((q * k,) + group_terms(q, k, c0))
            scs = scores([t[1] for t in terms], [[t[0]] for t in terms], [None])
            cols = [slice(g * LANES, (g + 1) * LANES) for g in (g0, g0 + 1)]
            outs = head_pair((g0, g0 + 1), [scs[0][0], scs[1][0]], [t[2] for t in terms], [t[3] for t in terms],
                             [hv[:, c] for c in cols], [t[4] for t in terms])
            for o, c in zip(outs, cols):
                finish(o, hgate[:, c], c.start)

        terms = []
        for p in range(GL_HEADS // 2):
            c0 = p * LANES
            q = gq[:, c0:c0 + LANES] * (GL_DK ** -0.5)
            k = gk[:, c0:c0 + LANES]
            terms.append((q * k,) + group_terms(q, k, HG_WIDTH + c0))
        scs = scores([t[1] for t in terms], [[jnp.where(m, t[0], 0.0) for m in half_mask] for t in terms], half_mask)
        for p in range(GL_HEADS // 2):
            _, _, qd, kd, g_last = terms[p]
            cols = [slice(h * LANES, (h + 1) * LANES) for h in (2 * p, 2 * p + 1)]
            outs = head_pair((HG_HEADS + 2 * p, HG_HEADS + 2 * p + 1), scs[p], [qd, qd],
                             [jnp.where(m, kd, 0.0) for m in half_mask], [gv[:, c] for c in cols], [g_last, g_last])
            for o, c in zip(outs, cols):
                finish(o, ggate[:, c], HG_WIDTH + c.start)
        return carry

    lax.fori_loop(0, TT // SUB, subtile, 0, unroll=True)

    hmix = jnp.dot(o_scr[...].astype(BF16), wout_ref[...], preferred_element_type=F32)
    x1 = _layer_norm(DEEPNORM_ALPHA * xt + hmix, ln1g_ref[...], ln1b_ref[...])
    x1h = x1.astype(BF16)
    x1l = (x1 - x1h.astype(F32)).astype(BF16)
    prod = jnp.dot(jnp.concatenate([x1h, x1l], axis=0), wr_ref[...], preferred_element_type=F32)
    logits = prod[:TT, :LANES] + prod[TT:, :LANES] + prod[:TT, LANES:]

    @pl.when(si < n_tiles)
    def _():
        for s in range(SLAB):
            x1s_ref[pl.ds(s, TT, stride=SLAB), :] = x1[:, s * LANES:(s + 1) * LANES]
        lg_scr[...] = logits


def _route_tile(si, logits, carry_scr, route_ref, meta_ref, counts_ref):
    lane_i = lax.broadcasted_iota(jnp.int32, (TT, LANES), 1)
    lane_f = lane_i.astype(F32)
    neg = jnp.float32(-jnp.inf)
    big = jnp.float32(1e9)

    gl = jnp.where(lane_i < N_GROUPS, logits, neg)
    gmax = jnp.max(gl, axis=-1, keepdims=True)
    gidx = jnp.min(jnp.where(gl == gmax, lane_f, big), axis=-1, keepdims=True)
    p_group = 1.0 / jnp.sum(jnp.exp(gl - gmax), axis=-1, keepdims=True)

    e_lo = ROUTE_BASE + EXPERTS_PER_GROUP * gidx
    el = jnp.where((lane_f >= e_lo) & (lane_f < e_lo + EXPERTS_PER_GROUP), logits, neg)
    m1 = jnp.max(el, axis=-1, keepdims=True)
    i1 = jnp.min(jnp.where(el == m1, lane_f, big), axis=-1, keepdims=True)
    el2 = jnp.where(lane_f == i1, neg, el)
    m2 = jnp.max(el2, axis=-1, keepdims=True)
    i2 = jnp.min(jnp.where(el2 == m2, lane_f, big), axis=-1, keepdims=True)
    r2 = jnp.exp(m2 - m1)
    gate1 = p_group / (1.0 + r2)
    gate2 = p_group * r2 / (1.0 + r2)

    hot1 = lane_f == i1
    hot2 = lane_f == i2
    cnt = jnp.where(si > 0, (hot1 | hot2).astype(F32), 0.0)
    base = carry_scr[...] + _prefix_rows(cnt)
    rank1 = jnp.sum(jnp.where(hot1, base, 0.0), axis=-1, keepdims=True)
    rank2 = jnp.sum(jnp.where(hot2, base, 0.0), axis=-1, keepdims=True)
    total = carry_scr[...] + jnp.sum(cnt, axis=0, keepdims=True)
    carry_scr[...] = total
    counts_ref[...] = total

    route = jnp.zeros((TT, LANES), F32)
    for lane0, vals in ((COL_EXPERT, (i1 - ROUTE_BASE, i2 - ROUTE_BASE)), (COL_GATE, (gate1, gate2)),
                        (COL_RANK, (rank1, rank2))):
        for k, v in enumerate(vals):
            route = jnp.where(lane_i == lane0 + k, v, route)
    route_ref[...] = route
    meta_ref[...] = route.T[0:META_ROWS, :].astype(jnp.int32)


def _dispatch_kernel(dest_ref, x1s_ref, xs_ref, xin, zbuf, in_sems, out_sems, fill_sem):
    i = pl.program_id(0)
    n_steps = pl.num_programs(0)
    tile_rows = TD * SLAB

    def tile_load(t, slot):
        return pltpu.make_async_copy(x1s_ref.at[pl.ds(pl.multiple_of(t * tile_rows, tile_rows), tile_rows), :],
                                     xin.at[slot], in_sems.at[slot])

    def wait_rows(slot, sem):
        for k in range(TOP_K):
            pltpu.make_async_copy(xin.at[slot], xs_ref.at[pl.ds(0, tile_rows), :], sem).wait()

    @pl.when(i == 0)
    def _():
        tile_load(0, 0).start()
        zbuf[...] = jnp.zeros_like(zbuf)
        fill = pltpu.make_async_copy(zbuf, xs_ref.at[pl.ds(xs_ref.shape[0] - BLK * SLAB, BLK * SLAB), :], fill_sem)
        fill.start()
        fill.wait()

    slot = i % 3

    @pl.when(i + 1 < n_steps)
    def _():
        tile_load(i + 1, (i + 1) % 3).start()

    tile_load(i, slot).wait()

    def scatter(static_slot):
        for r in range(TD):
            for k in range(TOP_K):
                dst = pl.multiple_of(dest_ref[r // TT, k, r % TT] * SLAB, SLAB)
                pltpu.make_async_copy(xin.at[static_slot, pl.ds(r * SLAB, SLAB), :], xs_ref.at[pl.ds(dst, SLAB), :],
                                      out_sems.at[i % 2]).start(priority=k)

    for static_slot in range(3):
        pl.when(slot == static_slot)(functools.partial(scatter, static_slot))

    @pl.when(i > 0)
    def _():
        wait_rows((i + 2) % 3, out_sems.at[(i + 1) % 2])

    @pl.when(i + 1 >= n_steps)
    def _():
        wait_rows(slot, out_sems.at[i % 2])


def _expert_kernel(be_ref, nv_ref, row0_ref, ord_ref, nblk_ref, xs_ref, wg_ref, wu_ref, wd_ref, yb_ref,
                   xbuf, wg_f, wu_f, wd_f, wg_s, wu_s, wd_s, sems, wsems):
    i = pl.program_id(0)
    nv = nv_ref[0]
    blk_rows = BLK * SLAB
    blk_a = i * BLOCKS_PER_STEP
    blk_b = blk_a + 1
    slot0 = (i % 2) * BLOCKS_PER_STEP

    def block_copy(b, slot):
        r = pl.multiple_of(row0_ref[b] * SLAB, SLAB)
        return pltpu.make_async_copy(xs_ref.at[pl.ds(r, blk_rows), :],
                                     xbuf.at[pl.ds(slot * blk_rows, blk_rows), :], sems.at[slot])

    def weight_copies(e, ws):
        return [pltpu.make_async_copy(src.at[e], dst.at[ws], wsems.at[ws])
                for src, dst in ((wg_ref, wg_f), (wu_ref, wu_f), (wd_ref, wd_f))]

    def fetch_step(first_block, first_slot):
        for sub in range(BLOCKS_PER_STEP):
            @pl.when(first_block + sub < nv)
            def _():
                block_copy(first_block + sub, first_slot + sub).start()

    @pl.when(i == 0)
    def _():
        fetch_step(0, 0)
        for c in weight_copies(be_ref[0], 0):
            c.start()

    fetch_step(blk_a + BLOCKS_PER_STEP, BLOCKS_PER_STEP - slot0)

    def take_weights(b):
        @pl.when(b < nv)
        def _():
            e = be_ref[b]
            ws = ord_ref[b] % 2

            @pl.when((b == 0) | (e != be_ref[jnp.maximum(b - 1, 0)]))
            def _():
                j = b + nblk_ref[e]

                @pl.when(j < nv)
                def _():
                    for c in weight_copies(be_ref[jnp.minimum(j, nv - 1)], 1 - ws):
                        c.start()

                for c in weight_copies(e, ws):
                    c.wait()
                wg_s[ws] = wg_f[ws].astype(BF16)
                wu_s[ws] = wu_f[ws].astype(BF16)
                wd_s[ws] = wd_f[ws].astype(BF16)

    take_weights(blk_a)
    take_weights(blk_b)

    def compute(b, sub):
        slot = slot0 + sub
        ws = ord_ref[b] % 2
        block_copy(b, slot).wait()
        xb = jnp.concatenate(
            [xbuf[pl.ds(slot * blk_rows + s, BLK, stride=SLAB), :].astype(BF16) for s in range(SLAB)],
            axis=1)
        a = jnp.dot(xb, wg_s[ws], preferred_element_type=F32)
        u = jnp.dot(xb, wu_s[ws], preferred_element_type=F32)
        h = (a * jax.nn.sigmoid(a)) * u
        y = jnp.dot(h.astype(BF16), wd_s[ws], preferred_element_type=F32)
        for s in range(SLAB):
            yb_ref[pl.ds(sub * blk_rows + s, BLK, stride=SLAB), :] = y[:, s * LANES:(s + 1) * LANES]

    def zero(sub):
        yb_ref[pl.ds(sub * blk_rows, blk_rows), :] = jnp.zeros((blk_rows, LANES), F32)

    @pl.when(blk_b < nv)
    def _():
        compute(blk_a, 0)
        compute(blk_b, 1)

    @pl.when((blk_a < nv) & (blk_b >= nv))
    def _():
        compute(blk_a, 0)
        zero(1)

    @pl.when(blk_a >= nv)
    def _():
        zero(0)
        zero(1)


def _combine_kernel(dest_ref, dest_next_ref, x1s_ref, route_ref, yb_ref, g_ref, b_ref, out_ref, ybuf_a, ybuf_b,
                    sems):
    i = pl.program_id(0)
    n_steps = pl.num_programs(0)
    half_rows = TD * SLAB
    par_rows = TOP_K * half_rows
    bufs = (ybuf_a, ybuf_b)

    def start_gather(d_ref, par):
        for r in range(TD):
            for k in range(TOP_K):
                src = pl.multiple_of(d_ref[r // TT, k, r % TT] * SLAB, SLAB)
                pltpu.make_async_copy(yb_ref.at[pl.ds(src, SLAB), :],
                                      bufs[par].at[pl.ds((k * TD + r) * SLAB, SLAB), :],
                                      sems.at[par]).start(priority=k)

    def wait_gather(par):
        pltpu.make_async_copy(yb_ref.at[pl.ds(0, par_rows), :], bufs[par], sems.at[par]).wait()

    @pl.when(i == 0)
    def _():
        start_gather(dest_ref, 0)

    def step(par):
        ybuf = bufs[par]
        wait_gather(par)
        start_gather(dest_next_ref, 1 - par)
        g1 = route_ref[:, COL_GATE:COL_GATE + 1]
        g2 = route_ref[:, COL_GATE + 1:COL_GATE + 2]
        zs = []
        for s in range(SLAB):
            y1 = ybuf[pl.ds(s, TD, stride=SLAB), :]
            y2 = ybuf[pl.ds(half_rows + s, TD, stride=SLAB), :]
            zs.append(DEEPNORM_ALPHA * x1s_ref[pl.ds(s, TD, stride=SLAB), :] + (g1 * y1 + g2 * y2))
        z = jnp.concatenate(zs, axis=1)
        out_ref[...] = _layer_norm(z, g_ref[...], b_ref[...])

        @pl.when(i + 1 >= n_steps)
        def _():
            wait_gather(1 - par)

    for par in range(2):
        pl.when(i % 2 == par)(functools.partial(step, par))


def _const_spec(shape):
    nd = len(shape)
    return pl.BlockSpec(shape, lambda *_: (0,) * nd, pipeline_mode=pl.Buffered(1))


def kernel(x, w_in, w_a2, b_a, lb_logits, norm_h, norm_g, w_out, ln1_g, ln1_b, w_group_router, w_expert_router,
           w_gate, w_up, w_down, ln2_g, ln2_b):
    B, T, D = x.shape
    n_tok = B * T
    assert D == D_MODEL == SLAB * LANES and T % TT == 0 and TD % TT == 0 and n_tok % TD == 0
    n_blocks = n_tok * TOP_K // BLK + N_EXPERTS
    assert n_blocks % BLOCKS_PER_STEP == 0 and BLOCKS_PER_STEP == 2
    n_slots = n_blocks * BLK

    wa2 =jnp.pad(w_a2[0], ((0, LANES - GATE_RANK), (0, 0))).astype(BF16)
    ba = b_a[0].reshape(1, GL_QK).astype(F32)
    gain = jnp.concatenate([norm_h[0], norm_g[0]]).reshape(1, D).astype(F32)
    wr = jnp.concatenate([w_group_router[0], w_expert_router[0]], axis=-1).astype(F32)
    wr = jnp.pad(wr, ((0, 0), (0, LANES - wr.shape[1])))
    wrh = wr.astype(BF16)
    wr2 = jnp.concatenate([wrh, (wr - wrh.astype(F32)).astype(BF16)], axis=1)
    wstk = jnp.asarray(_decay_matrices(), dtype=BF16)
    lev = jnp.asarray(_level_map())
    n_w = wstk.shape[0]

    n_steps = n_tok // TT
    x1s, route, meta, counts = pl.pallas_call(
        functools.partial(_mixer_kernel, steps_per_seq=T // TT),
        grid=(n_steps + 1,),
        in_specs=[
            pl.BlockSpec((TT, D), lambda i: (jnp.minimum(i, n_steps - 1), 0)),
            _const_spec((D, sum(IN_SPLITS))),
            _const_spec((LANES, GL_QK)),
            _const_spec((1, GL_QK)),
            _const_spec((DEPTH + 1, HG_WIDTH)),
            _const_spec((1, D)),
            _const_spec((D, D)),
            _const_spec((1, D)),
            _const_spec((1, D)),
            _const_spec((D, 2 * LANES)),
            _const_spec((n_w, 2 * SUB)),
            _const_spec((SUB, SUB)),
        ],
        out_specs=[
            pl.BlockSpec((TT * SLAB, LANES), lambda i: (jnp.minimum(i, n_steps - 1), 0)),
            pl.BlockSpec((TT, LANES), lambda i: (jnp.maximum(i - 1, 0), 0)),
            pl.BlockSpec((META_ROWS, TT), lambda i: (jnp.maximum(i - 1, 0), 0)),
            pl.BlockSpec((1, LANES), lambda i: (0, 0)),
        ],
        out_shape=[
            jax.ShapeDtypeStruct((n_tok * SLAB, LANES), F32),
            jax.ShapeDtypeStruct((n_tok, LANES), F32),
            jax.ShapeDtypeStruct((n_steps * META_ROWS, TT), jnp.int32),
            jax.ShapeDtypeStruct((1, LANES), F32),
        ],
        scratch_shapes=[
            pltpu.VMEM((D, PROJ_COLS), BF16),
            pltpu.VMEM((D, D), BF16),
            pltpu.VMEM((TT, PROJ_COLS), F32),
            pltpu.VMEM((TT, D), F32),
            pltpu.VMEM((SUB, HG_WIDTH + GL_QK), F32),
            pltpu.VMEM((TT, LANES), F32),
            pltpu.VMEM((HG_HEADS + GL_HEADS, LANES, LANES), F32),
            pltpu.VMEM((1, LANES), F32),
        ],
        compiler_params=pltpu.CompilerParams(
            dimension_semantics=("arbitrary",), vmem_limit_bytes=VMEM_LIMIT),
        name="mixer",
    )(x.reshape(n_tok, D), w_in[0], wa2, ba, lb_logits.astype(F32), gain, w_out[0], ln1_g[0].reshape(1, D),
      ln1_b[0].reshape(1, D), wr2, wstk, lev)

    cnt = counts[0, ROUTE_BASE:ROUTE_BASE + N_EXPERTS].astype(jnp.int32)
    padded = (cnt + BLK - 1) // BLK * BLK
    ends = jnp.cumsum(padded)
    starts = ends - padded
    n_valid = (ends[-1] // BLK).astype(jnp.int32)
    blk_ids = jnp.minimum(jnp.arange(n_blocks, dtype=jnp.int32), n_valid - 1)
    block_expert = jnp.sum((ends[None, :] <= (blk_ids * BLK)[:, None]).astype(jnp.int32), axis=1)
    block_expert = jnp.minimum(block_expert, N_EXPERTS - 1)
    blk_ord = jnp.cumsum(jnp.concatenate(
        [jnp.zeros((1,), jnp.int32), (block_expert[1:] != block_expert[:-1]).astype(jnp.int32)]))
    dstarts = jnp.cumsum(cnt) - cnt
    first_blk = starts // BLK
    pick = block_expert[None, :] == jnp.arange(N_EXPERTS, dtype=jnp.int32)[:, None]
    row0 = jnp.sum(jnp.where(pick, (dstarts - first_blk * BLK)[:, None], 0), axis=0) + blk_ids * BLK
    meta3 = meta.reshape(n_steps, META_ROWS, TT)
    hit = (meta3[None, :, COL_EXPERT:COL_EXPERT + TOP_K, :]
           == jnp.arange(N_EXPERTS, dtype=jnp.int32)[:, None, None, None])
    rank = meta3[:, COL_RANK:COL_RANK + TOP_K, :]
    dest_x = rank + jnp.sum(jnp.where(hit, dstarts[:, None, None, None], 0), axis=0)
    dest = rank + jnp.sum(jnp.where(hit, starts[:, None, None, None], 0), axis=0)

    xs_rows = (n_tok * TOP_K + BLK) * SLAB
    xs = pl.pallas_call(
        _dispatch_kernel,
        grid=(n_tok // TD,),
        in_specs=[
            pl.BlockSpec((TD // TT, TOP_K, TT), lambda i: (i, 0, 0), memory_space=pltpu.SMEM),
            pl.BlockSpec(memory_space=pl.ANY),
        ],
        out_specs=pl.BlockSpec(memory_space=pl.ANY),
        out_shape=jax.ShapeDtypeStruct((xs_rows, LANES), F32),
        scratch_shapes=[pltpu.VMEM((3, TD * SLAB, LANES), F32), pltpu.VMEM((BLK * SLAB, LANES), F32),
                        pltpu.SemaphoreType.DMA((3,)), pltpu.SemaphoreType.DMA((2,)), pltpu.SemaphoreType.DMA],
        compiler_params=pltpu.CompilerParams(dimension_semantics=("arbitrary",)),
        name="dispatch",
    )(dest_x, x1s)

    yb = pl.pallas_call(
        _expert_kernel,
        grid_spec=pltpu.PrefetchScalarGridSpec(
            num_scalar_prefetch=5,
            grid=(n_blocks // BLOCKS_PER_STEP,),
            in_specs=[pl.BlockSpec(memory_space=pl.ANY)] * 4,
            out_specs=pl.BlockSpec((BLOCKS_PER_STEP * BLK * SLAB, LANES), lambda i, *_: (i, 0)),
            scratch_shapes=[
                pltpu.VMEM((2 * BLOCKS_PER_STEP * BLK * SLAB, LANES), F32),
                pltpu.VMEM((2, D, D_EXPERT), F32),
                pltpu.VMEM((2, D, D_EXPERT), F32),
                pltpu.VMEM((2, D_EXPERT, D), F32),
                pltpu.VMEM((2, D, D_EXPERT), BF16),
                pltpu.VMEM((2, D, D_EXPERT), BF16),
                pltpu.VMEM((2, D_EXPERT, D), BF16),
                pltpu.SemaphoreType.DMA((2 * BLOCKS_PER_STEP,)),
                pltpu.SemaphoreType.DMA((2,)),
            ],
        ),
        out_shape=jax.ShapeDtypeStruct((n_slots * SLAB, LANES), F32),
        compiler_params=pltpu.CompilerParams(
            dimension_semantics=("arbitrary",), vmem_limit_bytes=VMEM_LIMIT),
        name="experts",
    )(block_expert, n_valid.reshape(1), row0, blk_ord, padded // BLK, xs, w_gate[0], w_up[0], w_down[0])

    last = n_tok // TD - 1
    out = pl.pallas_call(
        _combine_kernel,
        grid=(n_tok // TD,),
        in_specs=[
            pl.BlockSpec((TD // TT, TOP_K, TT), lambda i: (i, 0, 0), memory_space=pltpu.SMEM),
            pl.BlockSpec((TD // TT, TOP_K, TT), lambda i: (jnp.minimum(i + 1, last), 0, 0), memory_space=pltpu.SMEM),
            pl.BlockSpec((TD * SLAB, LANES), lambda i: (i, 0)),
            pl.BlockSpec((TD, LANES), lambda i: (i, 0)),
            pl.BlockSpec(memory_space=pl.ANY),
            pl.BlockSpec((1, D), lambda i: (0, 0)),
            pl.BlockSpec((1, D), lambda i: (0, 0)),
        ],
        out_specs=pl.BlockSpec((TD, D), lambda i: (i, 0)),
        scratch_shapes=[pltpu.VMEM((TOP_K * TD * SLAB, LANES), F32), pltpu.VMEM((TOP_K * TD * SLAB, LANES), F32),
                        pltpu.SemaphoreType.DMA((2,))],
        out_shape=jax.ShapeDtypeStruct((n_tok, D), F32),
        compiler_params=pltpu.CompilerParams(dimension_semantics=("arbitrary",)),
        name="combine",
    )(dest, dest, x1s, route, yb, ln2_g[0].reshape(1, D), ln2_b[0].reshape(1, D))

    return out.reshape(B, T, D).astype(x.dtype)
```
